```python
import jax, jax.numpy as jnp
from jax import lax
import numpy as np

D_MODEL = 1024
BATCH = 16
SEQ = 256
DEPTH = 2
DEC_BATCH = 8
DEC_SEQ = 2048
PAST_LEN = 256

GRID_W = 64
EPS = 1e-6
MLA_HEADS = 4
Q_LORA = 384
KV_LORA = 256
QK_NOPE = 128
QK_ROPE = 64
V_HEAD = 128
MLA_WIDTH = MLA_HEADS * V_HEAD
ROPE_BASE = 10000.0
Q_BLOCK = 128
LRU_WIDTH = 512
LRU_BLOCKS = 4
LRU_BLOCK = LRU_WIDTH // LRU_BLOCKS
CONV_W = 4
CONV_LEFT = 2
LRU_C = 8.0
IN_COLS = Q_LORA + KV_LORA + QK_ROPE + 2 * LRU_WIDTH
IN_SPLITS = (Q_LORA, Q_LORA + KV_LORA, Q_LORA + KV_LORA + QK_ROPE, Q_LORA + KV_LORA + QK_ROPE + LRU_WIDTH)
POOL_WINDOWS = (2, 4, 8, 16)
POOL_GROUP = D_MODEL // len(POOL_WINDOWS)
PEER_HEADS = 8
N_KEYS = 128
N_EXPERTS = N_KEYS * N_KEYS
PEER_DKEY = 256
PEER_TOPK = 16
PEER_CHUNK = 128

kernel_name = "hybrid_diffusion_mla_rglru_pool_peer_step"


def rmsnorm(x, g):
    xf = x.astype(jnp.float32)
    y = xf * lax.rsqrt(jnp.mean(xf * xf, axis=-1, keepdims=True) + EPS)
    return (y * g.astype(jnp.float32)).astype(x.dtype)


def ada_params(cvec, w_mod, b_mod):
    m = jax.nn.silu(cvec) @ w_mod + b_mod
    return jnp.split(m[:, None, :], 6, axis=-1)


def axial_rope(n_tok):
    n_rows = n_tok // GRID_W
    rows = jnp.repeat(jnp.arange(n_rows, dtype=jnp.float32), GRID_W)
    cols = jnp.tile(jnp.arange(GRID_W, dtype=jnp.float32), n_rows)
    axis_dim = QK_ROPE // 2
    inv_freq = ROPE_BASE ** (-jnp.arange(0, axis_dim, 2, dtype=jnp.float32) / axis_dim)
    ang = jnp.concatenate([rows[:, None] * inv_freq, cols[:, None] * inv_freq], axis=-1)
    return jnp.cos(ang), jnp.sin(ang)


def apply_rope(x, cos, sin):
    xf = x.astype(jnp.float32)
    x1, x2 = xf[..., 0::2], xf[..., 1::2]
    rot = jnp.stack([x1 * cos - x2 * sin, x1 * sin + x2 * cos], axis=-1)
    return rot.reshape(x.shape).astype(x.dtype)


def mla_attention(q_nope, q_rope, k_nope, k_rope, v):
    B, Sq, H, _ = q_nope.shape
    nb = Sq // Q_BLOCK
    scale = (QK_NOPE + QK_ROPE) ** -0.5

    def to_blocks(t):
        return jnp.moveaxis(t.reshape(B, nb, Q_BLOCK, *t.shape[2:]), 1, 0)

    def one_block(qs):
        qn, qr = qs
        s = (jnp.einsum('bqhd,bkhd->bhqk', qn, k_nope).astype(jnp.float32)
             + jnp.einsum('bqhd,bkd->bhqk', qr, k_rope).astype(jnp.float32)) * scale
        p = jax.nn.softmax(s, axis=-1).astype(v.dtype)
        return jnp.einsum('bhqk,bkhd->bqhd', p, v)

    out = lax.map(one_block, (to_blocks(q_nope), to_blocks(q_rope)))
    return jnp.moveaxis(out, 0, 1).reshape(B, Sq, H * V_HEAD)


def depthwise_conv(x, w, b):
    S = x.shape[1]
    xp = jnp.pad(x, ((0, 0), (CONV_LEFT, CONV_W - 1 - CONV_LEFT), (0, 0)))
    y = xp[:, 0:S] * w[0]
    for k in range(1, CONV_W):
        y = y + xp[:, k:k + S] * w[k]
    return y + b


def rglru_coeffs(x, w_r, b_r, w_i, b_i, lam):
    B, S, _ = x.shape
    xf = x.astype(jnp.float32)
    xb = xf.reshape(B, S, LRU_BLOCKS, LRU_BLOCK)
    r = jax.nn.sigmoid(jnp.einsum('bsnc,ncd->bsnd', xb, w_r.astype(jnp.float32)).reshape(B, S, LRU_WIDTH) + b_r)
    i = jax.nn.sigmoid(jnp.einsum('bsnc,ncd->bsnd', xb, w_i.astype(jnp.float32)).reshape(B, S, LRU_WIDTH) + b_i)
    log_a = -LRU_C * r * jax.nn.softplus(-lam.astype(jnp.float32))
    a = jnp.exp(log_a)
    bx = jnp.sqrt(-jnp.expm1(2.0 * log_a)) * (i * xf)
    return a, bx


def linear_scan(a, bx, h0, reverse):
    def combine(l, r):
        a_l, b_l = l
        a_r, b_r = r
        return a_l * a_r, a_r * b_l + b_r
    A, Bc = lax.associative_scan(combine, (a, bx), axis=1, reverse=reverse)
    return A * h0[:, None, :] + Bc


def attn_lru_mixer(h, w_in, g_q, w_uq, g_kv, w_ukv, conv_w, conv_b, w_rg, b_rg, w_ig, b_ig, lam, w_o, ctx=None):
    B, S, _ = h.shape
    cq, ckv, krope, ux, ug = jnp.split(h @ w_in, IN_SPLITS, axis=-1)
    q = (rmsnorm(cq, g_q) @ w_uq).reshape(B, S, MLA_HEADS, QK_NOPE + QK_ROPE)
    q_nope, q_rope = q[..., :QK_NOPE], q[..., QK_NOPE:]
    ckv = rmsnorm(ckv, g_kv)
    if ctx is None:
        ckv_keys, krope_keys = ckv, krope
    else:
        ctx_ckv, ctx_krope, ctx_lru = ctx
        cos, sin = axial_rope(S)
        q_rope = apply_rope(q_rope, cos[:, None, :], sin[:, None, :])
        ckv_keys = jnp.concatenate([ctx_ckv.astype(ckv.dtype), ckv], axis=1)
        krope_keys = jnp.concatenate([ctx_krope.astype(krope.dtype), apply_rope(krope, cos, sin)], axis=1)
    Sk = ckv_keys.shape[1]
    kv = (ckv_keys @ w_ukv).reshape(B, Sk, MLA_HEADS, QK_NOPE + V_HEAD)
    attn = mla_attention(q_nope, q_rope, kv[..., :QK_NOPE], krope_keys, kv[..., QK_NOPE:])
    xc = depthwise_conv(ux, conv_w, conv_b)
    a_f, b_f = rglru_coeffs(xc, w_rg[0], b_rg[0], w_ig[0], b_ig[0], lam[0])
    a_b, b_b = rglru_coeffs(xc, w_rg[1], b_rg[1], w_ig[1], b_ig[1], lam[1])
    if ctx is None:
        h0_f = jnp.zeros((B, LRU_WIDTH), jnp.float32)
        h0_b = jnp.zeros((B, LRU_WIDTH), jnp.float32)
    else:
        h0_f = ctx_lru[:, 0].astype(jnp.float32)
        h0_b = ctx_lru[:, 1].astype(jnp.float32)
    h_f = linear_scan(a_f, b_f, h0_f, reverse=False)
    h_b = linear_scan(a_b, b_b, h0_b, reverse=True)
    rec = ((h_f + h_b) * jax.nn.gelu(ug.astype(jnp.float32))).astype(attn.dtype)
    out = jnp.concatenate([attn, rec], axis=-1) @ w_o
    if ctx is None:
        return out, (ckv, krope, jnp.stack([h_f[:, -1], h_b[:, 0]], axis=1))
    return out, None


def pool_mixer(h, w_pool, s_pool):
    B, S, _ = h.shape
    hf = h.astype(jnp.float32)
    cs = jnp.pad(jnp.cumsum(hf, axis=1), ((0, 0), (1, 0), (0, 0)))
    t = jnp.arange(S)
    outs = []
    for g, w in enumerate(POOL_WINDOWS):
        lo = jnp.clip(t - w // 2, 0, S)
        hi = jnp.clip(t + w - w // 2, 0, S)
        csg = cs[..., g * POOL_GROUP:(g + 1) * POOL_GROUP]
        mean = (csg[:, hi] - csg[:, lo]) / (hi - lo).astype(jnp.float32)[None, :, None]
        outs.append(mean - hf[..., g * POOL_GROUP:(g + 1) * POOL_GROUP])
    d = jnp.stack(outs, axis=2)
    y = jnp.einsum('bsgc,gcd->bsgd', d, w_pool.astype(jnp.float32)).reshape(B, S, D_MODEL)
    return (y * s_pool).astype(h.dtype)


def peer_ffn(h, w_q, sub_keys, u, v):
    B, S, D = h.shape
    xs = h.reshape((B * S) // PEER_CHUNK, PEER_CHUNK, D)
    half = PEER_DKEY // 2

    def chunk(xc):
        q = (xc @ w_q).reshape(PEER_CHUNK, PEER_HEADS, 2, half)
        s = jnp.einsum('thpc,hpnc->thpn', q, sub_keys).astype(jnp.float32)
        sv, si = lax.top_k(s, PEER_TOPK)
        cand = sv[..., 0, :, None] + sv[..., 1, None, :]
        cidx = si[..., 0, :, None] * N_KEYS + si[..., 1, None, :]
        fv, fi = lax.top_k(cand.reshape(PEER_CHUNK, PEER_HEADS, PEER_TOPK * PEER_TOPK), PEER_TOPK)
        eidx = jnp.take_along_axis(cidx.reshape(PEER_CHUNK, PEER_HEADS, PEER_TOPK * PEER_TOPK), fi, axis=-1)
        gates = jax.nn.softmax(fv, axis=-1)
        ue = jnp.take(u, eidx, axis=0)
        ve = jnp.take(v, eidx, axis=0)
        act = jax.nn.gelu(jnp.einsum('thkd,td->thk', ue, xc).astype(jnp.float32))
        return jnp.einsum('thk,thkd->td', (gates * act).astype(xc.dtype), ve)

    return lax.map(chunk, xs).reshape(B, S, D)


def setup_inputs(seed: int = 0) -> dict:
    key = jax.random.key(seed)
    ks = iter(jax.random.split(key, 64))

    def nrm(shape, s):
        return jax.random.normal(next(ks), shape, jnp.float32) * s

    def gain(n):
        return 1.0 + nrm((n,), 0.05)

    def lam_init():
        a_c = jax.random.uniform(next(ks), (2, LRU_WIDTH), jnp.float32, 0.9, 0.999)
        a = a_c ** (1.0 / LRU_C)
        return jnp.log(a) - jnp.log1p(-a)

    d_in = D_MODEL ** -0.5
    inp = {}
    inp['x_prompt'] = nrm((BATCH, SEQ, D_MODEL), 1.0)
    inp['x_sample'] = nrm((DEC_BATCH, DEC_SEQ, D_MODEL), 1.0)
    inp['cache_ckv_l0'] = nrm((DEC_BATCH, PAST_LEN, KV_LORA), 1.0)
    inp['cache_krope_l0'] = nrm((DEC_BATCH, PAST_LEN, QK_ROPE), 1.0)
    inp['state_lru_l0'] = nrm((DEC_BATCH, 2, LRU_WIDTH), 0.5)
    inp['c'] = nrm((DEC_BATCH, D_MODEL), 1.0)
    inp['c_ctx'] = nrm((D_MODEL,), 1.0)
    inp['w_mod_l0'] = nrm((D_MODEL, 6 * D_MODEL), 0.5 * d_in)
    inp['b_mod_l0'] = nrm((6 * D_MODEL,), 0.01)
    inp['w_mod_l1'] = nrm((D_MODEL, 6 * D_MODEL), 0.5 * d_in)
    inp['b_mod_l1'] = nrm((6 * D_MODEL,), 0.01)
    inp['g_mix_l0'] = gain(D_MODEL)
    inp['g_ffn_l0'] = gain(D_MODEL)
    inp['g_mix_l1'] = gain(D_MODEL)
    inp['g_ffn_l1'] = gain(D_MODEL)
    inp['w_in_l0'] = nrm((D_MODEL, IN_COLS), d_in)
    inp['g_q_l0'] = gain(Q_LORA)
    inp['w_uq_l0'] = nrm((Q_LORA, MLA_HEADS * (QK_NOPE + QK_ROPE)), Q_LORA ** -0.5)
    inp['g_kv_l0'] = gain(KV_LORA)
    inp['w_ukv_l0'] = nrm((KV_LORA, MLA_HEADS * (QK_NOPE + V_HEAD)), KV_LORA ** -0.5)
    inp['conv_w_l0'] = nrm((CONV_W, LRU_WIDTH), CONV_W ** -0.5)
    inp['conv_b_l0'] = nrm((LRU_WIDTH,), 0.01)
    inp['w_rg_l0'] = nrm((2, LRU_BLOCKS, LRU_BLOCK, LRU_BLOCK), LRU_BLOCK ** -0.5)
    inp['b_rg_l0'] = nrm((2, LRU_WIDTH), 0.01)
    inp['w_ig_l0'] = nrm((2, LRU_BLOCKS, LRU_BLOCK, LRU_BLOCK), LRU_BLOCK ** -0.5)
    inp['b_ig_l0'] = nrm((2, LRU_WIDTH), 0.01)
    inp['lam_l0'] = lam_init()
    inp['w_o_l0'] = nrm((MLA_WIDTH + LRU_WIDTH, D_MODEL), (MLA_WIDTH + LRU_WIDTH) ** -0.5)
    inp['w_pool_l1'] = nrm((len(POOL_WINDOWS), POOL_GROUP, POOL_GROUP), POOL_GROUP ** -0.5)
    inp['s_pool_l1'] = 1.0 + nrm((D_MODEL,), 0.1)
    for l in range(DEPTH):
        inp['peer_wq_l%d' % l] = nrm((D_MODEL, PEER_HEADS * PEER_DKEY), d_in)
        inp['peer_keys_l%d' % l] = nrm((PEER_HEADS, 2, N_KEYS, PEER_DKEY // 2), (PEER_DKEY // 2) ** -0.5)
        inp['peer_u_l%d' % l] = nrm((N_EXPERTS, D_MODEL), d_in)
        inp['peer_v_l%d' % l] = nrm((N_EXPERTS, D_MODEL), PEER_HEADS ** -0.5)
    inp['g_final'] = gain(D_MODEL)
    return inp


def reference(x_prompt, x_sample, cache_ckv_l0, cache_krope_l0, state_lru_l0, c, c_ctx,
              w_mod_l0, b_mod_l0, w_mod_l1, b_mod_l1,
              g_mix_l0, g_ffn_l0, g_mix_l1, g_ffn_l1,
              w_in_l0, g_q_l0, w_uq_l0, g_kv_l0, w_ukv_l0, conv_w_l0, conv_b_l0,
              w_rg_l0, b_rg_l0, w_ig_l0, b_ig_l0, lam_l0, w_o_l0,
              w_pool_l1, s_pool_l1,
              peer_wq_l0, peer_keys_l0, peer_u_l0, peer_v_l0,
              peer_wq_l1, peer_keys_l1, peer_u_l1, peer_v_l1,
              g_final):
    w_mod = (w_mod_l0, w_mod_l1)
    b_mod = (b_mod_l0, b_mod_l1)
    g_mix = (g_mix_l0, g_mix_l1)
    g_ffn = (g_ffn_l0, g_ffn_l1)
    even_mixers = ((w_in_l0, g_q_l0, w_uq_l0, g_kv_l0, w_ukv_l0, conv_w_l0, conv_b_l0,
                    w_rg_l0, b_rg_l0, w_ig_l0, b_ig_l0, lam_l0, w_o_l0),)
    odd_mixers = ((w_pool_l1, s_pool_l1),)
    peers = ((peer_wq_l0, peer_keys_l0, peer_u_l0, peer_v_l0),
             (peer_wq_l1, peer_keys_l1, peer_u_l1, peer_v_l1))

    def trunk(x, cvec, ctx_caches):
        new_state = []
        for layer in range(DEPTH):
            sh_m, sc_m, gt_m, sh_f, sc_f, gt_f = ada_params(cvec, w_mod[layer], b_mod[layer])
            h = rmsnorm(x, g_mix[layer]) * (1.0 + sc_m) + sh_m
            if layer % 2 == 0:
                ctx = None if ctx_caches is None else ctx_caches[layer // 2]
                mix, st = attn_lru_mixer(h, *even_mixers[layer // 2], ctx=ctx)
                if ctx is None:
                    new_state.append(st)
            else:
                mix = pool_mixer(h, *odd_mixers[layer // 2])
            x = x + gt_m * mix
            h = rmsnorm(x, g_ffn[layer]) * (1.0 + sc_f) + sh_f
            x = x + gt_f * peer_ffn(h, *peers[layer])
        return rmsnorm(x, g_final), new_state

    y_prompt, prompt_state = trunk(x_prompt, c_ctx[None, :], None)
    y_sample, _ = trunk(x_sample, c, ((cache_ckv_l0, cache_krope_l0, state_lru_l0),))
    new_ckv_l0, new_krope_l0, new_lru_l0 = prompt_state[0]
    return (y_prompt, y_sample, new_ckv_l0, new_krope_l0, new_lru_l0)
```

```python
import functools

import numpy as np
import jax
import jax.numpy as jnp
from jax import lax
from jax.experimental import pallas as pl
from jax.experimental.pallas import tpu as pltpu

F32 = jnp.float32
BF16 = jnp.bfloat16

D_MODEL = 1024
EPS = 1e-6
GRID_W = 64
MLA_HEADS = 4
Q_LORA = 384
KV_LORA = 256
QK_NOPE = 128
QK_ROPE = 64
V_HEAD = 128
MLA_WIDTH = MLA_HEADS * V_HEAD
ROPE_BASE = 10000.0
LRU_WIDTH = 512
LRU_BLOCKS = 4
LRU_BLOCK = LRU_WIDTH // LRU_BLOCKS
CONV_W = 4
CONV_LEFT = 2
LRU_C = 8.0
POOL_WINDOWS = (2, 4, 8, 16)
POOL_GROUP = D_MODEL // len(POOL_WINDOWS)
PEER_HEADS = 8
N_KEYS = 128
PEER_DKEY = 256
PEER_TOPK = 16

SUBLANES = 8
LANES = 128
VMEM_LIMIT = 56 * 1024 * 1024

TM = 256
QK_PAD = 256
HALO = 16
PEER_TM = 512
PEER_EBLK = 512
IN_EXT = 2048


def _rms(x, g):
    return x * lax.rsqrt(jnp.mean(x * x, axis=-1, keepdims=True) + EPS) * g


def _cparams(sem):
    return pltpu.CompilerParams(dimension_semantics=sem, vmem_limit_bytes=VMEM_LIMIT)


def _ada_kernel(c_ref, w_ref, b_ref, o_ref):
    c = c_ref[...]
    o_ref[...] = jnp.dot(c * jax.nn.sigmoid(c), w_ref[...], preferred_element_type=F32) + b_ref[...]


def _ada(cpad, w_mod, b_mod):
    n = w_mod.shape[1]
    bn = 768
    return pl.pallas_call(
        _ada_kernel,
        grid=(n // bn,),
        in_specs=[pl.BlockSpec(cpad.shape, lambda j: (0, 0)),
                  pl.BlockSpec((D_MODEL, bn), lambda j: (0, j)),
                  pl.BlockSpec((1, bn), lambda j: (0, j))],
        out_specs=pl.BlockSpec((cpad.shape[0], bn), lambda j: (0, j)),
        out_shape=jax.ShapeDtypeStruct((cpad.shape[0], n), F32),
        compiler_params=_cparams(("arbitrary",)),
    )(cpad, w_mod, b_mod.reshape(1, n))


def _inproj_kernel(x_ref, mod_ref, g_ref, win_ref, gq_ref, wuq_ref, gkv_ref, ck_ref, sk_ref,
                   q_ref, ckv_ref, kr_ref, krr_ref, ux_ref, ug_ref):
    mod = mod_ref[0]
    h = _rms(x_ref[...], g_ref[...]) * (1.0 + mod[1:2]) + mod[0:1]
    y = jnp.dot(h.astype(BF16), win_ref[...], preferred_element_type=F32)
    cq = y[:, 0:Q_LORA]
    ckv = y[:, Q_LORA:Q_LORA + KV_LORA]
    o = Q_LORA + KV_LORA
    kr = y[:, o:o + QK_ROPE]
    krp = y[:, o + 128:o + 128 + QK_ROPE]
    krs = y[:, o + 256:o + 256 + QK_ROPE]
    ux_ref[...] = y[:, o + 384:o + 384 + LRU_WIDTH]
    ug_ref[...] = y[:, o + 384 + LRU_WIDTH:o + 384 + 2 * LRU_WIDTH]
    ckv_ref[...] = _rms(ckv, gkv_ref[...])
    kr_ref[...] = kr
    ck = ck_ref[...]
    sk = sk_ref[...]
    krr_ref[...] = (krp * ck + krs * sk).astype(BF16)
    q = jnp.dot(_rms(cq, gq_ref[...]).astype(BF16), wuq_ref[...], preferred_element_type=F32)
    nw = MLA_HEADS * QK_NOPE
    rw = MLA_HEADS * QK_ROPE
    for hd in range(MLA_HEADS):
        qp = q[:, nw + hd * QK_ROPE:nw + (hd + 1) * QK_ROPE]
        qs = q[:, nw + rw + hd * QK_ROPE:nw + rw + (hd + 1) * QK_ROPE]
        q_ref[hd, :, 0:QK_NOPE] = q[:, hd * QK_NOPE:(hd + 1) * QK_NOPE].astype(BF16)
        q_ref[hd, :, QK_NOPE:QK_NOPE + QK_ROPE] = (qp * ck + qs * sk).astype(BF16)
        q_ref[hd, :, QK_NOPE + QK_ROPE:QK_PAD] = jnp.zeros((TM, QK_PAD - QK_NOPE - QK_ROPE), BF16)


def _attn_kernel(*refs, has_cache, s_new, n_cache):
    if has_cache:
        q_ref, ckv_ref, krr_ref, cckv_ref, ckr_ref, wukv_ref, o_ref, kcat_ref, vv_ref = refs
    else:
        q_ref, ckv_ref, krr_ref, wukv_ref, o_ref, kcat_ref, vv_ref = refs
    sk = n_cache + s_new
    kw = MLA_HEADS * QK_NOPE
    zpad = jnp.zeros((TM, QK_PAD - QK_NOPE - QK_ROPE), BF16)

    def put_keys(row0, ckv_rows, kr_rows):
        kv = jnp.dot(ckv_rows.astype(BF16), wukv_ref[...], preferred_element_type=F32)
        rows = pl.ds(row0, TM)
        for hd in range(MLA_HEADS):
            kcat_ref[hd, rows, 0:QK_NOPE] = kv[:, hd * QK_NOPE:(hd + 1) * QK_NOPE].astype(BF16)
            kcat_ref[hd, rows, QK_NOPE:QK_NOPE + QK_ROPE] = kr_rows.astype(BF16)
            kcat_ref[hd, rows, QK_NOPE + QK_ROPE:QK_PAD] = zpad
        vv_ref[rows, :] = kv[:, kw:].astype(BF16)

    @pl.when(pl.program_id(1) == 0)
    def _():
        if has_cache:
            for c in range(n_cache // TM):
                put_keys(c * TM, cckv_ref[0, c * TM:(c + 1) * TM, :], ckr_ref[0, c * TM:(c + 1) * TM, :])

        def body(c, carry):
            r0 = pl.multiple_of(c * TM, TM)
            put_keys(n_cache + r0, ckv_ref[pl.ds(r0, TM), :], krr_ref[pl.ds(r0, TM), :])
            return carry
        lax.fori_loop(0, s_new // TM, body, 0)

    scale = (QK_NOPE + QK_ROPE) ** -0.5
    for hd in range(MLA_HEADS):
        s = lax.dot_general(q_ref[hd], kcat_ref[hd], (((1,), (1,)), ((), ())),
                            preferred_element_type=F32) * scale
        m = jnp.max(s, axis=-1, keepdims=True)
        e = jnp.exp(s - m)
        l = jnp.sum(e, axis=-1, keepdims=True)
        o = jnp.dot(e.astype(BF16), vv_ref[:, hd * V_HEAD:(hd + 1) * V_HEAD], preferred_element_type=F32)
        o_ref[:, hd * V_HEAD:(hd + 1) * V_HEAD] = (o / l).astype(BF16)
    del sk


def _lru_dir(xp_ref, xc_ref, xn_ref, valid_prev, valid_next, d, reverse, cw_ref, cb_ref,
             wr_ref, br_ref, wi_ref, bi_ref, lam_ref, carry):
    xp = jnp.where(valid_prev, xp_ref[...], 0.0)
    xn = jnp.where(valid_next, xn_ref[...], 0.0)
    xx = jnp.concatenate([xp, xc_ref[...], xn], axis=0)
    n = TM + 2 * SUBLANES
    xc = cb_ref[...]
    for k in range(CONV_W):
        sh = (CONV_LEFT - k) % n
        xs = xx if sh == 0 else pltpu.roll(xx, sh, 0)
        xc = xc + xs[SUBLANES:SUBLANES + TM] * cw_ref[k:k + 1, :]
    rs, is_ = [], []
    for b in range(LRU_BLOCKS):
        xb = xc[:, b * LRU_BLOCK:(b + 1) * LRU_BLOCK]
        rs.append(jnp.dot(xb, wr_ref[d, b], preferred_element_type=F32))
        is_.append(jnp.dot(xb, wi_ref[d, b], preferred_element_type=F32))
    r = jax.nn.sigmoid(jnp.concatenate(rs, axis=-1) + br_ref[d:d + 1, :])
    i = jax.nn.sigmoid(jnp.concatenate(is_, axis=-1) + bi_ref[d:d + 1, :])
    nl = -lam_ref[d:d + 1, :]
    softplus = jnp.maximum(nl, 0.0) + jnp.log1p(jnp.exp(-jnp.abs(nl)))
    log_a = -LRU_C * r * softplus
    a = jnp.exp(log_a)
    bx = jnp.sqrt(jnp.tanh(-log_a) * (a * a + 1.0)) * (i * xc)
    t = lax.broadcasted_iota(jnp.int32, (TM, 1), 0)
    step = 1
    while step < TM:
        if reverse:
            keep = t < TM - step
            sh = TM - step
        else:
            keep = t >= step
            sh = step
        a_s = jnp.where(keep, pltpu.roll(a, sh, 0), 1.0)
        b_s = jnp.where(keep, pltpu.roll(bx, sh, 0), 0.0)
        bx = a * b_s + bx
        a = a * a_s
        step *= 2
    return a * carry + bx


def _lru_kernel(fxp, fxc, fxn, bxp, bxc, bxn, cw_ref, cb_ref, wr_ref, br_ref, wi_ref, bi_ref, lam_ref,
                h0_ref, hf_ref, hb_ref, st_ref, cf_ref, cbk_ref, *, nc):
    c = pl.program_id(1)

    @pl.when(c == 0)
    def _():
        cf_ref[...] = h0_ref[0, 0:1, :]
        cbk_ref[...] = h0_ref[0, 1:2, :]

    params = (cw_ref, cb_ref, wr_ref, br_ref, wi_ref, bi_ref, lam_ref)
    hf = _lru_dir(fxp, fxc, fxn, c > 0, c < nc - 1, 0, False, *params, cf_ref[...])
    hf_ref[...] = hf
    cf_ref[...] = hf[TM - 1:TM, :]
    hb = _lru_dir(bxp, bxc, bxn, c < nc - 1, c > 0, 1, True, *params, cbk_ref[...])
    hb_ref[...] = hb
    cbk_ref[...] = hb[0:1, :]
    st_ref[0, 0:1, :] = hf[TM - 1:TM, :]
    st_ref[0, 1:2, :] = hb[0:1, :]


def _oproj_kernel(x_ref, mod_ref, at_ref, hf_ref, hb_ref, ug_ref, woa_ref, wor_ref, o_ref):
    mod = mod_ref[0]
    rec = ((hf_ref[...] + hb_ref[...]) * jax.nn.gelu(ug_ref[...])).astype(BF16)
    out = (jnp.dot(at_ref[...], woa_ref[...], preferred_element_type=F32)
           + jnp.dot(rec, wor_ref[...], preferred_element_type=F32))
    o_ref[...] = x_ref[...] + mod[2:3] * out


def _pool_kernel(xp_ref, xc_ref, xn_ref, mod_ref, g_ref, wp_ref, sp_ref, o_ref, *, tiles_per_seq, n_prompt_tiles):
    i = pl.program_id(0)
    j = jnp.where(i < n_prompt_tiles, 0, (i - n_prompt_tiles) % tiles_per_seq)
    ntile = jnp.where(i < n_prompt_tiles, 1, tiles_per_seq)
    mod = mod_ref[0]
    g = g_ref[...]

    def hmod(x):
        return _rms(x, g) * (1.0 + mod[1:2]) + mod[0:1]

    x = xc_ref[...]
    hc = hmod(x)
    hp = jnp.where(j > 0, hmod(xp_ref[...]), 0.0)
    hn = jnp.where(j < ntile - 1, hmod(xn_ref[...]), 0.0)
    hh = jnp.concatenate([hp, hc, hn], axis=0)
    n = TM + 2 * HALO
    seq_len = ntile * TM
    t = j * TM + lax.broadcasted_iota(jnp.int32, (TM, 1), 0)
    ys = []
    for gi, w in enumerate(POOL_WINDOWS):
        cols = slice(gi * POOL_GROUP, (gi + 1) * POOL_GROUP)
        p = hh[:, cols]
        p = p + pltpu.roll(p, 1, 0)
        half = 1
        while 2 * half < w:
            p = pltpu.roll(p, half, 0) + pltpu.roll(p, n - half, 0)
            half *= 2
        lo = jnp.maximum(t - w // 2, 0)
        hi = jnp.minimum(t + (w - w // 2), seq_len)
        mean = p[HALO:HALO + TM] / (hi - lo).astype(F32)
        dg = (mean - hc[:, cols]).astype(BF16)
        ys.append(jnp.dot(dg, wp_ref[gi], preferred_element_type=F32))
    y = jnp.concatenate(ys, axis=-1) * sp_ref[...]
    o_ref[...] = x + mod[2:3] * y


def _sort_pairs(n):
    pairs = []

    def merge(lo, cnt, r):
        step = r * 2
        if step < cnt:
            merge(lo, cnt, step)
            merge(lo + r, cnt, step)
            for i in range(lo + r, lo + cnt - r, step):
                pairs.append((i, i + r))
        else:
            pairs.append((lo, lo + r))

    def sort(lo, cnt):
        if cnt > 1:
            m = cnt // 2
            sort(lo, m)
            sort(lo + m, m)
            merge(lo, cnt, 1)

    sort(0, n)
    return pairs


_SORT16 = _sort_pairs(PEER_TOPK)
_HYPER = [(a, b) for a in range(PEER_TOPK) for b in range(PEER_TOPK) if (a + 1) * (b + 1) <= PEER_TOPK]


def _top16_sorted(s):
    k = PEER_TOPK
    x = [s[SUBLANES * r:SUBLANES * (r + 1), :] for r in range(N_KEYS // SUBLANES)]
    for (i, j) in _SORT16:
        hi = jnp.maximum(x[i], x[j])
        lo = jnp.minimum(x[i], x[j])
        x[i], x[j] = hi, lo
    for shift in (4, 2, 1):
        y = [jnp.maximum(x[r], pltpu.roll(x[k - 1 - r], shift, 0)) for r in range(k)]
        stride = k // 2
        while stride >= 1:
            for i in range(k):
                if i & stride == 0:
                    hi = jnp.maximum(y[i], y[i + stride])
                    lo = jnp.minimum(y[i], y[i + stride])
                    y[i], y[i + stride] = hi, lo
            stride //= 2
        x = y
    return x


def _pair_counts(sv1, sv2):
    one = jnp.ones_like(sv1[0])
    zero = jnp.zeros_like(sv1[0])
    cand = [sv1[a] + sv2[b] for (a, b) in _HYPER]
    cnt = [float((a + 1) * (b + 1) - 1) * one for (a, b) in _HYPER]
    for i, (ai, bi) in enumerate(_HYPER):
        for jx in range(i):
            aj, bj = _HYPER[jx]
            if (aj <= ai and bj <= bi) or (ai <= aj and bi <= bj):
                continue
            ge = jnp.where(cand[jx] >= cand[i], one, zero)
            cnt[i] = cnt[i] + ge
            cnt[jx] = cnt[jx] + (one - ge)
    e1 = [jnp.exp(sv1[a] - sv1[0]) for a in range(PEER_TOPK)]
    e2 = [jnp.exp(sv2[b] - sv2[0]) for b in range(PEER_TOPK)]
    n = [zero for _ in range(PEER_TOPK)]
    z = zero
    for i, (a, b) in enumerate(_HYPER):
        sel = jnp.where(cnt[i] < float(PEER_TOPK), one, zero)
        n[a] = n[a] + sel
        z = z + sel * (e1[a] * e2[b])
    return n, 1.0 / z


def _peer_kernel(x_ref, mod_ref, g_ref, wqt_ref, keys_ref, u_ref, vt_ref, gfin_ref, o_ref,
                 h2t_ref, acc_ref, sc_ref, sv_ref, cnt1_ref, e1n_ref, rank2_ref, e2_ref, *, final_norm):
    k = pl.program_id(1)
    tm = PEER_TM
    half = PEER_DKEY // 2

    @pl.when(k == 0)
    def _():
        mod = mod_ref[0]
        h2 = _rms(x_ref[...], g_ref[...]) * (1.0 + mod[4:5]) + mod[3:4]
        h2t_ref[...] = h2.T.astype(BF16)
        acc_ref[...] = jnp.zeros_like(acc_ref)
        qt = jnp.dot(wqt_ref[...], h2t_ref[...], preferred_element_type=F32).astype(BF16)
        for hd in range(PEER_HEADS):
            for p in range(2):
                r0 = (hd * 2 + p) * half
                s = jnp.dot(keys_ref[hd, p], qt[r0:r0 + half, :], preferred_element_type=F32)
                sc_ref[p, hd] = s
                top = _top16_sorted(s)
                for a in range(PEER_TOPK):
                    sv_ref[p, a, hd:hd + 1, :] = top[a][0:1, :]
        sv1 = [sv_ref[0, a] for a in range(PEER_TOPK)]
        sv2 = [sv_ref[1, a] for a in range(PEER_TOPK)]
        n, inv_z = _pair_counts(sv1, sv2)
        for hd in range(PEER_HEADS):
            s1 = sc_ref[0, hd]
            s2 = sc_ref[1, hd]
            cnt1 = jnp.zeros_like(s1)
            rank2 = jnp.zeros_like(s2)
            for a in range(PEER_TOPK):
                cnt1 = jnp.where(s1 == sv1[a][hd:hd + 1, :], n[a][hd:hd + 1, :], cnt1)
                rank2 = rank2 + jnp.where(sv2[a][hd:hd + 1, :] > s2, 1.0, 0.0)
            cnt1_ref[hd] = cnt1
            rank2_ref[hd] = rank2
            e1n_ref[hd] = jnp.exp(s1 - sv1[0][hd:hd + 1, :]) * inv_z[hd:hd + 1, :]
            e2_ref[hd] = jnp.exp(s2 - sv2[0][hd:hd + 1, :])

    act = jax.nn.gelu(jnp.dot(u_ref[...], h2t_ref[...], preferred_element_type=F32))
    ws = []
    for jx in range(PEER_EBLK // N_KEYS):
        i1 = k * (PEER_EBLK // N_KEYS) + jx
        w = jnp.zeros((N_KEYS, tm), F32)
        for hd in range(PEER_HEADS):
            cb = cnt1_ref[hd, pl.ds(i1, 1), :]
            eb = e1n_ref[hd, pl.ds(i1, 1), :]
            w = w + jnp.where(rank2_ref[hd] < cb, e2_ref[hd], 0.0) * eb
        ws.append(w)
    y = (act * jnp.concatenate(ws, axis=0)).astype(BF16)
    acc_ref[...] += jnp.dot(vt_ref[...], y, preferred_element_type=F32)

    @pl.when(k == pl.num_programs(1) - 1)
    def _():
        out = x_ref[...] + mod_ref[0][5:6] * acc_ref[...].T
        if final_norm:
            out = _rms(out, gfin_ref[...])
        o_ref[...] = out


def _rope_tables(s_prompt_tile, s_sample):
    n_rows = s_sample // GRID_W
    rows = jnp.repeat(jnp.arange(n_rows, dtype=F32), GRID_W)
    cols = jnp.tile(jnp.arange(GRID_W, dtype=F32), n_rows)
    axis_dim = QK_ROPE // 2
    inv_freq = ROPE_BASE ** (-jnp.arange(0, axis_dim, 2, dtype=F32) / axis_dim)
    ang = jnp.concatenate([rows[:, None] * inv_freq, cols[:, None] * inv_freq], axis=-1)
    cos, sin = jnp.cos(ang), jnp.sin(ang)
    ck = jnp.concatenate([cos, cos], axis=-1)
    sk = jnp.concatenate([-sin, sin], axis=-1)
    ident_c = jnp.ones((s_prompt_tile, QK_ROPE), F32)
    ident_s = jnp.zeros((s_prompt_tile, QK_ROPE), F32)
    return jnp.concatenate([ident_c, ck], axis=0), jnp.concatenate([ident_s, sk], axis=0)


def kernel(x_prompt, x_sample, cache_ckv_l0, cache_krope_l0, state_lru_l0, c, c_ctx, w_mod_l0, b_mod_l0, w_mod_l1, b_mod_l1, g_mix_l0, g_ffn_l0, g_mix_l1, g_ffn_l1, w_in_l0, g_q_l0, w_uq_l0, g_kv_l0, w_ukv_l0, conv_w_l0, conv_b_l0, w_rg_l0, b_rg_l0, w_ig_l0, b_ig_l0, lam_l0, w_o_l0, w_pool_l1, s_pool_l1, peer_wq_l0, peer_keys_l0, peer_u_l0, peer_v_l0, peer_wq_l1, peer_keys_l1, peer_u_l1, peer_v_l1, g_final):
    nb_p, s_p, d = x_prompt.shape
    nb_s, s_s, _ = x_sample.shape
    n_cache = cache_ckv_l0.shape[1]
    assert d == D_MODEL and s_p == TM and s_s % TM == 0 and n_cache % TM == 0
    t_p = nb_p * s_p
    t_s = nb_s * s_s
    t_all = t_p + t_s
    npt = t_p // TM
    tps = s_s // TM
    ntile = t_all // TM
    assert t_all % PEER_TM == 0 and t_p % PEER_TM == 0 and s_s % PEER_TM == 0

    x0 = jnp.concatenate([x_prompt.reshape(t_p, d), x_sample.reshape(t_s, d)], axis=0)

    ncond = 1 + nb_s
    cpad = jnp.zeros((2 * SUBLANES, d), F32).at[0].set(c_ctx).at[1:ncond].set(c)
    mod0 = _ada(cpad, w_mod_l0, b_mod_l0).reshape(2 * SUBLANES, 6, d)
    mod1 = _ada(cpad, w_mod_l1, b_mod_l1).reshape(2 * SUBLANES, 6, d)

    def cond_row(i):
        return jnp.where(i < npt, 0, 1 + (i - npt) // tps)

    mod_spec = pl.BlockSpec((1, 6, d), lambda i: (cond_row(i), 0, 0))
    row = lambda a: a.reshape(1, -1)

    perm = np.concatenate([np.arange(0, QK_ROPE, 2), np.arange(1, QK_ROPE, 2)])
    perm_sw = np.concatenate([np.arange(1, QK_ROPE, 2), np.arange(0, QK_ROPE, 2)])
    o1 = Q_LORA + KV_LORA
    w_kr = w_in_l0[:, o1:o1 + QK_ROPE]
    z64 = jnp.zeros((d, 128 - QK_ROPE), F32)
    w_in_ext = jnp.concatenate(
        [w_in_l0[:, :o1], w_kr, z64, w_kr[:, perm], z64, w_kr[:, perm_sw], z64, w_in_l0[:, o1 + QK_ROPE:]],
        axis=1).astype(BF16)
    assert w_in_ext.shape[1] == IN_EXT
    wq3 = w_uq_l0.reshape(Q_LORA, MLA_HEADS, QK_NOPE + QK_ROPE)
    w_uq_ext = jnp.concatenate(
        [wq3[:, :, :QK_NOPE].reshape(Q_LORA, -1),
         wq3[:, :, QK_NOPE:][:, :, perm].reshape(Q_LORA, -1),
         wq3[:, :, QK_NOPE:][:, :, perm_sw].reshape(Q_LORA, -1)], axis=1).astype(BF16)
    ck_tab, sk_tab = _rope_tables(TM, s_s)

    def rope_blk(i):
        return jnp.where(i < npt, 0, 1 + (i - npt) % tps)

    tok = lambda w: pl.BlockSpec((TM, w), lambda i: (i, 0))
    full = lambda a: pl.BlockSpec(a.shape, lambda *_: (0,) * a.ndim)
    q, ckv, kr, krr, ux, ug = pl.pallas_call(
        _inproj_kernel,
        grid=(ntile,),
        in_specs=[tok(d), mod_spec, full(row(g_mix_l0)), full(w_in_ext), full(row(g_q_l0)), full(w_uq_ext),
                  full(row(g_kv_l0)),
                  pl.BlockSpec((TM, QK_ROPE), lambda i: (rope_blk(i), 0)),
                  pl.BlockSpec((TM, QK_ROPE), lambda i: (rope_blk(i), 0))],
        out_specs=[pl.BlockSpec((MLA_HEADS, TM, QK_PAD), lambda i: (0, i, 0)),
                   tok(KV_LORA), tok(QK_ROPE), tok(QK_ROPE), tok(LRU_WIDTH), tok(LRU_WIDTH)],
        out_shape=[jax.ShapeDtypeStruct((MLA_HEADS, t_all, QK_PAD), BF16),
                   jax.ShapeDtypeStruct((t_all, KV_LORA), F32),
                   jax.ShapeDtypeStruct((t_all, QK_ROPE), F32),
                   jax.ShapeDtypeStruct((t_all, QK_ROPE), BF16),
                   jax.ShapeDtypeStruct((t_all, LRU_WIDTH), F32),
                   jax.ShapeDtypeStruct((t_all, LRU_WIDTH), F32)],
        compiler_params=_cparams(("arbitrary",)),
    )(x0, mod0, row(g_mix_l0), w_in_ext, row(g_q_l0), w_uq_ext, row(g_kv_l0), ck_tab, sk_tab)

    wkv3 = w_ukv_l0.reshape(KV_LORA, MLA_HEADS, QK_NOPE + V_HEAD)
    w_ukv_ext = jnp.concatenate([wkv3[:, :, :QK_NOPE].reshape(KV_LORA, -1),
                                 wkv3[:, :, QK_NOPE:].reshape(KV_LORA, -1)], axis=1).astype(BF16)

    def attn_call(nb, s_new, tile0, has_cache):
        nq = s_new // TM
        blk0 = tile0 * TM // s_new
        n_c = n_cache if has_cache else 0
        in_specs = [pl.BlockSpec((MLA_HEADS, TM, QK_PAD), lambda b, qi: (0, tile0 + b * nq + qi, 0)),
                    pl.BlockSpec((s_new, KV_LORA), lambda b, qi: (blk0 + b, 0)),
                    pl.BlockSpec((s_new, QK_ROPE), lambda b, qi: (blk0 + b, 0))]
        args = [q, ckv, krr]
        if has_cache:
            in_specs += [pl.BlockSpec((1, n_cache, KV_LORA), lambda b, qi: (b, 0, 0)),
                         pl.BlockSpec((1, n_cache, QK_ROPE), lambda b, qi: (b, 0, 0))]
            args += [cache_ckv_l0, cache_krope_l0[:, :, perm]]
        in_specs.append(pl.BlockSpec(w_ukv_ext.shape, lambda b, qi: (0, 0)))
        args.append(w_ukv_ext)
        return pl.pallas_call(
            functools.partial(_attn_kernel, has_cache=has_cache, s_new=s_new, n_cache=n_c),
            grid=(nb, nq),
            in_specs=in_specs,
            out_specs=pl.BlockSpec((TM, MLA_WIDTH), lambda b, qi: (b * nq + qi, 0)),
            out_shape=jax.ShapeDtypeStruct((nb * s_new, MLA_WIDTH), BF16),
            scratch_shapes=[pltpu.VMEM((MLA_HEADS, n_c + s_new, QK_PAD), BF16),
                            pltpu.VMEM((n_c + s_new, MLA_WIDTH), BF16)],
            compiler_params=_cparams(("arbitrary", "arbitrary")),
        )(*args)

    attn = jnp.concatenate([attn_call(nb_p, s_p, 0, False), attn_call(nb_s, s_s, npt, True)], axis=0)

    def lru_call(nb, s_new, tile0, h0):
        nc = s_new // TM
        r8 = TM // SUBLANES
        last8 = t_all // SUBLANES - 1

        def cur(rev):
            return pl.BlockSpec((TM, LRU_WIDTH),
                                lambda b, cc: (tile0 + b * nc + (nc - 1 - cc if rev else cc), 0))

        def prev(rev):
            return pl.BlockSpec((SUBLANES, LRU_WIDTH), lambda b, cc: (
                jnp.maximum((tile0 + b * nc + (nc - 1 - cc if rev else cc)) * r8 - 1, 0), 0))

        def nxt(rev):
            return pl.BlockSpec((SUBLANES, LRU_WIDTH), lambda b, cc: (
                jnp.minimum((tile0 + b * nc + (nc - 1 - cc if rev else cc) + 1) * r8, last8), 0))

        small = [conv_w_l0, row(conv_b_l0), w_rg_l0, b_rg_l0, w_ig_l0, b_ig_l0, lam_l0]
        return pl.pallas_call(
            functools.partial(_lru_kernel, nc=nc),
            grid=(nb, nc),
            in_specs=[prev(False), cur(False), nxt(False), prev(True), cur(True), nxt(True)]
                     + [full(a) for a in small]
                     + [pl.BlockSpec((1, 2, LRU_WIDTH), lambda b, cc: (b, 0, 0))],
            out_specs=[pl.BlockSpec((TM, LRU_WIDTH), lambda b, cc: (b * nc + cc, 0)),
                       pl.BlockSpec((TM, LRU_WIDTH), lambda b, cc: (b * nc + nc - 1 - cc, 0)),
                       pl.BlockSpec((1, 2, LRU_WIDTH), lambda b, cc: (b, 0, 0))],
            out_shape=[jax.ShapeDtypeStruct((nb * s_new, LRU_WIDTH), F32),
                       jax.ShapeDtypeStruct((nb * s_new, LRU_WIDTH), F32),
                       jax.ShapeDtypeStruct((nb, 2, LRU_WIDTH), F32)],
            scratch_shapes=[pltpu.VMEM((1, LRU_WIDTH), F32), pltpu.VMEM((1, LRU_WIDTH), F32)],
            compiler_params=_cparams(("arbitrary", "arbitrary")),
        )(ux, ux, ux, ux, ux, ux, *small, h0)

    hf_p, hb_p, new_lru = lru_call(nb_p, s_p, 0, jnp.zeros((nb_p, 2, LRU_WIDTH), F32))
    hf_s, hb_s, _ = lru_call(nb_s, s_s, npt, state_lru_l0.astype(F32))
    hf = jnp.concatenate([hf_p, hf_s], axis=0)
    hb = jnp.concatenate([hb_p, hb_s], axis=0)

    w_o = w_o_l0.astype(BF16)
    x1 = pl.pallas_call(
        _oproj_kernel,
        grid=(ntile,),
        in_specs=[tok(d), mod_spec, tok(MLA_WIDTH), tok(LRU_WIDTH), tok(LRU_WIDTH), tok(LRU_WIDTH),
                  pl.BlockSpec((MLA_WIDTH, d), lambda i: (0, 0)), pl.BlockSpec((LRU_WIDTH, d), lambda i: (1, 0))],
        out_specs=tok(d),
        out_shape=jax.ShapeDtypeStruct((t_all, d), F32),
        compiler_params=_cparams(("arbitrary",)),
    )(x0, mod0, attn, hf, hb, ug, w_o, w_o)

    def peer_call(x, mod, g_ffn, w_q, sub_keys, u, v, final_norm):
        n_exp = u.shape[0]
        assert n_exp == N_KEYS * N_KEYS and n_exp % PEER_EBLK == 0
        wqt = w_q.T.astype(BF16)
        keys = sub_keys.astype(BF16)
        ub = u.astype(BF16)
        vt = v.T.astype(BF16)
        tpp = PEER_TM // TM

        def cond_row_p(i):
            return cond_row(i * tpp)

        big = lambda: pltpu.VMEM((PEER_HEADS, N_KEYS, PEER_TM), F32)
        return pl.pallas_call(
            functools.partial(_peer_kernel, final_norm=final_norm),
            grid=(t_all // PEER_TM, n_exp // PEER_EBLK),
            in_specs=[pl.BlockSpec((PEER_TM, d), lambda i, k: (i, 0)),
                      pl.BlockSpec((1, 6, d), lambda i, k: (cond_row_p(i), 0, 0)),
                      pl.BlockSpec((1, d), lambda i, k: (0, 0)),
                      pl.BlockSpec(wqt.shape, lambda i, k: (0, 0)),
                      pl.BlockSpec(keys.shape, lambda i, k: (0, 0, 0, 0)),
                      pl.BlockSpec((PEER_EBLK, d), lambda i, k: (k, 0)),
                      pl.BlockSpec((d, PEER_EBLK), lambda i, k: (0, k)),
                      pl.BlockSpec((1, d), lambda i, k: (0, 0))],
            out_specs=pl.BlockSpec((PEER_TM, d), lambda i, k: (i, 0)),
            out_shape=jax.ShapeDtypeStruct((t_all, d), F32),
            scratch_shapes=[pltpu.VMEM((d, PEER_TM), BF16),
                            pltpu.VMEM((d, PEER_TM), F32),
                            pltpu.VMEM((2, PEER_HEADS, N_KEYS, PEER_TM), F32),
                            pltpu.VMEM((2, PEER_TOPK, PEER_HEADS, PEER_TM), F32),
                            big(), big(), big(), big()],
            compiler_params=_cparams(("arbitrary", "arbitrary")),
        )(x, mod, row(g_ffn), wqt, keys, ub, vt, row(g_final))

    x2 = peer_call(x1, mod0, g_ffn_l0, peer_wq_l0, peer_keys_l0, peer_u_l0, peer_v_l0, False)

    rh = TM // HALO
    lasth = t_all // HALO - 1
    x3 = pl.pallas_call(
        functools.partial(_pool_kernel, tiles_per_seq=tps, n_prompt_tiles=npt),
        grid=(ntile,),
        in_specs=[pl.BlockSpec((HALO, d), lambda i: (jnp.maximum(i * rh - 1, 0), 0)),
                  tok(d),
                  pl.BlockSpec((HALO, d), lambda i: (jnp.minimum((i + 1) * rh, lasth), 0)),
                  mod_spec, full(row(g_mix_l1)),
                  pl.BlockSpec(w_pool_l1.shape, lambda i: (0, 0, 0)), full(row(s_pool_l1))],
        out_specs=tok(d),
        out_shape=jax.ShapeDtypeStruct((t_all, d), F32),
        compiler_params=_cparams(("arbitrary",)),
    )(x2, x2, x2, mod1, row(g_mix_l1), w_pool_l1.astype(BF16), row(s_pool_l1))

    y = peer_call(x3, mod1, g_ffn_l1, peer_wq_l1, peer_keys_l1, peer_u_l1, peer_v_l1, True)

    y_prompt = y[:t_p].reshape(nb_p, s_p, d)
    y_sample = y[t_p:].reshape(nb_s, s_s, d)
    new_ckv = ckv[:t_p].reshape(nb_p, s_p, KV_LORA)
    new_krope = kr[:t_p].reshape(nb_p, s_p, QK_ROPE)
    return (y_prompt, y_sample, new_ckv, new_krope, new_lru)
```

```python
import functools

import numpy as np
import jax
import jax.numpy as jnp
from jax import lax
from jax.experimental import pallas as pl
from jax.experimental.pallas import tpu as pltpu

F32 = jnp.float32
BF16 = jnp.bfloat16

D_MODEL = 1024
EPS = 1e-6
GRID_W = 64
MLA_HEADS = 4
Q_LORA = 384
KV_LORA = 256
QK_NOPE = 128
QK_ROPE = 64
V_HEAD = 128
MLA_WIDTH = MLA_HEADS * V_HEAD
ROPE_BASE = 10000.0
LRU_WIDTH = 512
LRU_BLOCKS = 4
LRU_BLOCK = LRU_WIDTH // LRU_BLOCKS
CONV_W = 4
CONV_LEFT = 2
LRU_C = 8.0
POOL_WINDOWS = (2, 4, 8, 16)
POOL_GROUP = D_MODEL // len(POOL_WINDOWS)
PEER_HEADS = 8
N_KEYS = 128
PEER_DKEY = 256
PEER_TOPK = 16

SUBLANES = 8
LANES = 128
VMEM_LIMIT = 56 * 1024 * 1024

TM = 256
QK_PAD = 256
HALO = 16
PEER_TM = 512
PEER_EBLK = 512
PEER_SUB = 256
BF16_ROWS = 16
IN_EXT = 2048


def _rms(x, g):
    return x * lax.rsqrt(jnp.mean(x * x, axis=-1, keepdims=True) + EPS) * g


def _cparams(sem):
    return pltpu.CompilerParams(dimension_semantics=sem, vmem_limit_bytes=VMEM_LIMIT)


def _ada_kernel(c_ref, w_ref, b_ref, o_ref):
    c = c_ref[...]
    o_ref[...] = jnp.dot(c * jax.nn.sigmoid(c), w_ref[...], preferred_element_type=F32) + b_ref[...]


def _ada(cpad, w_mod, b_mod):
    n = w_mod.shape[1]
    bn = 768
    return pl.pallas_call(
        _ada_kernel,
        grid=(n // bn,),
        in_specs=[pl.BlockSpec(cpad.shape, lambda j: (0, 0)),
                  pl.BlockSpec((D_MODEL, bn), lambda j: (0, j)),
                  pl.BlockSpec((1, bn), lambda j: (0, j))],
        out_specs=pl.BlockSpec((cpad.shape[0], bn), lambda j: (0, j)),
        out_shape=jax.ShapeDtypeStruct((cpad.shape[0], n), F32),
        compiler_params=_cparams(("arbitrary",)),
    )(cpad, w_mod, b_mod.reshape(1, n))


def _inproj_kernel(x_ref, mod_ref, g_ref, win_ref, gq_ref, wuq_ref, gkv_ref, ck_ref, sk_ref,
                   q_ref, ckv_ref, kr_ref, krr_ref, ux_ref, ug_ref):
    mod = mod_ref[0]
    h = _rms(x_ref[...], g_ref[...]) * (1.0 + mod[1:2]) + mod[0:1]
    y = jnp.dot(h.astype(BF16), win_ref[...], preferred_element_type=F32)
    cq = y[:, 0:Q_LORA]
    ckv = y[:, Q_LORA:Q_LORA + KV_LORA]
    o = Q_LORA + KV_LORA
    kr = y[:, o:o + QK_ROPE]
    krp = y[:, o + 128:o + 128 + QK_ROPE]
    krs = y[:, o + 256:o + 256 + QK_ROPE]
    ux_ref[...] = y[:, o + 384:o + 384 + LRU_WIDTH]
    ug_ref[...] = y[:, o + 384 + LRU_WIDTH:o + 384 + 2 * LRU_WIDTH]
    ckv_ref[...] = _rms(ckv, gkv_ref[...])
    kr_ref[...] = kr
    ck = ck_ref[...]
    sk = sk_ref[...]
    krr_ref[...] = (krp * ck + krs * sk).astype(BF16)
    q = jnp.dot(_rms(cq, gq_ref[...]).astype(BF16), wuq_ref[...], preferred_element_type=F32)
    nw = MLA_HEADS * QK_NOPE
    rw = MLA_HEADS * QK_ROPE
    for hd in range(MLA_HEADS):
        qp = q[:, nw + hd * QK_ROPE:nw + (hd + 1) * QK_ROPE]
        qs = q[:, nw + rw + hd * QK_ROPE:nw + rw + (hd + 1) * QK_ROPE]
        q_ref[hd, :, 0:QK_NOPE] = q[:, hd * QK_NOPE:(hd + 1) * QK_NOPE].astype(BF16)
        q_ref[hd, :, QK_NOPE:QK_NOPE + QK_ROPE] = (qp * ck + qs * sk).astype(BF16)
        q_ref[hd, :, QK_NOPE + QK_ROPE:QK_PAD] = jnp.zeros((TM, QK_PAD - QK_NOPE - QK_ROPE), BF16)


def _attn_kernel(*refs, has_cache, s_new, n_cache):
    if has_cache:
        q_ref, ckv_ref, krr_ref, cckv_ref, ckr_ref, wukv_ref, o_ref, kcat_ref, vv_ref = refs
    else:
        q_ref, ckv_ref, krr_ref, wukv_ref, o_ref, kcat_ref, vv_ref = refs
    sk = n_cache + s_new
    kw = MLA_HEADS * QK_NOPE
    zpad = jnp.zeros((TM, QK_PAD - QK_NOPE - QK_ROPE), BF16)

    def put_keys(row0, ckv_rows, kr_rows):
        kv = jnp.dot(ckv_rows.astype(BF16), wukv_ref[...], preferred_element_type=F32)
        rows = pl.ds(row0, TM)
        for hd in range(MLA_HEADS):
            kcat_ref[hd, rows, 0:QK_NOPE] = kv[:, hd * QK_NOPE:(hd + 1) * QK_NOPE].astype(BF16)
            kcat_ref[hd, rows, QK_NOPE:QK_NOPE + QK_ROPE] = kr_rows.astype(BF16)
            kcat_ref[hd, rows, QK_NOPE + QK_ROPE:QK_PAD] = zpad
        vv_ref[rows, :] = kv[:, kw:].astype(BF16)

    @pl.when(pl.program_id(1) == 0)
    def _():
        if has_cache:
            for c in range(n_cache // TM):
                put_keys(c * TM, cckv_ref[0, c * TM:(c + 1) * TM, :], ckr_ref[0, c * TM:(c + 1) * TM, :])

        def body(c, carry):
            r0 = pl.multiple_of(c * TM, TM)
            put_keys(n_cache + r0, ckv_ref[pl.ds(r0, TM), :], krr_ref[pl.ds(r0, TM), :])
            return carry
        lax.fori_loop(0, s_new // TM, body, 0)

    scale = (QK_NOPE + QK_ROPE) ** -0.5
    for hd in range(MLA_HEADS):
        s = lax.dot_general(q_ref[hd], kcat_ref[hd], (((1,), (1,)), ((), ())),
                            preferred_element_type=F32) * scale
        m = jnp.max(s, axis=-1, keepdims=True)
        e = jnp.exp(s - m)
        l = jnp.sum(e, axis=-1, keepdims=True)
        o = jnp.dot(e.astype(BF16), vv_ref[:, hd * V_HEAD:(hd + 1) * V_HEAD], preferred_element_type=F32)
        o_ref[:, hd * V_HEAD:(hd + 1) * V_HEAD] = (o / l).astype(BF16)
    del sk


def _lru_dir(xp_ref, xc_ref, xn_ref, valid_prev, valid_next, d, reverse, cw_ref, cb_ref,
             wr_ref, br_ref, wi_ref, bi_ref, lam_ref, carry):
    xp = jnp.where(valid_prev, xp_ref[...], 0.0)
    xn = jnp.where(valid_next, xn_ref[...], 0.0)
    xx = jnp.concatenate([xp, xc_ref[...], xn], axis=0)
    n = TM + 2 * SUBLANES
    xc = cb_ref[...]
    for k in range(CONV_W):
        sh = (CONV_LEFT - k) % n
        xs = xx if sh == 0 else pltpu.roll(xx, sh, 0)
        xc = xc + xs[SUBLANES:SUBLANES + TM] * cw_ref[k:k + 1, :]
    rs, is_ = [], []
    for b in range(LRU_BLOCKS):
        xb = xc[:, b * LRU_BLOCK:(b + 1) * LRU_BLOCK]
        rs.append(jnp.dot(xb, wr_ref[d, b], preferred_element_type=F32))
        is_.append(jnp.dot(xb, wi_ref[d, b], preferred_element_type=F32))
    r = jax.nn.sigmoid(jnp.concatenate(rs, axis=-1) + br_ref[d:d + 1, :])
    i = jax.nn.sigmoid(jnp.concatenate(is_, axis=-1) + bi_ref[d:d + 1, :])
    nl = -lam_ref[d:d + 1, :]
    softplus = jnp.maximum(nl, 0.0) + jnp.log1p(jnp.exp(-jnp.abs(nl)))
    log_a = -LRU_C * r * softplus
    a = jnp.exp(log_a)
    bx = jnp.sqrt(jnp.tanh(-log_a) * (a * a + 1.0)) * (i * xc)
    t = lax.broadcasted_iota(jnp.int32, (TM, 1), 0)
    step = 1
    while step < TM:
        if reverse:
            keep = t < TM - step
            sh = TM - step
        else:
            keep = t >= step
            sh = step
        a_s = jnp.where(keep, pltpu.roll(a, sh, 0), 1.0)
        b_s = jnp.where(keep, pltpu.roll(bx, sh, 0), 0.0)
        bx = a * b_s + bx
        a = a * a_s
        step *= 2
    return a * carry + bx


def _lru_kernel(fxp, fxc, fxn, bxp, bxc, bxn, cw_ref, cb_ref, wr_ref, br_ref, wi_ref, bi_ref, lam_ref,
                h0_ref, hf_ref, hb_ref, st_ref, cf_ref, cbk_ref, *, nc):
    c = pl.program_id(1)

    @pl.when(c == 0)
    def _():
        cf_ref[...] = h0_ref[0, 0:1, :]
        cbk_ref[...] = h0_ref[0, 1:2, :]

    params = (cw_ref, cb_ref, wr_ref, br_ref, wi_ref, bi_ref, lam_ref)
    hf = _lru_dir(fxp, fxc, fxn, c > 0, c < nc - 1, 0, False, *params, cf_ref[...])
    hf_ref[...] = hf
    cf_ref[...] = hf[TM - 1:TM, :]
    hb = _lru_dir(bxp, bxc, bxn, c < nc - 1, c > 0, 1, True, *params, cbk_ref[...])
    hb_ref[...] = hb
    cbk_ref[...] = hb[0:1, :]
    st_ref[0, 0:1, :] = hf[TM - 1:TM, :]
    st_ref[0, 1:2, :] = hb[0:1, :]


def _oproj_kernel(x_ref, mod_ref, at_ref, hf_ref, hb_ref, ug_ref, woa_ref, wor_ref, o_ref):
    mod = mod_ref[0]
    rec = ((hf_ref[...] + hb_ref[...]) * jax.nn.gelu(ug_ref[...])).astype(BF16)
    out = (jnp.dot(at_ref[...], woa_ref[...], preferred_element_type=F32)
           + jnp.dot(rec, wor_ref[...], preferred_element_type=F32))
    o_ref[...] = x_ref[...] + mod[2:3] * out


def _pool_kernel(xp_ref, xc_ref, xn_ref, mod_ref, g_ref, wp_ref, sp_ref, o_ref, *, tiles_per_seq, n_prompt_tiles):
    i = pl.program_id(0)
    j = jnp.where(i < n_prompt_tiles, 0, (i - n_prompt_tiles) % tiles_per_seq)
    ntile = jnp.where(i < n_prompt_tiles, 1, tiles_per_seq)
    mod = mod_ref[0]
    g = g_ref[...]

    def hmod(x):
        return _rms(x, g) * (1.0 + mod[1:2]) + mod[0:1]

    x = xc_ref[...]
    hc = hmod(x)
    hp = jnp.where(j > 0, hmod(xp_ref[...]), 0.0)
    hn = jnp.where(j < ntile - 1, hmod(xn_ref[...]), 0.0)
    hh = jnp.concatenate([hp, hc, hn], axis=0)
    n = TM + 2 * HALO
    seq_len = ntile * TM
    t = j * TM + lax.broadcasted_iota(jnp.int32, (TM, 1), 0)
    ys = []
    for gi, w in enumerate(POOL_WINDOWS):
        cols = slice(gi * POOL_GROUP, (gi + 1) * POOL_GROUP)
        p = hh[:, cols]
        p = p + pltpu.roll(p, 1, 0)
        half = 1
        while 2 * half < w:
            p = pltpu.roll(p, half, 0) + pltpu.roll(p, n - half, 0)
            half *= 2
        lo = jnp.maximum(t - w // 2, 0)
        hi = jnp.minimum(t + (w - w // 2), seq_len)
        mean = p[HALO:HALO + TM] / (hi - lo).astype(F32)
        dg = (mean - hc[:, cols]).astype(BF16)
        ys.append(jnp.dot(dg, wp_ref[gi], preferred_element_type=F32))
    y = jnp.concatenate(ys, axis=-1) * sp_ref[...]
    o_ref[...] = x + mod[2:3] * y


def _sort_pairs(n):
    pairs = []

    def merge(lo, cnt, r):
        step = r * 2
        if step < cnt:
            merge(lo, cnt, step)
            merge(lo + r, cnt, step)
            for i in range(lo + r, lo + cnt - r, step):
                pairs.append((i, i + r))
        else:
            pairs.append((lo, lo + r))

    def sort(lo, cnt):
        if cnt > 1:
            m = cnt // 2
            sort(lo, m)
            sort(lo + m, m)
            merge(lo, cnt, 1)

    sort(0, n)
    return pairs


_SORT16 = _sort_pairs(PEER_TOPK)
_HYPER = [(a, b) for a in range(PEER_TOPK) for b in range(PEER_TOPK) if (a + 1) * (b + 1) <= PEER_TOPK]


def _top16_sorted(s):
    k = PEER_TOPK
    x = [s[SUBLANES * r:SUBLANES * (r + 1), :] for r in range(N_KEYS // SUBLANES)]
    for (i, j) in _SORT16:
        hi = jnp.maximum(x[i], x[j])
        lo = jnp.minimum(x[i], x[j])
        x[i], x[j] = hi, lo
    for shift in (4, 2, 1):
        y = [jnp.maximum(x[r], pltpu.roll(x[k - 1 - r], shift, 0)) for r in range(k)]
        stride = k // 2
        while stride >= 1:
            for i in range(k):
                if i & stride == 0:
                    hi = jnp.maximum(y[i], y[i + stride])
                    lo = jnp.minimum(y[i], y[i + stride])
                    y[i], y[i + stride] = hi, lo
            stride //= 2
        x = y
    return x


def _pair_counts(sv1, sv2):
    one = jnp.ones_like(sv1[0])
    zero = jnp.zeros_like(sv1[0])
    cand = [sv1[a] + sv2[b] for (a, b) in _HYPER]

    def ordered(i, j):
        (ai, bi), (aj, bj) = _HYPER[i], _HYPER[j]
        return (aj <= ai and bj <= bi) or (ai <= aj and bi <= bj)

    nh = len(_HYPER)
    cnt = [float((a + 1) * (b + 1) - 1 + sum(1 for j in range(i + 1, nh) if not ordered(i, j))) * one
           for i, (a, b) in enumerate(_HYPER)]
    for i in range(nh):
        for jx in range(i):
            if ordered(i, jx):
                continue
            ge = jnp.where(cand[jx] >= cand[i], one, zero)
            cnt[i] = cnt[i] + ge
            cnt[jx] = cnt[jx] - ge
    e1 = [jnp.exp(sv1[a] - sv1[0]) for a in range(PEER_TOPK)]
    e2 = [jnp.exp(sv2[b] - sv2[0]) for b in range(PEER_TOPK)]
    n = [zero for _ in range(PEER_TOPK)]
    z = zero
    for i, (a, b) in enumerate(_HYPER):
        sel = jnp.where(cnt[i] < float(PEER_TOPK), one, zero)
        n[a] = n[a] + sel
        z = z + sel * (e1[a] * e2[b])
    return n, 1.0 / z


def _head_tables(hd, sc_ref, sv_ref, nz_ref, ex_ref, cnt1_ref, e1n_ref, rank2_ref, e2_ref, exact):
    k = PEER_TOPK
    tm = sc_ref.shape[-1]
    nslab = N_KEYS // SUBLANES
    bad = []
    for g in range(tm // LANES):
        lanes = slice(g * LANES, (g + 1) * LANES)

        def bc(ref, *idx):
            return jnp.broadcast_to(ref[idx + (slice(hd, hd + 1), lanes)], (SUBLANES, LANES))

        sv1b = [bc(sv_ref, 0, a) for a in range(k)]
        nb = [bc(nz_ref, a) for a in range(k)]
        izb = bc(nz_ref, k)
        c1 = jnp.zeros((SUBLANES, LANES), F32)
        for r in range(nslab):
            rows = slice(r * SUBLANES, (r + 1) * SUBLANES)
            s1 = sc_ref[0, hd, rows, lanes]
            cnt1 = jnp.zeros_like(s1)
            if exact:
                rank1 = jnp.zeros_like(s1)
                for a in range(k):
                    rank1 = jnp.where(sv1b[a] > s1, float(a + 1), rank1)
                rank1 = rank1 + ex_ref[0, rows, lanes]
                for a in range(k):
                    cnt1 = jnp.where(rank1 == float(a), nb[a], cnt1)
            else:
                for a in range(k):
                    cnt1 = jnp.where(s1 == sv1b[a], nb[a], cnt1)
                c1 = c1 + jnp.where(s1 >= sv1b[k - 1], 1.0, 0.0)
            cnt1_ref[hd, rows, lanes] = cnt1
            e1n_ref[hd, rows, lanes] = jnp.exp(s1 - sv1b[0]) * izb
        sv2b = [bc(sv_ref, 1, a) for a in range(k)]
        c2 = jnp.zeros((SUBLANES, LANES), F32)
        for r in range(N_KEYS // BF16_ROWS):
            rk, e2 = [], []
            for q in range(BF16_ROWS // SUBLANES):
                r0 = r * BF16_ROWS + q * SUBLANES
                rows = slice(r0, r0 + SUBLANES)
                s2 = sc_ref[1, hd, rows, lanes]
                rank2 = jnp.zeros_like(s2)
                for a in range(k):
                    rank2 = jnp.where(sv2b[a] > s2, float(a + 1), rank2)
                if exact:
                    rank2 = jnp.minimum(rank2 + ex_ref[1, rows, lanes], float(k))
                else:
                    c2 = c2 + jnp.where(rank2 < float(k), 1.0, 0.0)
                rk.append(rank2)
                e2.append(jnp.exp(s2 - sv2b[0]))
            rows16 = slice(r * BF16_ROWS, (r + 1) * BF16_ROWS)
            rank2_ref[hd, rows16, lanes] = jnp.concatenate(rk, axis=0).astype(BF16)
            e2_ref[hd, rows16, lanes] = jnp.concatenate(e2, axis=0).astype(BF16)
        if not exact:
            n1 = jnp.sum(c1, axis=0, keepdims=True)
            n2 = jnp.sum(c2, axis=0, keepdims=True)
            b = jnp.where(n1 != float(k), 1.0, 0.0) + jnp.where(n2 != float(k), 1.0, 0.0)
            for svb in (sv1b, sv2b):
                for a in range(k - 1):
                    b = b + jnp.where(svb[a][0:1] == svb[a + 1][0:1], 1.0, 0.0)
            bad.append(b)
    return jnp.concatenate(bad, axis=1) if bad else None


def _tie_offsets(hd, sc_ref, ex_ref):
    tm = sc_ref.shape[-1]
    nidx = lax.broadcasted_iota(jnp.int32, (N_KEYS, tm), 0)
    for p in range(2):
        s = sc_ref[p, hd]

        def body(m, e, s=s, p=p):
            row = sc_ref[p, hd, pl.ds(m, 1), :]
            return e + jnp.where((s == row) & (nidx > m), 1.0, 0.0)

        ex_ref[p] = lax.fori_loop(0, N_KEYS, body, jnp.zeros((N_KEYS, tm), F32))


def _peer_kernel(x_ref, mod_ref, g_ref, wqt_ref, keys_ref, u_ref, vt_ref, gfin_ref, o_ref,
                 h2t_ref, acc_ref, sc_ref, sv_ref, nz_ref, ex_ref, cnt1_ref, e1n_ref, rank2_ref, e2_ref,
                 *, final_norm):
    k = pl.program_id(1)
    tm = PEER_TM
    half = PEER_DKEY // 2

    @pl.when(k == 0)
    def _():
        mod = mod_ref[0]
        h2 = _rms(x_ref[...], g_ref[...]) * (1.0 + mod[4:5]) + mod[3:4]
        h2t_ref[...] = h2.T.astype(BF16)
        acc_ref[...] = jnp.zeros_like(acc_ref)
        qt = jnp.dot(wqt_ref[...], h2t_ref[...], preferred_element_type=F32).astype(BF16)
        for hd in range(PEER_HEADS):
            for p in range(2):
                r0 = (hd * 2 + p) * half
                sc_ref[p, hd] = jnp.dot(keys_ref[hd, p], qt[r0:r0 + half, :],
                                        preferred_element_type=F32)
                for g in range(tm // LANES):
                    lanes = slice(g * LANES, (g + 1) * LANES)
                    top = _top16_sorted(sc_ref[p, hd, :, lanes])
                    for a in range(PEER_TOPK):
                        sv_ref[p, a, hd:hd + 1, lanes] = top[a][0:1, :]
        for g in range(tm // LANES):
            lanes = slice(g * LANES, (g + 1) * LANES)
            sv1 = [sv_ref[0, a, :, lanes] for a in range(PEER_TOPK)]
            sv2 = [sv_ref[1, a, :, lanes] for a in range(PEER_TOPK)]
            n, inv_z = _pair_counts(sv1, sv2)
            for a in range(PEER_TOPK):
                nz_ref[a, :, lanes] = n[a]
            nz_ref[PEER_TOPK, :, lanes] = inv_z
        tabs = (sc_ref, sv_ref, nz_ref, ex_ref, cnt1_ref, e1n_ref, rank2_ref, e2_ref)
        for hd in range(PEER_HEADS):
            bad = _head_tables(hd, *tabs, exact=False)

            @pl.when(jnp.max(bad) > 0.0)
            def _(hd=hd):
                _tie_offsets(hd, sc_ref, ex_ref)
                _head_tables(hd, *tabs, exact=True)

    zero = jnp.zeros((BF16_ROWS, tm), BF16)
    npiece = PEER_EBLK // N_KEYS
    per_sub = PEER_SUB // N_KEYS

    def up_dot(p):
        return jnp.dot(u_ref[p * N_KEYS:(p + 1) * N_KEYS, :], h2t_ref[...], preferred_element_type=F32)

    raw = up_dot(0)
    ys = []
    for p in range(npiece):
        nxt = up_dot(p + 1) if p + 1 < npiece else None
        i1 = k * npiece + p
        cbs = [jnp.broadcast_to(cnt1_ref[hd, pl.ds(i1, 1), :], (BF16_ROWS, tm)).astype(BF16)
               for hd in range(PEER_HEADS)]
        ebs = [jnp.broadcast_to(e1n_ref[hd, pl.ds(i1, 1), :], (BF16_ROWS, tm)).astype(BF16)
               for hd in range(PEER_HEADS)]
        act = jax.nn.gelu(raw).astype(BF16)
        for r in range(N_KEYS // BF16_ROWS):
            rows = slice(r * BF16_ROWS, (r + 1) * BF16_ROWS)
            w = None
            for hd in range(PEER_HEADS):
                t = jnp.where(rank2_ref[hd, rows, :] < cbs[hd], e2_ref[hd, rows, :], zero) * ebs[hd]
                w = t if w is None else w + t
            ys.append(act[rows, :] * w)
        raw = nxt
        if (p + 1) % per_sub == 0:
            e0 = (p + 1 - per_sub) * N_KEYS
            y = jnp.concatenate(ys, axis=0)
            ys = []
            acc_ref[...] += jnp.dot(vt_ref[:, e0:e0 + PEER_SUB], y, preferred_element_type=F32)

    @pl.when(k == pl.num_programs(1) - 1)
    def _():
        out = x_ref[...] + mod_ref[0][5:6] * acc_ref[...].T
        if final_norm:
            out = _rms(out, gfin_ref[...])
        o_ref[...] = out


def _rope_tables(s_prompt_tile, s_sample):
    n_rows = s_sample // GRID_W
    rows = jnp.repeat(jnp.arange(n_rows, dtype=F32), GRID_W)
    cols = jnp.tile(jnp.arange(GRID_W, dtype=F32), n_rows)
    axis_dim = QK_ROPE // 2
    inv_freq = ROPE_BASE ** (-jnp.arange(0, axis_dim, 2, dtype=F32) / axis_dim)
    ang = jnp.concatenate([rows[:, None] * inv_freq, cols[:, None] * inv_freq], axis=-1)
    cos, sin = jnp.cos(ang), jnp.sin(ang)
    ck = jnp.concatenate([cos, cos], axis=-1)
    sk = jnp.concatenate([-sin, sin], axis=-1)
    ident_c = jnp.ones((s_prompt_tile, QK_ROPE), F32)
    ident_s = jnp.zeros((s_prompt_tile, QK_ROPE), F32)
    return jnp.concatenate([ident_c, ck], axis=0), jnp.concatenate([ident_s, sk], axis=0)


def kernel(x_prompt, x_sample, cache_ckv_l0, cache_krope_l0, state_lru_l0, c, c_ctx, w_mod_l0, b_mod_l0, w_mod_l1, b_mod_l1, g_mix_l0, g_ffn_l0, g_mix_l1, g_ffn_l1, w_in_l0, g_q_l0, w_uq_l0, g_kv_l0, w_ukv_l0, conv_w_l0, conv_b_l0, w_rg_l0, b_rg_l0, w_ig_l0, b_ig_l0, lam_l0, w_o_l0, w_pool_l1, s_pool_l1, peer_wq_l0, peer_keys_l0, peer_u_l0, peer_v_l0, peer_wq_l1, peer_keys_l1, peer_u_l1, peer_v_l1, g_final):
    nb_p, s_p, d = x_prompt.shape
    nb_s, s_s, _ = x_sample.shape
    n_cache = cache_ckv_l0.shape[1]
    assert d == D_MODEL and s_p == TM and s_s % TM == 0 and n_cache % TM == 0
    t_p = nb_p * s_p
    t_s = nb_s * s_s
    t_all = t_p + t_s
    npt = t_p // TM
    tps = s_s // TM
    ntile = t_all // TM
    assert t_all % PEER_TM == 0 and t_p % PEER_TM == 0 and s_s % PEER_TM == 0

    x0 = jnp.concatenate([x_prompt.reshape(t_p, d), x_sample.reshape(t_s, d)], axis=0)

    ncond = 1 + nb_s
    cpad = jnp.zeros((2 * SUBLANES, d), F32).at[0].set(c_ctx).at[1:ncond].set(c)
    mod0 = _ada(cpad, w_mod_l0, b_mod_l0).reshape(2 * SUBLANES, 6, d)
    mod1 = _ada(cpad, w_mod_l1, b_mod_l1).reshape(2 * SUBLANES, 6, d)

    def cond_row(i):
        return jnp.where(i < npt, 0, 1 + (i - npt) // tps)

    mod_spec = pl.BlockSpec((1, 6, d), lambda i: (cond_row(i), 0, 0))
    row = lambda a: a.reshape(1, -1)

    perm = np.concatenate([np.arange(0, QK_ROPE, 2), np.arange(1, QK_ROPE, 2)])
    perm_sw = np.concatenate([np.arange(1, QK_ROPE, 2), np.arange(0, QK_ROPE, 2)])
    o1 = Q_LORA + KV_LORA
    w_kr = w_in_l0[:, o1:o1 + QK_ROPE]
    z64 = jnp.zeros((d, 128 - QK_ROPE), F32)
    w_in_ext = jnp.concatenate(
        [w_in_l0[:, :o1], w_kr, z64, w_kr[:, perm], z64, w_kr[:, perm_sw], z64, w_in_l0[:, o1 + QK_ROPE:]],
        axis=1).astype(BF16)
    assert w_in_ext.shape[1] == IN_EXT
    wq3 = w_uq_l0.reshape(Q_LORA, MLA_HEADS, QK_NOPE + QK_ROPE)
    w_uq_ext = jnp.concatenate(
        [wq3[:, :, :QK_NOPE].reshape(Q_LORA, -1),
         wq3[:, :, QK_NOPE:][:, :, perm].reshape(Q_LORA, -1),
         wq3[:, :, QK_NOPE:][:, :, perm_sw].reshape(Q_LORA, -1)], axis=1).astype(BF16)
    ck_tab, sk_tab = _rope_tables(TM, s_s)

    def rope_blk(i):
        return jnp.where(i < npt, 0, 1 + (i - npt) % tps)

    tok = lambda w: pl.BlockSpec((TM, w), lambda i: (i, 0))
    full = lambda a: pl.BlockSpec(a.shape, lambda *_: (0,) * a.ndim)
    q, ckv, kr, krr, ux, ug = pl.pallas_call(
        _inproj_kernel,
        grid=(ntile,),
        in_specs=[tok(d), mod_spec, full(row(g_mix_l0)), full(w_in_ext), full(row(g_q_l0)), full(w_uq_ext),
                  full(row(g_kv_l0)),
                  pl.BlockSpec((TM, QK_ROPE), lambda i: (rope_blk(i), 0)),
                  pl.BlockSpec((TM, QK_ROPE), lambda i: (rope_blk(i), 0))],
        out_specs=[pl.BlockSpec((MLA_HEADS, TM, QK_PAD), lambda i: (0, i, 0)),
                   tok(KV_LORA), tok(QK_ROPE), tok(QK_ROPE), tok(LRU_WIDTH), tok(LRU_WIDTH)],
        out_shape=[jax.ShapeDtypeStruct((MLA_HEADS, t_all, QK_PAD), BF16),
                   jax.ShapeDtypeStruct((t_all, KV_LORA), F32),
                   jax.ShapeDtypeStruct((t_all, QK_ROPE), F32),
                   jax.ShapeDtypeStruct((t_all, QK_ROPE), BF16),
                   jax.ShapeDtypeStruct((t_all, LRU_WIDTH), F32),
                   jax.ShapeDtypeStruct((t_all, LRU_WIDTH), F32)],
        compiler_params=_cparams(("arbitrary",)),
    )(x0, mod0, row(g_mix_l0), w_in_ext, row(g_q_l0), w_uq_ext, row(g_kv_l0), ck_tab, sk_tab)

    wkv3 = w_ukv_l0.reshape(KV_LORA, MLA_HEADS, QK_NOPE + V_HEAD)
    w_ukv_ext = jnp.concatenate([wkv3[:, :, :QK_NOPE].reshape(KV_LORA, -1),
                                 wkv3[:, :, QK_NOPE:].reshape(KV_LORA, -1)], axis=1).astype(BF16)

    def attn_call(nb, s_new, tile0, has_cache):
        nq = s_new // TM
        blk0 = tile0 * TM // s_new
        n_c = n_cache if has_cache else 0
        in_specs = [pl.BlockSpec((MLA_HEADS, TM, QK_PAD), lambda b, qi: (0, tile0 + b * nq + qi, 0)),
                    pl.BlockSpec((s_new, KV_LORA), lambda b, qi: (blk0 + b, 0)),
                    pl.BlockSpec((s_new, QK_ROPE), lambda b, qi: (blk0 + b, 0))]
        args = [q, ckv, krr]
        if has_cache:
            in_specs += [pl.BlockSpec((1, n_cache, KV_LORA), lambda b, qi: (b, 0, 0)),
                         pl.BlockSpec((1, n_cache, QK_ROPE), lambda b, qi: (b, 0, 0))]
            args += [cache_ckv_l0, cache_krope_l0[:, :, perm]]
        in_specs.append(pl.BlockSpec(w_ukv_ext.shape, lambda b, qi: (0, 0)))
        args.append(w_ukv_ext)
        return pl.pallas_call(
            functools.partial(_attn_kernel, has_cache=has_cache, s_new=s_new, n_cache=n_c),
            grid=(nb, nq),
            in_specs=in_specs,
            out_specs=pl.BlockSpec((TM, MLA_WIDTH), lambda b, qi: (b * nq + qi, 0)),
            out_shape=jax.ShapeDtypeStruct((nb * s_new, MLA_WIDTH), BF16),
            scratch_shapes=[pltpu.VMEM((MLA_HEADS, n_c + s_new, QK_PAD), BF16),
                            pltpu.VMEM((n_c + s_new, MLA_WIDTH), BF16)],
            compiler_params=_cparams(("arbitrary", "arbitrary")),
        )(*args)

    attn = jnp.concatenate([attn_call(nb_p, s_p, 0, False), attn_call(nb_s, s_s, npt, True)], axis=0)

    def lru_call(nb, s_new, tile0, h0):
        nc = s_new // TM
        r8 = TM // SUBLANES
        last8 = t_all // SUBLANES - 1

        def cur(rev):
            return pl.BlockSpec((TM, LRU_WIDTH),
                                lambda b, cc: (tile0 + b * nc + (nc - 1 - cc if rev else cc), 0))

        def prev(rev):
            return pl.BlockSpec((SUBLANES, LRU_WIDTH), lambda b, cc: (
                jnp.maximum((tile0 + b * nc + (nc - 1 - cc if rev else cc)) * r8 - 1, 0), 0))

        def nxt(rev):
            return pl.BlockSpec((SUBLANES, LRU_WIDTH), lambda b, cc: (
                jnp.minimum((tile0 + b * nc + (nc - 1 - cc if rev else cc) + 1) * r8, last8), 0))

        small = [conv_w_l0, row(conv_b_l0), w_rg_l0, b_rg_l0, w_ig_l0, b_ig_l0, lam_l0]
        return pl.pallas_call(
            functools.partial(_lru_kernel, nc=nc),
            grid=(nb, nc),
            in_specs=[prev(False), cur(False), nxt(False), prev(True), cur(True), nxt(True)]
                     + [full(a) for a in small]
                     + [pl.BlockSpec((1, 2, LRU_WIDTH), lambda b, cc: (b, 0, 0))],
            out_specs=[pl.BlockSpec((TM, LRU_WIDTH), lambda b, cc: (b * nc + cc, 0)),
                       pl.BlockSpec((TM, LRU_WIDTH), lambda b, cc: (b * nc + nc - 1 - cc, 0)),
                       pl.BlockSpec((1, 2, LRU_WIDTH), lambda b, cc: (b, 0, 0))],
            out_shape=[jax.ShapeDtypeStruct((nb * s_new, LRU_WIDTH), F32),
                       jax.ShapeDtypeStruct((nb * s_new, LRU_WIDTH), F32),
                       jax.ShapeDtypeStruct((nb, 2, LRU_WIDTH), F32)],
            scratch_shapes=[pltpu.VMEM((1, LRU_WIDTH), F32), pltpu.VMEM((1, LRU_WIDTH), F32)],
            compiler_params=_cparams(("arbitrary", "arbitrary")),
        )(ux, ux, ux, ux, ux, ux, *small, h0)

    hf_p, hb_p, new_lru = lru_call(nb_p, s_p, 0, jnp.zeros((nb_p, 2, LRU_WIDTH), F32))
    hf_s, hb_s, _ = lru_call(nb_s, s_s, npt, state_lru_l0.astype(F32))
    hf = jnp.concatenate([hf_p, hf_s], axis=0)
    hb = jnp.concatenate([hb_p, hb_s], axis=0)

    w_o = w_o_l0.astype(BF16)
    x1 = pl.pallas_call(
        _oproj_kernel,
        grid=(ntile,),
        in_specs=[tok(d), mod_spec, tok(MLA_WIDTH), tok(LRU_WIDTH), tok(LRU_WIDTH), tok(LRU_WIDTH),
                  pl.BlockSpec((MLA_WIDTH, d), lambda i: (0, 0)), pl.BlockSpec((LRU_WIDTH, d), lambda i: (1, 0))],
        out_specs=tok(d),
        out_shape=jax.ShapeDtypeStruct((t_all, d), F32),
        compiler_params=_cparams(("arbitrary",)),
    )(x0, mod0, attn, hf, hb, ug, w_o, w_o)

    def peer_call(x, mod, g_ffn, w_q, sub_keys, u, v, final_norm):
        n_exp = u.shape[0]
        assert n_exp == N_KEYS * N_KEYS and n_exp % PEER_EBLK == 0
        wqt = w_q.T.astype(BF16)
        keys = sub_keys.astype(BF16)
        ub = u.astype(BF16)
        vt = v.T.astype(BF16)
        tpp = PEER_TM // TM

        def cond_row_p(i):
            return cond_row(i * tpp)

        big = lambda dt: pltpu.VMEM((PEER_HEADS, N_KEYS, PEER_TM), dt)
        return pl.pallas_call(
            functools.partial(_peer_kernel, final_norm=final_norm),
            grid=(t_all // PEER_TM, n_exp // PEER_EBLK),
            in_specs=[pl.BlockSpec((PEER_TM, d), lambda i, k: (i, 0)),
                      pl.BlockSpec((1, 6, d), lambda i, k: (cond_row_p(i), 0, 0)),
                      pl.BlockSpec((1, d), lambda i, k: (0, 0)),
                      pl.BlockSpec(wqt.shape, lambda i, k: (0, 0)),
                      pl.BlockSpec(keys.shape, lambda i, k: (0, 0, 0, 0)),
                      pl.BlockSpec((PEER_EBLK, d), lambda i, k: (k, 0)),
                      pl.BlockSpec((d, PEER_EBLK), lambda i, k: (0, k)),
                      pl.BlockSpec((1, d), lambda i, k: (0, 0))],
            out_specs=pl.BlockSpec((PEER_TM, d), lambda i, k: (i, 0)),
            out_shape=jax.ShapeDtypeStruct((t_all, d), F32),
            scratch_shapes=[pltpu.VMEM((d, PEER_TM), BF16),
                            pltpu.VMEM((d, PEER_TM), F32),
                            pltpu.VMEM((2, PEER_HEADS, N_KEYS, PEER_TM), F32),
                            pltpu.VMEM((2, PEER_TOPK, PEER_HEADS, PEER_TM), F32),
                            pltpu.VMEM((PEER_TOPK + 1, PEER_HEADS, PEER_TM), F32),
                            pltpu.VMEM((2, N_KEYS, PEER_TM), F32),
                            big(F32), big(F32), big(BF16), big(BF16)],
            compiler_params=_cparams(("arbitrary", "arbitrary")),
        )(x, mod, row(g_ffn), wqt, keys, ub, vt, row(g_final))

    x2 = peer_call(x1, mod0, g_ffn_l0, peer_wq_l0, peer_keys_l0, peer_u_l0, peer_v_l0, False)

    rh = TM // HALO
    lasth = t_all // HALO - 1
    x3 = pl.pallas_call(
        functools.partial(_pool_kernel, tiles_per_seq=tps, n_prompt_tiles=npt),
        grid=(ntile,),
        in_specs=[pl.BlockSpec((HALO, d), lambda i: (jnp.maximum(i * rh - 1, 0), 0)),
                  tok(d),
                  pl.BlockSpec((HALO, d), lambda i: (jnp.minimum((i + 1) * rh, lasth), 0)),
                  mod_spec, full(row(g_mix_l1)),
                  pl.BlockSpec(w_pool_l1.shape, lambda i: (0, 0, 0)), full(row(s_pool_l1))],
        out_specs=tok(d),
        out_shape=jax.ShapeDtypeStruct((t_all, d), F32),
        compiler_params=_cparams(("arbitrary",)),
    )(x2, x2, x2, mod1, row(g_mix_l1), w_pool_l1.astype(BF16), row(s_pool_l1))

    y = peer_call(x3, mod1, g_ffn_l1, peer_wq_l1, peer_keys_l1, peer_u_l1, peer_v_l1, True)

    y_prompt = y[:t_p].reshape(nb_p, s_p, d)
    y_sample = y[t_p:].reshape(nb_s, s_s, d)
    new_ckv = ckv[:t_p].reshape(nb_p, s_p, KV_LORA)
    new_krope = kr[:t_p].reshape(nb_p, s_p, QK_ROPE)
    return (y_prompt, y_sample, new_ckv, new_krope, new_lru)
```

```python
import functools

import numpy as np
import jax
import jax.numpy as jnp
from jax import lax
from jax.experimental import pallas as pl
from jax.experimental.pallas import tpu as pltpu

F32 = jnp.float32
BF16 = jnp.bfloat16

D_MODEL = 1024
EPS = 1e-6
GRID_W = 64
MLA_HEADS = 4
Q_LORA = 384
KV_LORA = 256
QK_NOPE = 128
QK_ROPE = 64
V_HEAD = 128
MLA_WIDTH = MLA_HEADS * V_HEAD
ROPE_BASE = 10000.0
LRU_WIDTH = 512
LRU_BLOCKS = 4
LRU_BLOCK = LRU_WIDTH // LRU_BLOCKS
CONV_W = 4
CONV_LEFT = 2
LRU_C = 8.0
POOL_WINDOWS = (2, 4, 8, 16)
POOL_GROUP = D_MODEL // len(POOL_WINDOWS)
PEER_HEADS = 8
N_KEYS = 128
PEER_DKEY = 256
PEER_TOPK = 16

SUBLANES = 8
LANES = 128
VMEM_LIMIT = 56 * 1024 * 1024

TM = 256
QK_PAD = 256
HALO = 16
PEER_TM = 512
PEER_EBLK = 512
PEER_SUB = 256
BF16_ROWS = 16
IN_EXT = 2048


def _rms(x, g):
    return x * lax.rsqrt(jnp.mean(x * x, axis=-1, keepdims=True) + EPS) * g


def _cparams(sem):
    return pltpu.CompilerParams(dimension_semantics=sem, vmem_limit_bytes=VMEM_LIMIT)


def _ada_kernel(c_ref, w_ref, b_ref, o_ref):
    c = c_ref[...]
    o_ref[...] = jnp.dot(c * jax.nn.sigmoid(c), w_ref[...], preferred_element_type=F32) + b_ref[...]


def _ada(cpad, w_mod, b_mod):
    n = w_mod.shape[1]
    bn = 768
    return pl.pallas_call(
        _ada_kernel,
        grid=(n // bn,),
        in_specs=[pl.BlockSpec(cpad.shape, lambda j: (0, 0)),
                  pl.BlockSpec((D_MODEL, bn), lambda j: (0, j)),
                  pl.BlockSpec((1, bn), lambda j: (0, j))],
        out_specs=pl.BlockSpec((cpad.shape[0], bn), lambda j: (0, j)),
        out_shape=jax.ShapeDtypeStruct((cpad.shape[0], n), F32),
        compiler_params=_cparams(("arbitrary",)),
    )(cpad, w_mod, b_mod.reshape(1, n))


def _inproj_kernel(x_ref, mod_ref, g_ref, win_ref, gq_ref, wuq_ref, gkv_ref, ck_ref, sk_ref,
                   q_ref, ckv_ref, kr_ref, krr_ref, ux_ref, ug_ref):
    mod = mod_ref[0]
    h = _rms(x_ref[...], g_ref[...]) * (1.0 + mod[1:2]) + mod[0:1]
    y = jnp.dot(h.astype(BF16), win_ref[...], preferred_element_type=F32)
    cq = y[:, 0:Q_LORA]
    ckv = y[:, Q_LORA:Q_LORA + KV_LORA]
    o = Q_LORA + KV_LORA
    kr = y[:, o:o + QK_ROPE]
    krp = y[:, o + 128:o + 128 + QK_ROPE]
    krs = y[:, o + 256:o + 256 + QK_ROPE]
    ux_ref[...] = y[:, o + 384:o + 384 + LRU_WIDTH]
    ug_ref[...] = y[:, o + 384 + LRU_WIDTH:o + 384 + 2 * LRU_WIDTH]
    ckv_ref[...] = _rms(ckv, gkv_ref[...])
    kr_ref[...] = kr
    ck = ck_ref[...]
    sk = sk_ref[...]
    krr_ref[...] = (krp * ck + krs * sk).astype(BF16)
    q = jnp.dot(_rms(cq, gq_ref[...]).astype(BF16), wuq_ref[...], preferred_element_type=F32)
    nw = MLA_HEADS * QK_NOPE
    rw = MLA_HEADS * QK_ROPE
    for hd in range(MLA_HEADS):
        qp = q[:, nw + hd * QK_ROPE:nw + (hd + 1) * QK_ROPE]
        qs = q[:, nw + rw + hd * QK_ROPE:nw + rw + (hd + 1) * QK_ROPE]
        q_ref[hd, :, 0:QK_NOPE] = q[:, hd * QK_NOPE:(hd + 1) * QK_NOPE].astype(BF16)
        q_ref[hd, :, QK_NOPE:QK_NOPE + QK_ROPE] = (qp * ck + qs * sk).astype(BF16)
        q_ref[hd, :, QK_NOPE + QK_ROPE:QK_PAD] = jnp.zeros((TM, QK_PAD - QK_NOPE - QK_ROPE), BF16)


def _attn_kernel(*refs, has_cache, s_new, n_cache):
    if has_cache:
        q_ref, ckv_ref, krr_ref, cckv_ref, ckr_ref, wukv_ref, o_ref, kcat_ref, vv_ref = refs
    else:
        q_ref, ckv_ref, krr_ref, wukv_ref, o_ref, kcat_ref, vv_ref = refs
    sk = n_cache + s_new
    kw = MLA_HEADS * QK_NOPE
    zpad = jnp.zeros((TM, QK_PAD - QK_NOPE - QK_ROPE), BF16)

    def put_keys(row0, ckv_rows, kr_rows):
        kv = jnp.dot(ckv_rows.astype(BF16), wukv_ref[...], preferred_element_type=F32)
        rows = pl.ds(row0, TM)
        for hd in range(MLA_HEADS):
            kcat_ref[hd, rows, 0:QK_NOPE] = kv[:, hd * QK_NOPE:(hd + 1) * QK_NOPE].astype(BF16)
            kcat_ref[hd, rows, QK_NOPE:QK_NOPE + QK_ROPE] = kr_rows.astype(BF16)
            kcat_ref[hd, rows, QK_NOPE + QK_ROPE:QK_PAD] = zpad
        vv_ref[rows, :] = kv[:, kw:].astype(BF16)

    @pl.when(pl.program_id(1) == 0)
    def _():
        if has_cache:
            for c in range(n_cache // TM):
                put_keys(c * TM, cckv_ref[0, c * TM:(c + 1) * TM, :], ckr_ref[0, c * TM:(c + 1) * TM, :])

        def body(c, carry):
            r0 = pl.multiple_of(c * TM, TM)
            put_keys(n_cache + r0, ckv_ref[pl.ds(r0, TM), :], krr_ref[pl.ds(r0, TM), :])
            return carry
        lax.fori_loop(0, s_new // TM, body, 0)

    scale = (QK_NOPE + QK_ROPE) ** -0.5
    for hd in range(MLA_HEADS):
        s = lax.dot_general(q_ref[hd], kcat_ref[hd], (((1,), (1,)), ((), ())),
                            preferred_element_type=F32) * scale
        m = jnp.max(s, axis=-1, keepdims=True)
        e = jnp.exp(s - m)
        l = jnp.sum(e, axis=-1, keepdims=True)
        o = jnp.dot(e.astype(BF16), vv_ref[:, hd * V_HEAD:(hd + 1) * V_HEAD], preferred_element_type=F32)
        o_ref[:, hd * V_HEAD:(hd + 1) * V_HEAD] = (o / l).astype(BF16)
    del sk


def _lru_dir(xp_ref, xc_ref, xn_ref, valid_prev, valid_next, d, reverse, cw_ref, cb_ref,
             wr_ref, br_ref, wi_ref, bi_ref, lam_ref, carry):
    xp = jnp.where(valid_prev, xp_ref[...], 0.0)
    xn = jnp.where(valid_next, xn_ref[...], 0.0)
    xx = jnp.concatenate([xp, xc_ref[...], xn], axis=0)
    n = TM + 2 * SUBLANES
    xc = cb_ref[...]
    for k in range(CONV_W):
        sh = (CONV_LEFT - k) % n
        xs = xx if sh == 0 else pltpu.roll(xx, sh, 0)
        xc = xc + xs[SUBLANES:SUBLANES + TM] * cw_ref[k:k + 1, :]
    rs, is_ = [], []
    for b in range(LRU_BLOCKS):
        xb = xc[:, b * LRU_BLOCK:(b + 1) * LRU_BLOCK]
        rs.append(jnp.dot(xb, wr_ref[d, b], preferred_element_type=F32))
        is_.append(jnp.dot(xb, wi_ref[d, b], preferred_element_type=F32))
    r = jax.nn.sigmoid(jnp.concatenate(rs, axis=-1) + br_ref[d:d + 1, :])
    i = jax.nn.sigmoid(jnp.concatenate(is_, axis=-1) + bi_ref[d:d + 1, :])
    nl = -lam_ref[d:d + 1, :]
    softplus = jnp.maximum(nl, 0.0) + jnp.log1p(jnp.exp(-jnp.abs(nl)))
    log_a = -LRU_C * r * softplus
    a = jnp.exp(log_a)
    bx = jnp.sqrt(jnp.tanh(-log_a) * (a * a + 1.0)) * (i * xc)
    t = lax.broadcasted_iota(jnp.int32, (TM, 1), 0)
    step = 1
    while step < TM:
        if reverse:
            keep = t < TM - step
            sh = TM - step
        else:
            keep = t >= step
            sh = step
        a_s = jnp.where(keep, pltpu.roll(a, sh, 0), 1.0)
        b_s = jnp.where(keep, pltpu.roll(bx, sh, 0), 0.0)
        bx = a * b_s + bx
        a = a * a_s
        step *= 2
    return a * carry + bx


def _lru_kernel(fxp, fxc, fxn, bxp, bxc, bxn, cw_ref, cb_ref, wr_ref, br_ref, wi_ref, bi_ref, lam_ref,
                h0_ref, hf_ref, hb_ref, st_ref, cf_ref, cbk_ref, *, nc):
    c = pl.program_id(1)

    @pl.when(c == 0)
    def _():
        cf_ref[...] = h0_ref[0, 0:1, :]
        cbk_ref[...] = h0_ref[0, 1:2, :]

    params = (cw_ref, cb_ref, wr_ref, br_ref, wi_ref, bi_ref, lam_ref)
    hf = _lru_dir(fxp, fxc, fxn, c > 0, c < nc - 1, 0, False, *params, cf_ref[...])
    hf_ref[...] = hf
    cf_ref[...] = hf[TM - 1:TM, :]
    hb = _lru_dir(bxp, bxc, bxn, c < nc - 1, c > 0, 1, True, *params, cbk_ref[...])
    hb_ref[...] = hb
    cbk_ref[...] = hb[0:1, :]
    st_ref[0, 0:1, :] = hf[TM - 1:TM, :]
    st_ref[0, 1:2, :] = hb[0:1, :]


def _oproj_kernel(x_ref, mod_ref, at_ref, hf_ref, hb_ref, ug_ref, woa_ref, wor_ref, o_ref):
    mod = mod_ref[0]
    rec = ((hf_ref[...] + hb_ref[...]) * jax.nn.gelu(ug_ref[...])).astype(BF16)
    out = (jnp.dot(at_ref[...], woa_ref[...], preferred_element_type=F32)
           + jnp.dot(rec, wor_ref[...], preferred_element_type=F32))
    o_ref[...] = x_ref[...] + mod[2:3] * out


def _pool_kernel(xp_ref, xc_ref, xn_ref, mod_ref, g_ref, wp_ref, sp_ref, o_ref, *, tiles_per_seq, n_prompt_tiles):
    i = pl.program_id(0)
    j = jnp.where(i < n_prompt_tiles, 0, (i - n_prompt_tiles) % tiles_per_seq)
    ntile = jnp.where(i < n_prompt_tiles, 1, tiles_per_seq)
    mod = mod_ref[0]
    g = g_ref[...]

    def hmod(x):
        return _rms(x, g) * (1.0 + mod[1:2]) + mod[0:1]

    x = xc_ref[...]
    hc = hmod(x)
    hp = jnp.where(j > 0, hmod(xp_ref[...]), 0.0)
    hn = jnp.where(j < ntile - 1, hmod(xn_ref[...]), 0.0)
    hh = jnp.concatenate([hp, hc, hn], axis=0)
    n = TM + 2 * HALO
    seq_len = ntile * TM
    t = j * TM + lax.broadcasted_iota(jnp.int32, (TM, 1), 0)
    ys = []
    for gi, w in enumerate(POOL_WINDOWS):
        cols = slice(gi * POOL_GROUP, (gi + 1) * POOL_GROUP)
        p = hh[:, cols]
        p = p + pltpu.roll(p, 1, 0)
        half = 1
        while 2 * half < w:
            p = pltpu.roll(p, half, 0) + pltpu.roll(p, n - half, 0)
            half *= 2
        lo = jnp.maximum(t - w // 2, 0)
        hi = jnp.minimum(t + (w - w // 2), seq_len)
        mean = p[HALO:HALO + TM] / (hi - lo).astype(F32)
        dg = (mean - hc[:, cols]).astype(BF16)
        ys.append(jnp.dot(dg, wp_ref[gi], preferred_element_type=F32))
    y = jnp.concatenate(ys, axis=-1) * sp_ref[...]
    o_ref[...] = x + mod[2:3] * y


def _sort_pairs(n):
    pairs = []

    def merge(lo, cnt, r):
        step = r * 2
        if step < cnt:
            merge(lo, cnt, step)
            merge(lo + r, cnt, step)
            for i in range(lo + r, lo + cnt - r, step):
                pairs.append((i, i + r))
        else:
            pairs.append((lo, lo + r))

    def sort(lo, cnt):
        if cnt > 1:
            m = cnt // 2
            sort(lo, m)
            sort(lo + m, m)
            merge(lo, cnt, 1)

    sort(0, n)
    return pairs


_SORT16 = _sort_pairs(PEER_TOPK)
_HYPER = [(a, b) for a in range(PEER_TOPK) for b in range(PEER_TOPK) if (a + 1) * (b + 1) <= PEER_TOPK]


def _top16_sorted(s):
    k = PEER_TOPK
    x = [s[SUBLANES * r:SUBLANES * (r + 1), :] for r in range(N_KEYS // SUBLANES)]
    for (i, j) in _SORT16:
        hi = jnp.maximum(x[i], x[j])
        lo = jnp.minimum(x[i], x[j])
        x[i], x[j] = hi, lo
    for shift in (4, 2, 1):
        y = [jnp.maximum(x[r], pltpu.roll(x[k - 1 - r], shift, 0)) for r in range(k)]
        stride = k // 2
        while stride >= 1:
            for i in range(k):
                if i & stride == 0:
                    hi = jnp.maximum(y[i], y[i + stride])
                    lo = jnp.minimum(y[i], y[i + stride])
                    y[i], y[i + stride] = hi, lo
            stride //= 2
        x = y
    return x


def _pair_counts(sv1, sv2):
    one = jnp.ones_like(sv1[0])
    zero = jnp.zeros_like(sv1[0])
    cand = [sv1[a] + sv2[b] for (a, b) in _HYPER]

    def ordered(i, j):
        (ai, bi), (aj, bj) = _HYPER[i], _HYPER[j]
        return (aj <= ai and bj <= bi) or (ai <= aj and bi <= bj)

    nh = len(_HYPER)
    cnt = [float((a + 1) * (b + 1) - 1 + sum(1 for j in range(i + 1, nh) if not ordered(i, j))) * one
           for i, (a, b) in enumerate(_HYPER)]
    for i in range(nh):
        for jx in range(i):
            if ordered(i, jx):
                continue
            ge = jnp.where(cand[jx] >= cand[i], one, zero)
            cnt[i] = cnt[i] + ge
            cnt[jx] = cnt[jx] - ge
    e1 = [jnp.exp(sv1[a] - sv1[0]) for a in range(PEER_TOPK)]
    e2 = [jnp.exp(sv2[b] - sv2[0]) for b in range(PEER_TOPK)]
    n = [zero for _ in range(PEER_TOPK)]
    z = zero
    for i, (a, b) in enumerate(_HYPER):
        sel = jnp.where(cnt[i] < float(PEER_TOPK), one, zero)
        n[a] = n[a] + sel
        z = z + sel * (e1[a] * e2[b])
    return n, 1.0 / z


def _head_tables(hd, sc_ref, sv_ref, nz_ref, ex_ref, cnt1_ref, e1n_ref, rank2_ref, e2_ref, exact):
    k = PEER_TOPK
    tm = sc_ref.shape[-1]
    nslab = N_KEYS // SUBLANES
    bad = []
    for g in range(tm // LANES):
        lanes = slice(g * LANES, (g + 1) * LANES)

        def bc(ref, *idx):
            return jnp.broadcast_to(ref[idx + (slice(hd, hd + 1), lanes)], (SUBLANES, LANES))

        sv1b = [bc(sv_ref, 0, a) for a in range(k)]
        nb = [bc(nz_ref, a) for a in range(k)]
        izb = bc(nz_ref, k)
        c1 = jnp.zeros((SUBLANES, LANES), F32)
        for r in range(nslab):
            rows = slice(r * SUBLANES, (r + 1) * SUBLANES)
            s1 = sc_ref[0, hd, rows, lanes]
            cnt1 = jnp.zeros_like(s1)
            if exact:
                rank1 = jnp.zeros_like(s1)
                for a in range(k):
                    rank1 = jnp.where(sv1b[a] > s1, float(a + 1), rank1)
                rank1 = rank1 + ex_ref[0, rows, lanes]
                for a in range(k):
                    cnt1 = jnp.where(rank1 == float(a), nb[a], cnt1)
            else:
                for a in range(k):
                    cnt1 = jnp.where(s1 == sv1b[a], nb[a], cnt1)
                c1 = c1 + jnp.where(s1 >= sv1b[k - 1], 1.0, 0.0)
            cnt1_ref[hd, rows, lanes] = cnt1
            e1n_ref[hd, rows, lanes] = jnp.exp(s1 - sv1b[0]) * izb
        sv2b = [bc(sv_ref, 1, a) for a in range(k)]
        c2 = jnp.zeros((SUBLANES, LANES), F32)
        for r in range(N_KEYS // BF16_ROWS):
            rk, e2 = [], []
            for q in range(BF16_ROWS // SUBLANES):
                r0 = r * BF16_ROWS + q * SUBLANES
                rows = slice(r0, r0 + SUBLANES)
                s2 = sc_ref[1, hd, rows, lanes]
                rank2 = jnp.zeros_like(s2)
                for a in range(k):
                    rank2 = jnp.where(sv2b[a] > s2, float(a + 1), rank2)
                if exact:
                    rank2 = jnp.minimum(rank2 + ex_ref[1, rows, lanes], float(k))
                else:
                    c2 = c2 + jnp.where(rank2 < float(k), 1.0, 0.0)
                rk.append(rank2)
                e2.append(jnp.exp(s2 - sv2b[0]))
            rows16 = slice(r * BF16_ROWS, (r + 1) * BF16_ROWS)
            rank2_ref[hd, rows16, lanes] = jnp.concatenate(rk, axis=0).astype(BF16)
            e2_ref[hd, rows16, lanes] = jnp.concatenate(e2, axis=0).astype(BF16)
        if not exact:
            n1 = jnp.sum(c1, axis=0, keepdims=True)
            n2 = jnp.sum(c2, axis=0, keepdims=True)
            b = jnp.where(n1 != float(k), 1.0, 0.0) + jnp.where(n2 != float(k), 1.0, 0.0)
            for svb in (sv1b, sv2b):
                for a in range(k - 1):
                    b = b + jnp.where(svb[a][0:1] == svb[a + 1][0:1], 1.0, 0.0)
            bad.append(b)
    return jnp.concatenate(bad, axis=1) if bad else None


def _tie_offsets(hd, sc_ref, ex_ref):
    tm = sc_ref.shape[-1]
    nidx = lax.broadcasted_iota(jnp.int32, (N_KEYS, tm), 0)
    for p in range(2):
        s = sc_ref[p, hd]

        def body(m, e, s=s, p=p):
            row = sc_ref[p, hd, pl.ds(m, 1), :]
            return e + jnp.where((s == row) & (nidx > m), 1.0, 0.0)

        ex_ref[p] = lax.fori_loop(0, N_KEYS, body, jnp.zeros((N_KEYS, tm), F32))


def _peer_kernel(x_ref, mod_ref, g_ref, wqt_ref, keys_ref, u_ref, vt_ref, gfin_ref, o_ref,
                 h2t_ref, acc_ref, sc_ref, sv_ref, nz_ref, ex_ref, cnt1_ref, e1n_ref, rank2_ref, e2_ref,
                 *, final_norm):
    k = pl.program_id(1)
    tm = PEER_TM
    half = PEER_DKEY // 2

    @pl.when(k == 0)
    def _():
        mod = mod_ref[0]
        h2 = _rms(x_ref[...], g_ref[...]) * (1.0 + mod[4:5]) + mod[3:4]
        h2t_ref[...] = h2.T.astype(BF16)
        acc_ref[...] = jnp.zeros_like(acc_ref)
        qt = jnp.dot(wqt_ref[...], h2t_ref[...], preferred_element_type=F32).astype(BF16)
        for hd in range(PEER_HEADS):
            for p in range(2):
                r0 = (hd * 2 + p) * half
                sc_ref[p, hd] = jnp.dot(keys_ref[hd, p], qt[r0:r0 + half, :],
                                        preferred_element_type=F32)
                for g in range(tm // LANES):
                    lanes = slice(g * LANES, (g + 1) * LANES)
                    top = _top16_sorted(sc_ref[p, hd, :, lanes])
                    for a in range(PEER_TOPK):
                        sv_ref[p, a, hd:hd + 1, lanes] = top[a][0:1, :]
        for g in range(tm // LANES):
            lanes = slice(g * LANES, (g + 1) * LANES)
            sv1 = [sv_ref[0, a, :, lanes] for a in range(PEER_TOPK)]
            sv2 = [sv_ref[1, a, :, lanes] for a in range(PEER_TOPK)]
            n, inv_z = _pair_counts(sv1, sv2)
            for a in range(PEER_TOPK):
                nz_ref[a, :, lanes] = n[a]
            nz_ref[PEER_TOPK, :, lanes] = inv_z
        tabs = (sc_ref, sv_ref, nz_ref, ex_ref, cnt1_ref, e1n_ref, rank2_ref, e2_ref)
        for hd in range(PEER_HEADS):
            bad = _head_tables(hd, *tabs, exact=False)

            @pl.when(jnp.max(bad) > 0.0)
            def _(hd=hd):
                _tie_offsets(hd, sc_ref, ex_ref)
                _head_tables(hd, *tabs, exact=True)

    zero = jnp.zeros((BF16_ROWS, tm), BF16)
    npiece = PEER_EBLK // N_KEYS
    per_sub = PEER_SUB // N_KEYS

    def up_dot(p):
        return jnp.dot(u_ref[p * N_KEYS:(p + 1) * N_KEYS, :], h2t_ref[...], preferred_element_type=F32)

    raw = up_dot(0)
    ys = []
    for p in range(npiece):
        nxt = up_dot(p + 1) if p + 1 < npiece else None
        i1 = k * npiece + p
        cbs = [jnp.broadcast_to(cnt1_ref[hd, pl.ds(i1, 1), :], (BF16_ROWS, tm)).astype(BF16)
               for hd in range(PEER_HEADS)]
        ebs = [jnp.broadcast_to(e1n_ref[hd, pl.ds(i1, 1), :], (BF16_ROWS, tm)).astype(BF16)
               for hd in range(PEER_HEADS)]
        act = jax.nn.gelu(raw).astype(BF16)
        for r in range(N_KEYS // BF16_ROWS):
            rows = slice(r * BF16_ROWS, (r + 1) * BF16_ROWS)
            w = None
            for hd in range(PEER_HEADS):
                t = jnp.where(rank2_ref[hd, rows, :] < cbs[hd], e2_ref[hd, rows, :], zero) * ebs[hd]
                w = t if w is None else w + t
            ys.append(act[rows, :] * w)
        raw = nxt
        if (p + 1) % per_sub == 0:
            e0 = (p + 1 - per_sub) * N_KEYS
            y = jnp.concatenate(ys, axis=0)
            ys = []
            acc_ref[...] += jnp.dot(vt_ref[0, :, e0:e0 + PEER_SUB], y, preferred_element_type=F32)

    @pl.when(k == pl.num_programs(1) - 1)
    def _():
        out = x_ref[...] + mod_ref[0][5:6] * acc_ref[...].T
        if final_norm:
            out = _rms(out, gfin_ref[...])
        o_ref[...] = out


def _rope_tables(s_prompt_tile, s_sample):
    n_rows = s_sample // GRID_W
    rows = jnp.repeat(jnp.arange(n_rows, dtype=F32), GRID_W)
    cols = jnp.tile(jnp.arange(GRID_W, dtype=F32), n_rows)
    axis_dim = QK_ROPE // 2
    inv_freq = ROPE_BASE ** (-jnp.arange(0, axis_dim, 2, dtype=F32) / axis_dim)
    ang = jnp.concatenate([rows[:, None] * inv_freq, cols[:, None] * inv_freq], axis=-1)
    cos, sin = jnp.cos(ang), jnp.sin(ang)
    ck = jnp.concatenate([cos, cos], axis=-1)
    sk = jnp.concatenate([-sin, sin], axis=-1)
    ident_c = jnp.ones((s_prompt_tile, QK_ROPE), F32)
    ident_s = jnp.zeros((s_prompt_tile, QK_ROPE), F32)
    return jnp.concatenate([ident_c, ck], axis=0), jnp.concatenate([ident_s, sk], axis=0)


def kernel(x_prompt, x_sample, cache_ckv_l0, cache_krope_l0, state_lru_l0, c, c_ctx, w_mod_l0, b_mod_l0, w_mod_l1, b_mod_l1, g_mix_l0, g_ffn_l0, g_mix_l1, g_ffn_l1, w_in_l0, g_q_l0, w_uq_l0, g_kv_l0, w_ukv_l0, conv_w_l0, conv_b_l0, w_rg_l0, b_rg_l0, w_ig_l0, b_ig_l0, lam_l0, w_o_l0, w_pool_l1, s_pool_l1, peer_wq_l0, peer_keys_l0, peer_u_l0, peer_v_l0, peer_wq_l1, peer_keys_l1, peer_u_l1, peer_v_l1, g_final):
    nb_p, s_p, d = x_prompt.shape
    nb_s, s_s, _ = x_sample.shape
    n_cache = cache_ckv_l0.shape[1]
    assert d == D_MODEL and s_p == TM and s_s % TM == 0 and n_cache % TM == 0
    t_p = nb_p * s_p
    t_s = nb_s * s_s
    t_all = t_p + t_s
    npt = t_p // TM
    tps = s_s // TM
    ntile = t_all // TM
    assert t_all % PEER_TM == 0 and t_p % PEER_TM == 0 and s_s % PEER_TM == 0

    x0 = jnp.concatenate([x_prompt.reshape(t_p, d), x_sample.reshape(t_s, d)], axis=0)

    ncond = 1 + nb_s
    cpad = jnp.zeros((2 * SUBLANES, d), F32).at[0].set(c_ctx).at[1:ncond].set(c)
    mod0 = _ada(cpad, w_mod_l0, b_mod_l0).reshape(2 * SUBLANES, 6, d)
    mod1 = _ada(cpad, w_mod_l1, b_mod_l1).reshape(2 * SUBLANES, 6, d)

    def cond_row(i):
        return jnp.where(i < npt, 0, 1 + (i - npt) // tps)

    mod_spec = pl.BlockSpec((1, 6, d), lambda i: (cond_row(i), 0, 0))
    row = lambda a: a.reshape(1, -1)

    perm = np.concatenate([np.arange(0, QK_ROPE, 2), np.arange(1, QK_ROPE, 2)])
    perm_sw = np.concatenate([np.arange(1, QK_ROPE, 2), np.arange(0, QK_ROPE, 2)])
    o1 = Q_LORA + KV_LORA
    w_kr = w_in_l0[:, o1:o1 + QK_ROPE]
    z64 = jnp.zeros((d, 128 - QK_ROPE), F32)
    w_in_ext = jnp.concatenate(
        [w_in_l0[:, :o1], w_kr, z64, w_kr[:, perm], z64, w_kr[:, perm_sw], z64, w_in_l0[:, o1 + QK_ROPE:]],
        axis=1).astype(BF16)
    assert w_in_ext.shape[1] == IN_EXT
    wq3 = w_uq_l0.reshape(Q_LORA, MLA_HEADS, QK_NOPE + QK_ROPE)
    w_uq_ext = jnp.concatenate(
        [wq3[:, :, :QK_NOPE].reshape(Q_LORA, -1),
         wq3[:, :, QK_NOPE:][:, :, perm].reshape(Q_LORA, -1),
         wq3[:, :, QK_NOPE:][:, :, perm_sw].reshape(Q_LORA, -1)], axis=1).astype(BF16)
    ck_tab, sk_tab = _rope_tables(TM, s_s)

    def rope_blk(i):
        return jnp.where(i < npt, 0, 1 + (i - npt) % tps)

    tok = lambda w: pl.BlockSpec((TM, w), lambda i: (i, 0))
    full = lambda a: pl.BlockSpec(a.shape, lambda *_: (0,) * a.ndim)
    q, ckv, kr, krr, ux, ug = pl.pallas_call(
        _inproj_kernel,
        grid=(ntile,),
        in_specs=[tok(d), mod_spec, full(row(g_mix_l0)), full(w_in_ext), full(row(g_q_l0)), full(w_uq_ext),
                  full(row(g_kv_l0)),
                  pl.BlockSpec((TM, QK_ROPE), lambda i: (rope_blk(i), 0)),
                  pl.BlockSpec((TM, QK_ROPE), lambda i: (rope_blk(i), 0))],
        out_specs=[pl.BlockSpec((MLA_HEADS, TM, QK_PAD), lambda i: (0, i, 0)),
                   tok(KV_LORA), tok(QK_ROPE), tok(QK_ROPE), tok(LRU_WIDTH), tok(LRU_WIDTH)],
        out_shape=[jax.ShapeDtypeStruct((MLA_HEADS, t_all, QK_PAD), BF16),
                   jax.ShapeDtypeStruct((t_all, KV_LORA), F32),
                   jax.ShapeDtypeStruct((t_all, QK_ROPE), F32),
                   jax.ShapeDtypeStruct((t_all, QK_ROPE), BF16),
                   jax.ShapeDtypeStruct((t_all, LRU_WIDTH), F32),
                   jax.ShapeDtypeStruct((t_all, LRU_WIDTH), F32)],
        compiler_params=_cparams(("arbitrary",)),
    )(x0, mod0, row(g_mix_l0), w_in_ext, row(g_q_l0), w_uq_ext, row(g_kv_l0), ck_tab, sk_tab)

    wkv3 = w_ukv_l0.reshape(KV_LORA, MLA_HEADS, QK_NOPE + V_HEAD)
    w_ukv_ext = jnp.concatenate([wkv3[:, :, :QK_NOPE].reshape(KV_LORA, -1),
                                 wkv3[:, :, QK_NOPE:].reshape(KV_LORA, -1)], axis=1).astype(BF16)

    def attn_call(nb, s_new, tile0, has_cache):
        nq = s_new // TM
        blk0 = tile0 * TM // s_new
        n_c = n_cache if has_cache else 0
        in_specs = [pl.BlockSpec((MLA_HEADS, TM, QK_PAD), lambda b, qi: (0, tile0 + b * nq + qi, 0)),
                    pl.BlockSpec((s_new, KV_LORA), lambda b, qi: (blk0 + b, 0)),
                    pl.BlockSpec((s_new, QK_ROPE), lambda b, qi: (blk0 + b, 0))]
        args = [q, ckv, krr]
        if has_cache:
            in_specs += [pl.BlockSpec((1, n_cache, KV_LORA), lambda b, qi: (b, 0, 0)),
                         pl.BlockSpec((1, n_cache, QK_ROPE), lambda b, qi: (b, 0, 0))]
            args += [cache_ckv_l0, cache_krope_l0[:, :, perm]]
        in_specs.append(pl.BlockSpec(w_ukv_ext.shape, lambda b, qi: (0, 0)))
        args.append(w_ukv_ext)
        return pl.pallas_call(
            functools.partial(_attn_kernel, has_cache=has_cache, s_new=s_new, n_cache=n_c),
            grid=(nb, nq),
            in_specs=in_specs,
            out_specs=pl.BlockSpec((TM, MLA_WIDTH), lambda b, qi: (b * nq + qi, 0)),
            out_shape=jax.ShapeDtypeStruct((nb * s_new, MLA_WIDTH), BF16),
            scratch_shapes=[pltpu.VMEM((MLA_HEADS, n_c + s_new, QK_PAD), BF16),
                            pltpu.VMEM((n_c + s_new, MLA_WIDTH), BF16)],
            compiler_params=_cparams(("arbitrary", "arbitrary")),
        )(*args)

    attn = jnp.concatenate([attn_call(nb_p, s_p, 0, False), attn_call(nb_s, s_s, npt, True)], axis=0)

    def lru_call(nb, s_new, tile0, h0):
        nc = s_new // TM
        r8 = TM // SUBLANES
        last8 = t_all // SUBLANES - 1

        def cur(rev):
            return pl.BlockSpec((TM, LRU_WIDTH),
                                lambda b, cc: (tile0 + b * nc + (nc - 1 - cc if rev else cc), 0))

        def prev(rev):
            return pl.BlockSpec((SUBLANES, LRU_WIDTH), lambda b, cc: (
                jnp.maximum((tile0 + b * nc + (nc - 1 - cc if rev else cc)) * r8 - 1, 0), 0))

        def nxt(rev):
            return pl.BlockSpec((SUBLANES, LRU_WIDTH), lambda b, cc: (
                jnp.minimum((tile0 + b * nc + (nc - 1 - cc if rev else cc) + 1) * r8, last8), 0))

        small = [conv_w_l0, row(conv_b_l0), w_rg_l0, b_rg_l0, w_ig_l0, b_ig_l0, lam_l0]
        return pl.pallas_call(
            functools.partial(_lru_kernel, nc=nc),
            grid=(nb, nc),
            in_specs=[prev(False), cur(False), nxt(False), prev(True), cur(True), nxt(True)]
                     + [full(a) for a in small]
                     + [pl.BlockSpec((1, 2, LRU_WIDTH), lambda b, cc: (b, 0, 0))],
            out_specs=[pl.BlockSpec((TM, LRU_WIDTH), lambda b, cc: (b * nc + cc, 0)),
                       pl.BlockSpec((TM, LRU_WIDTH), lambda b, cc: (b * nc + nc - 1 - cc, 0)),
                       pl.BlockSpec((1, 2, LRU_WIDTH), lambda b, cc: (b, 0, 0))],
            out_shape=[jax.ShapeDtypeStruct((nb * s_new, LRU_WIDTH), F32),
                       jax.ShapeDtypeStruct((nb * s_new, LRU_WIDTH), F32),
                       jax.ShapeDtypeStruct((nb, 2, LRU_WIDTH), F32)],
            scratch_shapes=[pltpu.VMEM((1, LRU_WIDTH), F32), pltpu.VMEM((1, LRU_WIDTH), F32)],
            compiler_params=_cparams(("arbitrary", "arbitrary")),
        )(ux, ux, ux, ux, ux, ux, *small, h0)

    hf_p, hb_p, new_lru = lru_call(nb_p, s_p, 0, jnp.zeros((nb_p, 2, LRU_WIDTH), F32))
    hf_s, hb_s, _ = lru_call(nb_s, s_s, npt, state_lru_l0.astype(F32))
    hf = jnp.concatenate([hf_p, hf_s], axis=0)
    hb = jnp.concatenate([hb_p, hb_s], axis=0)

    w_o = w_o_l0.astype(BF16)
    x1 = pl.pallas_call(
        _oproj_kernel,
        grid=(ntile,),
        in_specs=[tok(d), mod_spec, tok(MLA_WIDTH), tok(LRU_WIDTH), tok(LRU_WIDTH), tok(LRU_WIDTH),
                  pl.BlockSpec((MLA_WIDTH, d), lambda i: (0, 0)), pl.BlockSpec((LRU_WIDTH, d), lambda i: (1, 0))],
        out_specs=tok(d),
        out_shape=jax.ShapeDtypeStruct((t_all, d), F32),
        compiler_params=_cparams(("arbitrary",)),
    )(x0, mod0, attn, hf, hb, ug, w_o, w_o)

    def peer_call(x, mod, g_ffn, w_q, sub_keys, u, v, final_norm):
        n_exp = u.shape[0]
        assert n_exp == N_KEYS * N_KEYS and n_exp % PEER_EBLK == 0
        wqt = w_q.T.astype(BF16)
        keys = sub_keys.astype(BF16)
        ub = u.astype(BF16)
        vt = v.reshape(n_exp // PEER_EBLK, PEER_EBLK, d).transpose(0, 2, 1).astype(BF16)
        tpp = PEER_TM // TM

        def cond_row_p(i):
            return cond_row(i * tpp)

        big = lambda dt: pltpu.VMEM((PEER_HEADS, N_KEYS, PEER_TM), dt)
        return pl.pallas_call(
            functools.partial(_peer_kernel, final_norm=final_norm),
            grid=(t_all // PEER_TM, n_exp // PEER_EBLK),
            in_specs=[pl.BlockSpec((PEER_TM, d), lambda i, k: (i, 0)),
                      pl.BlockSpec((1, 6, d), lambda i, k: (cond_row_p(i), 0, 0)),
                      pl.BlockSpec((1, d), lambda i, k: (0, 0)),
                      pl.BlockSpec(wqt.shape, lambda i, k: (0, 0)),
                      pl.BlockSpec(keys.shape, lambda i, k: (0, 0, 0, 0)),
                      pl.BlockSpec((PEER_EBLK, d), lambda i, k: (k, 0)),
                      pl.BlockSpec((1, d, PEER_EBLK), lambda i, k: (k, 0, 0)),
                      pl.BlockSpec((1, d), lambda i, k: (0, 0))],
            out_specs=pl.BlockSpec((PEER_TM, d), lambda i, k: (i, 0)),
            out_shape=jax.ShapeDtypeStruct((t_all, d), F32),
            scratch_shapes=[pltpu.VMEM((d, PEER_TM), BF16),
                            pltpu.VMEM((d, PEER_TM), F32),
                            pltpu.VMEM((2, PEER_HEADS, N_KEYS, PEER_TM), F32),
                            pltpu.VMEM((2, PEER_TOPK, PEER_HEADS, PEER_TM), F32),
                            pltpu.VMEM((PEER_TOPK + 1, PEER_HEADS, PEER_TM), F32),
                            pltpu.VMEM((2, N_KEYS, PEER_TM), F32),
                            big(F32), big(F32), big(BF16), big(BF16)],
            compiler_params=_cparams(("arbitrary", "arbitrary")),
        )(x, mod, row(g_ffn), wqt, keys, ub, vt, row(g_final))

    x2 = peer_call(x1, mod0, g_ffn_l0, peer_wq_l0, peer_keys_l0, peer_u_l0, peer_v_l0, False)

    rh = TM // HALO
    lasth = t_all // HALO - 1
    x3 = pl.pallas_call(
        functools.partial(_pool_kernel, tiles_per_seq=tps, n_prompt_tiles=npt),
        grid=(ntile,),
        in_specs=[pl.BlockSpec((HALO, d), lambda i: (jnp.maximum(i * rh - 1, 0), 0)),
                  tok(d),
                  pl.BlockSpec((HALO, d), lambda i: (jnp.minimum((i + 1) * rh, lasth), 0)),
                  mod_spec, full(row(g_mix_l1)),
                  pl.BlockSpec(w_pool_l1.shape, lambda i: (0, 0, 0)), full(row(s_pool_l1))],
        out_specs=tok(d),
        out_shape=jax.ShapeDtypeStruct((t_all, d), F32),
        compiler_params=_cparams(("arbitrary",)),
    )(x2, x2, x2, mod1, row(g_mix_l1), w_pool_l1.astype(BF16), row(s_pool_l1))

    y = peer_call(x3, mod1, g_ffn_l1, peer_wq_l1, peer_keys_l1, peer_u_l1, peer_v_l1, True)

    y_prompt = y[:t_p].reshape(nb_p, s_p, d)
    y_sample = y[t_p:].reshape(nb_s, s_s, d)
    new_ckv = ckv[:t_p].reshape(nb_p, s_p, KV_LORA)
    new_krope = kr[:t_p].reshape(nb_p, s_p, QK_ROPE)
    return (y_prompt, y_sample, new_ckv, new_krope, new_lru)
```

```python
import functools

import numpy as np
import jax
import jax.numpy as jnp
from jax import lax
from jax.experimental import pallas as pl
from jax.experimental.pallas import tpu as pltpu

F32 = jnp.float32
BF16 = jnp.bfloat16

D_MODEL = 1024
EPS = 1e-6
GRID_W = 64
MLA_HEADS = 4
Q_LORA = 384
KV_LORA = 256
QK_NOPE = 128
QK_ROPE = 64
V_HEAD = 128
MLA_WIDTH = MLA_HEADS * V_HEAD
ROPE_BASE = 10000.0
LRU_WIDTH = 512
LRU_BLOCKS = 4
LRU_BLOCK = LRU_WIDTH // LRU_BLOCKS
CONV_W = 4
CONV_LEFT = 2
LRU_C = 8.0
POOL_WINDOWS = (2, 4, 8, 16)
POOL_GROUP = D_MODEL // len(POOL_WINDOWS)
PEER_HEADS = 8
N_KEYS = 128
PEER_DKEY = 256
PEER_TOPK = 16

SUBLANES = 8
LANES = 128
VMEM_LIMIT = 56 * 1024 * 1024

TM = 256
QK_PAD = 256
HALO = 16
PEER_TM = 1024
PEER_EBLK = 512
PEER_SUB = 256
BF16_ROWS = 16
IN_EXT = 2048


def _rms(x, g):
    return x * lax.rsqrt(jnp.mean(x * x, axis=-1, keepdims=True) + EPS) * g


def _cparams(sem):
    return pltpu.CompilerParams(dimension_semantics=sem, vmem_limit_bytes=VMEM_LIMIT)


def _ada_kernel(c_ref, w_ref, b_ref, o_ref):
    c = c_ref[...]
    o_ref[...] = jnp.dot(c * jax.nn.sigmoid(c), w_ref[...], preferred_element_type=F32) + b_ref[...]


def _ada(cpad, w_mod, b_mod):
    n = w_mod.shape[1]
    bn = 768
    return pl.pallas_call(
        _ada_kernel,
        grid=(n // bn,),
        in_specs=[pl.BlockSpec(cpad.shape, lambda j: (0, 0)),
                  pl.BlockSpec((D_MODEL, bn), lambda j: (0, j)),
                  pl.BlockSpec((1, bn), lambda j: (0, j))],
        out_specs=pl.BlockSpec((cpad.shape[0], bn), lambda j: (0, j)),
        out_shape=jax.ShapeDtypeStruct((cpad.shape[0], n), F32),
        compiler_params=_cparams(("arbitrary",)),
    )(cpad, w_mod, b_mod.reshape(1, n))


def _inproj_kernel(x_ref, mod_ref, g_ref, win_ref, gq_ref, wuq_ref, gkv_ref, ck_ref, sk_ref,
                   q_ref, ckv_ref, kr_ref, krr_ref, ux_ref, ug_ref):
    mod = mod_ref[0]
    h = _rms(x_ref[...], g_ref[...]) * (1.0 + mod[1:2]) + mod[0:1]
    y = jnp.dot(h.astype(BF16), win_ref[...], preferred_element_type=F32)
    cq = y[:, 0:Q_LORA]
    ckv = y[:, Q_LORA:Q_LORA + KV_LORA]
    o = Q_LORA + KV_LORA
    kr = y[:, o:o + QK_ROPE]
    krp = y[:, o + 128:o + 128 + QK_ROPE]
    krs = y[:, o + 256:o + 256 + QK_ROPE]
    ux_ref[...] = y[:, o + 384:o + 384 + LRU_WIDTH]
    ug_ref[...] = y[:, o + 384 + LRU_WIDTH:o + 384 + 2 * LRU_WIDTH]
    ckv_ref[...] = _rms(ckv, gkv_ref[...])
    kr_ref[...] = kr
    ck = ck_ref[...]
    sk = sk_ref[...]
    krr_ref[...] = (krp * ck + krs * sk).astype(BF16)
    q = jnp.dot(_rms(cq, gq_ref[...]).astype(BF16), wuq_ref[...], preferred_element_type=F32)
    nw = MLA_HEADS * QK_NOPE
    rw = MLA_HEADS * QK_ROPE
    for hd in range(MLA_HEADS):
        qp = q[:, nw + hd * QK_ROPE:nw + (hd + 1) * QK_ROPE]
        qs = q[:, nw + rw + hd * QK_ROPE:nw + rw + (hd + 1) * QK_ROPE]
        q_ref[hd, :, 0:QK_NOPE] = q[:, hd * QK_NOPE:(hd + 1) * QK_NOPE].astype(BF16)
        q_ref[hd, :, QK_NOPE:QK_NOPE + QK_ROPE] = (qp * ck + qs * sk).astype(BF16)
        q_ref[hd, :, QK_NOPE + QK_ROPE:QK_PAD] = jnp.zeros((TM, QK_PAD - QK_NOPE - QK_ROPE), BF16)


def _attn_kernel(*refs, has_cache, s_new, n_cache):
    if has_cache:
        q_ref, ckv_ref, krr_ref, cckv_ref, ckr_ref, wukv_ref, o_ref, kcat_ref, vv_ref = refs
    else:
        q_ref, ckv_ref, krr_ref, wukv_ref, o_ref, kcat_ref, vv_ref = refs
    sk = n_cache + s_new
    kw = MLA_HEADS * QK_NOPE
    zpad = jnp.zeros((TM, QK_PAD - QK_NOPE - QK_ROPE), BF16)

    def put_keys(row0, ckv_rows, kr_rows):
        kv = jnp.dot(ckv_rows.astype(BF16), wukv_ref[...], preferred_element_type=F32)
        rows = pl.ds(row0, TM)
        for hd in range(MLA_HEADS):
            kcat_ref[hd, rows, 0:QK_NOPE] = kv[:, hd * QK_NOPE:(hd + 1) * QK_NOPE].astype(BF16)
            kcat_ref[hd, rows, QK_NOPE:QK_NOPE + QK_ROPE] = kr_rows.astype(BF16)
            kcat_ref[hd, rows, QK_NOPE + QK_ROPE:QK_PAD] = zpad
        vv_ref[rows, :] = kv[:, kw:].astype(BF16)

    @pl.when(pl.program_id(1) == 0)
    def _():
        if has_cache:
            for c in range(n_cache // TM):
                put_keys(c * TM, cckv_ref[0, c * TM:(c + 1) * TM, :], ckr_ref[0, c * TM:(c + 1) * TM, :])

        def body(c, carry):
            r0 = pl.multiple_of(c * TM, TM)
            put_keys(n_cache + r0, ckv_ref[pl.ds(r0, TM), :], krr_ref[pl.ds(r0, TM), :])
            return carry
        lax.fori_loop(0, s_new // TM, body, 0)

    scale = (QK_NOPE + QK_ROPE) ** -0.5
    for hd in range(MLA_HEADS):
        s = lax.dot_general(q_ref[hd], kcat_ref[hd], (((1,), (1,)), ((), ())),
                            preferred_element_type=F32) * scale
        m = jnp.max(s, axis=-1, keepdims=True)
        e = jnp.exp(s - m)
        l = jnp.sum(e, axis=-1, keepdims=True)
        o = jnp.dot(e.astype(BF16), vv_ref[:, hd * V_HEAD:(hd + 1) * V_HEAD], preferred_element_type=F32)
        o_ref[:, hd * V_HEAD:(hd + 1) * V_HEAD] = (o / l).astype(BF16)
    del sk


def _lru_dir(xp_ref, xc_ref, xn_ref, valid_prev, valid_next, d, reverse, cw_ref, cb_ref,
             wr_ref, br_ref, wi_ref, bi_ref, lam_ref, carry):
    xp = jnp.where(valid_prev, xp_ref[...], 0.0)
    xn = jnp.where(valid_next, xn_ref[...], 0.0)
    xx = jnp.concatenate([xp, xc_ref[...], xn], axis=0)
    n = TM + 2 * SUBLANES
    xc = cb_ref[...]
    for k in range(CONV_W):
        sh = (CONV_LEFT - k) % n
        xs = xx if sh == 0 else pltpu.roll(xx, sh, 0)
        xc = xc + xs[SUBLANES:SUBLANES + TM] * cw_ref[k:k + 1, :]
    rs, is_ = [], []
    for b in range(LRU_BLOCKS):
        xb = xc[:, b * LRU_BLOCK:(b + 1) * LRU_BLOCK]
        rs.append(jnp.dot(xb, wr_ref[d, b], preferred_element_type=F32))
        is_.append(jnp.dot(xb, wi_ref[d, b], preferred_element_type=F32))
    r = jax.nn.sigmoid(jnp.concatenate(rs, axis=-1) + br_ref[d:d + 1, :])
    i = jax.nn.sigmoid(jnp.concatenate(is_, axis=-1) + bi_ref[d:d + 1, :])
    nl = -lam_ref[d:d + 1, :]
    softplus = jnp.maximum(nl, 0.0) + jnp.log1p(jnp.exp(-jnp.abs(nl)))
    log_a = -LRU_C * r * softplus
    a = jnp.exp(log_a)
    bx = jnp.sqrt(jnp.tanh(-log_a) * (a * a + 1.0)) * (i * xc)
    t = lax.broadcasted_iota(jnp.int32, (TM, 1), 0)
    step = 1
    while step < TM:
        if reverse:
            keep = t < TM - step
            sh = TM - step
        else:
            keep = t >= step
            sh = step
        a_s = jnp.where(keep, pltpu.roll(a, sh, 0), 1.0)
        b_s = jnp.where(keep, pltpu.roll(bx, sh, 0), 0.0)
        bx = a * b_s + bx
        a = a * a_s
        step *= 2
    return a * carry + bx


def _lru_kernel(fxp, fxc, fxn, bxp, bxc, bxn, cw_ref, cb_ref, wr_ref, br_ref, wi_ref, bi_ref, lam_ref,
                h0_ref, hf_ref, hb_ref, st_ref, cf_ref, cbk_ref, *, nc):
    c = pl.program_id(1)

    @pl.when(c == 0)
    def _():
        cf_ref[...] = h0_ref[0, 0:1, :]
        cbk_ref[...] = h0_ref[0, 1:2, :]

    params = (cw_ref, cb_ref, wr_ref, br_ref, wi_ref, bi_ref, lam_ref)
    hf = _lru_dir(fxp, fxc, fxn, c > 0, c < nc - 1, 0, False, *params, cf_ref[...])
    hf_ref[...] = hf
    cf_ref[...] = hf[TM - 1:TM, :]
    hb = _lru_dir(bxp, bxc, bxn, c < nc - 1, c > 0, 1, True, *params, cbk_ref[...])
    hb_ref[...] = hb
    cbk_ref[...] = hb[0:1, :]
    st_ref[0, 0:1, :] = hf[TM - 1:TM, :]
    st_ref[0, 1:2, :] = hb[0:1, :]


def _oproj_kernel(x_ref, mod_ref, at_ref, hf_ref, hb_ref, ug_ref, woa_ref, wor_ref, o_ref):
    mod = mod_ref[0]
    rec = ((hf_ref[...] + hb_ref[...]) * jax.nn.gelu(ug_ref[...])).astype(BF16)
    out = (jnp.dot(at_ref[...], woa_ref[...], preferred_element_type=F32)
           + jnp.dot(rec, wor_ref[...], preferred_element_type=F32))
    o_ref[...] = x_ref[...] + mod[2:3] * out


def _pool_kernel(xp_ref, xc_ref, xn_ref, mod_ref, g_ref, wp_ref, sp_ref, o_ref, *, tiles_per_seq, n_prompt_tiles):
    i = pl.program_id(0)
    j = jnp.where(i < n_prompt_tiles, 0, (i - n_prompt_tiles) % tiles_per_seq)
    ntile = jnp.where(i < n_prompt_tiles, 1, tiles_per_seq)
    mod = mod_ref[0]
    g = g_ref[...]

    def hmod(x):
        return _rms(x, g) * (1.0 + mod[1:2]) + mod[0:1]

    x = xc_ref[...]
    hc = hmod(x)
    hp = jnp.where(j > 0, hmod(xp_ref[...]), 0.0)
    hn = jnp.where(j < ntile - 1, hmod(xn_ref[...]), 0.0)
    hh = jnp.concatenate([hp, hc, hn], axis=0)
    n = TM + 2 * HALO
    seq_len = ntile * TM
    t = j * TM + lax.broadcasted_iota(jnp.int32, (TM, 1), 0)
    ys = []
    for gi, w in enumerate(POOL_WINDOWS):
        cols = slice(gi * POOL_GROUP, (gi + 1) * POOL_GROUP)
        p = hh[:, cols]
        p = p + pltpu.roll(p, 1, 0)
        half = 1
        while 2 * half < w:
            p = pltpu.roll(p, half, 0) + pltpu.roll(p, n - half, 0)
            half *= 2
        lo = jnp.maximum(t - w // 2, 0)
        hi = jnp.minimum(t + (w - w // 2), seq_len)
        mean = p[HALO:HALO + TM] / (hi - lo).astype(F32)
        dg = (mean - hc[:, cols]).astype(BF16)
        ys.append(jnp.dot(dg, wp_ref[gi], preferred_element_type=F32))
    y = jnp.concatenate(ys, axis=-1) * sp_ref[...]
    o_ref[...] = x + mod[2:3] * y


def _sort_pairs(n):
    pairs = []

    def merge(lo, cnt, r):
        step = r * 2
        if step < cnt:
            merge(lo, cnt, step)
            merge(lo + r, cnt, step)
            for i in range(lo + r, lo + cnt - r, step):
                pairs.append((i, i + r))
        else:
            pairs.append((lo, lo + r))

    def sort(lo, cnt):
        if cnt > 1:
            m = cnt // 2
            sort(lo, m)
            sort(lo + m, m)
            merge(lo, cnt, 1)

    sort(0, n)
    return pairs


_SORT16 = _sort_pairs(PEER_TOPK)
_HYPER = [(a, b) for a in range(PEER_TOPK) for b in range(PEER_TOPK) if (a + 1) * (b + 1) <= PEER_TOPK]


def _top16_sorted(s):
    k = PEER_TOPK
    x = [s[SUBLANES * r:SUBLANES * (r + 1), :] for r in range(N_KEYS // SUBLANES)]
    for (i, j) in _SORT16:
        hi = jnp.maximum(x[i], x[j])
        lo = jnp.minimum(x[i], x[j])
        x[i], x[j] = hi, lo
    for shift in (4, 2, 1):
        y = [jnp.maximum(x[r], pltpu.roll(x[k - 1 - r], shift, 0)) for r in range(k)]
        stride = k // 2
        while stride >= 1:
            for i in range(k):
                if i & stride == 0:
                    hi = jnp.maximum(y[i], y[i + stride])
                    lo = jnp.minimum(y[i], y[i + stride])
                    y[i], y[i + stride] = hi, lo
            stride //= 2
        x = y
    return x


def _pair_counts(sv1, sv2):
    one = jnp.ones_like(sv1[0])
    zero = jnp.zeros_like(sv1[0])
    cand = [sv1[a] + sv2[b] for (a, b) in _HYPER]

    def ordered(i, j):
        (ai, bi), (aj, bj) = _HYPER[i], _HYPER[j]
        return (aj <= ai and bj <= bi) or (ai <= aj and bi <= bj)

    nh = len(_HYPER)
    cnt = [float((a + 1) * (b + 1) - 1 + sum(1 for j in range(i + 1, nh) if not ordered(i, j))) * one
           for i, (a, b) in enumerate(_HYPER)]
    for i in range(nh):
        for jx in range(i):
            if ordered(i, jx):
                continue
            ge = jnp.where(cand[jx] >= cand[i], one, zero)
            cnt[i] = cnt[i] + ge
            cnt[jx] = cnt[jx] - ge
    e1 = [jnp.exp(sv1[a] - sv1[0]) for a in range(PEER_TOPK)]
    e2 = [jnp.exp(sv2[b] - sv2[0]) for b in range(PEER_TOPK)]
    n = [zero for _ in range(PEER_TOPK)]
    z = zero
    for i, (a, b) in enumerate(_HYPER):
        sel = jnp.where(cnt[i] < float(PEER_TOPK), one, zero)
        n[a] = n[a] + sel
        z = z + sel * (e1[a] * e2[b])
    return n, 1.0 / z


def _head_tables(hd, lanes, sc_ref, sv_ref, nz_ref, ex_ref, cnt1_ref, e1n_ref, rank2_ref, e2_ref, exact):
    k = PEER_TOPK

    def bc(ref, *idx):
        return jnp.broadcast_to(ref[idx + (slice(hd, hd + 1), lanes)], (SUBLANES, LANES))

    sv1b = [bc(sv_ref, 0, a) for a in range(k)]
    nb = [bc(nz_ref, a) for a in range(k)]
    izb = bc(nz_ref, k)
    c1 = jnp.zeros((SUBLANES, LANES), F32)
    for r in range(N_KEYS // SUBLANES):
        rows = slice(r * SUBLANES, (r + 1) * SUBLANES)
        s1 = sc_ref[pl.ds(2 * hd * N_KEYS + r * SUBLANES, SUBLANES), lanes]
        cnt1 = jnp.zeros_like(s1)
        if exact:
            rank1 = jnp.zeros_like(s1)
            for a in range(k):
                rank1 = jnp.where(sv1b[a] > s1, float(a + 1), rank1)
            rank1 = rank1 + ex_ref[0, rows, lanes]
            for a in range(k):
                cnt1 = jnp.where(rank1 == float(a), nb[a], cnt1)
        else:
            for a in range(k):
                cnt1 = jnp.where(s1 == sv1b[a], nb[a], cnt1)
            c1 = c1 + jnp.where(s1 >= sv1b[k - 1], 1.0, 0.0)
        cnt1_ref[hd, rows, lanes] = cnt1
        e1n_ref[hd, rows, lanes] = jnp.exp(s1 - sv1b[0]) * izb
    sv2b = [bc(sv_ref, 1, a) for a in range(k)]
    c2 = jnp.zeros((SUBLANES, LANES), F32)
    for r in range(N_KEYS // BF16_ROWS):
        rk, e2 = [], []
        for q in range(BF16_ROWS // SUBLANES):
            r0 = r * BF16_ROWS + q * SUBLANES
            rows = slice(r0, r0 + SUBLANES)
            s2 = sc_ref[pl.ds((2 * hd + 1) * N_KEYS + r0, SUBLANES), lanes]
            rank2 = jnp.zeros_like(s2)
            for a in range(k):
                rank2 = jnp.where(sv2b[a] > s2, float(a + 1), rank2)
            if exact:
                rank2 = jnp.minimum(rank2 + ex_ref[1, rows, lanes], float(k))
            else:
                c2 = c2 + jnp.where(rank2 < float(k), 1.0, 0.0)
            rk.append(rank2)
            e2.append(jnp.exp(s2 - sv2b[0]))
        rows16 = pl.ds(hd * N_KEYS + r * BF16_ROWS, BF16_ROWS)
        rank2_ref[rows16, lanes] = jnp.concatenate(rk, axis=0).astype(BF16)
        e2_ref[rows16, lanes] = jnp.concatenate(e2, axis=0).astype(BF16)
    if exact:
        return None
    ex_ref[0, 0:SUBLANES, lanes] = c1
    ex_ref[1, 0:SUBLANES, lanes] = c2
    b = jnp.zeros((SUBLANES, LANES), F32)
    for svb in (sv1b, sv2b):
        for a in range(k - 1):
            b = b + jnp.where(svb[a] == svb[a + 1], 1.0, 0.0)
    return b


def _tie_offsets(hd, sc_ref, ex_ref):
    tm = sc_ref.shape[-1]
    nidx = lax.broadcasted_iota(jnp.int32, (N_KEYS, tm), 0)
    for p in range(2):
        s = sc_ref[pl.ds((2 * hd + p) * N_KEYS, N_KEYS), :]

        def body(m, e, s=s, p=p):
            row = sc_ref[pl.ds((2 * hd + p) * N_KEYS + m, 1), :]
            return e + jnp.where((s == row) & (nidx > m), 1.0, 0.0)

        ex_ref[p] = lax.fori_loop(0, N_KEYS, body, jnp.zeros((N_KEYS, tm), F32))


def _peer_prologue(x_ref, mod_ref, g_ref, wqt_ref, keys_ref, h2t_ref, qb_ref, sc_ref, sv_ref, nz_ref,
                   ex_ref, cnt1_ref, e1n_ref):
    tm = PEER_TM
    half = PEER_DKEY // 2
    ngroup = tm // LANES
    mod = mod_ref[0]
    h2 = _rms(x_ref[...], g_ref[...]) * (1.0 + mod[4:5]) + mod[3:4]
    h2t_ref[...] = h2.T.astype(BF16)

    def lane_group(g):
        return pl.ds(pl.multiple_of(g * LANES, LANES), LANES)

    qb_ref[...] = jnp.dot(wqt_ref[...], h2t_ref[...], preferred_element_type=F32).astype(BF16)
    for i in range(2 * PEER_HEADS):
        sc_ref[i * N_KEYS:(i + 1) * N_KEYS, :] = jnp.dot(keys_ref[i], qb_ref[i * half:(i + 1) * half, :],
                                                         preferred_element_type=F32)

        def group(g, c, i=i):
            lanes = lane_group(g)
            top = _top16_sorted(sc_ref[i * N_KEYS:(i + 1) * N_KEYS, lanes])
            for a in range(PEER_TOPK):
                sv_ref[i % 2, a, i // 2:i // 2 + 1, lanes] = top[a][0:1, :]
            return c
        lax.fori_loop(0, ngroup, group, 0)

    def counts(g, carry):
        lanes = lane_group(g)
        sv1 = [sv_ref[0, a, :, lanes] for a in range(PEER_TOPK)]
        sv2 = [sv_ref[1, a, :, lanes] for a in range(PEER_TOPK)]
        n, inv_z = _pair_counts(sv1, sv2)
        for a in range(PEER_TOPK):
            nz_ref[a, :, lanes] = n[a]
        nz_ref[PEER_TOPK, :, lanes] = inv_z
        return carry
    lax.fori_loop(0, ngroup, counts, 0)

    nrow = PEER_HEADS * N_KEYS
    tabs = (sc_ref, sv_ref, nz_ref, ex_ref, cnt1_ref, e1n_ref, qb_ref.at[0:nrow], qb_ref.at[nrow:2 * nrow])

    for hd in range(PEER_HEADS):
        def fast(g, bad, hd=hd):
            return bad + _head_tables(hd, lane_group(g), *tabs, exact=False)
        bad = lax.fori_loop(0, ngroup, fast, jnp.zeros((SUBLANES, LANES), F32))
        totals = jnp.sum(ex_ref[:, 0:SUBLANES, :], axis=1)
        ties = jnp.max(bad) + jnp.max(jnp.abs(totals - float(PEER_TOPK)))

        @pl.when(ties > 0.0)
        def _(hd=hd):
            _tie_offsets(hd, sc_ref, ex_ref)

            def slow(g, c):
                _head_tables(hd, lane_group(g), *tabs, exact=True)
                return c
            lax.fori_loop(0, ngroup, slow, 0)
    sc_ref[0:D_MODEL, :] = jnp.zeros((D_MODEL, tm), F32)


def _peer_kernel(x_ref, mod_ref, g_ref, wqt_ref, keys_ref, u_ref, vt_ref, gfin_ref, o_ref,
                 h2t_ref, qb_ref, sc_ref, sv_ref, nz_ref, ex_ref, cnt1_ref, e1n_ref, *, final_norm):
    k = pl.program_id(1)
    tm = PEER_TM
    nrow = PEER_HEADS * N_KEYS
    acc_ref = sc_ref.at[0:D_MODEL]
    rank2_ref = qb_ref.at[0:nrow]
    e2_ref = qb_ref.at[nrow:2 * nrow]

    @pl.when(k == 0)
    def _():
        _peer_prologue(x_ref, mod_ref, g_ref, wqt_ref, keys_ref, h2t_ref, qb_ref, sc_ref, sv_ref, nz_ref,
                       ex_ref, cnt1_ref, e1n_ref)

    zero = jnp.zeros((BF16_ROWS, tm), BF16)
    npiece = PEER_EBLK // N_KEYS
    per_sub = PEER_SUB // N_KEYS

    def up_dot(p):
        return jnp.dot(u_ref[p * N_KEYS:(p + 1) * N_KEYS, :], h2t_ref[...], preferred_element_type=F32)

    raw = up_dot(0)
    ys = []
    for p in range(npiece):
        nxt = up_dot(p + 1) if p + 1 < npiece else None
        i1 = k * npiece + p
        cbs = [jnp.broadcast_to(cnt1_ref[hd, pl.ds(i1, 1), :], (BF16_ROWS, tm)).astype(BF16)
               for hd in range(PEER_HEADS)]
        ebs = [jnp.broadcast_to(e1n_ref[hd, pl.ds(i1, 1), :], (BF16_ROWS, tm)).astype(BF16)
               for hd in range(PEER_HEADS)]
        act = jax.nn.gelu(raw).astype(BF16)
        for r in range(N_KEYS // BF16_ROWS):
            rows = slice(r * BF16_ROWS, (r + 1) * BF16_ROWS)
            w = None
            for hd in range(PEER_HEADS):
                hrows = pl.ds(hd * N_KEYS + r * BF16_ROWS, BF16_ROWS)
                t = jnp.where(rank2_ref[hrows, :] < cbs[hd], e2_ref[hrows, :], zero) * ebs[hd]
                w = t if w is None else w + t
            ys.append(act[rows, :] * w)
        raw = nxt
        if (p + 1) % per_sub == 0:
            e0 = (p + 1 - per_sub) * N_KEYS
            y = jnp.concatenate(ys, axis=0)
            ys = []
            acc_ref[...] += jnp.dot(vt_ref[0, :, e0:e0 + PEER_SUB], y, preferred_element_type=F32)

    @pl.when(k == pl.num_programs(1) - 1)
    def _():
        out = x_ref[...] + mod_ref[0][5:6] * acc_ref[...].T
        if final_norm:
            out = _rms(out, gfin_ref[...])
        o_ref[...] = out


def _rope_tables(s_prompt_tile, s_sample):
    n_rows = s_sample // GRID_W
    rows = jnp.repeat(jnp.arange(n_rows, dtype=F32), GRID_W)
    cols = jnp.tile(jnp.arange(GRID_W, dtype=F32), n_rows)
    axis_dim = QK_ROPE // 2
    inv_freq = ROPE_BASE ** (-jnp.arange(0, axis_dim, 2, dtype=F32) / axis_dim)
    ang = jnp.concatenate([rows[:, None] * inv_freq, cols[:, None] * inv_freq], axis=-1)
    cos, sin = jnp.cos(ang), jnp.sin(ang)
    ck = jnp.concatenate([cos, cos], axis=-1)
    sk = jnp.concatenate([-sin, sin], axis=-1)
    ident_c = jnp.ones((s_prompt_tile, QK_ROPE), F32)
    ident_s = jnp.zeros((s_prompt_tile, QK_ROPE), F32)
    return jnp.concatenate([ident_c, ck], axis=0), jnp.concatenate([ident_s, sk], axis=0)


def kernel(x_prompt, x_sample, cache_ckv_l0, cache_krope_l0, state_lru_l0, c, c_ctx, w_mod_l0, b_mod_l0, w_mod_l1, b_mod_l1, g_mix_l0, g_ffn_l0, g_mix_l1, g_ffn_l1, w_in_l0, g_q_l0, w_uq_l0, g_kv_l0, w_ukv_l0, conv_w_l0, conv_b_l0, w_rg_l0, b_rg_l0, w_ig_l0, b_ig_l0, lam_l0, w_o_l0, w_pool_l1, s_pool_l1, peer_wq_l0, peer_keys_l0, peer_u_l0, peer_v_l0, peer_wq_l1, peer_keys_l1, peer_u_l1, peer_v_l1, g_final):
    nb_p, s_p, d = x_prompt.shape
    nb_s, s_s, _ = x_sample.shape
    n_cache = cache_ckv_l0.shape[1]
    assert d == D_MODEL and s_p == TM and s_s % TM == 0 and n_cache % TM == 0
    t_p = nb_p * s_p
    t_s = nb_s * s_s
    t_all = t_p + t_s
    npt = t_p // TM
    tps = s_s // TM
    ntile = t_all // TM
    assert t_all % PEER_TM == 0 and t_p % PEER_TM == 0 and s_s % PEER_TM == 0

    x0 = jnp.concatenate([x_prompt.reshape(t_p, d), x_sample.reshape(t_s, d)], axis=0)

    ncond = 1 + nb_s
    cpad = jnp.zeros((2 * SUBLANES, d), F32).at[0].set(c_ctx).at[1:ncond].set(c)
    mod0 = _ada(cpad, w_mod_l0, b_mod_l0).reshape(2 * SUBLANES, 6, d)
    mod1 = _ada(cpad, w_mod_l1, b_mod_l1).reshape(2 * SUBLANES, 6, d)

    def cond_row(i):
        return jnp.where(i < npt, 0, 1 + (i - npt) // tps)

    mod_spec = pl.BlockSpec((1, 6, d), lambda i: (cond_row(i), 0, 0))
    row = lambda a: a.reshape(1, -1)

    perm = np.concatenate([np.arange(0, QK_ROPE, 2), np.arange(1, QK_ROPE, 2)])
    perm_sw = np.concatenate([np.arange(1, QK_ROPE, 2), np.arange(0, QK_ROPE, 2)])
    o1 = Q_LORA + KV_LORA
    w_kr = w_in_l0[:, o1:o1 + QK_ROPE]
    z64 = jnp.zeros((d, 128 - QK_ROPE), F32)
    w_in_ext = jnp.concatenate(
        [w_in_l0[:, :o1], w_kr, z64, w_kr[:, perm], z64, w_kr[:, perm_sw], z64, w_in_l0[:, o1 + QK_ROPE:]],
        axis=1).astype(BF16)
    assert w_in_ext.shape[1] == IN_EXT
    wq3 = w_uq_l0.reshape(Q_LORA, MLA_HEADS, QK_NOPE + QK_ROPE)
    w_uq_ext = jnp.concatenate(
        [wq3[:, :, :QK_NOPE].reshape(Q_LORA, -1),
         wq3[:, :, QK_NOPE:][:, :, perm].reshape(Q_LORA, -1),
         wq3[:, :, QK_NOPE:][:, :, perm_sw].reshape(Q_LORA, -1)], axis=1).astype(BF16)
    ck_tab, sk_tab = _rope_tables(TM, s_s)

    def rope_blk(i):
        return jnp.where(i < npt, 0, 1 + (i - npt) % tps)

    tok = lambda w: pl.BlockSpec((TM, w), lambda i: (i, 0))
    full = lambda a: pl.BlockSpec(a.shape, lambda *_: (0,) * a.ndim)
    q, ckv, kr, krr, ux, ug = pl.pallas_call(
        _inproj_kernel,
        grid=(ntile,),
        in_specs=[tok(d), mod_spec, full(row(g_mix_l0)), full(w_in_ext), full(row(g_q_l0)), full(w_uq_ext),
                  full(row(g_kv_l0)),
                  pl.BlockSpec((TM, QK_ROPE), lambda i: (rope_blk(i), 0)),
                  pl.BlockSpec((TM, QK_ROPE), lambda i: (rope_blk(i), 0))],
        out_specs=[pl.BlockSpec((MLA_HEADS, TM, QK_PAD), lambda i: (0, i, 0)),
                   tok(KV_LORA), tok(QK_ROPE), tok(QK_ROPE), tok(LRU_WIDTH), tok(LRU_WIDTH)],
        out_shape=[jax.ShapeDtypeStruct((MLA_HEADS, t_all, QK_PAD), BF16),
                   jax.ShapeDtypeStruct((t_all, KV_LORA), F32),
                   jax.ShapeDtypeStruct((t_all, QK_ROPE), F32),
                   jax.ShapeDtypeStruct((t_all, QK_ROPE), BF16),
                   jax.ShapeDtypeStruct((t_all, LRU_WIDTH), F32),
                   jax.ShapeDtypeStruct((t_all, LRU_WIDTH), F32)],
        compiler_params=_cparams(("arbitrary",)),
    )(x0, mod0, row(g_mix_l0), w_in_ext, row(g_q_l0), w_uq_ext, row(g_kv_l0), ck_tab, sk_tab)

    wkv3 = w_ukv_l0.reshape(KV_LORA, MLA_HEADS, QK_NOPE + V_HEAD)
    w_ukv_ext = jnp.concatenate([wkv3[:, :, :QK_NOPE].reshape(KV_LORA, -1),
                                 wkv3[:, :, QK_NOPE:].reshape(KV_LORA, -1)], axis=1).astype(BF16)

    def attn_call(nb, s_new, tile0, has_cache):
        nq = s_new // TM
        blk0 = tile0 * TM // s_new
        n_c = n_cache if has_cache else 0
        in_specs = [pl.BlockSpec((MLA_HEADS, TM, QK_PAD), lambda b, qi: (0, tile0 + b * nq + qi, 0)),
                    pl.BlockSpec((s_new, KV_LORA), lambda b, qi: (blk0 + b, 0)),
                    pl.BlockSpec((s_new, QK_ROPE), lambda b, qi: (blk0 + b, 0))]
        args = [q, ckv, krr]
        if has_cache:
            in_specs += [pl.BlockSpec((1, n_cache, KV_LORA), lambda b, qi: (b, 0, 0)),
                         pl.BlockSpec((1, n_cache, QK_ROPE), lambda b, qi: (b, 0, 0))]
            args += [cache_ckv_l0, cache_krope_l0[:, :, perm]]
        in_specs.append(pl.BlockSpec(w_ukv_ext.shape, lambda b, qi: (0, 0)))
        args.append(w_ukv_ext)
        return pl.pallas_call(
            functools.partial(_attn_kernel, has_cache=has_cache, s_new=s_new, n_cache=n_c),
            grid=(nb, nq),
            in_specs=in_specs,
            out_specs=pl.BlockSpec((TM, MLA_WIDTH), lambda b, qi: (b * nq + qi, 0)),
            out_shape=jax.ShapeDtypeStruct((nb * s_new, MLA_WIDTH), BF16),
            scratch_shapes=[pltpu.VMEM((MLA_HEADS, n_c + s_new, QK_PAD), BF16),
                            pltpu.VMEM((n_c + s_new, MLA_WIDTH), BF16)],
            compiler_params=_cparams(("arbitrary", "arbitrary")),
        )(*args)

    attn = jnp.concatenate([attn_call(nb_p, s_p, 0, False), attn_call(nb_s, s_s, npt, True)], axis=0)

    def lru_call(nb, s_new, tile0, h0):
        nc = s_new // TM
        r8 = TM // SUBLANES
        last8 = t_all // SUBLANES - 1

        def cur(rev):
            return pl.BlockSpec((TM, LRU_WIDTH),
                                lambda b, cc: (tile0 + b * nc + (nc - 1 - cc if rev else cc), 0))

        def prev(rev):
            return pl.BlockSpec((SUBLANES, LRU_WIDTH), lambda b, cc: (
                jnp.maximum((tile0 + b * nc + (nc - 1 - cc if rev else cc)) * r8 - 1, 0), 0))

        def nxt(rev):
            return pl.BlockSpec((SUBLANES, LRU_WIDTH), lambda b, cc: (
                jnp.minimum((tile0 + b * nc + (nc - 1 - cc if rev else cc) + 1) * r8, last8), 0))

        small = [conv_w_l0, row(conv_b_l0), w_rg_l0, b_rg_l0, w_ig_l0, b_ig_l0, lam_l0]
        return pl.pallas_call(
            functools.partial(_lru_kernel, nc=nc),
            grid=(nb, nc),
            in_specs=[prev(False), cur(False), nxt(False), prev(True), cur(True), nxt(True)]
                     + [full(a) for a in small]
                     + [pl.BlockSpec((1, 2, LRU_WIDTH), lambda b, cc: (b, 0, 0))],
            out_specs=[pl.BlockSpec((TM, LRU_WIDTH), lambda b, cc: (b * nc + cc, 0)),
                       pl.BlockSpec((TM, LRU_WIDTH), lambda b, cc: (b * nc + nc - 1 - cc, 0)),
                       pl.BlockSpec((1, 2, LRU_WIDTH), lambda b, cc: (b, 0, 0))],
            out_shape=[jax.ShapeDtypeStruct((nb * s_new, LRU_WIDTH), F32),
                       jax.ShapeDtypeStruct((nb * s_new, LRU_WIDTH), F32),
                       jax.ShapeDtypeStruct((nb, 2, LRU_WIDTH), F32)],
            scratch_shapes=[pltpu.VMEM((1, LRU_WIDTH), F32), pltpu.VMEM((1, LRU_WIDTH), F32)],
            compiler_params=_cparams(("arbitrary", "arbitrary")),
        )(ux, ux, ux, ux, ux, ux, *small, h0)

    hf_p, hb_p, new_lru = lru_call(nb_p, s_p, 0, jnp.zeros((nb_p, 2, LRU_WIDTH), F32))
    hf_s, hb_s, _ = lru_call(nb_s, s_s, npt, state_lru_l0.astype(F32))
    hf = jnp.concatenate([hf_p, hf_s], axis=0)
    hb = jnp.concatenate([hb_p, hb_s], axis=0)

    w_o = w_o_l0.astype(BF16)
    x1 = pl.pallas_call(
        _oproj_kernel,
        grid=(ntile,),
        in_specs=[tok(d), mod_spec, tok(MLA_WIDTH), tok(LRU_WIDTH), tok(LRU_WIDTH), tok(LRU_WIDTH),
                  pl.BlockSpec((MLA_WIDTH, d), lambda i: (0, 0)), pl.BlockSpec((LRU_WIDTH, d), lambda i: (1, 0))],
        out_specs=tok(d),
        out_shape=jax.ShapeDtypeStruct((t_all, d), F32),
        compiler_params=_cparams(("arbitrary",)),
    )(x0, mod0, attn, hf, hb, ug, w_o, w_o)

    def peer_call(x, mod, g_ffn, w_q, sub_keys, u, v, final_norm):
        n_exp = u.shape[0]
        assert n_exp == N_KEYS * N_KEYS and n_exp % PEER_EBLK == 0
        wqt = w_q.T.astype(BF16)
        keys = sub_keys.astype(BF16).reshape(2 * PEER_HEADS, N_KEYS, PEER_DKEY // 2)
        ub = u.astype(BF16)
        vt = v.reshape(n_exp // PEER_EBLK, PEER_EBLK, d).transpose(0, 2, 1).astype(BF16)
        tpp = PEER_TM // TM

        def cond_row_p(i):
            return cond_row(i * tpp)

        big = lambda dt: pltpu.VMEM((PEER_HEADS, N_KEYS, PEER_TM), dt)
        return pl.pallas_call(
            functools.partial(_peer_kernel, final_norm=final_norm),
            grid=(t_all // PEER_TM, n_exp // PEER_EBLK),
            in_specs=[pl.BlockSpec((PEER_TM, d), lambda i, k: (i, 0), pipeline_mode=pl.Buffered(1)),
                      pl.BlockSpec((1, 6, d), lambda i, k: (cond_row_p(i), 0, 0)),
                      pl.BlockSpec((1, d), lambda i, k: (0, 0)),
                      pl.BlockSpec(wqt.shape, lambda i, k: (0, 0)),
                      pl.BlockSpec(keys.shape, lambda i, k: (0, 0, 0)),
                      pl.BlockSpec((PEER_EBLK, d), lambda i, k: (k, 0)),
                      pl.BlockSpec((1, d, PEER_EBLK), lambda i, k: (k, 0, 0)),
                      pl.BlockSpec((1, d), lambda i, k: (0, 0))],
            out_specs=pl.BlockSpec((PEER_TM, d), lambda i, k: (i, 0), pipeline_mode=pl.Buffered(1)),
            out_shape=jax.ShapeDtypeStruct((t_all, d), F32),
            scratch_shapes=[pltpu.VMEM((d, PEER_TM), BF16),
                            pltpu.VMEM((2 * PEER_HEADS * N_KEYS, PEER_TM), BF16),
                            pltpu.VMEM((2 * PEER_HEADS * N_KEYS, PEER_TM), F32),
                            pltpu.VMEM((2, PEER_TOPK, PEER_HEADS, PEER_TM), F32),
                            pltpu.VMEM((PEER_TOPK + 1, PEER_HEADS, PEER_TM), F32),
                            pltpu.VMEM((2, N_KEYS, PEER_TM), F32),
                            big(F32), big(F32)],
            compiler_params=_cparams(("arbitrary", "arbitrary")),
        )(x, mod, row(g_ffn), wqt, keys, ub, vt, row(g_final))

    x2 = peer_call(x1, mod0, g_ffn_l0, peer_wq_l0, peer_keys_l0, peer_u_l0, peer_v_l0, False)

    rh = TM // HALO
    lasth = t_all // HALO - 1
    x3 = pl.pallas_call(
        functools.partial(_pool_kernel, tiles_per_seq=tps, n_prompt_tiles=npt),
        grid=(ntile,),
        in_specs=[pl.BlockSpec((HALO, d), lambda i: (jnp.maximum(i * rh - 1, 0), 0)),
                  tok(d),
                  pl.BlockSpec((HALO, d), lambda i: (jnp.minimum((i + 1) * rh, lasth), 0)),
                  mod_spec, full(row(g_mix_l1)),
                  pl.BlockSpec(w_pool_l1.shape, lambda i: (0, 0, 0)), full(row(s_pool_l1))],
        out_specs=tok(d),
        out_shape=jax.ShapeDtypeStruct((t_all, d), F32),
        compiler_params=_cparams(("arbitrary",)),
    )(x2, x2, x2, mod1, row(g_mix_l1), w_pool_l1.astype(BF16), row(s_pool_l1))

    y = peer_call(x3, mod1, g_ffn_l1, peer_wq_l1, peer_keys_l1, peer_u_l1, peer_v_l1, True)

    y_prompt = y[:t_p].reshape(nb_p, s_p, d)
    y_sample = y[t_p:].reshape(nb_s, s_s, d)
    new_ckv = ckv[:t_p].reshape(nb_p, s_p, KV_LORA)
    new_krope = kr[:t_p].reshape(nb_p, s_p, QK_ROPE)
    return (y_prompt, y_sample, new_ckv, new_krope, new_lru)
```

```python
import functools

import numpy as np
import jax
import jax.numpy as jnp
from jax import lax
from jax.experimental import pallas as pl
from jax.experimental.pallas import tpu as pltpu

F32 = jnp.float32
BF16 = jnp.bfloat16

D_MODEL = 1024
EPS = 1e-6
GRID_W = 64
MLA_HEADS = 4
Q_LORA = 384
KV_LORA = 256
QK_NOPE = 128
QK_ROPE = 64
V_HEAD = 128
MLA_WIDTH = MLA_HEADS * V_HEAD
ROPE_BASE = 10000.0
LRU_WIDTH = 512
LRU_BLOCKS = 4
LRU_BLOCK = LRU_WIDTH // LRU_BLOCKS
CONV_W = 4
CONV_LEFT = 2
LRU_C = 8.0
POOL_WINDOWS = (2, 4, 8, 16)
POOL_GROUP = D_MODEL // len(POOL_WINDOWS)
PEER_HEADS = 8
N_KEYS = 128
PEER_DKEY = 256
PEER_TOPK = 16

SUBLANES = 8
LANES = 128
VMEM_LIMIT = 56 * 1024 * 1024

TM = 256
QK_PAD = 256
HALO = 16
PEER_TM = 1024
PEER_EBLK = 512
PEER_SUB = 256
BF16_ROWS = 16
IN_EXT = 2048


def _rms(x, g):
    return x * lax.rsqrt(jnp.mean(x * x, axis=-1, keepdims=True) + EPS) * g


def _cparams(sem):
    return pltpu.CompilerParams(dimension_semantics=sem, vmem_limit_bytes=VMEM_LIMIT)


def _ada_kernel(c_ref, w_ref, b_ref, o_ref):
    c = c_ref[...]
    o_ref[...] = jnp.dot(c * jax.nn.sigmoid(c), w_ref[...], preferred_element_type=F32) + b_ref[...]


def _ada(cpad, w_mod, b_mod):
    n = w_mod.shape[1]
    bn = 768
    return pl.pallas_call(
        _ada_kernel,
        grid=(n // bn,),
        in_specs=[pl.BlockSpec(cpad.shape, lambda j: (0, 0)),
                  pl.BlockSpec((D_MODEL, bn), lambda j: (0, j)),
                  pl.BlockSpec((1, bn), lambda j: (0, j))],
        out_specs=pl.BlockSpec((cpad.shape[0], bn), lambda j: (0, j)),
        out_shape=jax.ShapeDtypeStruct((cpad.shape[0], n), F32),
        compiler_params=_cparams(("arbitrary",)),
    )(cpad, w_mod, b_mod.reshape(1, n))


def _inproj_kernel(x_ref, mod_ref, g_ref, win_ref, gq_ref, wuq_ref, gkv_ref, ck_ref, sk_ref,
                   q_ref, ckv_ref, kr_ref, krr_ref, ux_ref, ug_ref):
    mod = mod_ref[0]
    h = _rms(x_ref[...], g_ref[...]) * (1.0 + mod[1:2]) + mod[0:1]
    y = jnp.dot(h.astype(BF16), win_ref[...], preferred_element_type=F32)
    cq = y[:, 0:Q_LORA]
    ckv = y[:, Q_LORA:Q_LORA + KV_LORA]
    o = Q_LORA + KV_LORA
    kr = y[:, o:o + QK_ROPE]
    krp = y[:, o + 128:o + 128 + QK_ROPE]
    krs = y[:, o + 256:o + 256 + QK_ROPE]
    ux_ref[...] = y[:, o + 384:o + 384 + LRU_WIDTH]
    ug_ref[...] = y[:, o + 384 + LRU_WIDTH:o + 384 + 2 * LRU_WIDTH]
    ckv_ref[...] = _rms(ckv, gkv_ref[...])
    kr_ref[...] = kr
    ck = ck_ref[...]
    sk = sk_ref[...]
    krr_ref[...] = (krp * ck + krs * sk).astype(BF16)
    q = jnp.dot(_rms(cq, gq_ref[...]).astype(BF16), wuq_ref[...], preferred_element_type=F32)
    nw = MLA_HEADS * QK_NOPE
    rw = MLA_HEADS * QK_ROPE
    for hd in range(MLA_HEADS):
        qp = q[:, nw + hd * QK_ROPE:nw + (hd + 1) * QK_ROPE]
        qs = q[:, nw + rw + hd * QK_ROPE:nw + rw + (hd + 1) * QK_ROPE]
        q_ref[hd, :, 0:QK_NOPE] = q[:, hd * QK_NOPE:(hd + 1) * QK_NOPE].astype(BF16)
        q_ref[hd, :, QK_NOPE:QK_NOPE + QK_ROPE] = (qp * ck + qs * sk).astype(BF16)
        q_ref[hd, :, QK_NOPE + QK_ROPE:QK_PAD] = jnp.zeros((TM, QK_PAD - QK_NOPE - QK_ROPE), BF16)


def _attn_kernel(*refs, has_cache, s_new, n_cache):
    if has_cache:
        q_ref, ckv_ref, krr_ref, cckv_ref, ckr_ref, wukv_ref, o_ref, kcat_ref, vv_ref = refs
    else:
        q_ref, ckv_ref, krr_ref, wukv_ref, o_ref, kcat_ref, vv_ref = refs
    sk = n_cache + s_new
    kw = MLA_HEADS * QK_NOPE
    zpad = jnp.zeros((TM, QK_PAD - QK_NOPE - QK_ROPE), BF16)

    def put_keys(row0, ckv_rows, kr_rows):
        kv = jnp.dot(ckv_rows.astype(BF16), wukv_ref[...], preferred_element_type=F32)
        rows = pl.ds(row0, TM)
        for hd in range(MLA_HEADS):
            kcat_ref[hd, rows, 0:QK_NOPE] = kv[:, hd * QK_NOPE:(hd + 1) * QK_NOPE].astype(BF16)
            kcat_ref[hd, rows, QK_NOPE:QK_NOPE + QK_ROPE] = kr_rows.astype(BF16)
            kcat_ref[hd, rows, QK_NOPE + QK_ROPE:QK_PAD] = zpad
        vv_ref[rows, :] = kv[:, kw:].astype(BF16)

    @pl.when(pl.program_id(1) == 0)
    def _():
        if has_cache:
            for c in range(n_cache // TM):
                put_keys(c * TM, cckv_ref[0, c * TM:(c + 1) * TM, :], ckr_ref[0, c * TM:(c + 1) * TM, :])

        def body(c, carry):
            r0 = pl.multiple_of(c * TM, TM)
            put_keys(n_cache + r0, ckv_ref[pl.ds(r0, TM), :], krr_ref[pl.ds(r0, TM), :])
            return carry
        lax.fori_loop(0, s_new // TM, body, 0)

    scale = (QK_NOPE + QK_ROPE) ** -0.5
    for hd in range(MLA_HEADS):
        s = lax.dot_general(q_ref[hd], kcat_ref[hd], (((1,), (1,)), ((), ())),
                            preferred_element_type=F32) * scale
        m = jnp.max(s, axis=-1, keepdims=True)
        e = jnp.exp(s - m)
        l = jnp.sum(e, axis=-1, keepdims=True)
        o = jnp.dot(e.astype(BF16), vv_ref[:, hd * V_HEAD:(hd + 1) * V_HEAD], preferred_element_type=F32)
        o_ref[:, hd * V_HEAD:(hd + 1) * V_HEAD] = (o / l).astype(BF16)
    del sk


def _lru_dir(xp_ref, xc_ref, xn_ref, valid_prev, valid_next, d, reverse, cw_ref, cb_ref,
             wr_ref, br_ref, wi_ref, bi_ref, lam_ref, carry):
    xp = jnp.where(valid_prev, xp_ref[...], 0.0)
    xn = jnp.where(valid_next, xn_ref[...], 0.0)
    xx = jnp.concatenate([xp, xc_ref[...], xn], axis=0)
    n = TM + 2 * SUBLANES
    xc = cb_ref[...]
    for k in range(CONV_W):
        sh = (CONV_LEFT - k) % n
        xs = xx if sh == 0 else pltpu.roll(xx, sh, 0)
        xc = xc + xs[SUBLANES:SUBLANES + TM] * cw_ref[k:k + 1, :]
    rs, is_ = [], []
    for b in range(LRU_BLOCKS):
        xb = xc[:, b * LRU_BLOCK:(b + 1) * LRU_BLOCK]
        rs.append(jnp.dot(xb, wr_ref[d, b], preferred_element_type=F32))
        is_.append(jnp.dot(xb, wi_ref[d, b], preferred_element_type=F32))
    r = jax.nn.sigmoid(jnp.concatenate(rs, axis=-1) + br_ref[d:d + 1, :])
    i = jax.nn.sigmoid(jnp.concatenate(is_, axis=-1) + bi_ref[d:d + 1, :])
    nl = -lam_ref[d:d + 1, :]
    softplus = jnp.maximum(nl, 0.0) + jnp.log1p(jnp.exp(-jnp.abs(nl)))
    log_a = -LRU_C * r * softplus
    a = jnp.exp(log_a)
    bx = jnp.sqrt(jnp.tanh(-log_a) * (a * a + 1.0)) * (i * xc)
    t = lax.broadcasted_iota(jnp.int32, (TM, 1), 0)
    step = 1
    while step < TM:
        if reverse:
            keep = t < TM - step
            sh = TM - step
        else:
            keep = t >= step
            sh = step
        a_s = jnp.where(keep, pltpu.roll(a, sh, 0), 1.0)
        b_s = jnp.where(keep, pltpu.roll(bx, sh, 0), 0.0)
        bx = a * b_s + bx
        a = a * a_s
        step *= 2
    return a * carry + bx


def _lru_kernel(fxp, fxc, fxn, bxp, bxc, bxn, cw_ref, cb_ref, wr_ref, br_ref, wi_ref, bi_ref, lam_ref,
                h0_ref, hf_ref, hb_ref, st_ref, cf_ref, cbk_ref, *, nc):
    c = pl.program_id(1)

    @pl.when(c == 0)
    def _():
        cf_ref[...] = h0_ref[0, 0:1, :]
        cbk_ref[...] = h0_ref[0, 1:2, :]

    params = (cw_ref, cb_ref, wr_ref, br_ref, wi_ref, bi_ref, lam_ref)
    hf = _lru_dir(fxp, fxc, fxn, c > 0, c < nc - 1, 0, False, *params, cf_ref[...])
    hf_ref[...] = hf
    cf_ref[...] = hf[TM - 1:TM, :]
    hb = _lru_dir(bxp, bxc, bxn, c < nc - 1, c > 0, 1, True, *params, cbk_ref[...])
    hb_ref[...] = hb
    cbk_ref[...] = hb[0:1, :]
    st_ref[0, 0:1, :] = hf[TM - 1:TM, :]
    st_ref[0, 1:2, :] = hb[0:1, :]


def _oproj_kernel(x_ref, mod_ref, at_ref, hf_ref, hb_ref, ug_ref, woa_ref, wor_ref, o_ref):
    mod = mod_ref[0]
    rec = ((hf_ref[...] + hb_ref[...]) * jax.nn.gelu(ug_ref[...])).astype(BF16)
    out = (jnp.dot(at_ref[...], woa_ref[...], preferred_element_type=F32)
           + jnp.dot(rec, wor_ref[...], preferred_element_type=F32))
    o_ref[...] = x_ref[...] + mod[2:3] * out


def _pool_kernel(xp_ref, xc_ref, xn_ref, mod_ref, g_ref, wp_ref, sp_ref, o_ref, *, tiles_per_seq, n_prompt_tiles):
    i = pl.program_id(0)
    j = jnp.where(i < n_prompt_tiles, 0, (i - n_prompt_tiles) % tiles_per_seq)
    ntile = jnp.where(i < n_prompt_tiles, 1, tiles_per_seq)
    mod = mod_ref[0]
    g = g_ref[...]

    def hmod(x):
        return _rms(x, g) * (1.0 + mod[1:2]) + mod[0:1]

    x = xc_ref[...]
    hc = hmod(x)
    hp = jnp.where(j > 0, hmod(xp_ref[...]), 0.0)
    hn = jnp.where(j < ntile - 1, hmod(xn_ref[...]), 0.0)
    hh = jnp.concatenate([hp, hc, hn], axis=0)
    n = TM + 2 * HALO
    seq_len = ntile * TM
    t = j * TM + lax.broadcasted_iota(jnp.int32, (TM, 1), 0)
    ys = []
    for gi, w in enumerate(POOL_WINDOWS):
        cols = slice(gi * POOL_GROUP, (gi + 1) * POOL_GROUP)
        p = hh[:, cols]
        p = p + pltpu.roll(p, 1, 0)
        half = 1
        while 2 * half < w:
            p = pltpu.roll(p, half, 0) + pltpu.roll(p, n - half, 0)
            half *= 2
        lo = jnp.maximum(t - w // 2, 0)
        hi = jnp.minimum(t + (w - w // 2), seq_len)
        mean = p[HALO:HALO + TM] / (hi - lo).astype(F32)
        dg = (mean - hc[:, cols]).astype(BF16)
        ys.append(jnp.dot(dg, wp_ref[gi], preferred_element_type=F32))
    y = jnp.concatenate(ys, axis=-1) * sp_ref[...]
    o_ref[...] = x + mod[2:3] * y


def _sort_pairs(n):
    pairs = []

    def merge(lo, cnt, r):
        step = r * 2
        if step < cnt:
            merge(lo, cnt, step)
            merge(lo + r, cnt, step)
            for i in range(lo + r, lo + cnt - r, step):
                pairs.append((i, i + r))
        else:
            pairs.append((lo, lo + r))

    def sort(lo, cnt):
        if cnt > 1:
            m = cnt // 2
            sort(lo, m)
            sort(lo + m, m)
            merge(lo, cnt, 1)

    sort(0, n)
    return pairs


_SORT16 = _sort_pairs(PEER_TOPK)
_HYPER = [(a, b) for a in range(PEER_TOPK) for b in range(PEER_TOPK) if (a + 1) * (b + 1) <= PEER_TOPK]


def _top16_sorted(s):
    k = PEER_TOPK
    x = [s[SUBLANES * r:SUBLANES * (r + 1), :] for r in range(N_KEYS // SUBLANES)]
    for (i, j) in _SORT16:
        hi = jnp.maximum(x[i], x[j])
        lo = jnp.minimum(x[i], x[j])
        x[i], x[j] = hi, lo
    for shift in (4, 2, 1):
        y = [jnp.maximum(x[r], pltpu.roll(x[k - 1 - r], shift, 0)) for r in range(k)]
        stride = k // 2
        while stride >= 1:
            for i in range(k):
                if i & stride == 0:
                    hi = jnp.maximum(y[i], y[i + stride])
                    lo = jnp.minimum(y[i], y[i + stride])
                    y[i], y[i + stride] = hi, lo
            stride //= 2
        x = y
    return x


def _pair_counts(sv1, sv2):
    one = jnp.ones_like(sv1[0])
    zero = jnp.zeros_like(sv1[0])
    cand = [sv1[a] + sv2[b] for (a, b) in _HYPER]

    def ordered(i, j):
        (ai, bi), (aj, bj) = _HYPER[i], _HYPER[j]
        return (aj <= ai and bj <= bi) or (ai <= aj and bi <= bj)

    nh = len(_HYPER)
    cnt = [float((a + 1) * (b + 1) - 1 + sum(1 for j in range(i + 1, nh) if not ordered(i, j))) * one
           for i, (a, b) in enumerate(_HYPER)]
    for i in range(nh):
        for jx in range(i):
            if ordered(i, jx):
                continue
            ge = jnp.where(cand[jx] >= cand[i], one, zero)
            cnt[i] = cnt[i] + ge
            cnt[jx] = cnt[jx] - ge
    e1 = [jnp.exp(sv1[a] - sv1[0]) for a in range(PEER_TOPK)]
    e2 = [jnp.exp(sv2[b] - sv2[0]) for b in range(PEER_TOPK)]
    n = [zero for _ in range(PEER_TOPK)]
    z = zero
    for i, (a, b) in enumerate(_HYPER):
        sel = jnp.where(cnt[i] < float(PEER_TOPK), one, zero)
        n[a] = n[a] + sel
        z = z + sel * (e1[a] * e2[b])
    return n, 1.0 / z


def _head_tables(hd, lanes, sc_ref, sv_ref, nz_ref, ex_ref, cnt1_ref, e1n_ref, rank2_ref, e2_ref, exact):
    k = PEER_TOPK

    def bc(ref, *idx):
        return jnp.broadcast_to(ref[idx + (slice(hd, hd + 1), lanes)], (SUBLANES, LANES))

    sv1b = [bc(sv_ref, 0, a) for a in range(k)]
    nb = [bc(nz_ref, a) for a in range(k)]
    izb = bc(nz_ref, k)
    c1 = jnp.zeros((SUBLANES, LANES), F32)
    for r in range(N_KEYS // SUBLANES):
        rows = slice(r * SUBLANES, (r + 1) * SUBLANES)
        s1 = sc_ref[pl.ds(2 * hd * N_KEYS + r * SUBLANES, SUBLANES), lanes]
        cnt1 = jnp.zeros_like(s1)
        if exact:
            rank1 = jnp.zeros_like(s1)
            for a in range(k):
                rank1 = jnp.where(sv1b[a] > s1, float(a + 1), rank1)
            rank1 = rank1 + ex_ref[0, rows, lanes]
            for a in range(k):
                cnt1 = jnp.where(rank1 == float(a), nb[a], cnt1)
        else:
            for a in range(k):
                cnt1 = jnp.where(s1 == sv1b[a], nb[a], cnt1)
            c1 = c1 + jnp.where(s1 >= sv1b[k - 1], 1.0, 0.0)
        cnt1_ref[hd, rows, lanes] = cnt1
        e1n_ref[hd, rows, lanes] = jnp.exp(s1 - sv1b[0]) * izb
    sv2b = [bc(sv_ref, 1, a) for a in range(k)]
    c2 = jnp.zeros((SUBLANES, LANES), F32)
    for r in range(N_KEYS // BF16_ROWS):
        rk, e2 = [], []
        for q in range(BF16_ROWS // SUBLANES):
            r0 = r * BF16_ROWS + q * SUBLANES
            rows = slice(r0, r0 + SUBLANES)
            s2 = sc_ref[pl.ds((2 * hd + 1) * N_KEYS + r0, SUBLANES), lanes]
            rank2 = jnp.zeros_like(s2)
            for a in range(k):
                rank2 = jnp.where(sv2b[a] > s2, float(a + 1), rank2)
            if exact:
                rank2 = jnp.minimum(rank2 + ex_ref[1, rows, lanes], float(k))
            else:
                c2 = c2 + jnp.where(rank2 < float(k), 1.0, 0.0)
            rk.append(rank2)
            e2.append(jnp.exp(s2 - sv2b[0]))
        rows16 = pl.ds(hd * N_KEYS + r * BF16_ROWS, BF16_ROWS)
        rank2_ref[rows16, lanes] = jnp.concatenate(rk, axis=0).astype(BF16)
        e2_ref[rows16, lanes] = jnp.concatenate(e2, axis=0).astype(BF16)
    if exact:
        return None
    ex_ref[0, 0:SUBLANES, lanes] = c1
    ex_ref[1, 0:SUBLANES, lanes] = c2
    b = jnp.zeros((SUBLANES, LANES), F32)
    for svb in (sv1b, sv2b):
        for a in range(k - 1):
            b = b + jnp.where(svb[a] == svb[a + 1], 1.0, 0.0)
    return b


def _tie_offsets(hd, sc_ref, ex_ref):
    tm = sc_ref.shape[-1]
    nidx = lax.broadcasted_iota(jnp.int32, (N_KEYS, tm), 0)
    for p in range(2):
        s = sc_ref[pl.ds((2 * hd + p) * N_KEYS, N_KEYS), :]

        def body(m, e, s=s, p=p):
            row = sc_ref[pl.ds((2 * hd + p) * N_KEYS + m, 1), :]
            return e + jnp.where((s == row) & (nidx > m), 1.0, 0.0)

        ex_ref[p] = lax.fori_loop(0, N_KEYS, body, jnp.zeros((N_KEYS, tm), F32))


def _peer_prologue(x_ref, mod_ref, g_ref, wqt_ref, keys_ref, h2t_ref, qb_ref, sc_ref, sv_ref, nz_ref,
                   ex_ref, cnt1_ref, e1n_ref):
    tm = PEER_TM
    half = PEER_DKEY // 2
    ngroup = tm // LANES
    mod = mod_ref[0]
    h2 = _rms(x_ref[...], g_ref[...]) * (1.0 + mod[4:5]) + mod[3:4]
    h2t_ref[...] = h2.T.astype(BF16)

    def lane_group(g):
        return pl.ds(pl.multiple_of(g * LANES, LANES), LANES)

    qb_ref[...] = jnp.dot(wqt_ref[...], h2t_ref[...], preferred_element_type=F32).astype(BF16)
    for i in range(2 * PEER_HEADS):
        sc_ref[i * N_KEYS:(i + 1) * N_KEYS, :] = jnp.dot(keys_ref[i], qb_ref[i * half:(i + 1) * half, :],
                                                         preferred_element_type=F32)

        def group(g, c, i=i):
            lanes = lane_group(g)
            top = _top16_sorted(sc_ref[i * N_KEYS:(i + 1) * N_KEYS, lanes])
            for a in range(PEER_TOPK):
                sv_ref[i % 2, a, i // 2:i // 2 + 1, lanes] = top[a][0:1, :]
            return c
        lax.fori_loop(0, ngroup, group, 0)

    def counts(g, carry):
        lanes = lane_group(g)
        sv1 = [sv_ref[0, a, :, lanes] for a in range(PEER_TOPK)]
        sv2 = [sv_ref[1, a, :, lanes] for a in range(PEER_TOPK)]
        n, inv_z = _pair_counts(sv1, sv2)
        for a in range(PEER_TOPK):
            nz_ref[a, :, lanes] = n[a]
        nz_ref[PEER_TOPK, :, lanes] = inv_z
        return carry
    lax.fori_loop(0, ngroup, counts, 0)

    nrow = PEER_HEADS * N_KEYS
    tabs = (sc_ref, sv_ref, nz_ref, ex_ref, cnt1_ref, e1n_ref, qb_ref.at[0:nrow], qb_ref.at[nrow:2 * nrow])

    for hd in range(PEER_HEADS):
        def fast(g, bad, hd=hd):
            return bad + _head_tables(hd, lane_group(g), *tabs, exact=False)
        bad = lax.fori_loop(0, ngroup, fast, jnp.zeros((SUBLANES, LANES), F32))
        totals = jnp.sum(ex_ref[:, 0:SUBLANES, :], axis=1)
        ties = jnp.max(bad) + jnp.max(jnp.abs(totals - float(PEER_TOPK)))

        @pl.when(ties > 0.0)
        def _(hd=hd):
            _tie_offsets(hd, sc_ref, ex_ref)

            def slow(g, c):
                _head_tables(hd, lane_group(g), *tabs, exact=True)
                return c
            lax.fori_loop(0, ngroup, slow, 0)
    sc_ref[0:D_MODEL, :] = jnp.zeros((D_MODEL, tm), F32)


def _peer_kernel(x_ref, mod_ref, g_ref, wqt_ref, keys_ref, u_ref, vt_ref, gfin_ref, o_ref,
                 h2t_ref, qb_ref, sc_ref, sv_ref, nz_ref, ex_ref, cnt1_ref, e1n_ref, y_ref, *, final_norm):
    k = pl.program_id(1)
    tm = PEER_TM
    nrow = PEER_HEADS * N_KEYS
    acc_ref = sc_ref.at[0:D_MODEL]
    rank2_ref = qb_ref.at[0:nrow]
    e2_ref = qb_ref.at[nrow:2 * nrow]

    nstep = pl.num_programs(1) - 1
    zero = jnp.zeros((BF16_ROWS, LANES), BF16)
    npiece = PEER_EBLK // N_KEYS

    def up_dot(p):
        return jnp.dot(u_ref[p * N_KEYS:(p + 1) * N_KEYS, :], h2t_ref[...], preferred_element_type=F32)

    dchunk = D_MODEL // npiece

    def down_dot(c):
        rows = slice(c * dchunk, (c + 1) * dchunk)
        acc_ref[rows, :] += jnp.dot(vt_ref[0, rows, :], y_ref[(k + 1) % 2], preferred_element_type=F32)

    def expert_block(with_down):
        slot = k % 2
        raw = up_dot(0)
        for p in range(npiece):
            nxt = up_dot(p + 1) if p + 1 < npiece else None
            if with_down:
                down_dot(p)
            i1 = k * npiece + p
            act = jax.nn.gelu(raw.astype(BF16))
            crow = [cnt1_ref[hd, pl.ds(i1, 1), :] for hd in range(PEER_HEADS)]
            erow = [e1n_ref[hd, pl.ds(i1, 1), :] for hd in range(PEER_HEADS)]
            for g in range(tm // LANES):
                lanes = slice(g * LANES, (g + 1) * LANES)
                cbs = [jnp.broadcast_to(crow[hd][:, lanes], (BF16_ROWS, LANES)).astype(BF16)
                       for hd in range(PEER_HEADS)]
                ebs = [jnp.broadcast_to(erow[hd][:, lanes], (BF16_ROWS, LANES)).astype(BF16)
                       for hd in range(PEER_HEADS)]
                for r in range(N_KEYS // BF16_ROWS):
                    rows = slice(r * BF16_ROWS, (r + 1) * BF16_ROWS)
                    w = None
                    for hd in range(PEER_HEADS):
                        hrows = pl.ds(hd * N_KEYS + r * BF16_ROWS, BF16_ROWS)
                        t = jnp.where(rank2_ref[hrows, lanes] < cbs[hd], e2_ref[hrows, lanes], zero) * ebs[hd]
                        w = t if w is None else w + t
                    y_ref[slot, pl.ds(p * N_KEYS + r * BF16_ROWS, BF16_ROWS), lanes] = act[rows, lanes] * w
            raw = nxt

    @pl.when(k == 0)
    def _():
        _peer_prologue(x_ref, mod_ref, g_ref, wqt_ref, keys_ref, h2t_ref, qb_ref, sc_ref, sv_ref, nz_ref,
                       ex_ref, cnt1_ref, e1n_ref)
        expert_block(False)

    @pl.when((k > 0) & (k < nstep))
    def _():
        expert_block(True)

    @pl.when(k == nstep)
    def _():
        for c in range(npiece):
            down_dot(c)
        out = x_ref[...] + mod_ref[0][5:6] * acc_ref[...].T
        if final_norm:
            out = _rms(out, gfin_ref[...])
        o_ref[...] = out


def _rope_tables(s_prompt_tile, s_sample):
    n_rows = s_sample // GRID_W
    rows = jnp.repeat(jnp.arange(n_rows, dtype=F32), GRID_W)
    cols = jnp.tile(jnp.arange(GRID_W, dtype=F32), n_rows)
    axis_dim = QK_ROPE // 2
    inv_freq = ROPE_BASE ** (-jnp.arange(0, axis_dim, 2, dtype=F32) / axis_dim)
    ang = jnp.concatenate([rows[:, None] * inv_freq, cols[:, None] * inv_freq], axis=-1)
    cos, sin = jnp.cos(ang), jnp.sin(ang)
    ck = jnp.concatenate([cos, cos], axis=-1)
    sk = jnp.concatenate([-sin, sin], axis=-1)
    ident_c = jnp.ones((s_prompt_tile, QK_ROPE), F32)
    ident_s = jnp.zeros((s_prompt_tile, QK_ROPE), F32)
    return jnp.concatenate([ident_c, ck], axis=0), jnp.concatenate([ident_s, sk], axis=0)


def kernel(x_prompt, x_sample, cache_ckv_l0, cache_krope_l0, state_lru_l0, c, c_ctx, w_mod_l0, b_mod_l0, w_mod_l1, b_mod_l1, g_mix_l0, g_ffn_l0, g_mix_l1, g_ffn_l1, w_in_l0, g_q_l0, w_uq_l0, g_kv_l0, w_ukv_l0, conv_w_l0, conv_b_l0, w_rg_l0, b_rg_l0, w_ig_l0, b_ig_l0, lam_l0, w_o_l0, w_pool_l1, s_pool_l1, peer_wq_l0, peer_keys_l0, peer_u_l0, peer_v_l0, peer_wq_l1, peer_keys_l1, peer_u_l1, peer_v_l1, g_final):
    nb_p, s_p, d = x_prompt.shape
    nb_s, s_s, _ = x_sample.shape
    n_cache = cache_ckv_l0.shape[1]
    assert d == D_MODEL and s_p == TM and s_s % TM == 0 and n_cache % TM == 0
    t_p = nb_p * s_p
    t_s = nb_s * s_s
    t_all = t_p + t_s
    npt = t_p // TM
    tps = s_s // TM
    ntile = t_all // TM
    assert t_all % PEER_TM == 0 and t_p % PEER_TM == 0 and s_s % PEER_TM == 0

    x0 = jnp.concatenate([x_prompt.reshape(t_p, d), x_sample.reshape(t_s, d)], axis=0)

    ncond = 1 + nb_s
    cpad = jnp.zeros((2 * SUBLANES, d), F32).at[0].set(c_ctx).at[1:ncond].set(c)
    mod0 = _ada(cpad, w_mod_l0, b_mod_l0).reshape(2 * SUBLANES, 6, d)
    mod1 = _ada(cpad, w_mod_l1, b_mod_l1).reshape(2 * SUBLANES, 6, d)

    def cond_row(i):
        return jnp.where(i < npt, 0, 1 + (i - npt) // tps)

    mod_spec = pl.BlockSpec((1, 6, d), lambda i: (cond_row(i), 0, 0))
    row = lambda a: a.reshape(1, -1)

    perm = np.concatenate([np.arange(0, QK_ROPE, 2), np.arange(1, QK_ROPE, 2)])
    perm_sw = np.concatenate([np.arange(1, QK_ROPE, 2), np.arange(0, QK_ROPE, 2)])
    o1 = Q_LORA + KV_LORA
    w_kr = w_in_l0[:, o1:o1 + QK_ROPE]
    z64 = jnp.zeros((d, 128 - QK_ROPE), F32)
    w_in_ext = jnp.concatenate(
        [w_in_l0[:, :o1], w_kr, z64, w_kr[:, perm], z64, w_kr[:, perm_sw], z64, w_in_l0[:, o1 + QK_ROPE:]],
        axis=1).astype(BF16)
    assert w_in_ext.shape[1] == IN_EXT
    wq3 = w_uq_l0.reshape(Q_LORA, MLA_HEADS, QK_NOPE + QK_ROPE)
    w_uq_ext = jnp.concatenate(
        [wq3[:, :, :QK_NOPE].reshape(Q_LORA, -1),
         wq3[:, :, QK_NOPE:][:, :, perm].reshape(Q_LORA, -1),
         wq3[:, :, QK_NOPE:][:, :, perm_sw].reshape(Q_LORA, -1)], axis=1).astype(BF16)
    ck_tab, sk_tab = _rope_tables(TM, s_s)

    def rope_blk(i):
        return jnp.where(i < npt, 0, 1 + (i - npt) % tps)

    tok = lambda w: pl.BlockSpec((TM, w), lambda i: (i, 0))
    full = lambda a: pl.BlockSpec(a.shape, lambda *_: (0,) * a.ndim)
    q, ckv, kr, krr, ux, ug = pl.pallas_call(
        _inproj_kernel,
        grid=(ntile,),
        in_specs=[tok(d), mod_spec, full(row(g_mix_l0)), full(w_in_ext), full(row(g_q_l0)), full(w_uq_ext),
                  full(row(g_kv_l0)),
                  pl.BlockSpec((TM, QK_ROPE), lambda i: (rope_blk(i), 0)),
                  pl.BlockSpec((TM, QK_ROPE), lambda i: (rope_blk(i), 0))],
        out_specs=[pl.BlockSpec((MLA_HEADS, TM, QK_PAD), lambda i: (0, i, 0)),
                   tok(KV_LORA), tok(QK_ROPE), tok(QK_ROPE), tok(LRU_WIDTH), tok(LRU_WIDTH)],
        out_shape=[jax.ShapeDtypeStruct((MLA_HEADS, t_all, QK_PAD), BF16),
                   jax.ShapeDtypeStruct((t_all, KV_LORA), F32),
                   jax.ShapeDtypeStruct((t_all, QK_ROPE), F32),
                   jax.ShapeDtypeStruct((t_all, QK_ROPE), BF16),
                   jax.ShapeDtypeStruct((t_all, LRU_WIDTH), F32),
                   jax.ShapeDtypeStruct((t_all, LRU_WIDTH), F32)],
        compiler_params=_cparams(("arbitrary",)),
    )(x0, mod0, row(g_mix_l0), w_in_ext, row(g_q_l0), w_uq_ext, row(g_kv_l0), ck_tab, sk_tab)

    wkv3 = w_ukv_l0.reshape(KV_LORA, MLA_HEADS, QK_NOPE + V_HEAD)
    w_ukv_ext = jnp.concatenate([wkv3[:, :, :QK_NOPE].reshape(KV_LORA, -1),
                                 wkv3[:, :, QK_NOPE:].reshape(KV_LORA, -1)], axis=1).astype(BF16)

    def attn_call(nb, s_new, tile0, has_cache):
        nq = s_new // TM
        blk0 = tile0 * TM // s_new
        n_c = n_cache if has_cache else 0
        in_specs = [pl.BlockSpec((MLA_HEADS, TM, QK_PAD), lambda b, qi: (0, tile0 + b * nq + qi, 0)),
                    pl.BlockSpec((s_new, KV_LORA), lambda b, qi: (blk0 + b, 0)),
                    pl.BlockSpec((s_new, QK_ROPE), lambda b, qi: (blk0 + b, 0))]
        args = [q, ckv, krr]
        if has_cache:
            in_specs += [pl.BlockSpec((1, n_cache, KV_LORA), lambda b, qi: (b, 0, 0)),
                         pl.BlockSpec((1, n_cache, QK_ROPE), lambda b, qi: (b, 0, 0))]
            args += [cache_ckv_l0, cache_krope_l0[:, :, perm]]
        in_specs.append(pl.BlockSpec(w_ukv_ext.shape, lambda b, qi: (0, 0)))
        args.append(w_ukv_ext)
        return pl.pallas_call(
            functools.partial(_attn_kernel, has_cache=has_cache, s_new=s_new, n_cache=n_c),
            grid=(nb, nq),
            in_specs=in_specs,
            out_specs=pl.BlockSpec((TM, MLA_WIDTH), lambda b, qi: (b * nq + qi, 0)),
            out_shape=jax.ShapeDtypeStruct((nb * s_new, MLA_WIDTH), BF16),
            scratch_shapes=[pltpu.VMEM((MLA_HEADS, n_c + s_new, QK_PAD), BF16),
                            pltpu.VMEM((n_c + s_new, MLA_WIDTH), BF16)],
            compiler_params=_cparams(("arbitrary", "arbitrary")),
        )(*args)

    attn = jnp.concatenate([attn_call(nb_p, s_p, 0, False), attn_call(nb_s, s_s, npt, True)], axis=0)

    def lru_call(nb, s_new, tile0, h0):
        nc = s_new // TM
        r8 = TM // SUBLANES
        last8 = t_all // SUBLANES - 1

        def cur(rev):
            return pl.BlockSpec((TM, LRU_WIDTH),
                                lambda b, cc: (tile0 + b * nc + (nc - 1 - cc if rev else cc), 0))

        def prev(rev):
            return pl.BlockSpec((SUBLANES, LRU_WIDTH), lambda b, cc: (
                jnp.maximum((tile0 + b * nc + (nc - 1 - cc if rev else cc)) * r8 - 1, 0), 0))

        def nxt(rev):
            return pl.BlockSpec((SUBLANES, LRU_WIDTH), lambda b, cc: (
                jnp.minimum((tile0 + b * nc + (nc - 1 - cc if rev else cc) + 1) * r8, last8), 0))

        small = [conv_w_l0, row(conv_b_l0), w_rg_l0, b_rg_l0, w_ig_l0, b_ig_l0, lam_l0]
        return pl.pallas_call(
            functools.partial(_lru_kernel, nc=nc),
            grid=(nb, nc),
            in_specs=[prev(False), cur(False), nxt(False), prev(True), cur(True), nxt(True)]
                     + [full(a) for a in small]
                     + [pl.BlockSpec((1, 2, LRU_WIDTH), lambda b, cc: (b, 0, 0))],
            out_specs=[pl.BlockSpec((TM, LRU_WIDTH), lambda b, cc: (b * nc + cc, 0)),
                       pl.BlockSpec((TM, LRU_WIDTH), lambda b, cc: (b * nc + nc - 1 - cc, 0)),
                       pl.BlockSpec((1, 2, LRU_WIDTH), lambda b, cc: (b, 0, 0))],
            out_shape=[jax.ShapeDtypeStruct((nb * s_new, LRU_WIDTH), F32),
                       jax.ShapeDtypeStruct((nb * s_new, LRU_WIDTH), F32),
                       jax.ShapeDtypeStruct((nb, 2, LRU_WIDTH), F32)],
            scratch_shapes=[pltpu.VMEM((1, LRU_WIDTH), F32), pltpu.VMEM((1, LRU_WIDTH), F32)],
            compiler_params=_cparams(("arbitrary", "arbitrary")),
        )(ux, ux, ux, ux, ux, ux, *small, h0)

    hf_p, hb_p, new_lru = lru_call(nb_p, s_p, 0, jnp.zeros((nb_p, 2, LRU_WIDTH), F32))
    hf_s, hb_s, _ = lru_call(nb_s, s_s, npt, state_lru_l0.astype(F32))
    hf = jnp.concatenate([hf_p, hf_s], axis=0)
    hb = jnp.concatenate([hb_p, hb_s], axis=0)

    w_o = w_o_l0.astype(BF16)
    x1 = pl.pallas_call(
        _oproj_kernel,
        grid=(ntile,),
        in_specs=[tok(d), mod_spec, tok(MLA_WIDTH), tok(LRU_WIDTH), tok(LRU_WIDTH), tok(LRU_WIDTH),
                  pl.BlockSpec((MLA_WIDTH, d), lambda i: (0, 0)), pl.BlockSpec((LRU_WIDTH, d), lambda i: (1, 0))],
        out_specs=tok(d),
        out_shape=jax.ShapeDtypeStruct((t_all, d), F32),
        compiler_params=_cparams(("arbitrary",)),
    )(x0, mod0, attn, hf, hb, ug, w_o, w_o)

    def peer_call(x, mod, g_ffn, w_q, sub_keys, u, v, final_norm):
        n_exp = u.shape[0]
        assert n_exp == N_KEYS * N_KEYS and n_exp % PEER_EBLK == 0
        wqt = w_q.T.astype(BF16)
        keys = sub_keys.astype(BF16).reshape(2 * PEER_HEADS, N_KEYS, PEER_DKEY // 2)
        ub = u.astype(BF16)
        vt = v.reshape(n_exp // PEER_EBLK, PEER_EBLK, d).transpose(0, 2, 1).astype(BF16)
        tpp = PEER_TM // TM

        nblk = n_exp // PEER_EBLK

        def cond_row_p(i):
            return cond_row(i * tpp)

        big = lambda dt: pltpu.VMEM((PEER_HEADS, N_KEYS, PEER_TM), dt)
        return pl.pallas_call(
            functools.partial(_peer_kernel, final_norm=final_norm),
            grid=(t_all // PEER_TM, nblk + 1),
            in_specs=[pl.BlockSpec((PEER_TM, d), lambda i, k: (i, 0), pipeline_mode=pl.Buffered(1)),
                      pl.BlockSpec((1, 6, d), lambda i, k: (cond_row_p(i), 0, 0)),
                      pl.BlockSpec((1, d), lambda i, k: (0, 0)),
                      pl.BlockSpec(wqt.shape, lambda i, k: (0, 0)),
                      pl.BlockSpec(keys.shape, lambda i, k: (0, 0, 0)),
                      pl.BlockSpec((PEER_EBLK, d), lambda i, k: (jnp.minimum(k, nblk - 1), 0)),
                      pl.BlockSpec((1, d, PEER_EBLK), lambda i, k: (jnp.maximum(k - 1, 0), 0, 0)),
                      pl.BlockSpec((1, d), lambda i, k: (0, 0))],
            out_specs=pl.BlockSpec((PEER_TM, d), lambda i, k: (i, 0), pipeline_mode=pl.Buffered(1)),
            out_shape=jax.ShapeDtypeStruct((t_all, d), F32),
            scratch_shapes=[pltpu.VMEM((d, PEER_TM), BF16),
                            pltpu.VMEM((2 * PEER_HEADS * N_KEYS, PEER_TM), BF16),
                            pltpu.VMEM((2 * PEER_HEADS * N_KEYS, PEER_TM), F32),
                            pltpu.VMEM((2, PEER_TOPK, PEER_HEADS, PEER_TM), F32),
                            pltpu.VMEM((PEER_TOPK + 1, PEER_HEADS, PEER_TM), F32),
                            pltpu.VMEM((2, N_KEYS, PEER_TM), F32),
                            big(F32), big(F32),
                            pltpu.VMEM((2, PEER_EBLK, PEER_TM), BF16)],
            compiler_params=_cparams(("arbitrary", "arbitrary")),
        )(x, mod, row(g_ffn), wqt, keys, ub, vt, row(g_final))

    x2 = peer_call(x1, mod0, g_ffn_l0, peer_wq_l0, peer_keys_l0, peer_u_l0, peer_v_l0, False)

    rh = TM // HALO
    lasth = t_all // HALO - 1
    x3 = pl.pallas_call(
        functools.partial(_pool_kernel, tiles_per_seq=tps, n_prompt_tiles=npt),
        grid=(ntile,),
        in_specs=[pl.BlockSpec((HALO, d), lambda i: (jnp.maximum(i * rh - 1, 0), 0)),
                  tok(d),
                  pl.BlockSpec((HALO, d), lambda i: (jnp.minimum((i + 1) * rh, lasth), 0)),
                  mod_spec, full(row(g_mix_l1)),
                  pl.BlockSpec(w_pool_l1.shape, lambda i: (0, 0, 0)), full(row(s_pool_l1))],
        out_specs=tok(d),
        out_shape=jax.ShapeDtypeStruct((t_all, d), F32),
        compiler_params=_cparams(("arbitrary",)),
    )(x2, x2, x2, mod1, row(g_mix_l1), w_pool_l1.astype(BF16), row(s_pool_l1))

    y = peer_call(x3, mod1, g_ffn_l1, peer_wq_l1, peer_keys_l1, peer_u_l1, peer_v_l1, True)

    y_prompt = y[:t_p].reshape(nb_p, s_p, d)
    y_sample = y[t_p:].reshape(nb_s, s_s, d)
    new_ckv = ckv[:t_p].reshape(nb_p, s_p, KV_LORA)
    new_krope = kr[:t_p].reshape(nb_p, s_p, QK_ROPE)
    return (y_prompt, y_sample, new_ckv, new_krope, new_lru)
```

```python
import functools

import numpy as np
import jax
import jax.numpy as jnp
from jax import lax
from jax.experimental import pallas as pl
from jax.experimental.pallas import tpu as pltpu

F32 = jnp.float32
BF16 = jnp.bfloat16

D_MODEL = 1024
EPS = 1e-6
GRID_W = 64
MLA_HEADS = 4
Q_LORA = 384
KV_LORA = 256
QK_NOPE = 128
QK_ROPE = 64
V_HEAD = 128
MLA_WIDTH = MLA_HEADS * V_HEAD
ROPE_BASE = 10000.0
LRU_WIDTH = 512
LRU_BLOCKS = 4
LRU_BLOCK = LRU_WIDTH // LRU_BLOCKS
CONV_W = 4
CONV_LEFT = 2
LRU_C = 8.0
POOL_WINDOWS = (2, 4, 8, 16)
POOL_GROUP = D_MODEL // len(POOL_WINDOWS)
PEER_HEADS = 8
N_KEYS = 128
PEER_DKEY = 256
PEER_TOPK = 16

SUBLANES = 8
LANES = 128
VMEM_BYTES = 64 * 1024 * 1024
VMEM_LIMIT = VMEM_BYTES - 4 * 1024 * 1024

TM = 256
QK_PAD = 256
HALO = 16
PEER_TM = 1024
PEER_EBLK = 512
PEER_SUB = 256
BF16_ROWS = 16
IN_EXT = 2048


def _rms(x, g):
    return x * lax.rsqrt(jnp.mean(x * x, axis=-1, keepdims=True) + EPS) * g


def _cparams(sem):
    return pltpu.CompilerParams(dimension_semantics=sem, vmem_limit_bytes=VMEM_LIMIT)


def _ada_kernel(c_ref, w_ref, b_ref, o_ref):
    c = c_ref[...]
    o_ref[...] = jnp.dot(c * jax.nn.sigmoid(c), w_ref[...], preferred_element_type=F32) + b_ref[...]


def _ada(cpad, w_mod, b_mod):
    n = w_mod.shape[1]
    bn = 768
    return pl.pallas_call(
        _ada_kernel,
        grid=(n // bn,),
        in_specs=[pl.BlockSpec(cpad.shape, lambda j: (0, 0)),
                  pl.BlockSpec((D_MODEL, bn), lambda j: (0, j)),
                  pl.BlockSpec((1, bn), lambda j: (0, j))],
        out_specs=pl.BlockSpec((cpad.shape[0], bn), lambda j: (0, j)),
        out_shape=jax.ShapeDtypeStruct((cpad.shape[0], n), F32),
        compiler_params=_cparams(("arbitrary",)),
    )(cpad, w_mod, b_mod.reshape(1, n))


def _inproj_kernel(x_ref, mod_ref, g_ref, win_ref, gq_ref, wuq_ref, gkv_ref, ck_ref, sk_ref,
                   q_ref, ckv_ref, kr_ref, krr_ref, ux_ref, ug_ref):
    mod = mod_ref[0]
    h = _rms(x_ref[...], g_ref[...]) * (1.0 + mod[1:2]) + mod[0:1]
    y = jnp.dot(h.astype(BF16), win_ref[...], preferred_element_type=F32)
    cq = y[:, 0:Q_LORA]
    ckv = y[:, Q_LORA:Q_LORA + KV_LORA]
    o = Q_LORA + KV_LORA
    kr = y[:, o:o + QK_ROPE]
    krp = y[:, o + 128:o + 128 + QK_ROPE]
    krs = y[:, o + 256:o + 256 + QK_ROPE]
    ux_ref[...] = y[:, o + 384:o + 384 + LRU_WIDTH]
    ug_ref[...] = y[:, o + 384 + LRU_WIDTH:o + 384 + 2 * LRU_WIDTH]
    ckv_ref[...] = _rms(ckv, gkv_ref[...])
    kr_ref[...] = kr
    ck = ck_ref[...]
    sk = sk_ref[...]
    krr_ref[...] = (krp * ck + krs * sk).astype(BF16)
    q = jnp.dot(_rms(cq, gq_ref[...]).astype(BF16), wuq_ref[...], preferred_element_type=F32)
    nw = MLA_HEADS * QK_NOPE
    rw = MLA_HEADS * QK_ROPE
    for hd in range(MLA_HEADS):
        qp = q[:, nw + hd * QK_ROPE:nw + (hd + 1) * QK_ROPE]
        qs = q[:, nw + rw + hd * QK_ROPE:nw + rw + (hd + 1) * QK_ROPE]
        q_ref[hd, :, 0:QK_NOPE] = q[:, hd * QK_NOPE:(hd + 1) * QK_NOPE].astype(BF16)
        q_ref[hd, :, QK_NOPE:QK_NOPE + QK_ROPE] = (qp * ck + qs * sk).astype(BF16)
        q_ref[hd, :, QK_NOPE + QK_ROPE:QK_PAD] = jnp.zeros((TM, QK_PAD - QK_NOPE - QK_ROPE), BF16)


def _attn_kernel(*refs, has_cache, s_new, n_cache):
    if has_cache:
        q_ref, ckv_ref, krr_ref, cckv_ref, ckr_ref, wukv_ref, o_ref, kcat_ref, vv_ref = refs
    else:
        q_ref, ckv_ref, krr_ref, wukv_ref, o_ref, kcat_ref, vv_ref = refs
    sk = n_cache + s_new
    kw = MLA_HEADS * QK_NOPE
    zpad = jnp.zeros((TM, QK_PAD - QK_NOPE - QK_ROPE), BF16)

    def put_keys(row0, ckv_rows, kr_rows):
        kv = jnp.dot(ckv_rows.astype(BF16), wukv_ref[...], preferred_element_type=F32)
        rows = pl.ds(row0, TM)
        for hd in range(MLA_HEADS):
            kcat_ref[hd, rows, 0:QK_NOPE] = kv[:, hd * QK_NOPE:(hd + 1) * QK_NOPE].astype(BF16)
            kcat_ref[hd, rows, QK_NOPE:QK_NOPE + QK_ROPE] = kr_rows.astype(BF16)
            kcat_ref[hd, rows, QK_NOPE + QK_ROPE:QK_PAD] = zpad
        vv_ref[rows, :] = kv[:, kw:].astype(BF16)

    @pl.when(pl.program_id(1) == 0)
    def _():
        if has_cache:
            for c in range(n_cache // TM):
                put_keys(c * TM, cckv_ref[0, c * TM:(c + 1) * TM, :], ckr_ref[0, c * TM:(c + 1) * TM, :])

        def body(c, carry):
            r0 = pl.multiple_of(c * TM, TM)
            put_keys(n_cache + r0, ckv_ref[pl.ds(r0, TM), :], krr_ref[pl.ds(r0, TM), :])
            return carry
        lax.fori_loop(0, s_new // TM, body, 0)

    scale = (QK_NOPE + QK_ROPE) ** -0.5
    for hd in range(MLA_HEADS):
        s = lax.dot_general(q_ref[hd], kcat_ref[hd], (((1,), (1,)), ((), ())),
                            preferred_element_type=F32) * scale
        m = jnp.max(s, axis=-1, keepdims=True)
        e = jnp.exp(s - m)
        l = jnp.sum(e, axis=-1, keepdims=True)
        o = jnp.dot(e.astype(BF16), vv_ref[:, hd * V_HEAD:(hd + 1) * V_HEAD], preferred_element_type=F32)
        o_ref[:, hd * V_HEAD:(hd + 1) * V_HEAD] = (o / l).astype(BF16)
    del sk


def _lru_dir(xp_ref, xc_ref, xn_ref, valid_prev, valid_next, d, reverse, cw_ref, cb_ref,
             wr_ref, br_ref, wi_ref, bi_ref, lam_ref, carry):
    xp = jnp.where(valid_prev, xp_ref[...], 0.0)
    xn = jnp.where(valid_next, xn_ref[...], 0.0)
    xx = jnp.concatenate([xp, xc_ref[...], xn], axis=0)
    n = TM + 2 * SUBLANES
    xc = cb_ref[...]
    for k in range(CONV_W):
        sh = (CONV_LEFT - k) % n
        xs = xx if sh == 0 else pltpu.roll(xx, sh, 0)
        xc = xc + xs[SUBLANES:SUBLANES + TM] * cw_ref[k:k + 1, :]
    rs, is_ = [], []
    for b in range(LRU_BLOCKS):
        xb = xc[:, b * LRU_BLOCK:(b + 1) * LRU_BLOCK]
        rs.append(jnp.dot(xb, wr_ref[d, b], preferred_element_type=F32))
        is_.append(jnp.dot(xb, wi_ref[d, b], preferred_element_type=F32))
    r = jax.nn.sigmoid(jnp.concatenate(rs, axis=-1) + br_ref[d:d + 1, :])
    i = jax.nn.sigmoid(jnp.concatenate(is_, axis=-1) + bi_ref[d:d + 1, :])
    nl = -lam_ref[d:d + 1, :]
    softplus = jnp.maximum(nl, 0.0) + jnp.log1p(jnp.exp(-jnp.abs(nl)))
    log_a = -LRU_C * r * softplus
    a = jnp.exp(log_a)
    bx = jnp.sqrt(jnp.tanh(-log_a) * (a * a + 1.0)) * (i * xc)
    t = lax.broadcasted_iota(jnp.int32, (TM, 1), 0)
    step = 1
    while step < TM:
        if reverse:
            keep = t < TM - step
            sh = TM - step
        else:
            keep = t >= step
            sh = step
        a_s = jnp.where(keep, pltpu.roll(a, sh, 0), 1.0)
        b_s = jnp.where(keep, pltpu.roll(bx, sh, 0), 0.0)
        bx = a * b_s + bx
        a = a * a_s
        step *= 2
    return a * carry + bx


def _lru_kernel(fxp, fxc, fxn, bxp, bxc, bxn, cw_ref, cb_ref, wr_ref, br_ref, wi_ref, bi_ref, lam_ref,
                h0_ref, hf_ref, hb_ref, st_ref, cf_ref, cbk_ref, *, nc):
    c = pl.program_id(1)

    @pl.when(c == 0)
    def _():
        cf_ref[...] = h0_ref[0, 0:1, :]
        cbk_ref[...] = h0_ref[0, 1:2, :]

    params = (cw_ref, cb_ref, wr_ref, br_ref, wi_ref, bi_ref, lam_ref)
    hf = _lru_dir(fxp, fxc, fxn, c > 0, c < nc - 1, 0, False, *params, cf_ref[...])
    hf_ref[...] = hf
    cf_ref[...] = hf[TM - 1:TM, :]
    hb = _lru_dir(bxp, bxc, bxn, c < nc - 1, c > 0, 1, True, *params, cbk_ref[...])
    hb_ref[...] = hb
    cbk_ref[...] = hb[0:1, :]
    st_ref[0, 0:1, :] = hf[TM - 1:TM, :]
    st_ref[0, 1:2, :] = hb[0:1, :]


def _oproj_kernel(x_ref, mod_ref, at_ref, hf_ref, hb_ref, ug_ref, woa_ref, wor_ref, o_ref):
    mod = mod_ref[0]
    rec = ((hf_ref[...] + hb_ref[...]) * jax.nn.gelu(ug_ref[...])).astype(BF16)
    out = (jnp.dot(at_ref[...], woa_ref[...], preferred_element_type=F32)
           + jnp.dot(rec, wor_ref[...], preferred_element_type=F32))
    o_ref[...] = x_ref[...] + mod[2:3] * out


def _pool_kernel(xp_ref, xc_ref, xn_ref, mod_ref, g_ref, wp_ref, sp_ref, o_ref, *, tiles_per_seq, n_prompt_tiles):
    i = pl.program_id(0)
    j = jnp.where(i < n_prompt_tiles, 0, (i - n_prompt_tiles) % tiles_per_seq)
    ntile = jnp.where(i < n_prompt_tiles, 1, tiles_per_seq)
    mod = mod_ref[0]
    g = g_ref[...]

    def hmod(x):
        return _rms(x, g) * (1.0 + mod[1:2]) + mod[0:1]

    x = xc_ref[...]
    hc = hmod(x)
    hp = jnp.where(j > 0, hmod(xp_ref[...]), 0.0)
    hn = jnp.where(j < ntile - 1, hmod(xn_ref[...]), 0.0)
    hh = jnp.concatenate([hp, hc, hn], axis=0)
    n = TM + 2 * HALO
    seq_len = ntile * TM
    t = j * TM + lax.broadcasted_iota(jnp.int32, (TM, 1), 0)
    ys = []
    for gi, w in enumerate(POOL_WINDOWS):
        cols = slice(gi * POOL_GROUP, (gi + 1) * POOL_GROUP)
        p = hh[:, cols]
        p = p + pltpu.roll(p, 1, 0)
        half = 1
        while 2 * half < w:
            p = pltpu.roll(p, half, 0) + pltpu.roll(p, n - half, 0)
            half *= 2
        lo = jnp.maximum(t - w // 2, 0)
        hi = jnp.minimum(t + (w - w // 2), seq_len)
        mean = p[HALO:HALO + TM] / (hi - lo).astype(F32)
        dg = (mean - hc[:, cols]).astype(BF16)
        ys.append(jnp.dot(dg, wp_ref[gi], preferred_element_type=F32))
    y = jnp.concatenate(ys, axis=-1) * sp_ref[...]
    o_ref[...] = x + mod[2:3] * y


def _sort_pairs(n):
    pairs = []

    def merge(lo, cnt, r):
        step = r * 2
        if step < cnt:
            merge(lo, cnt, step)
            merge(lo + r, cnt, step)
            for i in range(lo + r, lo + cnt - r, step):
                pairs.append((i, i + r))
        else:
            pairs.append((lo, lo + r))

    def sort(lo, cnt):
        if cnt > 1:
            m = cnt // 2
            sort(lo, m)
            sort(lo + m, m)
            merge(lo, cnt, 1)

    sort(0, n)
    return pairs


_SORT16 = _sort_pairs(PEER_TOPK)
_HYPER = [(a, b) for a in range(PEER_TOPK) for b in range(PEER_TOPK) if (a + 1) * (b + 1) <= PEER_TOPK]


def _top16_sorted(s):
    k = PEER_TOPK
    x = [s[SUBLANES * r:SUBLANES * (r + 1), :] for r in range(N_KEYS // SUBLANES)]
    for (i, j) in _SORT16:
        hi = jnp.maximum(x[i], x[j])
        lo = jnp.minimum(x[i], x[j])
        x[i], x[j] = hi, lo
    for shift in (4, 2, 1):
        y = [jnp.maximum(x[r], pltpu.roll(x[k - 1 - r], shift, 0)) for r in range(k)]
        stride = k // 2
        while stride >= 1:
            for i in range(k):
                if i & stride == 0:
                    hi = jnp.maximum(y[i], y[i + stride])
                    lo = jnp.minimum(y[i], y[i + stride])
                    y[i], y[i + stride] = hi, lo
            stride //= 2
        x = y
    return x


def _pair_counts(sv1, sv2):
    one = jnp.ones_like(sv1[0])
    zero = jnp.zeros_like(sv1[0])
    cand = [sv1[a] + sv2[b] for (a, b) in _HYPER]

    def ordered(i, j):
        (ai, bi), (aj, bj) = _HYPER[i], _HYPER[j]
        return (aj <= ai and bj <= bi) or (ai <= aj and bi <= bj)

    nh = len(_HYPER)
    cnt = [float((a + 1) * (b + 1) - 1 + sum(1 for j in range(i + 1, nh) if not ordered(i, j))) * one
           for i, (a, b) in enumerate(_HYPER)]
    for i in range(nh):
        for jx in range(i):
            if ordered(i, jx):
                continue
            ge = jnp.where(cand[jx] >= cand[i], one, zero)
            cnt[i] = cnt[i] + ge
            cnt[jx] = cnt[jx] - ge
    e1 = [jnp.exp(sv1[a] - sv1[0]) for a in range(PEER_TOPK)]
    e2 = [jnp.exp(sv2[b] - sv2[0]) for b in range(PEER_TOPK)]
    n = [zero for _ in range(PEER_TOPK)]
    z = zero
    for i, (a, b) in enumerate(_HYPER):
        sel = jnp.where(cnt[i] < float(PEER_TOPK), one, zero)
        n[a] = n[a] + sel
        z = z + sel * (e1[a] * e2[b])
    return n, 1.0 / z


def _head_tables(hd, lanes, sc_ref, sv_ref, nz_ref, ex_ref, cnt1_ref, e1n_ref, rank2_ref, e2_ref, exact):
    k = PEER_TOPK

    def bc(ref, *idx):
        return jnp.broadcast_to(ref[idx + (slice(hd, hd + 1), lanes)], (SUBLANES, LANES))

    sv1b = [bc(sv_ref, 0, a) for a in range(k)]
    nb = [bc(nz_ref, a) for a in range(k)]
    izb = bc(nz_ref, k)
    c1 = jnp.zeros((SUBLANES, LANES), F32)
    for r in range(N_KEYS // SUBLANES):
        rows = slice(r * SUBLANES, (r + 1) * SUBLANES)
        s1 = sc_ref[pl.ds(2 * hd * N_KEYS + r * SUBLANES, SUBLANES), lanes]
        cnt1 = jnp.zeros_like(s1)
        if exact:
            rank1 = jnp.zeros_like(s1)
            for a in range(k):
                rank1 = jnp.where(sv1b[a] > s1, float(a + 1), rank1)
            rank1 = rank1 + ex_ref[0, rows, lanes]
            for a in range(k):
                cnt1 = jnp.where(rank1 == float(a), nb[a], cnt1)
        else:
            for a in range(k):
                cnt1 = jnp.where(s1 == sv1b[a], nb[a], cnt1)
            c1 = c1 + jnp.where(s1 >= sv1b[k - 1], 1.0, 0.0)
        cnt1_ref[hd, rows, lanes] = cnt1
        e1n_ref[hd, rows, lanes] = jnp.exp(s1 - sv1b[0]) * izb
    sv2b = [bc(sv_ref, 1, a) for a in range(k)]
    c2 = jnp.zeros((SUBLANES, LANES), F32)
    for r in range(N_KEYS // BF16_ROWS):
        rk, e2 = [], []
        for q in range(BF16_ROWS // SUBLANES):
            r0 = r * BF16_ROWS + q * SUBLANES
            rows = slice(r0, r0 + SUBLANES)
            s2 = sc_ref[pl.ds((2 * hd + 1) * N_KEYS + r0, SUBLANES), lanes]
            rank2 = jnp.zeros_like(s2)
            for a in range(k):
                rank2 = jnp.where(sv2b[a] > s2, float(a + 1), rank2)
            if exact:
                rank2 = jnp.minimum(rank2 + ex_ref[1, rows, lanes], float(k))
            else:
                c2 = c2 + jnp.where(rank2 < float(k), 1.0, 0.0)
            rk.append(rank2)
            e2.append(jnp.exp(s2 - sv2b[0]))
        rows16 = pl.ds(hd * N_KEYS + r * BF16_ROWS, BF16_ROWS)
        rank2_ref[rows16, lanes] = jnp.concatenate(rk, axis=0).astype(BF16)
        e2_ref[rows16, lanes] = jnp.concatenate(e2, axis=0).astype(BF16)
    if exact:
        return None
    ex_ref[0, 0:SUBLANES, lanes] = c1
    ex_ref[1, 0:SUBLANES, lanes] = c2
    b = jnp.zeros((SUBLANES, LANES), F32)
    for svb in (sv1b, sv2b):
        for a in range(k - 1):
            b = b + jnp.where(svb[a] == svb[a + 1], 1.0, 0.0)
    return b


def _tie_offsets(hd, sc_ref, ex_ref):
    tm = sc_ref.shape[-1]
    nidx = lax.broadcasted_iota(jnp.int32, (N_KEYS, tm), 0)
    for p in range(2):
        s = sc_ref[pl.ds((2 * hd + p) * N_KEYS, N_KEYS), :]

        def body(m, e, s=s, p=p):
            row = sc_ref[pl.ds((2 * hd + p) * N_KEYS + m, 1), :]
            return e + jnp.where((s == row) & (nidx > m), 1.0, 0.0)

        ex_ref[p] = lax.fori_loop(0, N_KEYS, body, jnp.zeros((N_KEYS, tm), F32))


def _peer_prologue(x_ref, mod_ref, g_ref, wqt_ref, keys_ref, h2t_ref, qb_ref, sc_ref, sv_ref, nz_ref,
                   ex_ref, cnt1_ref, e1n_ref):
    tm = PEER_TM
    half = PEER_DKEY // 2
    ngroup = tm // LANES
    mod = mod_ref[0]
    h2 = _rms(x_ref[...], g_ref[...]) * (1.0 + mod[4:5]) + mod[3:4]
    h2t_ref[...] = h2.T.astype(BF16)

    def lane_group(g):
        return pl.ds(pl.multiple_of(g * LANES, LANES), LANES)

    qb_ref[...] = jnp.dot(wqt_ref[...], h2t_ref[...], preferred_element_type=F32).astype(BF16)
    for i in range(2 * PEER_HEADS):
        sc_ref[i * N_KEYS:(i + 1) * N_KEYS, :] = jnp.dot(keys_ref[i], qb_ref[i * half:(i + 1) * half, :],
                                                         preferred_element_type=F32)

        def group(g, c, i=i):
            lanes = lane_group(g)
            top = _top16_sorted(sc_ref[i * N_KEYS:(i + 1) * N_KEYS, lanes])
            for a in range(PEER_TOPK):
                sv_ref[i % 2, a, i // 2:i // 2 + 1, lanes] = top[a][0:1, :]
            return c
        lax.fori_loop(0, ngroup, group, 0)

    def counts(g, carry):
        lanes = lane_group(g)
        sv1 = [sv_ref[0, a, :, lanes] for a in range(PEER_TOPK)]
        sv2 = [sv_ref[1, a, :, lanes] for a in range(PEER_TOPK)]
        n, inv_z = _pair_counts(sv1, sv2)
        for a in range(PEER_TOPK):
            nz_ref[a, :, lanes] = n[a]
        nz_ref[PEER_TOPK, :, lanes] = inv_z
        return carry
    lax.fori_loop(0, ngroup, counts, 0)

    nrow = PEER_HEADS * N_KEYS
    tabs = (sc_ref, sv_ref, nz_ref, ex_ref, cnt1_ref, e1n_ref, qb_ref.at[0:nrow], qb_ref.at[nrow:2 * nrow])

    for hd in range(PEER_HEADS):
        def fast(g, bad, hd=hd):
            return bad + _head_tables(hd, lane_group(g), *tabs, exact=False)
        bad = lax.fori_loop(0, ngroup, fast, jnp.zeros((SUBLANES, LANES), F32))
        totals = jnp.sum(ex_ref[:, 0:SUBLANES, :], axis=1)
        ties = jnp.max(bad) + jnp.max(jnp.abs(totals - float(PEER_TOPK)))

        @pl.when(ties > 0.0)
        def _(hd=hd):
            _tie_offsets(hd, sc_ref, ex_ref)

            def slow(g, c):
                _head_tables(hd, lane_group(g), *tabs, exact=True)
                return c
            lax.fori_loop(0, ngroup, slow, 0)
    sc_ref[0:D_MODEL, :] = jnp.zeros((D_MODEL, tm), F32)


def _peer_kernel(x_ref, mod_ref, g_ref, wqt_ref, keys_ref, u0_ref, u_ref, vt_ref, gfin_ref, o_ref,
                 h2t_ref, qb_ref, sc_ref, sv_ref, nz_ref, ex_ref, cnt1_ref, e1n_ref, raw_ref, y_ref,
                 *, final_norm):
    k = pl.program_id(1)
    tm = PEER_TM
    nrow = PEER_HEADS * N_KEYS
    acc_ref = sc_ref.at[0:D_MODEL]
    rank2_ref = qb_ref.at[0:nrow]
    e2_ref = qb_ref.at[nrow:2 * nrow]

    nstep = pl.num_programs(1) - 1
    zero = jnp.zeros((BF16_ROWS, LANES), BF16)
    npair = PEER_EBLK // (2 * N_KEYS)
    cur = k % 2
    oth = (k + 1) % 2

    def up_next():
        raw_ref[oth] = jnp.dot(u_ref[...], h2t_ref[...], preferred_element_type=F32)

    def down_prev():
        acc_ref[...] += jnp.dot(vt_ref[0], y_ref[oth], preferred_element_type=F32)

    def gate_block():
        for q in range(npair):
            i1 = (k * npair + q) * 2
            crow = [[cnt1_ref[hd, pl.ds(i1 + j, 1), :] for hd in range(PEER_HEADS)] for j in range(2)]
            erow = [[e1n_ref[hd, pl.ds(i1 + j, 1), :] for hd in range(PEER_HEADS)] for j in range(2)]
            for g in range(tm // LANES):
                lanes = slice(g * LANES, (g + 1) * LANES)
                cbs = [[jnp.broadcast_to(crow[j][hd][:, lanes], (BF16_ROWS, LANES)).astype(BF16)
                        for hd in range(PEER_HEADS)] for j in range(2)]
                ebs = [[jnp.broadcast_to(erow[j][hd][:, lanes], (BF16_ROWS, LANES)).astype(BF16)
                        for hd in range(PEER_HEADS)] for j in range(2)]
                for r in range(N_KEYS // BF16_ROWS):
                    w = [None, None]
                    for hd in range(PEER_HEADS):
                        hrows = pl.ds(hd * N_KEYS + r * BF16_ROWS, BF16_ROWS)
                        rk = rank2_ref[hrows, lanes]
                        ev = e2_ref[hrows, lanes]
                        for j in range(2):
                            t = jnp.where(rk < cbs[j][hd], ev, zero) * ebs[j][hd]
                            w[j] = t if w[j] is None else w[j] + t
                    for j in range(2):
                        rows = pl.ds((2 * q + j) * N_KEYS + r * BF16_ROWS, BF16_ROWS)
                        act = jax.nn.gelu(raw_ref[cur, rows, lanes].astype(BF16))
                        y_ref[cur, rows, lanes] = act * w[j]

    @pl.when(k == 0)
    def _():
        _peer_prologue(x_ref, mod_ref, g_ref, wqt_ref, keys_ref, h2t_ref, qb_ref, sc_ref, sv_ref, nz_ref,
                       ex_ref, cnt1_ref, e1n_ref)
        raw_ref[0] = jnp.dot(u0_ref[...], h2t_ref[...], preferred_element_type=F32)
        up_next()
        gate_block()

    @pl.when((k > 0) & (k < nstep))
    def _():
        up_next()
        down_prev()
        gate_block()

    @pl.when(k == nstep)
    def _():
        down_prev()
        out = x_ref[...] + mod_ref[0][5:6] * acc_ref[...].T
        if final_norm:
            out = _rms(out, gfin_ref[...])
        o_ref[...] = out


def _rope_tables(s_prompt_tile, s_sample):
    n_rows = s_sample // GRID_W
    rows = jnp.repeat(jnp.arange(n_rows, dtype=F32), GRID_W)
    cols = jnp.tile(jnp.arange(GRID_W, dtype=F32), n_rows)
    axis_dim = QK_ROPE // 2
    inv_freq = ROPE_BASE ** (-jnp.arange(0, axis_dim, 2, dtype=F32) / axis_dim)
    ang = jnp.concatenate([rows[:, None] * inv_freq, cols[:, None] * inv_freq], axis=-1)
    cos, sin = jnp.cos(ang), jnp.sin(ang)
    ck = jnp.concatenate([cos, cos], axis=-1)
    sk = jnp.concatenate([-sin, sin], axis=-1)
    ident_c = jnp.ones((s_prompt_tile, QK_ROPE), F32)
    ident_s = jnp.zeros((s_prompt_tile, QK_ROPE), F32)
    return jnp.concatenate([ident_c, ck], axis=0), jnp.concatenate([ident_s, sk], axis=0)


def kernel(x_prompt, x_sample, cache_ckv_l0, cache_krope_l0, state_lru_l0, c, c_ctx, w_mod_l0, b_mod_l0, w_mod_l1, b_mod_l1, g_mix_l0, g_ffn_l0, g_mix_l1, g_ffn_l1, w_in_l0, g_q_l0, w_uq_l0, g_kv_l0, w_ukv_l0, conv_w_l0, conv_b_l0, w_rg_l0, b_rg_l0, w_ig_l0, b_ig_l0, lam_l0, w_o_l0, w_pool_l1, s_pool_l1, peer_wq_l0, peer_keys_l0, peer_u_l0, peer_v_l0, peer_wq_l1, peer_keys_l1, peer_u_l1, peer_v_l1, g_final):
    nb_p, s_p, d = x_prompt.shape
    nb_s, s_s, _ = x_sample.shape
    n_cache = cache_ckv_l0.shape[1]
    assert d == D_MODEL and s_p == TM and s_s % TM == 0 and n_cache % TM == 0
    t_p = nb_p * s_p
    t_s = nb_s * s_s
    t_all = t_p + t_s
    npt = t_p // TM
    tps = s_s // TM
    ntile = t_all // TM
    assert t_all % PEER_TM == 0 and t_p % PEER_TM == 0 and s_s % PEER_TM == 0

    x0 = jnp.concatenate([x_prompt.reshape(t_p, d), x_sample.reshape(t_s, d)], axis=0)

    ncond = 1 + nb_s
    cpad = jnp.zeros((2 * SUBLANES, d), F32).at[0].set(c_ctx).at[1:ncond].set(c)
    mod0 = _ada(cpad, w_mod_l0, b_mod_l0).reshape(2 * SUBLANES, 6, d)
    mod1 = _ada(cpad, w_mod_l1, b_mod_l1).reshape(2 * SUBLANES, 6, d)

    def cond_row(i):
        return jnp.where(i < npt, 0, 1 + (i - npt) // tps)

    mod_spec = pl.BlockSpec((1, 6, d), lambda i: (cond_row(i), 0, 0))
    row = lambda a: a.reshape(1, -1)

    perm = np.concatenate([np.arange(0, QK_ROPE, 2), np.arange(1, QK_ROPE, 2)])
    perm_sw = np.concatenate([np.arange(1, QK_ROPE, 2), np.arange(0, QK_ROPE, 2)])
    o1 = Q_LORA + KV_LORA
    w_kr = w_in_l0[:, o1:o1 + QK_ROPE]
    z64 = jnp.zeros((d, 128 - QK_ROPE), F32)
    w_in_ext = jnp.concatenate(
        [w_in_l0[:, :o1], w_kr, z64, w_kr[:, perm], z64, w_kr[:, perm_sw], z64, w_in_l0[:, o1 + QK_ROPE:]],
        axis=1).astype(BF16)
    assert w_in_ext.shape[1] == IN_EXT
    wq3 = w_uq_l0.reshape(Q_LORA, MLA_HEADS, QK_NOPE + QK_ROPE)
    w_uq_ext = jnp.concatenate(
        [wq3[:, :, :QK_NOPE].reshape(Q_LORA, -1),
         wq3[:, :, QK_NOPE:][:, :, perm].reshape(Q_LORA, -1),
         wq3[:, :, QK_NOPE:][:, :, perm_sw].reshape(Q_LORA, -1)], axis=1).astype(BF16)
    ck_tab, sk_tab = _rope_tables(TM, s_s)

    def rope_blk(i):
        return jnp.where(i < npt, 0, 1 + (i - npt) % tps)

    tok = lambda w: pl.BlockSpec((TM, w), lambda i: (i, 0))
    full = lambda a: pl.BlockSpec(a.shape, lambda *_: (0,) * a.ndim)
    q, ckv, kr, krr, ux, ug = pl.pallas_call(
        _inproj_kernel,
        grid=(ntile,),
        in_specs=[tok(d), mod_spec, full(row(g_mix_l0)), full(w_in_ext), full(row(g_q_l0)), full(w_uq_ext),
                  full(row(g_kv_l0)),
                  pl.BlockSpec((TM, QK_ROPE), lambda i: (rope_blk(i), 0)),
                  pl.BlockSpec((TM, QK_ROPE), lambda i: (rope_blk(i), 0))],
        out_specs=[pl.BlockSpec((MLA_HEADS, TM, QK_PAD), lambda i: (0, i, 0)),
                   tok(KV_LORA), tok(QK_ROPE), tok(QK_ROPE), tok(LRU_WIDTH), tok(LRU_WIDTH)],
        out_shape=[jax.ShapeDtypeStruct((MLA_HEADS, t_all, QK_PAD), BF16),
                   jax.ShapeDtypeStruct((t_all, KV_LORA), F32),
                   jax.ShapeDtypeStruct((t_all, QK_ROPE), F32),
                   jax.ShapeDtypeStruct((t_all, QK_ROPE), BF16),
                   jax.ShapeDtypeStruct((t_all, LRU_WIDTH), F32),
                   jax.ShapeDtypeStruct((t_all, LRU_WIDTH), F32)],
        compiler_params=_cparams(("arbitrary",)),
    )(x0, mod0, row(g_mix_l0), w_in_ext, row(g_q_l0), w_uq_ext, row(g_kv_l0), ck_tab, sk_tab)

    wkv3 = w_ukv_l0.reshape(KV_LORA, MLA_HEADS, QK_NOPE + V_HEAD)
    w_ukv_ext = jnp.concatenate([wkv3[:, :, :QK_NOPE].reshape(KV_LORA, -1),
                                 wkv3[:, :, QK_NOPE:].reshape(KV_LORA, -1)], axis=1).astype(BF16)

    def attn_call(nb, s_new, tile0, has_cache):
        nq = s_new // TM
        blk0 = tile0 * TM // s_new
        n_c = n_cache if has_cache else 0
        in_specs = [pl.BlockSpec((MLA_HEADS, TM, QK_PAD), lambda b, qi: (0, tile0 + b * nq + qi, 0)),
                    pl.BlockSpec((s_new, KV_LORA), lambda b, qi: (blk0 + b, 0)),
                    pl.BlockSpec((s_new, QK_ROPE), lambda b, qi: (blk0 + b, 0))]
        args = [q, ckv, krr]
        if has_cache:
            in_specs += [pl.BlockSpec((1, n_cache, KV_LORA), lambda b, qi: (b, 0, 0)),
                         pl.BlockSpec((1, n_cache, QK_ROPE), lambda b, qi: (b, 0, 0))]
            args += [cache_ckv_l0, cache_krope_l0[:, :, perm]]
        in_specs.append(pl.BlockSpec(w_ukv_ext.shape, lambda b, qi: (0, 0)))
        args.append(w_ukv_ext)
        return pl.pallas_call(
            functools.partial(_attn_kernel, has_cache=has_cache, s_new=s_new, n_cache=n_c),
            grid=(nb, nq),
            in_specs=in_specs,
            out_specs=pl.BlockSpec((TM, MLA_WIDTH), lambda b, qi: (b * nq + qi, 0)),
            out_shape=jax.ShapeDtypeStruct((nb * s_new, MLA_WIDTH), BF16),
            scratch_shapes=[pltpu.VMEM((MLA_HEADS, n_c + s_new, QK_PAD), BF16),
                            pltpu.VMEM((n_c + s_new, MLA_WIDTH), BF16)],
            compiler_params=_cparams(("arbitrary", "arbitrary")),
        )(*args)

    attn = jnp.concatenate([attn_call(nb_p, s_p, 0, False), attn_call(nb_s, s_s, npt, True)], axis=0)

    def lru_call(nb, s_new, tile0, h0):
        nc = s_new // TM
        r8 = TM // SUBLANES
        last8 = t_all // SUBLANES - 1

        def cur(rev):
            return pl.BlockSpec((TM, LRU_WIDTH),
                                lambda b, cc: (tile0 + b * nc + (nc - 1 - cc if rev else cc), 0))

        def prev(rev):
            return pl.BlockSpec((SUBLANES, LRU_WIDTH), lambda b, cc: (
                jnp.maximum((tile0 + b * nc + (nc - 1 - cc if rev else cc)) * r8 - 1, 0), 0))

        def nxt(rev):
            return pl.BlockSpec((SUBLANES, LRU_WIDTH), lambda b, cc: (
                jnp.minimum((tile0 + b * nc + (nc - 1 - cc if rev else cc) + 1) * r8, last8), 0))

        small = [conv_w_l0, row(conv_b_l0), w_rg_l0, b_rg_l0, w_ig_l0, b_ig_l0, lam_l0]
        return pl.pallas_call(
            functools.partial(_lru_kernel, nc=nc),
            grid=(nb, nc),
            in_specs=[prev(False), cur(False), nxt(False), prev(True), cur(True), nxt(True)]
                     + [full(a) for a in small]
                     + [pl.BlockSpec((1, 2, LRU_WIDTH), lambda b, cc: (b, 0, 0))],
            out_specs=[pl.BlockSpec((TM, LRU_WIDTH), lambda b, cc: (b * nc + cc, 0)),
                       pl.BlockSpec((TM, LRU_WIDTH), lambda b, cc: (b * nc + nc - 1 - cc, 0)),
                       pl.BlockSpec((1, 2, LRU_WIDTH), lambda b, cc: (b, 0, 0))],
            out_shape=[jax.ShapeDtypeStruct((nb * s_new, LRU_WIDTH), F32),
                       jax.ShapeDtypeStruct((nb * s_new, LRU_WIDTH), F32),
                       jax.ShapeDtypeStruct((nb, 2, LRU_WIDTH), F32)],
            scratch_shapes=[pltpu.VMEM((1, LRU_WIDTH), F32), pltpu.VMEM((1, LRU_WIDTH), F32)],
            compiler_params=_cparams(("arbitrary", "arbitrary")),
        )(ux, ux, ux, ux, ux, ux, *small, h0)

    hf_p, hb_p, new_lru = lru_call(nb_p, s_p, 0, jnp.zeros((nb_p, 2, LRU_WIDTH), F32))
    hf_s, hb_s, _ = lru_call(nb_s, s_s, npt, state_lru_l0.astype(F32))
    hf = jnp.concatenate([hf_p, hf_s], axis=0)
    hb = jnp.concatenate([hb_p, hb_s], axis=0)

    w_o = w_o_l0.astype(BF16)
    x1 = pl.pallas_call(
        _oproj_kernel,
        grid=(ntile,),
        in_specs=[tok(d), mod_spec, tok(MLA_WIDTH), tok(LRU_WIDTH), tok(LRU_WIDTH), tok(LRU_WIDTH),
                  pl.BlockSpec((MLA_WIDTH, d), lambda i: (0, 0)), pl.BlockSpec((LRU_WIDTH, d), lambda i: (1, 0))],
        out_specs=tok(d),
        out_shape=jax.ShapeDtypeStruct((t_all, d), F32),
        compiler_params=_cparams(("arbitrary",)),
    )(x0, mod0, attn, hf, hb, ug, w_o, w_o)

    def peer_call(x, mod, g_ffn, w_q, sub_keys, u, v, final_norm):
        n_exp = u.shape[0]
        assert n_exp == N_KEYS * N_KEYS and n_exp % PEER_EBLK == 0
        wqt = w_q.T.astype(BF16)
        keys = sub_keys.astype(BF16).reshape(2 * PEER_HEADS, N_KEYS, PEER_DKEY // 2)
        ub = u.astype(BF16)
        vt = v.reshape(n_exp // PEER_EBLK, PEER_EBLK, d).transpose(0, 2, 1).astype(BF16)
        tpp = PEER_TM // TM

        nblk = n_exp // PEER_EBLK

        def cond_row_p(i):
            return cond_row(i * tpp)

        big = lambda dt: pltpu.VMEM((PEER_HEADS, N_KEYS, PEER_TM), dt)
        return pl.pallas_call(
            functools.partial(_peer_kernel, final_norm=final_norm),
            grid=(t_all // PEER_TM, nblk + 1),
            in_specs=[pl.BlockSpec((PEER_TM, d), lambda i, k: (i, 0), pipeline_mode=pl.Buffered(1)),
                      pl.BlockSpec((1, 6, d), lambda i, k: (cond_row_p(i), 0, 0)),
                      pl.BlockSpec((1, d), lambda i, k: (0, 0)),
                      pl.BlockSpec(wqt.shape, lambda i, k: (0, 0)),
                      pl.BlockSpec(keys.shape, lambda i, k: (0, 0, 0)),
                      pl.BlockSpec((PEER_EBLK, d), lambda i, k: (0, 0), pipeline_mode=pl.Buffered(1)),
                      pl.BlockSpec((PEER_EBLK, d), lambda i, k: (jnp.minimum(k + 1, nblk - 1), 0)),
                      pl.BlockSpec((1, d, PEER_EBLK), lambda i, k: (jnp.maximum(k - 1, 0), 0, 0)),
                      pl.BlockSpec((1, d), lambda i, k: (0, 0))],
            out_specs=pl.BlockSpec((PEER_TM, d), lambda i, k: (i, 0), pipeline_mode=pl.Buffered(1)),
            out_shape=jax.ShapeDtypeStruct((t_all, d), F32),
            scratch_shapes=[pltpu.VMEM((d, PEER_TM), BF16),
                            pltpu.VMEM((2 * PEER_HEADS * N_KEYS, PEER_TM), BF16),
                            pltpu.VMEM((2 * PEER_HEADS * N_KEYS, PEER_TM), F32),
                            pltpu.VMEM((2, PEER_TOPK, PEER_HEADS, PEER_TM), F32),
                            pltpu.VMEM((PEER_TOPK + 1, PEER_HEADS, PEER_TM), F32),
                            pltpu.VMEM((2, N_KEYS, PEER_TM), F32),
                            big(F32), big(F32),
                            pltpu.VMEM((2, PEER_EBLK, PEER_TM), F32),
                            pltpu.VMEM((2, PEER_EBLK, PEER_TM), BF16)],
            compiler_params=_cparams(("arbitrary", "arbitrary")),
        )(x, mod, row(g_ffn), wqt, keys, ub, ub, vt, row(g_final))

    x2 = peer_call(x1, mod0, g_ffn_l0, peer_wq_l0, peer_keys_l0, peer_u_l0, peer_v_l0, False)

    rh = TM // HALO
    lasth = t_all // HALO - 1
    x3 = pl.pallas_call(
        functools.partial(_pool_kernel, tiles_per_seq=tps, n_prompt_tiles=npt),
        grid=(ntile,),
        in_specs=[pl.BlockSpec((HALO, d), lambda i: (jnp.maximum(i * rh - 1, 0), 0)),
                  tok(d),
                  pl.BlockSpec((HALO, d), lambda i: (jnp.minimum((i + 1) * rh, lasth), 0)),
                  mod_spec, full(row(g_mix_l1)),
                  pl.BlockSpec(w_pool_l1.shape, lambda i: (0, 0, 0)), full(row(s_pool_l1))],
        out_specs=tok(d),
        out_shape=jax.ShapeDtypeStruct((t_all, d), F32),
        compiler_params=_cparams(("arbitrary",)),
    )(x2, x2, x2, mod1, row(g_mix_l1), w_pool_l1.astype(BF16), row(s_pool_l1))

    y = peer_call(x3, mod1, g_ffn_l1, peer_wq_l1, peer_keys_l1, peer_u_l1, peer_v_l1, True)

    y_prompt = y[:t_p].reshape(nb_p, s_p, d)
    y_sample = y[t_p:].reshape(nb_s, s_s, d)
    new_ckv = ckv[:t_p].reshape(nb_p, s_p, KV_LORA)
    new_krope = kr[:t_p].reshape(nb_p, s_p, QK_ROPE)
    return (y_prompt, y_sample, new_ckv, new_krope, new_lru)
```

```python
import functools

import numpy as np
import jax
import jax.numpy as jnp
from jax import lax
from jax.experimental import pallas as pl
from jax.experimental.pallas import tpu as pltpu

F32 = jnp.float32
BF16 = jnp.bfloat16

D_MODEL = 1024
EPS = 1e-6
GRID_W = 64
MLA_HEADS = 4
Q_LORA = 384
KV_LORA = 256
QK_NOPE = 128
QK_ROPE = 64
V_HEAD = 128
MLA_WIDTH = MLA_HEADS * V_HEAD
ROPE_BASE = 10000.0
LRU_WIDTH = 512
LRU_BLOCKS = 4
LRU_BLOCK = LRU_WIDTH // LRU_BLOCKS
CONV_W = 4
CONV_LEFT = 2
LRU_C = 8.0
POOL_WINDOWS = (2, 4, 8, 16)
POOL_GROUP = D_MODEL // len(POOL_WINDOWS)
PEER_HEADS = 8
N_KEYS = 128
PEER_DKEY = 256
PEER_TOPK = 16

SUBLANES = 8
LANES = 128
VMEM_BYTES = 64 * 1024 * 1024
VMEM_LIMIT = VMEM_BYTES - 4 * 1024 * 1024

TM = 256
QK_PAD = 256
HALO = 16
PEER_TM = 1024
PEER_EBLK = 512
PEER_SUB = 256
BF16_ROWS = 16
IN_EXT = 2048


def _rms(x, g):
    return x * lax.rsqrt(jnp.mean(x * x, axis=-1, keepdims=True) + EPS) * g


def _cparams(sem):
    return pltpu.CompilerParams(dimension_semantics=sem, vmem_limit_bytes=VMEM_LIMIT)


def _ada_kernel(c_ref, w_ref, b_ref, o_ref):
    c = c_ref[...]
    o_ref[...] = jnp.dot(c * jax.nn.sigmoid(c), w_ref[...], preferred_element_type=F32) + b_ref[...]


def _ada(cpad, w_mod, b_mod):
    n = w_mod.shape[1]
    bn = 768
    return pl.pallas_call(
        _ada_kernel,
        grid=(n // bn,),
        in_specs=[pl.BlockSpec(cpad.shape, lambda j: (0, 0)),
                  pl.BlockSpec((D_MODEL, bn), lambda j: (0, j)),
                  pl.BlockSpec((1, bn), lambda j: (0, j))],
        out_specs=pl.BlockSpec((cpad.shape[0], bn), lambda j: (0, j)),
        out_shape=jax.ShapeDtypeStruct((cpad.shape[0], n), F32),
        compiler_params=_cparams(("arbitrary",)),
    )(cpad, w_mod, b_mod.reshape(1, n))


def _inproj_kernel(x_ref, mod_ref, g_ref, win_ref, gq_ref, wuq_ref, gkv_ref, ck_ref, sk_ref,
                   q_ref, ckv_ref, kr_ref, krr_ref, ux_ref, ug_ref):
    mod = mod_ref[0]
    h = _rms(x_ref[...], g_ref[...]) * (1.0 + mod[1:2]) + mod[0:1]
    y = jnp.dot(h.astype(BF16), win_ref[...], preferred_element_type=F32)
    cq = y[:, 0:Q_LORA]
    ckv = y[:, Q_LORA:Q_LORA + KV_LORA]
    o = Q_LORA + KV_LORA
    kr = y[:, o:o + QK_ROPE]
    krp = y[:, o + 128:o + 128 + QK_ROPE]
    krs = y[:, o + 256:o + 256 + QK_ROPE]
    ux_ref[...] = y[:, o + 384:o + 384 + LRU_WIDTH]
    ug_ref[...] = y[:, o + 384 + LRU_WIDTH:o + 384 + 2 * LRU_WIDTH]
    ckv_ref[...] = _rms(ckv, gkv_ref[...])
    kr_ref[...] = kr
    ck = ck_ref[...]
    sk = sk_ref[...]
    krr_ref[...] = (krp * ck + krs * sk).astype(BF16)
    q = jnp.dot(_rms(cq, gq_ref[...]).astype(BF16), wuq_ref[...], preferred_element_type=F32)
    nw = MLA_HEADS * QK_NOPE
    rw = MLA_HEADS * QK_ROPE
    for hd in range(MLA_HEADS):
        qp = q[:, nw + hd * QK_ROPE:nw + (hd + 1) * QK_ROPE]
        qs = q[:, nw + rw + hd * QK_ROPE:nw + rw + (hd + 1) * QK_ROPE]
        q_ref[hd, :, 0:QK_NOPE] = q[:, hd * QK_NOPE:(hd + 1) * QK_NOPE].astype(BF16)
        q_ref[hd, :, QK_NOPE:QK_NOPE + QK_ROPE] = (qp * ck + qs * sk).astype(BF16)
        q_ref[hd, :, QK_NOPE + QK_ROPE:QK_PAD] = jnp.zeros((TM, QK_PAD - QK_NOPE - QK_ROPE), BF16)


def _attn_kernel(*refs, has_cache, s_new, n_cache):
    if has_cache:
        q_ref, ckv_ref, krr_ref, cckv_ref, ckr_ref, wukv_ref, o_ref, kcat_ref, vv_ref = refs
    else:
        q_ref, ckv_ref, krr_ref, wukv_ref, o_ref, kcat_ref, vv_ref = refs
    sk = n_cache + s_new
    kw = MLA_HEADS * QK_NOPE
    zpad = jnp.zeros((TM, QK_PAD - QK_NOPE - QK_ROPE), BF16)

    def put_keys(row0, ckv_rows, kr_rows):
        kv = jnp.dot(ckv_rows.astype(BF16), wukv_ref[...], preferred_element_type=F32)
        rows = pl.ds(row0, TM)
        for hd in range(MLA_HEADS):
            kcat_ref[hd, rows, 0:QK_NOPE] = kv[:, hd * QK_NOPE:(hd + 1) * QK_NOPE].astype(BF16)
            kcat_ref[hd, rows, QK_NOPE:QK_NOPE + QK_ROPE] = kr_rows.astype(BF16)
            kcat_ref[hd, rows, QK_NOPE + QK_ROPE:QK_PAD] = zpad
        vv_ref[rows, :] = kv[:, kw:].astype(BF16)

    @pl.when(pl.program_id(1) == 0)
    def _():
        if has_cache:
            for c in range(n_cache // TM):
                put_keys(c * TM, cckv_ref[0, c * TM:(c + 1) * TM, :], ckr_ref[0, c * TM:(c + 1) * TM, :])

        def body(c, carry):
            r0 = pl.multiple_of(c * TM, TM)
            put_keys(n_cache + r0, ckv_ref[pl.ds(r0, TM), :], krr_ref[pl.ds(r0, TM), :])
            return carry
        lax.fori_loop(0, s_new // TM, body, 0)

    scale = (QK_NOPE + QK_ROPE) ** -0.5
    for hd in range(MLA_HEADS):
        s = lax.dot_general(q_ref[hd], kcat_ref[hd], (((1,), (1,)), ((), ())),
                            preferred_element_type=F32) * scale
        m = jnp.max(s, axis=-1, keepdims=True)
        e = jnp.exp(s - m)
        l = jnp.sum(e, axis=-1, keepdims=True)
        o = jnp.dot(e.astype(BF16), vv_ref[:, hd * V_HEAD:(hd + 1) * V_HEAD], preferred_element_type=F32)
        o_ref[:, hd * V_HEAD:(hd + 1) * V_HEAD] = (o / l).astype(BF16)
    del sk


def _lru_dir(xp_ref, xc_ref, xn_ref, valid_prev, valid_next, d, reverse, cw_ref, cb_ref,
             wr_ref, br_ref, wi_ref, bi_ref, lam_ref, carry):
    xp = jnp.where(valid_prev, xp_ref[...], 0.0)
    xn = jnp.where(valid_next, xn_ref[...], 0.0)
    xx = jnp.concatenate([xp, xc_ref[...], xn], axis=0)
    n = TM + 2 * SUBLANES
    xc = cb_ref[...]
    for k in range(CONV_W):
        sh = (CONV_LEFT - k) % n
        xs = xx if sh == 0 else pltpu.roll(xx, sh, 0)
        xc = xc + xs[SUBLANES:SUBLANES + TM] * cw_ref[k:k + 1, :]
    rs, is_ = [], []
    for b in range(LRU_BLOCKS):
        xb = xc[:, b * LRU_BLOCK:(b + 1) * LRU_BLOCK]
        rs.append(jnp.dot(xb, wr_ref[d, b], preferred_element_type=F32))
        is_.append(jnp.dot(xb, wi_ref[d, b], preferred_element_type=F32))
    r = jax.nn.sigmoid(jnp.concatenate(rs, axis=-1) + br_ref[d:d + 1, :])
    i = jax.nn.sigmoid(jnp.concatenate(is_, axis=-1) + bi_ref[d:d + 1, :])
    nl = -lam_ref[d:d + 1, :]
    softplus = jnp.maximum(nl, 0.0) + jnp.log1p(jnp.exp(-jnp.abs(nl)))
    log_a = -LRU_C * r * softplus
    a = jnp.exp(log_a)
    bx = jnp.sqrt(jnp.tanh(-log_a) * (a * a + 1.0)) * (i * xc)
    t = lax.broadcasted_iota(jnp.int32, (TM, 1), 0)
    step = 1
    while step < TM:
        if reverse:
            keep = t < TM - step
            sh = TM - step
        else:
            keep = t >= step
            sh = step
        a_s = jnp.where(keep, pltpu.roll(a, sh, 0), 1.0)
        b_s = jnp.where(keep, pltpu.roll(bx, sh, 0), 0.0)
        bx = a * b_s + bx
        a = a * a_s
        step *= 2
    return a * carry + bx


def _lru_kernel(fxp, fxc, fxn, bxp, bxc, bxn, cw_ref, cb_ref, wr_ref, br_ref, wi_ref, bi_ref, lam_ref,
                h0_ref, hf_ref, hb_ref, st_ref, cf_ref, cbk_ref, *, nc):
    c = pl.program_id(1)

    @pl.when(c == 0)
    def _():
        cf_ref[...] = h0_ref[0, 0:1, :]
        cbk_ref[...] = h0_ref[0, 1:2, :]

    params = (cw_ref, cb_ref, wr_ref, br_ref, wi_ref, bi_ref, lam_ref)
    hf = _lru_dir(fxp, fxc, fxn, c > 0, c < nc - 1, 0, False, *params, cf_ref[...])
    hf_ref[...] = hf
    cf_ref[...] = hf[TM - 1:TM, :]
    hb = _lru_dir(bxp, bxc, bxn, c < nc - 1, c > 0, 1, True, *params, cbk_ref[...])
    hb_ref[...] = hb
    cbk_ref[...] = hb[0:1, :]
    st_ref[0, 0:1, :] = hf[TM - 1:TM, :]
    st_ref[0, 1:2, :] = hb[0:1, :]


def _oproj_kernel(x_ref, mod_ref, at_ref, hf_ref, hb_ref, ug_ref, woa_ref, wor_ref, o_ref):
    mod = mod_ref[0]
    rec = ((hf_ref[...] + hb_ref[...]) * jax.nn.gelu(ug_ref[...])).astype(BF16)
    out = (jnp.dot(at_ref[...], woa_ref[...], preferred_element_type=F32)
           + jnp.dot(rec, wor_ref[...], preferred_element_type=F32))
    o_ref[...] = x_ref[...] + mod[2:3] * out


def _pool_kernel(xp_ref, xc_ref, xn_ref, mod_ref, g_ref, wp_ref, sp_ref, o_ref, *, tiles_per_seq, n_prompt_tiles):
    i = pl.program_id(0)
    j = jnp.where(i < n_prompt_tiles, 0, (i - n_prompt_tiles) % tiles_per_seq)
    ntile = jnp.where(i < n_prompt_tiles, 1, tiles_per_seq)
    mod = mod_ref[0]
    g = g_ref[...]

    def hmod(x):
        return _rms(x, g) * (1.0 + mod[1:2]) + mod[0:1]

    x = xc_ref[...]
    hc = hmod(x)
    hp = jnp.where(j > 0, hmod(xp_ref[...]), 0.0)
    hn = jnp.where(j < ntile - 1, hmod(xn_ref[...]), 0.0)
    hh = jnp.concatenate([hp, hc, hn], axis=0)
    n = TM + 2 * HALO
    seq_len = ntile * TM
    t = j * TM + lax.broadcasted_iota(jnp.int32, (TM, 1), 0)
    ys = []
    for gi, w in enumerate(POOL_WINDOWS):
        cols = slice(gi * POOL_GROUP, (gi + 1) * POOL_GROUP)
        p = hh[:, cols]
        p = p + pltpu.roll(p, 1, 0)
        half = 1
        while 2 * half < w:
            p = pltpu.roll(p, half, 0) + pltpu.roll(p, n - half, 0)
            half *= 2
        lo = jnp.maximum(t - w // 2, 0)
        hi = jnp.minimum(t + (w - w // 2), seq_len)
        mean = p[HALO:HALO + TM] / (hi - lo).astype(F32)
        dg = (mean - hc[:, cols]).astype(BF16)
        ys.append(jnp.dot(dg, wp_ref[gi], preferred_element_type=F32))
    y = jnp.concatenate(ys, axis=-1) * sp_ref[...]
    o_ref[...] = x + mod[2:3] * y


def _sort_pairs(n):
    pairs = []

    def merge(lo, cnt, r):
        step = r * 2
        if step < cnt:
            merge(lo, cnt, step)
            merge(lo + r, cnt, step)
            for i in range(lo + r, lo + cnt - r, step):
                pairs.append((i, i + r))
        else:
            pairs.append((lo, lo + r))

    def sort(lo, cnt):
        if cnt > 1:
            m = cnt // 2
            sort(lo, m)
            sort(lo + m, m)
            merge(lo, cnt, 1)

    sort(0, n)
    return pairs


_SORT16 = _sort_pairs(PEER_TOPK)
_HYPER = [(a, b) for a in range(PEER_TOPK) for b in range(PEER_TOPK) if (a + 1) * (b + 1) <= PEER_TOPK]


def _top16_sorted(s):
    k = PEER_TOPK
    x = [s[SUBLANES * r:SUBLANES * (r + 1), :] for r in range(N_KEYS // SUBLANES)]
    for (i, j) in _SORT16:
        hi = jnp.maximum(x[i], x[j])
        lo = jnp.minimum(x[i], x[j])
        x[i], x[j] = hi, lo
    for shift in (4, 2, 1):
        y = [jnp.maximum(x[r], pltpu.roll(x[k - 1 - r], shift, 0)) for r in range(k)]
        stride = k // 2
        while stride >= 1:
            for i in range(k):
                if i & stride == 0:
                    hi = jnp.maximum(y[i], y[i + stride])
                    lo = jnp.minimum(y[i], y[i + stride])
                    y[i], y[i + stride] = hi, lo
            stride //= 2
        x = y
    return x


def _pair_counts(sv1, sv2):
    one = jnp.ones_like(sv1[0])
    zero = jnp.zeros_like(sv1[0])
    cand = [sv1[a] + sv2[b] for (a, b) in _HYPER]

    def ordered(i, j):
        (ai, bi), (aj, bj) = _HYPER[i], _HYPER[j]
        return (aj <= ai and bj <= bi) or (ai <= aj and bi <= bj)

    nh = len(_HYPER)
    cnt = [float((a + 1) * (b + 1) - 1 + sum(1 for j in range(i + 1, nh) if not ordered(i, j))) * one
           for i, (a, b) in enumerate(_HYPER)]
    for i in range(nh):
        for jx in range(i):
            if ordered(i, jx):
                continue
            ge = jnp.where(cand[jx] >= cand[i], one, zero)
            cnt[i] = cnt[i] + ge
            cnt[jx] = cnt[jx] - ge
    e1 = [jnp.exp(sv1[a] - sv1[0]) for a in range(PEER_TOPK)]
    e2 = [jnp.exp(sv2[b] - sv2[0]) for b in range(PEER_TOPK)]
    n = [zero for _ in range(PEER_TOPK)]
    z = zero
    for i, (a, b) in enumerate(_HYPER):
        sel = jnp.where(cnt[i] < float(PEER_TOPK), one, zero)
        n[a] = n[a] + sel
        z = z + sel * (e1[a] * e2[b])
    return n, 1.0 / z


def _head_tables(hd, lanes, sc_ref, sv_ref, nz_ref, ex_ref, cnt1_ref, e1n_ref, rank2_ref, e2_ref, exact):
    k = PEER_TOPK

    def bc(ref, *idx):
        return jnp.broadcast_to(ref[idx + (slice(hd, hd + 1), lanes)], (SUBLANES, LANES))

    sv1b = [bc(sv_ref, 0, a) for a in range(k)]
    nb = [bc(nz_ref, a) for a in range(k)]
    izb = bc(nz_ref, k)
    c1 = jnp.zeros((SUBLANES, LANES), F32)
    for r in range(N_KEYS // SUBLANES):
        rows = slice(r * SUBLANES, (r + 1) * SUBLANES)
        s1 = sc_ref[pl.ds(2 * hd * N_KEYS + r * SUBLANES, SUBLANES), lanes]
        cnt1 = jnp.zeros_like(s1)
        if exact:
            rank1 = jnp.zeros_like(s1)
            for a in range(k):
                rank1 = jnp.where(sv1b[a] > s1, float(a + 1), rank1)
            rank1 = rank1 + ex_ref[0, rows, lanes]
            for a in range(k):
                cnt1 = jnp.where(rank1 == float(a), nb[a], cnt1)
        else:
            for a in range(k):
                cnt1 = jnp.where(s1 == sv1b[a], nb[a], cnt1)
            c1 = c1 + jnp.where(s1 >= sv1b[k - 1], 1.0, 0.0)
        cnt1_ref[hd, rows, lanes] = cnt1
        e1n_ref[hd, rows, lanes] = jnp.exp(s1 - sv1b[0]) * izb
    sv2b = [bc(sv_ref, 1, a) for a in range(k)]
    c2 = jnp.zeros((SUBLANES, LANES), F32)
    for r in range(N_KEYS // BF16_ROWS):
        rk, e2 = [], []
        for q in range(BF16_ROWS // SUBLANES):
            r0 = r * BF16_ROWS + q * SUBLANES
            rows = slice(r0, r0 + SUBLANES)
            s2 = sc_ref[pl.ds((2 * hd + 1) * N_KEYS + r0, SUBLANES), lanes]
            rank2 = jnp.zeros_like(s2)
            for a in range(k):
                rank2 = jnp.where(sv2b[a] > s2, float(a + 1), rank2)
            if exact:
                rank2 = jnp.minimum(rank2 + ex_ref[1, rows, lanes], float(k))
            else:
                c2 = c2 + jnp.where(rank2 < float(k), 1.0, 0.0)
            rk.append(rank2)
            e2.append(jnp.exp(s2 - sv2b[0]))
        rows16 = pl.ds(hd * N_KEYS + r * BF16_ROWS, BF16_ROWS)
        rank2_ref[rows16, lanes] = jnp.concatenate(rk, axis=0).astype(BF16)
        e2_ref[rows16, lanes] = jnp.concatenate(e2, axis=0).astype(BF16)
    if exact:
        return None
    ex_ref[0, 0:SUBLANES, lanes] = c1
    ex_ref[1, 0:SUBLANES, lanes] = c2
    b = jnp.zeros((SUBLANES, LANES), F32)
    for svb in (sv1b, sv2b):
        for a in range(k - 1):
            b = b + jnp.where(svb[a] == svb[a + 1], 1.0, 0.0)
    return b


def _tie_offsets(hd, sc_ref, ex_ref):
    tm = sc_ref.shape[-1]
    nidx = lax.broadcasted_iota(jnp.int32, (N_KEYS, tm), 0)
    for p in range(2):
        s = sc_ref[pl.ds((2 * hd + p) * N_KEYS, N_KEYS), :]

        def body(m, e, s=s, p=p):
            row = sc_ref[pl.ds((2 * hd + p) * N_KEYS + m, 1), :]
            return e + jnp.where((s == row) & (nidx > m), 1.0, 0.0)

        ex_ref[p] = lax.fori_loop(0, N_KEYS, body, jnp.zeros((N_KEYS, tm), F32))


def _peer_prologue(x_ref, mod_ref, g_ref, wqt_ref, keys_ref, h2t_ref, qb_ref, sc_ref, sv_ref, nz_ref,
                   ex_ref, cnt1_ref, e1n_ref):
    tm = PEER_TM
    half = PEER_DKEY // 2
    ngroup = tm // LANES
    mod = mod_ref[0]
    h2 = _rms(x_ref[...], g_ref[...]) * (1.0 + mod[4:5]) + mod[3:4]
    h2t_ref[...] = h2.T.astype(BF16)

    def lane_group(g):
        return pl.ds(pl.multiple_of(g * LANES, LANES), LANES)

    qb_ref[...] = jnp.dot(wqt_ref[...], h2t_ref[...], preferred_element_type=F32).astype(BF16)
    for i in range(2 * PEER_HEADS):
        sc_ref[i * N_KEYS:(i + 1) * N_KEYS, :] = jnp.dot(keys_ref[i], qb_ref[i * half:(i + 1) * half, :],
                                                         preferred_element_type=F32)

        def group(g, c, i=i):
            lanes = lane_group(g)
            top = _top16_sorted(sc_ref[i * N_KEYS:(i + 1) * N_KEYS, lanes])
            for a in range(PEER_TOPK):
                sv_ref[i % 2, a, i // 2:i // 2 + 1, lanes] = top[a][0:1, :]
            return c
        lax.fori_loop(0, ngroup, group, 0)

    def counts(g, carry):
        lanes = lane_group(g)
        sv1 = [sv_ref[0, a, :, lanes] for a in range(PEER_TOPK)]
        sv2 = [sv_ref[1, a, :, lanes] for a in range(PEER_TOPK)]
        n, inv_z = _pair_counts(sv1, sv2)
        for a in range(PEER_TOPK):
            nz_ref[a, :, lanes] = n[a]
        nz_ref[PEER_TOPK, :, lanes] = inv_z
        return carry
    lax.fori_loop(0, ngroup, counts, 0)

    nrow = PEER_HEADS * N_KEYS
    tabs = (sc_ref, sv_ref, nz_ref, ex_ref, cnt1_ref, e1n_ref, qb_ref.at[0:nrow], qb_ref.at[nrow:2 * nrow])

    for hd in range(PEER_HEADS):
        def fast(g, bad, hd=hd):
            return bad + _head_tables(hd, lane_group(g), *tabs, exact=False)
        bad = lax.fori_loop(0, ngroup, fast, jnp.zeros((SUBLANES, LANES), F32))
        totals = jnp.sum(ex_ref[:, 0:SUBLANES, :], axis=1)
        ties = jnp.max(bad) + jnp.max(jnp.abs(totals - float(PEER_TOPK)))

        @pl.when(ties > 0.0)
        def _(hd=hd):
            _tie_offsets(hd, sc_ref, ex_ref)

            def slow(g, c):
                _head_tables(hd, lane_group(g), *tabs, exact=True)
                return c
            lax.fori_loop(0, ngroup, slow, 0)
    sc_ref[0:D_MODEL, :] = jnp.zeros((D_MODEL, tm), F32)


def _peer_kernel(x_ref, mod_ref, g_ref, wqt_ref, keys_ref, u0_ref, u_ref, vt_ref, gfin_ref, o_ref,
                 h2t_ref, qb_ref, sc_ref, sv_ref, nz_ref, ex_ref, cnt1_ref, e1n_ref, raw_ref, y_ref,
                 *, final_norm):
    k = pl.program_id(1)
    tm = PEER_TM
    nrow = PEER_HEADS * N_KEYS
    acc_ref = sc_ref.at[0:D_MODEL]
    rank2_ref = qb_ref.at[0:nrow]
    e2_ref = qb_ref.at[nrow:2 * nrow]

    nstep = pl.num_programs(1) - 1
    zero = jnp.zeros((BF16_ROWS, LANES), BF16)
    npair = PEER_EBLK // (2 * N_KEYS)
    cur = k % 2
    oth = (k + 1) % 2

    def up_next():
        raw_ref[oth] = jnp.dot(u_ref[...], h2t_ref[...], preferred_element_type=F32)

    def down_prev():
        acc_ref[...] += jnp.dot(vt_ref[0], y_ref[oth], preferred_element_type=F32)

    def gate_block():
        for q in range(npair):
            i1 = (k * npair + q) * 2
            crow = [[cnt1_ref[hd, pl.ds(i1 + j, 1), :] for hd in range(PEER_HEADS)] for j in range(2)]
            erow = [[e1n_ref[hd, pl.ds(i1 + j, 1), :] for hd in range(PEER_HEADS)] for j in range(2)]
            for g in range(tm // LANES):
                lanes = slice(g * LANES, (g + 1) * LANES)
                cbs = [[jnp.broadcast_to(crow[j][hd][:, lanes], (BF16_ROWS, LANES)).astype(BF16)
                        for hd in range(PEER_HEADS)] for j in range(2)]
                ebs = [[jnp.broadcast_to(erow[j][hd][:, lanes], (BF16_ROWS, LANES)).astype(BF16)
                        for hd in range(PEER_HEADS)] for j in range(2)]
                for r in range(N_KEYS // BF16_ROWS):
                    w = [None, None]
                    for hd in range(PEER_HEADS):
                        hrows = pl.ds(hd * N_KEYS + r * BF16_ROWS, BF16_ROWS)
                        rk = rank2_ref[hrows, lanes]
                        ev = e2_ref[hrows, lanes]
                        for j in range(2):
                            t = jnp.where(rk < cbs[j][hd], ev, zero) * ebs[j][hd]
                            w[j] = t if w[j] is None else w[j] + t
                    for j in range(2):
                        rows = pl.ds((2 * q + j) * N_KEYS + r * BF16_ROWS, BF16_ROWS)
                        act = jax.nn.gelu(raw_ref[cur, rows, lanes]).astype(BF16)
                        y_ref[cur, rows, lanes] = act * w[j]

    @pl.when(k == 0)
    def _():
        _peer_prologue(x_ref, mod_ref, g_ref, wqt_ref, keys_ref, h2t_ref, qb_ref, sc_ref, sv_ref, nz_ref,
                       ex_ref, cnt1_ref, e1n_ref)
        raw_ref[0] = jnp.dot(u0_ref[...], h2t_ref[...], preferred_element_type=F32)
        up_next()
        gate_block()

    @pl.when((k > 0) & (k < nstep))
    def _():
        up_next()
        down_prev()
        gate_block()

    @pl.when(k == nstep)
    def _():
        down_prev()
        out = x_ref[...] + mod_ref[0][5:6] * acc_ref[...].T
        if final_norm:
            out = _rms(out, gfin_ref[...])
        o_ref[...] = out


def _rope_tables(s_prompt_tile, s_sample):
    n_rows = s_sample // GRID_W
    rows = jnp.repeat(jnp.arange(n_rows, dtype=F32), GRID_W)
    cols = jnp.tile(jnp.arange(GRID_W, dtype=F32), n_rows)
    axis_dim = QK_ROPE // 2
    inv_freq = ROPE_BASE ** (-jnp.arange(0, axis_dim, 2, dtype=F32) / axis_dim)
    ang = jnp.concatenate([rows[:, None] * inv_freq, cols[:, None] * inv_freq], axis=-1)
    cos, sin = jnp.cos(ang), jnp.sin(ang)
    ck = jnp.concatenate([cos, cos], axis=-1)
    sk = jnp.concatenate([-sin, sin], axis=-1)
    ident_c = jnp.ones((s_prompt_tile, QK_ROPE), F32)
    ident_s = jnp.zeros((s_prompt_tile, QK_ROPE), F32)
    return jnp.concatenate([ident_c, ck], axis=0), jnp.concatenate([ident_s, sk], axis=0)


def kernel(x_prompt, x_sample, cache_ckv_l0, cache_krope_l0, state_lru_l0, c, c_ctx, w_mod_l0, b_mod_l0, w_mod_l1, b_mod_l1, g_mix_l0, g_ffn_l0, g_mix_l1, g_ffn_l1, w_in_l0, g_q_l0, w_uq_l0, g_kv_l0, w_ukv_l0, conv_w_l0, conv_b_l0, w_rg_l0, b_rg_l0, w_ig_l0, b_ig_l0, lam_l0, w_o_l0, w_pool_l1, s_pool_l1, peer_wq_l0, peer_keys_l0, peer_u_l0, peer_v_l0, peer_wq_l1, peer_keys_l1, peer_u_l1, peer_v_l1, g_final):
    nb_p, s_p, d = x_prompt.shape
    nb_s, s_s, _ = x_sample.shape
    n_cache = cache_ckv_l0.shape[1]
    assert d == D_MODEL and s_p == TM and s_s % TM == 0 and n_cache % TM == 0
    t_p = nb_p * s_p
    t_s = nb_s * s_s
    t_all = t_p + t_s
    npt = t_p // TM
    tps = s_s // TM
    ntile = t_all // TM
    assert t_all % PEER_TM == 0 and t_p % PEER_TM == 0 and s_s % PEER_TM == 0

    x0 = jnp.concatenate([x_prompt.reshape(t_p, d), x_sample.reshape(t_s, d)], axis=0)

    ncond = 1 + nb_s
    cpad = jnp.zeros((2 * SUBLANES, d), F32).at[0].set(c_ctx).at[1:ncond].set(c)
    mod0 = _ada(cpad, w_mod_l0, b_mod_l0).reshape(2 * SUBLANES, 6, d)
    mod1 = _ada(cpad, w_mod_l1, b_mod_l1).reshape(2 * SUBLANES, 6, d)

    def cond_row(i):
        return jnp.where(i < npt, 0, 1 + (i - npt) // tps)

    mod_spec = pl.BlockSpec((1, 6, d), lambda i: (cond_row(i), 0, 0))
    row = lambda a: a.reshape(1, -1)

    perm = np.concatenate([np.arange(0, QK_ROPE, 2), np.arange(1, QK_ROPE, 2)])
    perm_sw = np.concatenate([np.arange(1, QK_ROPE, 2), np.arange(0, QK_ROPE, 2)])
    o1 = Q_LORA + KV_LORA
    w_kr = w_in_l0[:, o1:o1 + QK_ROPE]
    z64 = jnp.zeros((d, 128 - QK_ROPE), F32)
    w_in_ext = jnp.concatenate(
        [w_in_l0[:, :o1], w_kr, z64, w_kr[:, perm], z64, w_kr[:, perm_sw], z64, w_in_l0[:, o1 + QK_ROPE:]],
        axis=1).astype(BF16)
    assert w_in_ext.shape[1] == IN_EXT
    wq3 = w_uq_l0.reshape(Q_LORA, MLA_HEADS, QK_NOPE + QK_ROPE)
    w_uq_ext = jnp.concatenate(
        [wq3[:, :, :QK_NOPE].reshape(Q_LORA, -1),
         wq3[:, :, QK_NOPE:][:, :, perm].reshape(Q_LORA, -1),
         wq3[:, :, QK_NOPE:][:, :, perm_sw].reshape(Q_LORA, -1)], axis=1).astype(BF16)
    ck_tab, sk_tab = _rope_tables(TM, s_s)

    def rope_blk(i):
        return jnp.where(i < npt, 0, 1 + (i - npt) % tps)

    tok = lambda w: pl.BlockSpec((TM, w), lambda i: (i, 0))
    full = lambda a: pl.BlockSpec(a.shape, lambda *_: (0,) * a.ndim)
    q, ckv, kr, krr, ux, ug = pl.pallas_call(
        _inproj_kernel,
        grid=(ntile,),
        in_specs=[tok(d), mod_spec, full(row(g_mix_l0)), full(w_in_ext), full(row(g_q_l0)), full(w_uq_ext),
                  full(row(g_kv_l0)),
                  pl.BlockSpec((TM, QK_ROPE), lambda i: (rope_blk(i), 0)),
                  pl.BlockSpec((TM, QK_ROPE), lambda i: (rope_blk(i), 0))],
        out_specs=[pl.BlockSpec((MLA_HEADS, TM, QK_PAD), lambda i: (0, i, 0)),
                   tok(KV_LORA), tok(QK_ROPE), tok(QK_ROPE), tok(LRU_WIDTH), tok(LRU_WIDTH)],
        out_shape=[jax.ShapeDtypeStruct((MLA_HEADS, t_all, QK_PAD), BF16),
                   jax.ShapeDtypeStruct((t_all, KV_LORA), F32),
                   jax.ShapeDtypeStruct((t_all, QK_ROPE), F32),
                   jax.ShapeDtypeStruct((t_all, QK_ROPE), BF16),
                   jax.ShapeDtypeStruct((t_all, LRU_WIDTH), F32),
                   jax.ShapeDtypeStruct((t_all, LRU_WIDTH), F32)],
        compiler_params=_cparams(("arbitrary",)),
    )(x0, mod0, row(g_mix_l0), w_in_ext, row(g_q_l0), w_uq_ext, row(g_kv_l0), ck_tab, sk_tab)

    wkv3 = w_ukv_l0.reshape(KV_LORA, MLA_HEADS, QK_NOPE + V_HEAD)
    w_ukv_ext = jnp.concatenate([wkv3[:, :, :QK_NOPE].reshape(KV_LORA, -1),
                                 wkv3[:, :, QK_NOPE:].reshape(KV_LORA, -1)], axis=1).astype(BF16)

    def attn_call(nb, s_new, tile0, has_cache):
        nq = s_new // TM
        blk0 = tile0 * TM // s_new
        n_c = n_cache if has_cache else 0
        in_specs = [pl.BlockSpec((MLA_HEADS, TM, QK_PAD), lambda b, qi: (0, tile0 + b * nq + qi, 0)),
                    pl.BlockSpec((s_new, KV_LORA), lambda b, qi: (blk0 + b, 0)),
                    pl.BlockSpec((s_new, QK_ROPE), lambda b, qi: (blk0 + b, 0))]
        args = [q, ckv, krr]
        if has_cache:
            in_specs += [pl.BlockSpec((1, n_cache, KV_LORA), lambda b, qi: (b, 0, 0)),
                         pl.BlockSpec((1, n_cache, QK_ROPE), lambda b, qi: (b, 0, 0))]
            args += [cache_ckv_l0, cache_krope_l0[:, :, perm]]
        in_specs.append(pl.BlockSpec(w_ukv_ext.shape, lambda b, qi: (0, 0)))
        args.append(w_ukv_ext)
        return pl.pallas_call(
            functools.partial(_attn_kernel, has_cache=has_cache, s_new=s_new, n_cache=n_c),
            grid=(nb, nq),
            in_specs=in_specs,
            out_specs=pl.BlockSpec((TM, MLA_WIDTH), lambda b, qi: (b * nq + qi, 0)),
            out_shape=jax.ShapeDtypeStruct((nb * s_new, MLA_WIDTH), BF16),
            scratch_shapes=[pltpu.VMEM((MLA_HEADS, n_c + s_new, QK_PAD), BF16),
                            pltpu.VMEM((n_c + s_new, MLA_WIDTH), BF16)],
            compiler_params=_cparams(("arbitrary", "arbitrary")),
        )(*args)

    attn = jnp.concatenate([attn_call(nb_p, s_p, 0, False), attn_call(nb_s, s_s, npt, True)], axis=0)

    def lru_call(nb, s_new, tile0, h0):
        nc = s_new // TM
        r8 = TM // SUBLANES
        last8 = t_all // SUBLANES - 1

        def cur(rev):
            return pl.BlockSpec((TM, LRU_WIDTH),
                                lambda b, cc: (tile0 + b * nc + (nc - 1 - cc if rev else cc), 0))

        def prev(rev):
            return pl.BlockSpec((SUBLANES, LRU_WIDTH), lambda b, cc: (
                jnp.maximum((tile0 + b * nc + (nc - 1 - cc if rev else cc)) * r8 - 1, 0), 0))

        def nxt(rev):
            return pl.BlockSpec((SUBLANES, LRU_WIDTH), lambda b, cc: (
                jnp.minimum((tile0 + b * nc + (nc - 1 - cc if rev else cc) + 1) * r8, last8), 0))

        small = [conv_w_l0, row(conv_b_l0), w_rg_l0, b_rg_l0, w_ig_l0, b_ig_l0, lam_l0]
        return pl.pallas_call(
            functools.partial(_lru_kernel, nc=nc),
            grid=(nb, nc),
            in_specs=[prev(False), cur(False), nxt(False), prev(True), cur(True), nxt(True)]
                     + [full(a) for a in small]
                     + [pl.BlockSpec((1, 2, LRU_WIDTH), lambda b, cc: (b, 0, 0))],
            out_specs=[pl.BlockSpec((TM, LRU_WIDTH), lambda b, cc: (b * nc + cc, 0)),
                       pl.BlockSpec((TM, LRU_WIDTH), lambda b, cc: (b * nc + nc - 1 - cc, 0)),
                       pl.BlockSpec((1, 2, LRU_WIDTH), lambda b, cc: (b, 0, 0))],
            out_shape=[jax.ShapeDtypeStruct((nb * s_new, LRU_WIDTH), F32),
                       jax.ShapeDtypeStruct((nb * s_new, LRU_WIDTH), F32),
                       jax.ShapeDtypeStruct((nb, 2, LRU_WIDTH), F32)],
            scratch_shapes=[pltpu.VMEM((1, LRU_WIDTH), F32), pltpu.VMEM((1, LRU_WIDTH), F32)],
            compiler_params=_cparams(("arbitrary", "arbitrary")),
        )(ux, ux, ux, ux, ux, ux, *small, h0)

    hf_p, hb_p, new_lru = lru_call(nb_p, s_p, 0, jnp.zeros((nb_p, 2, LRU_WIDTH), F32))
    hf_s, hb_s, _ = lru_call(nb_s, s_s, npt, state_lru_l0.astype(F32))
    hf = jnp.concatenate([hf_p, hf_s], axis=0)
    hb = jnp.concatenate([hb_p, hb_s], axis=0)

    w_o = w_o_l0.astype(BF16)
    x1 = pl.pallas_call(
        _oproj_kernel,
        grid=(ntile,),
        in_specs=[tok(d), mod_spec, tok(MLA_WIDTH), tok(LRU_WIDTH), tok(LRU_WIDTH), tok(LRU_WIDTH),
                  pl.BlockSpec((MLA_WIDTH, d), lambda i: (0, 0)), pl.BlockSpec((LRU_WIDTH, d), lambda i: (1, 0))],
        out_specs=tok(d),
        out_shape=jax.ShapeDtypeStruct((t_all, d), F32),
        compiler_params=_cparams(("arbitrary",)),
    )(x0, mod0, attn, hf, hb, ug, w_o, w_o)

    def peer_call(x, mod, g_ffn, w_q, sub_keys, u, v, final_norm):
        n_exp = u.shape[0]
        assert n_exp == N_KEYS * N_KEYS and n_exp % PEER_EBLK == 0
        wqt = w_q.T.astype(BF16)
        keys = sub_keys.astype(BF16).reshape(2 * PEER_HEADS, N_KEYS, PEER_DKEY // 2)
        ub = u.astype(BF16)
        vt = v.reshape(n_exp // PEER_EBLK, PEER_EBLK, d).transpose(0, 2, 1).astype(BF16)
        tpp = PEER_TM // TM

        nblk = n_exp // PEER_EBLK

        def cond_row_p(i):
            return cond_row(i * tpp)

        big = lambda dt: pltpu.VMEM((PEER_HEADS, N_KEYS, PEER_TM), dt)
        return pl.pallas_call(
            functools.partial(_peer_kernel, final_norm=final_norm),
            grid=(t_all // PEER_TM, nblk + 1),
            in_specs=[pl.BlockSpec((PEER_TM, d), lambda i, k: (i, 0), pipeline_mode=pl.Buffered(1)),
                      pl.BlockSpec((1, 6, d), lambda i, k: (cond_row_p(i), 0, 0)),
                      pl.BlockSpec((1, d), lambda i, k: (0, 0)),
                      pl.BlockSpec(wqt.shape, lambda i, k: (0, 0)),
                      pl.BlockSpec(keys.shape, lambda i, k: (0, 0, 0)),
                      pl.BlockSpec((PEER_EBLK, d), lambda i, k: (0, 0), pipeline_mode=pl.Buffered(1)),
                      pl.BlockSpec((PEER_EBLK, d), lambda i, k: (jnp.minimum(k + 1, nblk - 1), 0)),
                      pl.BlockSpec((1, d, PEER_EBLK), lambda i, k: (jnp.maximum(k - 1, 0), 0, 0)),
                      pl.BlockSpec((1, d), lambda i, k: (0, 0))],
            out_specs=pl.BlockSpec((PEER_TM, d), lambda i, k: (i, 0), pipeline_mode=pl.Buffered(1)),
            out_shape=jax.ShapeDtypeStruct((t_all, d), F32),
            scratch_shapes=[pltpu.VMEM((d, PEER_TM), BF16),
                            pltpu.VMEM((2 * PEER_HEADS * N_KEYS, PEER_TM), BF16),
                            pltpu.VMEM((2 * PEER_HEADS * N_KEYS, PEER_TM), F32),
                            pltpu.VMEM((2, PEER_TOPK, PEER_HEADS, PEER_TM), F32),
                            pltpu.VMEM((PEER_TOPK + 1, PEER_HEADS, PEER_TM), F32),
                            pltpu.VMEM((2, N_KEYS, PEER_TM), F32),
                            big(F32), big(F32),
                            pltpu.VMEM((2, PEER_EBLK, PEER_TM), F32),
                            pltpu.VMEM((2, PEER_EBLK, PEER_TM), BF16)],
            compiler_params=_cparams(("arbitrary", "arbitrary")),
        )(x, mod, row(g_ffn), wqt, keys, ub, ub, vt, row(g_final))

    x2 = peer_call(x1, mod0, g_ffn_l0, peer_wq_l0, peer_keys_l0, peer_u_l0, peer_v_l0, False)

    rh = TM // HALO
    lasth = t_all // HALO - 1
    x3 = pl.pallas_call(
        functools.partial(_pool_kernel, tiles_per_seq=tps, n_prompt_tiles=npt),
        grid=(ntile,),
        in_specs=[pl.BlockSpec((HALO, d), lambda i: (jnp.maximum(i * rh - 1, 0), 0)),
                  tok(d),
                  pl.BlockSpec((HALO, d), lambda i: (jnp.minimum((i + 1) * rh, lasth), 0)),
                  mod_spec, full(row(g_mix_l1)),
                  pl.BlockSpec(w_pool_l1.shape, lambda i: (0, 0, 0)), full(row(s_pool_l1))],
        out_specs=tok(d),
        out_shape=jax.ShapeDtypeStruct((t_all, d), F32),
        compiler_params=_cparams(("arbitrary",)),
    )(x2, x2, x2, mod1, row(g_mix_l1), w_pool_l1.astype(BF16), row(s_pool_l1))

    y = peer_call(x3, mod1, g_ffn_l1, peer_wq_l1, peer_keys_l1, peer_u_l1, peer_v_l1, True)

    y_prompt = y[:t_p].reshape(nb_p, s_p, d)
    y_sample = y[t_p:].reshape(nb_s, s_s, d)
    new_ckv = ckv[:t_p].reshape(nb_p, s_p, KV_LORA)
    new_krope = kr[:t_p].reshape(nb_p, s_p, QK_ROPE)
    return (y_prompt, y_sample, new_ckv, new_krope, new_lru)
```

```python
import functools

import numpy as np
import jax
import jax.numpy as jnp
from jax import lax
from jax.experimental import pallas as pl
from jax.experimental.pallas import tpu as pltpu

F32 = jnp.float32
BF16 = jnp.bfloat16

D_MODEL = 1024
EPS = 1e-6
GRID_W = 64
MLA_HEADS = 4
Q_LORA = 384
KV_LORA = 256
QK_NOPE = 128
QK_ROPE = 64
V_HEAD = 128
MLA_WIDTH = MLA_HEADS * V_HEAD
ROPE_BASE = 10000.0
LRU_WIDTH = 512
LRU_BLOCKS = 4
LRU_BLOCK = LRU_WIDTH // LRU_BLOCKS
CONV_W = 4
CONV_LEFT = 2
LRU_C = 8.0
POOL_WINDOWS = (2, 4, 8, 16)
POOL_GROUP = D_MODEL // len(POOL_WINDOWS)
PEER_HEADS = 8
N_KEYS = 128
PEER_DKEY = 256
PEER_TOPK = 16

SUBLANES = 8
LANES = 128
VMEM_BYTES = 64 * 1024 * 1024
VMEM_LIMIT = VMEM_BYTES - 4 * 1024 * 1024

TM = 256
QK_PAD = 256
HALO = 16
PEER_TM = 1024
PEER_EBLK = 512
PEER_SUB = 256
BF16_ROWS = 16
IN_EXT = 2048


def _rms(x, g):
    return x * lax.rsqrt(jnp.mean(x * x, axis=-1, keepdims=True) + EPS) * g


def _cparams(sem):
    return pltpu.CompilerParams(dimension_semantics=sem, vmem_limit_bytes=VMEM_LIMIT)


def _ada_kernel(c_ref, w_ref, b_ref, o_ref):
    c = c_ref[...]
    o_ref[...] = jnp.dot(c * jax.nn.sigmoid(c), w_ref[...], preferred_element_type=F32) + b_ref[...]


def _ada(cpad, w_mod, b_mod):
    n = w_mod.shape[1]
    bn = 768
    return pl.pallas_call(
        _ada_kernel,
        grid=(n // bn,),
        in_specs=[pl.BlockSpec(cpad.shape, lambda j: (0, 0)),
                  pl.BlockSpec((D_MODEL, bn), lambda j: (0, j)),
                  pl.BlockSpec((1, bn), lambda j: (0, j))],
        out_specs=pl.BlockSpec((cpad.shape[0], bn), lambda j: (0, j)),
        out_shape=jax.ShapeDtypeStruct((cpad.shape[0], n), F32),
        compiler_params=_cparams(("arbitrary",)),
    )(cpad, w_mod, b_mod.reshape(1, n))


def _inproj_kernel(x_ref, mod_ref, g_ref, win_ref, gq_ref, wuq_ref, gkv_ref, ck_ref, sk_ref,
                   q_ref, ckv_ref, kr_ref, krr_ref, ux_ref, ug_ref):
    mod = mod_ref[0]
    h = _rms(x_ref[...], g_ref[...]) * (1.0 + mod[1:2]) + mod[0:1]
    y = jnp.dot(h.astype(BF16), win_ref[...], preferred_element_type=F32)
    cq = y[:, 0:Q_LORA]
    ckv = y[:, Q_LORA:Q_LORA + KV_LORA]
    o = Q_LORA + KV_LORA
    kr = y[:, o:o + QK_ROPE]
    krp = y[:, o + 128:o + 128 + QK_ROPE]
    krs = y[:, o + 256:o + 256 + QK_ROPE]
    ux_ref[...] = y[:, o + 384:o + 384 + LRU_WIDTH]
    ug_ref[...] = y[:, o + 384 + LRU_WIDTH:o + 384 + 2 * LRU_WIDTH]
    ckv_ref[...] = _rms(ckv, gkv_ref[...])
    kr_ref[...] = kr
    ck = ck_ref[...]
    sk = sk_ref[...]
    krr_ref[...] = (krp * ck + krs * sk).astype(BF16)
    q = jnp.dot(_rms(cq, gq_ref[...]).astype(BF16), wuq_ref[...], preferred_element_type=F32)
    nw = MLA_HEADS * QK_NOPE
    rw = MLA_HEADS * QK_ROPE
    for hd in range(MLA_HEADS):
        qp = q[:, nw + hd * QK_ROPE:nw + (hd + 1) * QK_ROPE]
        qs = q[:, nw + rw + hd * QK_ROPE:nw + rw + (hd + 1) * QK_ROPE]
        q_ref[hd, :, 0:QK_NOPE] = q[:, hd * QK_NOPE:(hd + 1) * QK_NOPE].astype(BF16)
        q_ref[hd, :, QK_NOPE:QK_NOPE + QK_ROPE] = (qp * ck + qs * sk).astype(BF16)
        q_ref[hd, :, QK_NOPE + QK_ROPE:QK_PAD] = jnp.zeros((TM, QK_PAD - QK_NOPE - QK_ROPE), BF16)


def _attn_kernel(*refs, has_cache, s_new, n_cache):
    if has_cache:
        q_ref, ckv_ref, krr_ref, cckv_ref, ckr_ref, wukv_ref, o_ref, kcat_ref, vv_ref = refs
    else:
        q_ref, ckv_ref, krr_ref, wukv_ref, o_ref, kcat_ref, vv_ref = refs
    sk = n_cache + s_new
    kw = MLA_HEADS * QK_NOPE
    zpad = jnp.zeros((TM, QK_PAD - QK_NOPE - QK_ROPE), BF16)

    def put_keys(row0, ckv_rows, kr_rows):
        kv = jnp.dot(ckv_rows.astype(BF16), wukv_ref[...], preferred_element_type=F32)
        rows = pl.ds(row0, TM)
        for hd in range(MLA_HEADS):
            kcat_ref[hd, rows, 0:QK_NOPE] = kv[:, hd * QK_NOPE:(hd + 1) * QK_NOPE].astype(BF16)
            kcat_ref[hd, rows, QK_NOPE:QK_NOPE + QK_ROPE] = kr_rows.astype(BF16)
            kcat_ref[hd, rows, QK_NOPE + QK_ROPE:QK_PAD] = zpad
        vv_ref[rows, :] = kv[:, kw:].astype(BF16)

    @pl.when(pl.program_id(1) == 0)
    def _():
        if has_cache:
            for c in range(n_cache // TM):
                put_keys(c * TM, cckv_ref[0, c * TM:(c + 1) * TM, :], ckr_ref[0, c * TM:(c + 1) * TM, :])

        def body(c, carry):
            r0 = pl.multiple_of(c * TM, TM)
            put_keys(n_cache + r0, ckv_ref[pl.ds(r0, TM), :], krr_ref[pl.ds(r0, TM), :])
            return carry
        lax.fori_loop(0, s_new // TM, body, 0)

    scale = (QK_NOPE + QK_ROPE) ** -0.5
    for hd in range(MLA_HEADS):
        s = lax.dot_general(q_ref[hd], kcat_ref[hd], (((1,), (1,)), ((), ())),
                            preferred_element_type=F32) * scale
        m = jnp.max(s, axis=-1, keepdims=True)
        e = jnp.exp(s - m)
        l = jnp.sum(e, axis=-1, keepdims=True)
        o = jnp.dot(e.astype(BF16), vv_ref[:, hd * V_HEAD:(hd + 1) * V_HEAD], preferred_element_type=F32)
        o_ref[:, hd * V_HEAD:(hd + 1) * V_HEAD] = (o / l).astype(BF16)
    del sk


def _lru_dir(xp_ref, xc_ref, xn_ref, valid_prev, valid_next, d, reverse, cw_ref, cb_ref,
             wr_ref, br_ref, wi_ref, bi_ref, lam_ref, carry):
    xp = jnp.where(valid_prev, xp_ref[...], 0.0)
    xn = jnp.where(valid_next, xn_ref[...], 0.0)
    xx = jnp.concatenate([xp, xc_ref[...], xn], axis=0)
    n = TM + 2 * SUBLANES
    xc = cb_ref[...]
    for k in range(CONV_W):
        sh = (CONV_LEFT - k) % n
        xs = xx if sh == 0 else pltpu.roll(xx, sh, 0)
        xc = xc + xs[SUBLANES:SUBLANES + TM] * cw_ref[k:k + 1, :]
    rs, is_ = [], []
    for b in range(LRU_BLOCKS):
        xb = xc[:, b * LRU_BLOCK:(b + 1) * LRU_BLOCK]
        rs.append(jnp.dot(xb, wr_ref[d, b], preferred_element_type=F32))
        is_.append(jnp.dot(xb, wi_ref[d, b], preferred_element_type=F32))
    r = jax.nn.sigmoid(jnp.concatenate(rs, axis=-1) + br_ref[d:d + 1, :])
    i = jax.nn.sigmoid(jnp.concatenate(is_, axis=-1) + bi_ref[d:d + 1, :])
    nl = -lam_ref[d:d + 1, :]
    softplus = jnp.maximum(nl, 0.0) + jnp.log1p(jnp.exp(-jnp.abs(nl)))
    log_a = -LRU_C * r * softplus
    a = jnp.exp(log_a)
    bx = jnp.sqrt(jnp.tanh(-log_a) * (a * a + 1.0)) * (i * xc)
    t = lax.broadcasted_iota(jnp.int32, (TM, 1), 0)
    step = 1
    while step < TM:
        if reverse:
            keep = t < TM - step
            sh = TM - step
        else:
            keep = t >= step
            sh = step
        a_s = jnp.where(keep, pltpu.roll(a, sh, 0), 1.0)
        b_s = jnp.where(keep, pltpu.roll(bx, sh, 0), 0.0)
        bx = a * b_s + bx
        a = a * a_s
        step *= 2
    return a * carry + bx


def _lru_kernel(fxp, fxc, fxn, bxp, bxc, bxn, cw_ref, cb_ref, wr_ref, br_ref, wi_ref, bi_ref, lam_ref,
                h0_ref, hf_ref, hb_ref, st_ref, cf_ref, cbk_ref, *, nc):
    c = pl.program_id(1)

    @pl.when(c == 0)
    def _():
        cf_ref[...] = h0_ref[0, 0:1, :]
        cbk_ref[...] = h0_ref[0, 1:2, :]

    params = (cw_ref, cb_ref, wr_ref, br_ref, wi_ref, bi_ref, lam_ref)
    hf = _lru_dir(fxp, fxc, fxn, c > 0, c < nc - 1, 0, False, *params, cf_ref[...])
    hf_ref[...] = hf
    cf_ref[...] = hf[TM - 1:TM, :]
    hb = _lru_dir(bxp, bxc, bxn, c < nc - 1, c > 0, 1, True, *params, cbk_ref[...])
    hb_ref[...] = hb
    cbk_ref[...] = hb[0:1, :]
    st_ref[0, 0:1, :] = hf[TM - 1:TM, :]
    st_ref[0, 1:2, :] = hb[0:1, :]


def _oproj_kernel(x_ref, mod_ref, at_ref, hf_ref, hb_ref, ug_ref, woa_ref, wor_ref, o_ref):
    mod = mod_ref[0]
    rec = ((hf_ref[...] + hb_ref[...]) * jax.nn.gelu(ug_ref[...])).astype(BF16)
    out = (jnp.dot(at_ref[...], woa_ref[...], preferred_element_type=F32)
           + jnp.dot(rec, wor_ref[...], preferred_element_type=F32))
    o_ref[...] = x_ref[...] + mod[2:3] * out


def _pool_kernel(xp_ref, xc_ref, xn_ref, mod_ref, g_ref, wp_ref, sp_ref, o_ref, *, tiles_per_seq, n_prompt_tiles):
    i = pl.program_id(0)
    j = jnp.where(i < n_prompt_tiles, 0, (i - n_prompt_tiles) % tiles_per_seq)
    ntile = jnp.where(i < n_prompt_tiles, 1, tiles_per_seq)
    mod = mod_ref[0]
    g = g_ref[...]

    def hmod(x):
        return _rms(x, g) * (1.0 + mod[1:2]) + mod[0:1]

    x = xc_ref[...]
    hc = hmod(x)
    hp = jnp.where(j > 0, hmod(xp_ref[...]), 0.0)
    hn = jnp.where(j < ntile - 1, hmod(xn_ref[...]), 0.0)
    hh = jnp.concatenate([hp, hc, hn], axis=0)
    n = TM + 2 * HALO
    seq_len = ntile * TM
    t = j * TM + lax.broadcasted_iota(jnp.int32, (TM, 1), 0)
    ys = []
    for gi, w in enumerate(POOL_WINDOWS):
        cols = slice(gi * POOL_GROUP, (gi + 1) * POOL_GROUP)
        p = hh[:, cols]
        p = p + pltpu.roll(p, 1, 0)
        half = 1
        while 2 * half < w:
            p = pltpu.roll(p, half, 0) + pltpu.roll(p, n - half, 0)
            half *= 2
        lo = jnp.maximum(t - w // 2, 0)
        hi = jnp.minimum(t + (w - w // 2), seq_len)
        mean = p[HALO:HALO + TM] / (hi - lo).astype(F32)
        dg = (mean - hc[:, cols]).astype(BF16)
        ys.append(jnp.dot(dg, wp_ref[gi], preferred_element_type=F32))
    y = jnp.concatenate(ys, axis=-1) * sp_ref[...]
    o_ref[...] = x + mod[2:3] * y


def _sort_pairs(n):
    pairs = []

    def merge(lo, cnt, r):
        step = r * 2
        if step < cnt:
            merge(lo, cnt, step)
            merge(lo + r, cnt, step)
            for i in range(lo + r, lo + cnt - r, step):
                pairs.append((i, i + r))
        else:
            pairs.append((lo, lo + r))

    def sort(lo, cnt):
        if cnt > 1:
            m = cnt // 2
            sort(lo, m)
            sort(lo + m, m)
            merge(lo, cnt, 1)

    sort(0, n)
    return pairs


_SORT16 = _sort_pairs(PEER_TOPK)
_HYPER = [(a, b) for a in range(PEER_TOPK) for b in range(PEER_TOPK) if (a + 1) * (b + 1) <= PEER_TOPK]


def _top16_sorted(s):
    k = PEER_TOPK
    x = [s[SUBLANES * r:SUBLANES * (r + 1), :] for r in range(N_KEYS // SUBLANES)]
    for (i, j) in _SORT16:
        hi = jnp.maximum(x[i], x[j])
        lo = jnp.minimum(x[i], x[j])
        x[i], x[j] = hi, lo
    for shift in (4, 2, 1):
        y = [jnp.maximum(x[r], pltpu.roll(x[k - 1 - r], shift, 0)) for r in range(k)]
        stride = k // 2
        while stride >= 1:
            for i in range(k):
                if i & stride == 0:
                    hi = jnp.maximum(y[i], y[i + stride])
                    lo = jnp.minimum(y[i], y[i + stride])
                    y[i], y[i + stride] = hi, lo
            stride //= 2
        x = y
    return x


def _pair_counts(sv1, sv2):
    one = jnp.ones_like(sv1[0])
    zero = jnp.zeros_like(sv1[0])
    cand = [sv1[a] + sv2[b] for (a, b) in _HYPER]

    def ordered(i, j):
        (ai, bi), (aj, bj) = _HYPER[i], _HYPER[j]
        return (aj <= ai and bj <= bi) or (ai <= aj and bi <= bj)

    nh = len(_HYPER)
    cnt = [float((a + 1) * (b + 1) - 1 + sum(1 for j in range(i + 1, nh) if not ordered(i, j))) * one
           for i, (a, b) in enumerate(_HYPER)]
    for i in range(nh):
        for jx in range(i):
            if ordered(i, jx):
                continue
            ge = jnp.where(cand[jx] >= cand[i], one, zero)
            cnt[i] = cnt[i] + ge
            cnt[jx] = cnt[jx] - ge
    e1 = [jnp.exp(sv1[a] - sv1[0]) for a in range(PEER_TOPK)]
    e2 = [jnp.exp(sv2[b] - sv2[0]) for b in range(PEER_TOPK)]
    n = [zero for _ in range(PEER_TOPK)]
    z = zero
    for i, (a, b) in enumerate(_HYPER):
        sel = jnp.where(cnt[i] < float(PEER_TOPK), one, zero)
        n[a] = n[a] + sel
        z = z + sel * (e1[a] * e2[b])
    return n, 1.0 / z


def _head_tables(hd, g, lanes, sc_ref, sv_ref, nz_ref, ex_ref, cnt1_ref, e1n_ref, tab_ref, exact):
    k = PEER_TOPK

    def bc(ref, *idx):
        return jnp.broadcast_to(ref[idx + (slice(hd, hd + 1), lanes)], (SUBLANES, LANES))

    sv1b = [bc(sv_ref, 0, a) for a in range(k)]
    nb = [bc(nz_ref, a) for a in range(k)]
    izb = bc(nz_ref, k)
    c1 = jnp.zeros((SUBLANES, LANES), F32)
    for r in range(N_KEYS // SUBLANES):
        rows = slice(r * SUBLANES, (r + 1) * SUBLANES)
        s1 = sc_ref[pl.ds(2 * hd * N_KEYS + r * SUBLANES, SUBLANES), lanes]
        cnt1 = jnp.zeros_like(s1)
        if exact:
            rank1 = jnp.zeros_like(s1)
            for a in range(k):
                rank1 = jnp.where(sv1b[a] > s1, float(a + 1), rank1)
            rank1 = rank1 + ex_ref[0, rows, lanes]
            for a in range(k):
                cnt1 = jnp.where(rank1 == float(a), nb[a], cnt1)
        else:
            for a in range(k):
                cnt1 = jnp.where(s1 == sv1b[a], nb[a], cnt1)
            c1 = c1 + jnp.where(s1 >= sv1b[k - 1], 1.0, 0.0)
        cnt1_ref[hd, rows, lanes] = cnt1
        e1n_ref[hd, rows, lanes] = jnp.exp(s1 - sv1b[0]) * izb
    sv2b = [bc(sv_ref, 1, a) for a in range(k)]
    c2 = jnp.zeros((SUBLANES, LANES), F32)
    for r in range(N_KEYS // BF16_ROWS):
        rk, e2 = [], []
        for q in range(BF16_ROWS // SUBLANES):
            r0 = r * BF16_ROWS + q * SUBLANES
            rows = slice(r0, r0 + SUBLANES)
            s2 = sc_ref[pl.ds((2 * hd + 1) * N_KEYS + r0, SUBLANES), lanes]
            rank2 = jnp.zeros_like(s2)
            for a in range(k):
                rank2 = jnp.where(sv2b[a] > s2, float(a + 1), rank2)
            if exact:
                rank2 = jnp.minimum(rank2 + ex_ref[1, rows, lanes], float(k))
            else:
                c2 = c2 + jnp.where(rank2 < float(k), 1.0, 0.0)
            rk.append(rank2)
            e2.append(jnp.exp(s2 - sv2b[0]))
        tab_ref[r, g, hd, 0] = jnp.concatenate(rk, axis=0).astype(BF16)
        tab_ref[r, g, hd, 1] = jnp.concatenate(e2, axis=0).astype(BF16)
    if exact:
        return None
    ex_ref[0, 0:SUBLANES, lanes] = c1
    ex_ref[1, 0:SUBLANES, lanes] = c2
    b = jnp.zeros((SUBLANES, LANES), F32)
    for svb in (sv1b, sv2b):
        for a in range(k - 1):
            b = b + jnp.where(svb[a] == svb[a + 1], 1.0, 0.0)
    return b


def _tie_offsets(hd, sc_ref, ex_ref):
    tm = sc_ref.shape[-1]
    nidx = lax.broadcasted_iota(jnp.int32, (N_KEYS, tm), 0)
    for p in range(2):
        s = sc_ref[pl.ds((2 * hd + p) * N_KEYS, N_KEYS), :]

        def body(m, e, s=s, p=p):
            row = sc_ref[pl.ds((2 * hd + p) * N_KEYS + m, 1), :]
            return e + jnp.where((s == row) & (nidx > m), 1.0, 0.0)

        ex_ref[p] = lax.fori_loop(0, N_KEYS, body, jnp.zeros((N_KEYS, tm), F32))


def _peer_prologue(x_ref, mod_ref, g_ref, wqt_ref, keys_ref, h2t_ref, tab_ref, sc_ref, sv_ref, nz_ref,
                   ex_ref, cnt1_ref, e1n_ref, y_ref):
    tm = PEER_TM
    half = PEER_DKEY // 2
    ngroup = tm // LANES
    mod = mod_ref[0]
    h2 = _rms(x_ref[...], g_ref[...]) * (1.0 + mod[4:5]) + mod[3:4]
    h2t_ref[...] = h2.T.astype(BF16)

    def lane_group(g):
        return pl.ds(pl.multiple_of(g * LANES, LANES), LANES)

    nslot, qrows = y_ref.shape[0], y_ref.shape[1]
    per_pass = nslot * qrows // half
    for i in range(2 * PEER_HEADS):
        if i % per_pass == 0:
            qt = jnp.dot(wqt_ref[i * half:(i + per_pass) * half, :], h2t_ref[...],
                         preferred_element_type=F32).astype(BF16)
            for s in range(nslot):
                y_ref[s] = qt[s * qrows:(s + 1) * qrows, :]
        j = (i % per_pass) * half
        sc_ref[i * N_KEYS:(i + 1) * N_KEYS, :] = jnp.dot(
            keys_ref[i], y_ref[j // qrows, j % qrows:j % qrows + half, :],
            preferred_element_type=F32)

        def group(g, c, i=i):
            lanes = lane_group(g)
            top = _top16_sorted(sc_ref[i * N_KEYS:(i + 1) * N_KEYS, lanes])
            for a in range(PEER_TOPK):
                sv_ref[i % 2, a, i // 2:i // 2 + 1, lanes] = top[a][0:1, :]
            return c
        lax.fori_loop(0, ngroup, group, 0)

    def counts(g, carry):
        lanes = lane_group(g)
        sv1 = [sv_ref[0, a, :, lanes] for a in range(PEER_TOPK)]
        sv2 = [sv_ref[1, a, :, lanes] for a in range(PEER_TOPK)]
        n, inv_z = _pair_counts(sv1, sv2)
        for a in range(PEER_TOPK):
            nz_ref[a, :, lanes] = n[a]
        nz_ref[PEER_TOPK, :, lanes] = inv_z
        return carry
    lax.fori_loop(0, ngroup, counts, 0)

    tabs = (sc_ref, sv_ref, nz_ref, ex_ref, cnt1_ref, e1n_ref, tab_ref)

    for hd in range(PEER_HEADS):
        def fast(g, bad, hd=hd):
            return bad + _head_tables(hd, g, lane_group(g), *tabs, exact=False)
        bad = lax.fori_loop(0, ngroup, fast, jnp.zeros((SUBLANES, LANES), F32))
        totals = jnp.sum(ex_ref[:, 0:SUBLANES, :], axis=1)
        ties = jnp.max(bad) + jnp.max(jnp.abs(totals - float(PEER_TOPK)))

        @pl.when(ties > 0.0)
        def _(hd=hd):
            _tie_offsets(hd, sc_ref, ex_ref)

            def slow(g, c):
                _head_tables(hd, g, lane_group(g), *tabs, exact=True)
                return c
            lax.fori_loop(0, ngroup, slow, 0)
    sc_ref[0:D_MODEL, :] = jnp.zeros((D_MODEL, tm), F32)


def _peer_kernel(x_ref, mod_ref, g_ref, wqt_ref, keys_ref, u0_ref, u_ref, vt_ref, gfin_ref, o_ref,
                 h2t_ref, tab_ref, sc_ref, sv_ref, nz_ref, ex_ref, cnt1_ref, e1n_ref, raw_ref, y_ref,
                 *, final_norm):
    k = pl.program_id(1)
    tm = PEER_TM
    acc_ref = sc_ref.at[0:D_MODEL]

    nstep = pl.num_programs(1) - 1
    zero = jnp.zeros((BF16_ROWS, LANES), BF16)
    npair = PEER_EBLK // (2 * N_KEYS)
    cur = k % 2
    oth = (k + 1) % 2

    def up_next():
        raw_ref[oth] = jnp.dot(u_ref[...], h2t_ref[...], preferred_element_type=F32)

    def down_prev():
        acc_ref[...] += jnp.dot(vt_ref[0], y_ref[oth], preferred_element_type=F32)

    def gate_block():
        for q in range(npair):
            i1 = (k * npair + q) * 2
            crow = [[cnt1_ref[hd, pl.ds(i1 + j, 1), :] for hd in range(PEER_HEADS)] for j in range(2)]
            erow = [[e1n_ref[hd, pl.ds(i1 + j, 1), :] for hd in range(PEER_HEADS)] for j in range(2)]
            for g in range(tm // LANES):
                lanes = slice(g * LANES, (g + 1) * LANES)
                cbs = [[jnp.broadcast_to(crow[j][hd][:, lanes], (BF16_ROWS, LANES)).astype(BF16)
                        for hd in range(PEER_HEADS)] for j in range(2)]
                ebs = [[jnp.broadcast_to(erow[j][hd][:, lanes], (BF16_ROWS, LANES)).astype(BF16)
                        for hd in range(PEER_HEADS)] for j in range(2)]
                for r in range(N_KEYS // BF16_ROWS):
                    w = [None, None]
                    for hd in range(PEER_HEADS):
                        rk = tab_ref[r, g, hd, 0]
                        ev = tab_ref[r, g, hd, 1]
                        for j in range(2):
                            t = jnp.where(rk < cbs[j][hd], ev, zero) * ebs[j][hd]
                            w[j] = t if w[j] is None else w[j] + t
                    for j in range(2):
                        rows = pl.ds((2 * q + j) * N_KEYS + r * BF16_ROWS, BF16_ROWS)
                        act = jax.nn.gelu(raw_ref[cur, rows, lanes].astype(BF16))
                        y_ref[cur, rows, lanes] = act * w[j]

    @pl.when(k == 0)
    def _():
        _peer_prologue(x_ref, mod_ref, g_ref, wqt_ref, keys_ref, h2t_ref, tab_ref, sc_ref, sv_ref, nz_ref,
                       ex_ref, cnt1_ref, e1n_ref, y_ref)
        raw_ref[0] = jnp.dot(u0_ref[...], h2t_ref[...], preferred_element_type=F32)
        up_next()
        gate_block()

    @pl.when((k > 0) & (k < nstep))
    def _():
        up_next()
        down_prev()
        gate_block()

    @pl.when(k == nstep)
    def _():
        down_prev()
        out = x_ref[...] + mod_ref[0][5:6] * acc_ref[...].T
        if final_norm:
            out = _rms(out, gfin_ref[...])
        o_ref[...] = out


def _rope_tables(s_prompt_tile, s_sample):
    n_rows = s_sample // GRID_W
    rows = jnp.repeat(jnp.arange(n_rows, dtype=F32), GRID_W)
    cols = jnp.tile(jnp.arange(GRID_W, dtype=F32), n_rows)
    axis_dim = QK_ROPE // 2
    inv_freq = ROPE_BASE ** (-jnp.arange(0, axis_dim, 2, dtype=F32) / axis_dim)
    ang = jnp.concatenate([rows[:, None] * inv_freq, cols[:, None] * inv_freq], axis=-1)
    cos, sin = jnp.cos(ang), jnp.sin(ang)
    ck = jnp.concatenate([cos, cos], axis=-1)
    sk = jnp.concatenate([-sin, sin], axis=-1)
    ident_c = jnp.ones((s_prompt_tile, QK_ROPE), F32)
    ident_s = jnp.zeros((s_prompt_tile, QK_ROPE), F32)
    return jnp.concatenate([ident_c, ck], axis=0), jnp.concatenate([ident_s, sk], axis=0)


def kernel(x_prompt, x_sample, cache_ckv_l0, cache_krope_l0, state_lru_l0, c, c_ctx, w_mod_l0, b_mod_l0, w_mod_l1, b_mod_l1, g_mix_l0, g_ffn_l0, g_mix_l1, g_ffn_l1, w_in_l0, g_q_l0, w_uq_l0, g_kv_l0, w_ukv_l0, conv_w_l0, conv_b_l0, w_rg_l0, b_rg_l0, w_ig_l0, b_ig_l0, lam_l0, w_o_l0, w_pool_l1, s_pool_l1, peer_wq_l0, peer_keys_l0, peer_u_l0, peer_v_l0, peer_wq_l1, peer_keys_l1, peer_u_l1, peer_v_l1, g_final):
    nb_p, s_p, d = x_prompt.shape
    nb_s, s_s, _ = x_sample.shape
    n_cache = cache_ckv_l0.shape[1]
    assert d == D_MODEL and s_p == TM and s_s % TM == 0 and n_cache % TM == 0
    t_p = nb_p * s_p
    t_s = nb_s * s_s
    t_all = t_p + t_s
    npt = t_p // TM
    tps = s_s // TM
    ntile = t_all // TM
    assert t_all % PEER_TM == 0 and t_p % PEER_TM == 0 and s_s % PEER_TM == 0

    x0 = jnp.concatenate([x_prompt.reshape(t_p, d), x_sample.reshape(t_s, d)], axis=0)

    ncond = 1 + nb_s
    cpad = jnp.zeros((2 * SUBLANES, d), F32).at[0].set(c_ctx).at[1:ncond].set(c)
    mod0 = _ada(cpad, w_mod_l0, b_mod_l0).reshape(2 * SUBLANES, 6, d)
    mod1 = _ada(cpad, w_mod_l1, b_mod_l1).reshape(2 * SUBLANES, 6, d)

    def cond_row(i):
        return jnp.where(i < npt, 0, 1 + (i - npt) // tps)

    mod_spec = pl.BlockSpec((1, 6, d), lambda i: (cond_row(i), 0, 0))
    row = lambda a: a.reshape(1, -1)

    perm = np.concatenate([np.arange(0, QK_ROPE, 2), np.arange(1, QK_ROPE, 2)])
    perm_sw = np.concatenate([np.arange(1, QK_ROPE, 2), np.arange(0, QK_ROPE, 2)])
    o1 = Q_LORA + KV_LORA
    w_kr = w_in_l0[:, o1:o1 + QK_ROPE]
    z64 = jnp.zeros((d, 128 - QK_ROPE), F32)
    w_in_ext = jnp.concatenate(
        [w_in_l0[:, :o1], w_kr, z64, w_kr[:, perm], z64, w_kr[:, perm_sw], z64, w_in_l0[:, o1 + QK_ROPE:]],
        axis=1).astype(BF16)
    assert w_in_ext.shape[1] == IN_EXT
    wq3 = w_uq_l0.reshape(Q_LORA, MLA_HEADS, QK_NOPE + QK_ROPE)
    w_uq_ext = jnp.concatenate(
        [wq3[:, :, :QK_NOPE].reshape(Q_LORA, -1),
         wq3[:, :, QK_NOPE:][:, :, perm].reshape(Q_LORA, -1),
         wq3[:, :, QK_NOPE:][:, :, perm_sw].reshape(Q_LORA, -1)], axis=1).astype(BF16)
    ck_tab, sk_tab = _rope_tables(TM, s_s)

    def rope_blk(i):
        return jnp.where(i < npt, 0, 1 + (i - npt) % tps)

    tok = lambda w: pl.BlockSpec((TM, w), lambda i: (i, 0))
    full = lambda a: pl.BlockSpec(a.shape, lambda *_: (0,) * a.ndim)
    q, ckv, kr, krr, ux, ug = pl.pallas_call(
        _inproj_kernel,
        grid=(ntile,),
        in_specs=[tok(d), mod_spec, full(row(g_mix_l0)), full(w_in_ext), full(row(g_q_l0)), full(w_uq_ext),
                  full(row(g_kv_l0)),
                  pl.BlockSpec((TM, QK_ROPE), lambda i: (rope_blk(i), 0)),
                  pl.BlockSpec((TM, QK_ROPE), lambda i: (rope_blk(i), 0))],
        out_specs=[pl.BlockSpec((MLA_HEADS, TM, QK_PAD), lambda i: (0, i, 0)),
                   tok(KV_LORA), tok(QK_ROPE), tok(QK_ROPE), tok(LRU_WIDTH), tok(LRU_WIDTH)],
        out_shape=[jax.ShapeDtypeStruct((MLA_HEADS, t_all, QK_PAD), BF16),
                   jax.ShapeDtypeStruct((t_all, KV_LORA), F32),
                   jax.ShapeDtypeStruct((t_all, QK_ROPE), F32),
                   jax.ShapeDtypeStruct((t_all, QK_ROPE), BF16),
                   jax.ShapeDtypeStruct((t_all, LRU_WIDTH), F32),
                   jax.ShapeDtypeStruct((t_all, LRU_WIDTH), F32)],
        compiler_params=_cparams(("arbitrary",)),
    )(x0, mod0, row(g_mix_l0), w_in_ext, row(g_q_l0), w_uq_ext, row(g_kv_l0), ck_tab, sk_tab)

    wkv3 = w_ukv_l0.reshape(KV_LORA, MLA_HEADS, QK_NOPE + V_HEAD)
    w_ukv_ext = jnp.concatenate([wkv3[:, :, :QK_NOPE].reshape(KV_LORA, -1),
                                 wkv3[:, :, QK_NOPE:].reshape(KV_LORA, -1)], axis=1).astype(BF16)

    def attn_call(nb, s_new, tile0, has_cache):
        nq = s_new // TM
        blk0 = tile0 * TM // s_new
        n_c = n_cache if has_cache else 0
        in_specs = [pl.BlockSpec((MLA_HEADS, TM, QK_PAD), lambda b, qi: (0, tile0 + b * nq + qi, 0)),
                    pl.BlockSpec((s_new, KV_LORA), lambda b, qi: (blk0 + b, 0)),
                    pl.BlockSpec((s_new, QK_ROPE), lambda b, qi: (blk0 + b, 0))]
        args = [q, ckv, krr]
        if has_cache:
            in_specs += [pl.BlockSpec((1, n_cache, KV_LORA), lambda b, qi: (b, 0, 0)),
                         pl.BlockSpec((1, n_cache, QK_ROPE), lambda b, qi: (b, 0, 0))]
            args += [cache_ckv_l0, cache_krope_l0[:, :, perm]]
        in_specs.append(pl.BlockSpec(w_ukv_ext.shape, lambda b, qi: (0, 0)))
        args.append(w_ukv_ext)
        return pl.pallas_call(
            functools.partial(_attn_kernel, has_cache=has_cache, s_new=s_new, n_cache=n_c),
            grid=(nb, nq),
            in_specs=in_specs,
            out_specs=pl.BlockSpec((TM, MLA_WIDTH), lambda b, qi: (b * nq + qi, 0)),
            out_shape=jax.ShapeDtypeStruct((nb * s_new, MLA_WIDTH), BF16),
            scratch_shapes=[pltpu.VMEM((MLA_HEADS, n_c + s_new, QK_PAD), BF16),
                            pltpu.VMEM((n_c + s_new, MLA_WIDTH), BF16)],
            compiler_params=_cparams(("arbitrary", "arbitrary")),
        )(*args)

    attn = jnp.concatenate([attn_call(nb_p, s_p, 0, False), attn_call(nb_s, s_s, npt, True)], axis=0)

    def lru_call(nb, s_new, tile0, h0):
        nc = s_new // TM
        r8 = TM // SUBLANES
        last8 = t_all // SUBLANES - 1

        def cur(rev):
            return pl.BlockSpec((TM, LRU_WIDTH),
                                lambda b, cc: (tile0 + b * nc + (nc - 1 - cc if rev else cc), 0))

        def prev(rev):
            return pl.BlockSpec((SUBLANES, LRU_WIDTH), lambda b, cc: (
                jnp.maximum((tile0 + b * nc + (nc - 1 - cc if rev else cc)) * r8 - 1, 0), 0))

        def nxt(rev):
            return pl.BlockSpec((SUBLANES, LRU_WIDTH), lambda b, cc: (
                jnp.minimum((tile0 + b * nc + (nc - 1 - cc if rev else cc) + 1) * r8, last8), 0))

        small = [conv_w_l0, row(conv_b_l0), w_rg_l0, b_rg_l0, w_ig_l0, b_ig_l0, lam_l0]
        return pl.pallas_call(
            functools.partial(_lru_kernel, nc=nc),
            grid=(nb, nc),
            in_specs=[prev(False), cur(False), nxt(False), prev(True), cur(True), nxt(True)]
                     + [full(a) for a in small]
                     + [pl.BlockSpec((1, 2, LRU_WIDTH), lambda b, cc: (b, 0, 0))],
            out_specs=[pl.BlockSpec((TM, LRU_WIDTH), lambda b, cc: (b * nc + cc, 0)),
                       pl.BlockSpec((TM, LRU_WIDTH), lambda b, cc: (b * nc + nc - 1 - cc, 0)),
                       pl.BlockSpec((1, 2, LRU_WIDTH), lambda b, cc: (b, 0, 0))],
            out_shape=[jax.ShapeDtypeStruct((nb * s_new, LRU_WIDTH), F32),
                       jax.ShapeDtypeStruct((nb * s_new, LRU_WIDTH), F32),
                       jax.ShapeDtypeStruct((nb, 2, LRU_WIDTH), F32)],
            scratch_shapes=[pltpu.VMEM((1, LRU_WIDTH), F32), pltpu.VMEM((1, LRU_WIDTH), F32)],
            compiler_params=_cparams(("arbitrary", "arbitrary")),
        )(ux, ux, ux, ux, ux, ux, *small, h0)

    hf_p, hb_p, new_lru = lru_call(nb_p, s_p, 0, jnp.zeros((nb_p, 2, LRU_WIDTH), F32))
    hf_s, hb_s, _ = lru_call(nb_s, s_s, npt, state_lru_l0.astype(F32))
    hf = jnp.concatenate([hf_p, hf_s], axis=0)
    hb = jnp.concatenate([hb_p, hb_s], axis=0)

    w_o = w_o_l0.astype(BF16)
    x1 = pl.pallas_call(
        _oproj_kernel,
        grid=(ntile,),
        in_specs=[tok(d), mod_spec, tok(MLA_WIDTH), tok(LRU_WIDTH), tok(LRU_WIDTH), tok(LRU_WIDTH),
                  pl.BlockSpec((MLA_WIDTH, d), lambda i: (0, 0)), pl.BlockSpec((LRU_WIDTH, d), lambda i: (1, 0))],
        out_specs=tok(d),
        out_shape=jax.ShapeDtypeStruct((t_all, d), F32),
        compiler_params=_cparams(("arbitrary",)),
    )(x0, mod0, attn, hf, hb, ug, w_o, w_o)

    def peer_call(x, mod, g_ffn, w_q, sub_keys, u, v, final_norm):
        n_exp = u.shape[0]
        assert n_exp == N_KEYS * N_KEYS and n_exp % PEER_EBLK == 0
        wqt = w_q.T.astype(BF16)
        keys = sub_keys.astype(BF16).reshape(2 * PEER_HEADS, N_KEYS, PEER_DKEY // 2)
        ub = u.astype(BF16)
        vt = v.reshape(n_exp // PEER_EBLK, PEER_EBLK, d).transpose(0, 2, 1).astype(BF16)
        tpp = PEER_TM // TM

        nblk = n_exp // PEER_EBLK

        def cond_row_p(i):
            return cond_row(i * tpp)

        big = lambda dt: pltpu.VMEM((PEER_HEADS, N_KEYS, PEER_TM), dt)
        return pl.pallas_call(
            functools.partial(_peer_kernel, final_norm=final_norm),
            grid=(t_all // PEER_TM, nblk + 1),
            in_specs=[pl.BlockSpec((PEER_TM, d), lambda i, k: (i, 0), pipeline_mode=pl.Buffered(1)),
                      pl.BlockSpec((1, 6, d), lambda i, k: (cond_row_p(i), 0, 0)),
                      pl.BlockSpec((1, d), lambda i, k: (0, 0)),
                      pl.BlockSpec(wqt.shape, lambda i, k: (0, 0)),
                      pl.BlockSpec(keys.shape, lambda i, k: (0, 0, 0)),
                      pl.BlockSpec((PEER_EBLK, d), lambda i, k: (0, 0), pipeline_mode=pl.Buffered(1)),
                      pl.BlockSpec((PEER_EBLK, d), lambda i, k: (jnp.minimum(k + 1, nblk - 1), 0)),
                      pl.BlockSpec((1, d, PEER_EBLK), lambda i, k: (jnp.maximum(k - 1, 0), 0, 0)),
                      pl.BlockSpec((1, d), lambda i, k: (0, 0))],
            out_specs=pl.BlockSpec((PEER_TM, d), lambda i, k: (i, 0), pipeline_mode=pl.Buffered(1)),
            out_shape=jax.ShapeDtypeStruct((t_all, d), F32),
            scratch_shapes=[pltpu.VMEM((d, PEER_TM), BF16),
                            pltpu.VMEM((N_KEYS // BF16_ROWS, PEER_TM // LANES, PEER_HEADS, 2, BF16_ROWS, LANES),
                                       BF16),
                            pltpu.VMEM((2 * PEER_HEADS * N_KEYS, PEER_TM), F32),
                            pltpu.VMEM((2, PEER_TOPK, PEER_HEADS, PEER_TM), F32),
                            pltpu.VMEM((PEER_TOPK + 1, PEER_HEADS, PEER_TM), F32),
                            pltpu.VMEM((2, N_KEYS, PEER_TM), F32),
                            big(F32), big(F32),
                            pltpu.VMEM((2, PEER_EBLK, PEER_TM), F32),
                            pltpu.VMEM((2, PEER_EBLK, PEER_TM), BF16)],
            compiler_params=_cparams(("arbitrary", "arbitrary")),
        )(x, mod, row(g_ffn), wqt, keys, ub, ub, vt, row(g_final))

    x2 = peer_call(x1, mod0, g_ffn_l0, peer_wq_l0, peer_keys_l0, peer_u_l0, peer_v_l0, False)

    rh = TM // HALO
    lasth = t_all // HALO - 1
    x3 = pl.pallas_call(
        functools.partial(_pool_kernel, tiles_per_seq=tps, n_prompt_tiles=npt),
        grid=(ntile,),
        in_specs=[pl.BlockSpec((HALO, d), lambda i: (jnp.maximum(i * rh - 1, 0), 0)),
                  tok(d),
                  pl.BlockSpec((HALO, d), lambda i: (jnp.minimum((i + 1) * rh, lasth), 0)),
                  mod_spec, full(row(g_mix_l1)),
                  pl.BlockSpec(w_pool_l1.shape, lambda i: (0, 0, 0)), full(row(s_pool_l1))],
        out_specs=tok(d),
        out_shape=jax.ShapeDtypeStruct((t_all, d), F32),
        compiler_params=_cparams(("arbitrary",)),
    )(x2, x2, x2, mod1, row(g_mix_l1), w_pool_l1.astype(BF16), row(s_pool_l1))

    y = peer_call(x3, mod1, g_ffn_l1, peer_wq_l1, peer_keys_l1, peer_u_l1, peer_v_l1, True)

    y_prompt = y[:t_p].reshape(nb_p, s_p, d)
    y_sample = y[t_p:].reshape(nb_s, s_s, d)
    new_ckv = ckv[:t_p].reshape(nb_p, s_p, KV_LORA)
    new_krope = kr[:t_p].reshape(nb_p, s_p, QK_ROPE)
    return (y_prompt, y_sample, new_ckv, new_krope, new_lru)
```

```python
import functools

import numpy as np
import jax
import jax.numpy as jnp
from jax import lax
from jax.experimental import pallas as pl
from jax.experimental.pallas import tpu as pltpu

F32 = jnp.float32
BF16 = jnp.bfloat16

D_MODEL = 1024
EPS = 1e-6
GRID_W = 64
MLA_HEADS = 4
Q_LORA = 384
KV_LORA = 256
QK_NOPE = 128
QK_ROPE = 64
V_HEAD = 128
MLA_WIDTH = MLA_HEADS * V_HEAD
ROPE_BASE = 10000.0
LRU_WIDTH = 512
LRU_BLOCKS = 4
LRU_BLOCK = LRU_WIDTH // LRU_BLOCKS
CONV_W = 4
CONV_LEFT = 2
LRU_C = 8.0
POOL_WINDOWS = (2, 4, 8, 16)
POOL_GROUP = D_MODEL // len(POOL_WINDOWS)
PEER_HEADS = 8
N_KEYS = 128
PEER_DKEY = 256
PEER_TOPK = 16

SUBLANES = 8
LANES = 128
VMEM_BYTES = 64 * 1024 * 1024
VMEM_LIMIT = VMEM_BYTES - 4 * 1024 * 1024

TM = 256
QK_PAD = 256
HALO = 16
PEER_TM = 1024
PEER_EBLK = 512
PEER_SUB = 256
BF16_ROWS = 16
IN_EXT = 2048


def _rms(x, g):
    return x * lax.rsqrt(jnp.mean(x * x, axis=-1, keepdims=True) + EPS) * g


def _cparams(sem):
    return pltpu.CompilerParams(dimension_semantics=sem, vmem_limit_bytes=VMEM_LIMIT)


def _ada_kernel(c_ref, w_ref, b_ref, o_ref):
    c = c_ref[...]
    o_ref[...] = jnp.dot(c * jax.nn.sigmoid(c), w_ref[...], preferred_element_type=F32) + b_ref[...]


def _ada(cpad, w_mod, b_mod):
    n = w_mod.shape[1]
    bn = 768
    return pl.pallas_call(
        _ada_kernel,
        grid=(n // bn,),
        in_specs=[pl.BlockSpec(cpad.shape, lambda j: (0, 0)),
                  pl.BlockSpec((D_MODEL, bn), lambda j: (0, j)),
                  pl.BlockSpec((1, bn), lambda j: (0, j))],
        out_specs=pl.BlockSpec((cpad.shape[0], bn), lambda j: (0, j)),
        out_shape=jax.ShapeDtypeStruct((cpad.shape[0], n), F32),
        compiler_params=_cparams(("arbitrary",)),
    )(cpad, w_mod, b_mod.reshape(1, n))


def _inproj_kernel(x_ref, mod_ref, g_ref, win_ref, gq_ref, wuq_ref, gkv_ref, ck_ref, sk_ref,
                   q_ref, ckv_ref, kr_ref, krr_ref, ux_ref, ug_ref):
    mod = mod_ref[0]
    h = _rms(x_ref[...], g_ref[...]) * (1.0 + mod[1:2]) + mod[0:1]
    y = jnp.dot(h.astype(BF16), win_ref[...], preferred_element_type=F32)
    cq = y[:, 0:Q_LORA]
    ckv = y[:, Q_LORA:Q_LORA + KV_LORA]
    o = Q_LORA + KV_LORA
    kr = y[:, o:o + QK_ROPE]
    krp = y[:, o + 128:o + 128 + QK_ROPE]
    krs = y[:, o + 256:o + 256 + QK_ROPE]
    ux_ref[...] = y[:, o + 384:o + 384 + LRU_WIDTH]
    ug_ref[...] = y[:, o + 384 + LRU_WIDTH:o + 384 + 2 * LRU_WIDTH]
    ckv_ref[...] = _rms(ckv, gkv_ref[...])
    kr_ref[...] = kr
    ck = ck_ref[...]
    sk = sk_ref[...]
    krr_ref[...] = (krp * ck + krs * sk).astype(BF16)
    q = jnp.dot(_rms(cq, gq_ref[...]).astype(BF16), wuq_ref[...], preferred_element_type=F32)
    nw = MLA_HEADS * QK_NOPE
    rw = MLA_HEADS * QK_ROPE
    for hd in range(MLA_HEADS):
        qp = q[:, nw + hd * QK_ROPE:nw + (hd + 1) * QK_ROPE]
        qs = q[:, nw + rw + hd * QK_ROPE:nw + rw + (hd + 1) * QK_ROPE]
        q_ref[hd, :, 0:QK_NOPE] = q[:, hd * QK_NOPE:(hd + 1) * QK_NOPE].astype(BF16)
        q_ref[hd, :, QK_NOPE:QK_NOPE + QK_ROPE] = (qp * ck + qs * sk).astype(BF16)
        q_ref[hd, :, QK_NOPE + QK_ROPE:QK_PAD] = jnp.zeros((TM, QK_PAD - QK_NOPE - QK_ROPE), BF16)


def _attn_kernel(*refs, has_cache, s_new, n_cache):
    if has_cache:
        q_ref, ckv_ref, krr_ref, cckv_ref, ckr_ref, wukv_ref, o_ref, kcat_ref, vv_ref = refs
    else:
        q_ref, ckv_ref, krr_ref, wukv_ref, o_ref, kcat_ref, vv_ref = refs
    sk = n_cache + s_new
    kw = MLA_HEADS * QK_NOPE
    zpad = jnp.zeros((TM, QK_PAD - QK_NOPE - QK_ROPE), BF16)

    def put_keys(row0, ckv_rows, kr_rows):
        kv = jnp.dot(ckv_rows.astype(BF16), wukv_ref[...], preferred_element_type=F32)
        rows = pl.ds(row0, TM)
        for hd in range(MLA_HEADS):
            kcat_ref[hd, rows, 0:QK_NOPE] = kv[:, hd * QK_NOPE:(hd + 1) * QK_NOPE].astype(BF16)
            kcat_ref[hd, rows, QK_NOPE:QK_NOPE + QK_ROPE] = kr_rows.astype(BF16)
            kcat_ref[hd, rows, QK_NOPE + QK_ROPE:QK_PAD] = zpad
        vv_ref[rows, :] = kv[:, kw:].astype(BF16)

    @pl.when(pl.program_id(1) == 0)
    def _():
        if has_cache:
            for c in range(n_cache // TM):
                put_keys(c * TM, cckv_ref[0, c * TM:(c + 1) * TM, :], ckr_ref[0, c * TM:(c + 1) * TM, :])

        def body(c, carry):
            r0 = pl.multiple_of(c * TM, TM)
            put_keys(n_cache + r0, ckv_ref[pl.ds(r0, TM), :], krr_ref[pl.ds(r0, TM), :])
            return carry
        lax.fori_loop(0, s_new // TM, body, 0)

    scale = (QK_NOPE + QK_ROPE) ** -0.5
    for hd in range(MLA_HEADS):
        s = lax.dot_general(q_ref[hd], kcat_ref[hd], (((1,), (1,)), ((), ())),
                            preferred_element_type=F32) * scale
        m = jnp.max(s, axis=-1, keepdims=True)
        e = jnp.exp(s - m)
        l = jnp.sum(e, axis=-1, keepdims=True)
        o = jnp.dot(e.astype(BF16), vv_ref[:, hd * V_HEAD:(hd + 1) * V_HEAD], preferred_element_type=F32)
        o_ref[:, hd * V_HEAD:(hd + 1) * V_HEAD] = (o / l).astype(BF16)
    del sk


def _lru_dir(xp_ref, xc_ref, xn_ref, valid_prev, valid_next, d, reverse, cw_ref, cb_ref,
             wr_ref, br_ref, wi_ref, bi_ref, lam_ref, carry):
    xp = jnp.where(valid_prev, xp_ref[...], 0.0)
    xn = jnp.where(valid_next, xn_ref[...], 0.0)
    xx = jnp.concatenate([xp, xc_ref[...], xn], axis=0)
    n = TM + 2 * SUBLANES
    xc = cb_ref[...]
    for k in range(CONV_W):
        sh = (CONV_LEFT - k) % n
        xs = xx if sh == 0 else pltpu.roll(xx, sh, 0)
        xc = xc + xs[SUBLANES:SUBLANES + TM] * cw_ref[k:k + 1, :]
    rs, is_ = [], []
    for b in range(LRU_BLOCKS):
        xb = xc[:, b * LRU_BLOCK:(b + 1) * LRU_BLOCK]
        rs.append(jnp.dot(xb, wr_ref[d, b], preferred_element_type=F32))
        is_.append(jnp.dot(xb, wi_ref[d, b], preferred_element_type=F32))
    r = jax.nn.sigmoid(jnp.concatenate(rs, axis=-1) + br_ref[d:d + 1, :])
    i = jax.nn.sigmoid(jnp.concatenate(is_, axis=-1) + bi_ref[d:d + 1, :])
    nl = -lam_ref[d:d + 1, :]
    softplus = jnp.maximum(nl, 0.0) + jnp.log1p(jnp.exp(-jnp.abs(nl)))
    log_a = -LRU_C * r * softplus
    a = jnp.exp(log_a)
    bx = jnp.sqrt(jnp.tanh(-log_a) * (a * a + 1.0)) * (i * xc)
    t = lax.broadcasted_iota(jnp.int32, (TM, 1), 0)
    step = 1
    while step < TM:
        if reverse:
            keep = t < TM - step
            sh = TM - step
        else:
            keep = t >= step
            sh = step
        a_s = jnp.where(keep, pltpu.roll(a, sh, 0), 1.0)
        b_s = jnp.where(keep, pltpu.roll(bx, sh, 0), 0.0)
        bx = a * b_s + bx
        a = a * a_s
        step *= 2
    return a * carry + bx


def _lru_kernel(fxp, fxc, fxn, bxp, bxc, bxn, cw_ref, cb_ref, wr_ref, br_ref, wi_ref, bi_ref, lam_ref,
                h0_ref, hf_ref, hb_ref, st_ref, cf_ref, cbk_ref, *, nc):
    c = pl.program_id(1)

    @pl.when(c == 0)
    def _():
        cf_ref[...] = h0_ref[0, 0:1, :]
        cbk_ref[...] = h0_ref[0, 1:2, :]

    params = (cw_ref, cb_ref, wr_ref, br_ref, wi_ref, bi_ref, lam_ref)
    hf = _lru_dir(fxp, fxc, fxn, c > 0, c < nc - 1, 0, False, *params, cf_ref[...])
    hf_ref[...] = hf
    cf_ref[...] = hf[TM - 1:TM, :]
    hb = _lru_dir(bxp, bxc, bxn, c < nc - 1, c > 0, 1, True, *params, cbk_ref[...])
    hb_ref[...] = hb
    cbk_ref[...] = hb[0:1, :]
    st_ref[0, 0:1, :] = hf[TM - 1:TM, :]
    st_ref[0, 1:2, :] = hb[0:1, :]


def _oproj_kernel(x_ref, mod_ref, at_ref, hf_ref, hb_ref, ug_ref, woa_ref, wor_ref, o_ref):
    mod = mod_ref[0]
    rec = ((hf_ref[...] + hb_ref[...]) * jax.nn.gelu(ug_ref[...])).astype(BF16)
    out = (jnp.dot(at_ref[...], woa_ref[...], preferred_element_type=F32)
           + jnp.dot(rec, wor_ref[...], preferred_element_type=F32))
    o_ref[...] = x_ref[...] + mod[2:3] * out


def _pool_kernel(xp_ref, xc_ref, xn_ref, mod_ref, g_ref, wp_ref, sp_ref, o_ref, *, tiles_per_seq, n_prompt_tiles):
    i = pl.program_id(0)
    j = jnp.where(i < n_prompt_tiles, 0, (i - n_prompt_tiles) % tiles_per_seq)
    ntile = jnp.where(i < n_prompt_tiles, 1, tiles_per_seq)
    mod = mod_ref[0]
    g = g_ref[...]

    def hmod(x):
        return _rms(x, g) * (1.0 + mod[1:2]) + mod[0:1]

    x = xc_ref[...]
    hc = hmod(x)
    hp = jnp.where(j > 0, hmod(xp_ref[...]), 0.0)
    hn = jnp.where(j < ntile - 1, hmod(xn_ref[...]), 0.0)
    hh = jnp.concatenate([hp, hc, hn], axis=0)
    n = TM + 2 * HALO
    seq_len = ntile * TM
    t = j * TM + lax.broadcasted_iota(jnp.int32, (TM, 1), 0)
    ys = []
    for gi, w in enumerate(POOL_WINDOWS):
        cols = slice(gi * POOL_GROUP, (gi + 1) * POOL_GROUP)
        p = hh[:, cols]
        p = p + pltpu.roll(p, 1, 0)
        half = 1
        while 2 * half < w:
            p = pltpu.roll(p, half, 0) + pltpu.roll(p, n - half, 0)
            half *= 2
        lo = jnp.maximum(t - w // 2, 0)
        hi = jnp.minimum(t + (w - w // 2), seq_len)
        mean = p[HALO:HALO + TM] / (hi - lo).astype(F32)
        dg = (mean - hc[:, cols]).astype(BF16)
        ys.append(jnp.dot(dg, wp_ref[gi], preferred_element_type=F32))
    y = jnp.concatenate(ys, axis=-1) * sp_ref[...]
    o_ref[...] = x + mod[2:3] * y


def _sort_pairs(n):
    pairs = []

    def merge(lo, cnt, r):
        step = r * 2
        if step < cnt:
            merge(lo, cnt, step)
            merge(lo + r, cnt, step)
            for i in range(lo + r, lo + cnt - r, step):
                pairs.append((i, i + r))
        else:
            pairs.append((lo, lo + r))

    def sort(lo, cnt):
        if cnt > 1:
            m = cnt // 2
            sort(lo, m)
            sort(lo + m, m)
            merge(lo, cnt, 1)

    sort(0, n)
    return pairs


_SORT16 = _sort_pairs(PEER_TOPK)
_HYPER = [(a, b) for a in range(PEER_TOPK) for b in range(PEER_TOPK) if (a + 1) * (b + 1) <= PEER_TOPK]


def _top16_sorted(s):
    k = PEER_TOPK
    x = [s[SUBLANES * r:SUBLANES * (r + 1), :] for r in range(N_KEYS // SUBLANES)]
    for (i, j) in _SORT16:
        hi = jnp.maximum(x[i], x[j])
        lo = jnp.minimum(x[i], x[j])
        x[i], x[j] = hi, lo
    for shift in (4, 2, 1):
        y = [jnp.maximum(x[r], pltpu.roll(x[k - 1 - r], shift, 0)) for r in range(k)]
        stride = k // 2
        while stride >= 1:
            for i in range(k):
                if i & stride == 0:
                    hi = jnp.maximum(y[i], y[i + stride])
                    lo = jnp.minimum(y[i], y[i + stride])
                    y[i], y[i + stride] = hi, lo
            stride //= 2
        x = y
    return x


def _pair_counts(sv1, sv2):
    one = jnp.ones_like(sv1[0])
    zero = jnp.zeros_like(sv1[0])
    cand = [sv1[a] + sv2[b] for (a, b) in _HYPER]

    def ordered(i, j):
        (ai, bi), (aj, bj) = _HYPER[i], _HYPER[j]
        return (aj <= ai and bj <= bi) or (ai <= aj and bi <= bj)

    nh = len(_HYPER)
    cnt = [float((a + 1) * (b + 1) - 1 + sum(1 for j in range(i + 1, nh) if not ordered(i, j))) * one
           for i, (a, b) in enumerate(_HYPER)]
    for i in range(nh):
        for jx in range(i):
            if ordered(i, jx):
                continue
            ge = jnp.where(cand[jx] >= cand[i], one, zero)
            cnt[i] = cnt[i] + ge
            cnt[jx] = cnt[jx] - ge
    e1 = [jnp.exp(sv1[a] - sv1[0]) for a in range(PEER_TOPK)]
    e2 = [jnp.exp(sv2[b] - sv2[0]) for b in range(PEER_TOPK)]
    n = [zero for _ in range(PEER_TOPK)]
    z = zero
    for i, (a, b) in enumerate(_HYPER):
        sel = jnp.where(cnt[i] < float(PEER_TOPK), one, zero)
        n[a] = n[a] + sel
        z = z + sel * (e1[a] * e2[b])
    return n, 1.0 / z


def _head_tables(hd, g, lanes, sc_ref, sv_ref, nz_ref, ex_ref, cnt1_ref, e1n_ref, tab_ref, exact):
    k = PEER_TOPK

    def bc(ref, *idx):
        return jnp.broadcast_to(ref[idx + (slice(hd, hd + 1), lanes)], (SUBLANES, LANES))

    sv1b = [bc(sv_ref, 0, a) for a in range(k)]
    nb = [bc(nz_ref, a) for a in range(k)]
    izb = bc(nz_ref, k)
    c1 = jnp.zeros((SUBLANES, LANES), F32)
    for r in range(N_KEYS // SUBLANES):
        rows = slice(r * SUBLANES, (r + 1) * SUBLANES)
        s1 = sc_ref[pl.ds(2 * hd * N_KEYS + r * SUBLANES, SUBLANES), lanes]
        cnt1 = jnp.zeros_like(s1)
        if exact:
            rank1 = jnp.zeros_like(s1)
            for a in range(k):
                rank1 = jnp.where(sv1b[a] > s1, float(a + 1), rank1)
            rank1 = rank1 + ex_ref[0, rows, lanes]
            for a in range(k):
                cnt1 = jnp.where(rank1 == float(a), nb[a], cnt1)
        else:
            for a in range(k):
                cnt1 = jnp.where(s1 == sv1b[a], nb[a], cnt1)
            c1 = c1 + jnp.where(s1 >= sv1b[k - 1], 1.0, 0.0)
        cnt1_ref[hd, rows, lanes] = cnt1
        e1n_ref[hd, rows, lanes] = jnp.exp(s1 - sv1b[0]) * izb
    sv2b = [bc(sv_ref, 1, a) for a in range(k)]
    c2 = jnp.zeros((SUBLANES, LANES), F32)
    for r in range(N_KEYS // BF16_ROWS):
        rk, e2 = [], []
        for q in range(BF16_ROWS // SUBLANES):
            r0 = r * BF16_ROWS + q * SUBLANES
            rows = slice(r0, r0 + SUBLANES)
            s2 = sc_ref[pl.ds((2 * hd + 1) * N_KEYS + r0, SUBLANES), lanes]
            rank2 = jnp.zeros_like(s2)
            for a in range(k):
                rank2 = jnp.where(sv2b[a] > s2, float(a + 1), rank2)
            if exact:
                rank2 = jnp.minimum(rank2 + ex_ref[1, rows, lanes], float(k))
            else:
                c2 = c2 + jnp.where(rank2 < float(k), 1.0, 0.0)
            rk.append(rank2)
            e2.append(jnp.exp(s2 - sv2b[0]))
        tab_ref[r, g, hd, 0] = jnp.concatenate(rk, axis=0).astype(BF16)
        tab_ref[r, g, hd, 1] = jnp.concatenate(e2, axis=0).astype(BF16)
    if exact:
        return None
    ex_ref[0, 0:SUBLANES, lanes] = c1
    ex_ref[1, 0:SUBLANES, lanes] = c2
    b = jnp.zeros((SUBLANES, LANES), F32)
    for svb in (sv1b, sv2b):
        for a in range(k - 1):
            b = b + jnp.where(svb[a] == svb[a + 1], 1.0, 0.0)
    return b


def _tie_offsets(hd, sc_ref, ex_ref):
    tm = sc_ref.shape[-1]
    nidx = lax.broadcasted_iota(jnp.int32, (N_KEYS, tm), 0)
    for p in range(2):
        s = sc_ref[pl.ds((2 * hd + p) * N_KEYS, N_KEYS), :]

        def body(m, e, s=s, p=p):
            row = sc_ref[pl.ds((2 * hd + p) * N_KEYS + m, 1), :]
            return e + jnp.where((s == row) & (nidx > m), 1.0, 0.0)

        ex_ref[p] = lax.fori_loop(0, N_KEYS, body, jnp.zeros((N_KEYS, tm), F32))


def _peer_prologue(x_ref, mod_ref, g_ref, wqt_ref, keys_ref, h2t_ref, tab_ref, sc_ref, sv_ref, nz_ref,
                   ex_ref, cnt1_ref, e1n_ref, y_ref):
    tm = PEER_TM
    half = PEER_DKEY // 2
    ngroup = tm // LANES
    mod = mod_ref[0]
    h2 = _rms(x_ref[...], g_ref[...]) * (1.0 + mod[4:5]) + mod[3:4]
    h2t_ref[...] = h2.T.astype(BF16)

    def lane_group(g):
        return pl.ds(pl.multiple_of(g * LANES, LANES), LANES)

    nslot, qrows = y_ref.shape[0], y_ref.shape[1]
    per_pass = nslot * qrows // half
    for i in range(2 * PEER_HEADS):
        if i % per_pass == 0:
            qt = jnp.dot(wqt_ref[i * half:(i + per_pass) * half, :], h2t_ref[...],
                         preferred_element_type=F32).astype(BF16)
            for s in range(nslot):
                y_ref[s, :, 0:tm] = qt[s * qrows:(s + 1) * qrows, :]
        j = (i % per_pass) * half
        sc_ref[i * N_KEYS:(i + 1) * N_KEYS, :] = jnp.dot(
            keys_ref[i], y_ref[j // qrows, j % qrows:j % qrows + half, 0:tm],
            preferred_element_type=F32)

        def group(g, c, i=i):
            lanes = lane_group(g)
            top = _top16_sorted(sc_ref[i * N_KEYS:(i + 1) * N_KEYS, lanes])
            for a in range(PEER_TOPK):
                sv_ref[i % 2, a, i // 2:i // 2 + 1, lanes] = top[a][0:1, :]
            return c
        lax.fori_loop(0, ngroup, group, 0)

    def counts(g, carry):
        lanes = lane_group(g)
        sv1 = [sv_ref[0, a, :, lanes] for a in range(PEER_TOPK)]
        sv2 = [sv_ref[1, a, :, lanes] for a in range(PEER_TOPK)]
        n, inv_z = _pair_counts(sv1, sv2)
        for a in range(PEER_TOPK):
            nz_ref[a, :, lanes] = n[a]
        nz_ref[PEER_TOPK, :, lanes] = inv_z
        return carry
    lax.fori_loop(0, ngroup, counts, 0)

    tabs = (sc_ref, sv_ref, nz_ref, ex_ref, cnt1_ref, e1n_ref, tab_ref)

    for hd in range(PEER_HEADS):
        def fast(g, bad, hd=hd):
            return bad + _head_tables(hd, g, lane_group(g), *tabs, exact=False)
        bad = lax.fori_loop(0, ngroup, fast, jnp.zeros((SUBLANES, LANES), F32))
        totals = jnp.sum(ex_ref[:, 0:SUBLANES, :], axis=1)
        ties = jnp.max(bad) + jnp.max(jnp.abs(totals - float(PEER_TOPK)))

        @pl.when(ties > 0.0)
        def _(hd=hd):
            _tie_offsets(hd, sc_ref, ex_ref)

            def slow(g, c):
                _head_tables(hd, g, lane_group(g), *tabs, exact=True)
                return c
            lax.fori_loop(0, ngroup, slow, 0)
    sc_ref[0:D_MODEL, :] = jnp.zeros((D_MODEL, tm), F32)


def _peer_kernel(x_ref, mod_ref, g_ref, wqt_ref, keys_ref, u0_ref, u_ref, vt_ref, gfin_ref, o_ref,
                 h2t_ref, tab_ref, sc_ref, sv_ref, nz_ref, ex_ref, cnt1_ref, e1n_ref, raw_ref, y_ref,
                 *, final_norm):
    k = pl.program_id(1)
    tm = PEER_TM
    acc_ref = sc_ref.at[0:D_MODEL]

    nstep = pl.num_programs(1) - 1
    zero = jnp.zeros((BF16_ROWS, LANES), BF16)
    npair = PEER_EBLK // (2 * N_KEYS)
    cur = k % 2
    oth = (k + 1) % 2

    def up_next():
        raw_ref[oth, :, 0:tm] = jnp.dot(u_ref[...], h2t_ref[...], preferred_element_type=F32)

    def down_prev():
        acc_ref[...] += jnp.dot(vt_ref[0], y_ref[oth, :, 0:tm], preferred_element_type=F32)

    def gate_block():
        for q in range(npair):
            i1 = (k * npair + q) * 2
            crow = [[cnt1_ref[hd, pl.ds(i1 + j, 1), :] for hd in range(PEER_HEADS)] for j in range(2)]
            erow = [[e1n_ref[hd, pl.ds(i1 + j, 1), :] for hd in range(PEER_HEADS)] for j in range(2)]
            for g in range(tm // LANES):
                lanes = slice(g * LANES, (g + 1) * LANES)
                cbs = [[jnp.broadcast_to(crow[j][hd][:, lanes], (BF16_ROWS, LANES)).astype(BF16)
                        for hd in range(PEER_HEADS)] for j in range(2)]
                ebs = [[jnp.broadcast_to(erow[j][hd][:, lanes], (BF16_ROWS, LANES)).astype(BF16)
                        for hd in range(PEER_HEADS)] for j in range(2)]
                for r in range(N_KEYS // BF16_ROWS):
                    w = [None, None]
                    for hd in range(PEER_HEADS):
                        rk = tab_ref[r, g, hd, 0]
                        ev = tab_ref[r, g, hd, 1]
                        for j in range(2):
                            t = jnp.where(rk < cbs[j][hd], ev, zero) * ebs[j][hd]
                            w[j] = t if w[j] is None else w[j] + t
                    for j in range(2):
                        rows = pl.ds((2 * q + j) * N_KEYS + r * BF16_ROWS, BF16_ROWS)
                        act = jax.nn.gelu(raw_ref[cur, rows, lanes].astype(BF16))
                        y_ref[cur, rows, lanes] = act * w[j]

    @pl.when(k == 0)
    def _():
        _peer_prologue(x_ref, mod_ref, g_ref, wqt_ref, keys_ref, h2t_ref, tab_ref, sc_ref, sv_ref, nz_ref,
                       ex_ref, cnt1_ref, e1n_ref, y_ref)
        raw_ref[0, :, 0:tm] = jnp.dot(u0_ref[...], h2t_ref[...], preferred_element_type=F32)
        up_next()
        gate_block()

    @pl.when((k > 0) & (k < nstep))
    def _():
        up_next()
        down_prev()
        gate_block()

    @pl.when(k == nstep)
    def _():
        down_prev()
        out = x_ref[...] + mod_ref[0][5:6] * acc_ref[...].T
        if final_norm:
            out = _rms(out, gfin_ref[...])
        o_ref[...] = out


def _rope_tables(s_prompt_tile, s_sample):
    n_rows = s_sample // GRID_W
    rows = jnp.repeat(jnp.arange(n_rows, dtype=F32), GRID_W)
    cols = jnp.tile(jnp.arange(GRID_W, dtype=F32), n_rows)
    axis_dim = QK_ROPE // 2
    inv_freq = ROPE_BASE ** (-jnp.arange(0, axis_dim, 2, dtype=F32) / axis_dim)
    ang = jnp.concatenate([rows[:, None] * inv_freq, cols[:, None] * inv_freq], axis=-1)
    cos, sin = jnp.cos(ang), jnp.sin(ang)
    ck = jnp.concatenate([cos, cos], axis=-1)
    sk = jnp.concatenate([-sin, sin], axis=-1)
    ident_c = jnp.ones((s_prompt_tile, QK_ROPE), F32)
    ident_s = jnp.zeros((s_prompt_tile, QK_ROPE), F32)
    return jnp.concatenate([ident_c, ck], axis=0), jnp.concatenate([ident_s, sk], axis=0)


def kernel(x_prompt, x_sample, cache_ckv_l0, cache_krope_l0, state_lru_l0, c, c_ctx, w_mod_l0, b_mod_l0, w_mod_l1, b_mod_l1, g_mix_l0, g_ffn_l0, g_mix_l1, g_ffn_l1, w_in_l0, g_q_l0, w_uq_l0, g_kv_l0, w_ukv_l0, conv_w_l0, conv_b_l0, w_rg_l0, b_rg_l0, w_ig_l0, b_ig_l0, lam_l0, w_o_l0, w_pool_l1, s_pool_l1, peer_wq_l0, peer_keys_l0, peer_u_l0, peer_v_l0, peer_wq_l1, peer_keys_l1, peer_u_l1, peer_v_l1, g_final):
    nb_p, s_p, d = x_prompt.shape
    nb_s, s_s, _ = x_sample.shape
    n_cache = cache_ckv_l0.shape[1]
    assert d == D_MODEL and s_p == TM and s_s % TM == 0 and n_cache % TM == 0
    t_p = nb_p * s_p
    t_s = nb_s * s_s
    t_all = t_p + t_s
    npt = t_p // TM
    tps = s_s // TM
    ntile = t_all // TM
    assert t_all % PEER_TM == 0 and t_p % PEER_TM == 0 and s_s % PEER_TM == 0

    x0 = jnp.concatenate([x_prompt.reshape(t_p, d), x_sample.reshape(t_s, d)], axis=0)

    ncond = 1 + nb_s
    cpad = jnp.zeros((2 * SUBLANES, d), F32).at[0].set(c_ctx).at[1:ncond].set(c)
    mod0 = _ada(cpad, w_mod_l0, b_mod_l0).reshape(2 * SUBLANES, 6, d)
    mod1 = _ada(cpad, w_mod_l1, b_mod_l1).reshape(2 * SUBLANES, 6, d)

    def cond_row(i):
        return jnp.where(i < npt, 0, 1 + (i - npt) // tps)

    mod_spec = pl.BlockSpec((1, 6, d), lambda i: (cond_row(i), 0, 0))
    row = lambda a: a.reshape(1, -1)

    perm = np.concatenate([np.arange(0, QK_ROPE, 2), np.arange(1, QK_ROPE, 2)])
    perm_sw = np.concatenate([np.arange(1, QK_ROPE, 2), np.arange(0, QK_ROPE, 2)])
    o1 = Q_LORA + KV_LORA
    w_kr = w_in_l0[:, o1:o1 + QK_ROPE]
    z64 = jnp.zeros((d, 128 - QK_ROPE), F32)
    w_in_ext = jnp.concatenate(
        [w_in_l0[:, :o1], w_kr, z64, w_kr[:, perm], z64, w_kr[:, perm_sw], z64, w_in_l0[:, o1 + QK_ROPE:]],
        axis=1).astype(BF16)
    assert w_in_ext.shape[1] == IN_EXT
    wq3 = w_uq_l0.reshape(Q_LORA, MLA_HEADS, QK_NOPE + QK_ROPE)
    w_uq_ext = jnp.concatenate(
        [wq3[:, :, :QK_NOPE].reshape(Q_LORA, -1),
         wq3[:, :, QK_NOPE:][:, :, perm].reshape(Q_LORA, -1),
         wq3[:, :, QK_NOPE:][:, :, perm_sw].reshape(Q_LORA, -1)], axis=1).astype(BF16)
    ck_tab, sk_tab = _rope_tables(TM, s_s)

    def rope_blk(i):
        return jnp.where(i < npt, 0, 1 + (i - npt) % tps)

    tok = lambda w: pl.BlockSpec((TM, w), lambda i: (i, 0))
    full = lambda a: pl.BlockSpec(a.shape, lambda *_: (0,) * a.ndim)
    q, ckv, kr, krr, ux, ug = pl.pallas_call(
        _inproj_kernel,
        grid=(ntile,),
        in_specs=[tok(d), mod_spec, full(row(g_mix_l0)), full(w_in_ext), full(row(g_q_l0)), full(w_uq_ext),
                  full(row(g_kv_l0)),
                  pl.BlockSpec((TM, QK_ROPE), lambda i: (rope_blk(i), 0)),
                  pl.BlockSpec((TM, QK_ROPE), lambda i: (rope_blk(i), 0))],
        out_specs=[pl.BlockSpec((MLA_HEADS, TM, QK_PAD), lambda i: (0, i, 0)),
                   tok(KV_LORA), tok(QK_ROPE), tok(QK_ROPE), tok(LRU_WIDTH), tok(LRU_WIDTH)],
        out_shape=[jax.ShapeDtypeStruct((MLA_HEADS, t_all, QK_PAD), BF16),
                   jax.ShapeDtypeStruct((t_all, KV_LORA), F32),
                   jax.ShapeDtypeStruct((t_all, QK_ROPE), F32),
                   jax.ShapeDtypeStruct((t_all, QK_ROPE), BF16),
                   jax.ShapeDtypeStruct((t_all, LRU_WIDTH), F32),
                   jax.ShapeDtypeStruct((t_all, LRU_WIDTH), F32)],
        compiler_params=_cparams(("arbitrary",)),
    )(x0, mod0, row(g_mix_l0), w_in_ext, row(g_q_l0), w_uq_ext, row(g_kv_l0), ck_tab, sk_tab)

    wkv3 = w_ukv_l0.reshape(KV_LORA, MLA_HEADS, QK_NOPE + V_HEAD)
    w_ukv_ext = jnp.concatenate([wkv3[:, :, :QK_NOPE].reshape(KV_LORA, -1),
                                 wkv3[:, :, QK_NOPE:].reshape(KV_LORA, -1)], axis=1).astype(BF16)

    def attn_call(nb, s_new, tile0, has_cache):
        nq = s_new // TM
        blk0 = tile0 * TM // s_new
        n_c = n_cache if has_cache else 0
        in_specs = [pl.BlockSpec((MLA_HEADS, TM, QK_PAD), lambda b, qi: (0, tile0 + b * nq + qi, 0)),
                    pl.BlockSpec((s_new, KV_LORA), lambda b, qi: (blk0 + b, 0)),
                    pl.BlockSpec((s_new, QK_ROPE), lambda b, qi: (blk0 + b, 0))]
        args = [q, ckv, krr]
        if has_cache:
            in_specs += [pl.BlockSpec((1, n_cache, KV_LORA), lambda b, qi: (b, 0, 0)),
                         pl.BlockSpec((1, n_cache, QK_ROPE), lambda b, qi: (b, 0, 0))]
            args += [cache_ckv_l0, cache_krope_l0[:, :, perm]]
        in_specs.append(pl.BlockSpec(w_ukv_ext.shape, lambda b, qi: (0, 0)))
        args.append(w_ukv_ext)
        return pl.pallas_call(
            functools.partial(_attn_kernel, has_cache=has_cache, s_new=s_new, n_cache=n_c),
            grid=(nb, nq),
            in_specs=in_specs,
            out_specs=pl.BlockSpec((TM, MLA_WIDTH), lambda b, qi: (b * nq + qi, 0)),
            out_shape=jax.ShapeDtypeStruct((nb * s_new, MLA_WIDTH), BF16),
            scratch_shapes=[pltpu.VMEM((MLA_HEADS, n_c + s_new, QK_PAD), BF16),
                            pltpu.VMEM((n_c + s_new, MLA_WIDTH), BF16)],
            compiler_params=_cparams(("arbitrary", "arbitrary")),
        )(*args)

    attn = jnp.concatenate([attn_call(nb_p, s_p, 0, False), attn_call(nb_s, s_s, npt, True)], axis=0)

    def lru_call(nb, s_new, tile0, h0):
        nc = s_new // TM
        r8 = TM // SUBLANES
        last8 = t_all // SUBLANES - 1

        def cur(rev):
            return pl.BlockSpec((TM, LRU_WIDTH),
                                lambda b, cc: (tile0 + b * nc + (nc - 1 - cc if rev else cc), 0))

        def prev(rev):
            return pl.BlockSpec((SUBLANES, LRU_WIDTH), lambda b, cc: (
                jnp.maximum((tile0 + b * nc + (nc - 1 - cc if rev else cc)) * r8 - 1, 0), 0))

        def nxt(rev):
            return pl.BlockSpec((SUBLANES, LRU_WIDTH), lambda b, cc: (
                jnp.minimum((tile0 + b * nc + (nc - 1 - cc if rev else cc) + 1) * r8, last8), 0))

        small = [conv_w_l0, row(conv_b_l0), w_rg_l0, b_rg_l0, w_ig_l0, b_ig_l0, lam_l0]
        return pl.pallas_call(
            functools.partial(_lru_kernel, nc=nc),
            grid=(nb, nc),
            in_specs=[prev(False), cur(False), nxt(False), prev(True), cur(True), nxt(True)]
                     + [full(a) for a in small]
                     + [pl.BlockSpec((1, 2, LRU_WIDTH), lambda b, cc: (b, 0, 0))],
            out_specs=[pl.BlockSpec((TM, LRU_WIDTH), lambda b, cc: (b * nc + cc, 0)),
                       pl.BlockSpec((TM, LRU_WIDTH), lambda b, cc: (b * nc + nc - 1 - cc, 0)),
                       pl.BlockSpec((1, 2, LRU_WIDTH), lambda b, cc: (b, 0, 0))],
            out_shape=[jax.ShapeDtypeStruct((nb * s_new, LRU_WIDTH), F32),
                       jax.ShapeDtypeStruct((nb * s_new, LRU_WIDTH), F32),
                       jax.ShapeDtypeStruct((nb, 2, LRU_WIDTH), F32)],
            scratch_shapes=[pltpu.VMEM((1, LRU_WIDTH), F32), pltpu.VMEM((1, LRU_WIDTH), F32)],
            compiler_params=_cparams(("arbitrary", "arbitrary")),
        )(ux, ux, ux, ux, ux, ux, *small, h0)

    hf_p, hb_p, new_lru = lru_call(nb_p, s_p, 0, jnp.zeros((nb_p, 2, LRU_WIDTH), F32))
    hf_s, hb_s, _ = lru_call(nb_s, s_s, npt, state_lru_l0.astype(F32))
    hf = jnp.concatenate([hf_p, hf_s], axis=0)
    hb = jnp.concatenate([hb_p, hb_s], axis=0)

    w_o = w_o_l0.astype(BF16)
    x1 = pl.pallas_call(
        _oproj_kernel,
        grid=(ntile,),
        in_specs=[tok(d), mod_spec, tok(MLA_WIDTH), tok(LRU_WIDTH), tok(LRU_WIDTH), tok(LRU_WIDTH),
                  pl.BlockSpec((MLA_WIDTH, d), lambda i: (0, 0)), pl.BlockSpec((LRU_WIDTH, d), lambda i: (1, 0))],
        out_specs=tok(d),
        out_shape=jax.ShapeDtypeStruct((t_all, d), F32),
        compiler_params=_cparams(("arbitrary",)),
    )(x0, mod0, attn, hf, hb, ug, w_o, w_o)

    def peer_call(x, mod, g_ffn, w_q, sub_keys, u, v, final_norm):
        n_exp = u.shape[0]
        assert n_exp == N_KEYS * N_KEYS and n_exp % PEER_EBLK == 0
        wqt = w_q.T.astype(BF16)
        keys = sub_keys.astype(BF16).reshape(2 * PEER_HEADS, N_KEYS, PEER_DKEY // 2)
        ub = u.astype(BF16)
        vt = v.reshape(n_exp // PEER_EBLK, PEER_EBLK, d).transpose(0, 2, 1).astype(BF16)
        tpp = PEER_TM // TM

        nblk = n_exp // PEER_EBLK

        def cond_row_p(i):
            return cond_row(i * tpp)

        big = lambda dt: pltpu.VMEM((PEER_HEADS, N_KEYS, PEER_TM), dt)
        return pl.pallas_call(
            functools.partial(_peer_kernel, final_norm=final_norm),
            grid=(t_all // PEER_TM, nblk + 1),
            in_specs=[pl.BlockSpec((PEER_TM, d), lambda i, k: (i, 0), pipeline_mode=pl.Buffered(1)),
                      pl.BlockSpec((1, 6, d), lambda i, k: (cond_row_p(i), 0, 0)),
                      pl.BlockSpec((1, d), lambda i, k: (0, 0)),
                      pl.BlockSpec(wqt.shape, lambda i, k: (0, 0)),
                      pl.BlockSpec(keys.shape, lambda i, k: (0, 0, 0)),
                      pl.BlockSpec((PEER_EBLK, d), lambda i, k: (0, 0), pipeline_mode=pl.Buffered(1)),
                      pl.BlockSpec((PEER_EBLK, d), lambda i, k: (jnp.minimum(k + 1, nblk - 1), 0)),
                      pl.BlockSpec((1, d, PEER_EBLK), lambda i, k: (jnp.maximum(k - 1, 0), 0, 0)),
                      pl.BlockSpec((1, d), lambda i, k: (0, 0))],
            out_specs=pl.BlockSpec((PEER_TM, d), lambda i, k: (i, 0), pipeline_mode=pl.Buffered(1)),
            out_shape=jax.ShapeDtypeStruct((t_all, d), F32),
            scratch_shapes=[pltpu.VMEM((d, PEER_TM), BF16),
                            pltpu.VMEM((N_KEYS // BF16_ROWS, PEER_TM // LANES, PEER_HEADS, 2, BF16_ROWS, LANES),
                                       BF16),
                            pltpu.VMEM((2 * PEER_HEADS * N_KEYS, PEER_TM), F32),
                            pltpu.VMEM((2, PEER_TOPK, PEER_HEADS, PEER_TM), F32),
                            pltpu.VMEM((PEER_TOPK + 1, PEER_HEADS, PEER_TM), F32),
                            pltpu.VMEM((2, N_KEYS, PEER_TM), F32),
                            big(F32), big(F32),
                            pltpu.VMEM((2, PEER_EBLK, PEER_TM + LANES), F32),
                            pltpu.VMEM((2, PEER_EBLK, PEER_TM + LANES), BF16)],
            compiler_params=_cparams(("arbitrary", "arbitrary")),
        )(x, mod, row(g_ffn), wqt, keys, ub, ub, vt, row(g_final))

    x2 = peer_call(x1, mod0, g_ffn_l0, peer_wq_l0, peer_keys_l0, peer_u_l0, peer_v_l0, False)

    rh = TM // HALO
    lasth = t_all // HALO - 1
    x3 = pl.pallas_call(
        functools.partial(_pool_kernel, tiles_per_seq=tps, n_prompt_tiles=npt),
        grid=(ntile,),
        in_specs=[pl.BlockSpec((HALO, d), lambda i: (jnp.maximum(i * rh - 1, 0), 0)),
                  tok(d),
                  pl.BlockSpec((HALO, d), lambda i: (jnp.minimum((i + 1) * rh, lasth), 0)),
                  mod_spec, full(row(g_mix_l1)),
                  pl.BlockSpec(w_pool_l1.shape, lambda i: (0, 0, 0)), full(row(s_pool_l1))],
        out_specs=tok(d),
        out_shape=jax.ShapeDtypeStruct((t_all, d), F32),
        compiler_params=_cparams(("arbitrary",)),
    )(x2, x2, x2, mod1, row(g_mix_l1), w_pool_l1.astype(BF16), row(s_pool_l1))

    y = peer_call(x3, mod1, g_ffn_l1, peer_wq_l1, peer_keys_l1, peer_u_l1, peer_v_l1, True)

    y_prompt = y[:t_p].reshape(nb_p, s_p, d)
    y_sample = y[t_p:].reshape(nb_s, s_s, d)
    new_ckv = ckv[:t_p].reshape(nb_p, s_p, KV_LORA)
    new_krope = kr[:t_p].reshape(nb_p, s_p, QK_ROPE)
    return (y_prompt, y_sample, new_ckv, new_krope, new_lru)
```

```python
import functools

import numpy as np
import jax
import jax.numpy as jnp
from jax import lax
from jax.experimental import pallas as pl
from jax.experimental.pallas import tpu as pltpu

F32 = jnp.float32
BF16 = jnp.bfloat16

D_MODEL = 1024
EPS = 1e-6
GRID_W = 64
MLA_HEADS = 4
Q_LORA = 384
KV_LORA = 256
QK_NOPE = 128
QK_ROPE = 64
V_HEAD = 128
MLA_WIDTH = MLA_HEADS * V_HEAD
ROPE_BASE = 10000.0
LRU_WIDTH = 512
LRU_BLOCKS = 4
LRU_BLOCK = LRU_WIDTH // LRU_BLOCKS
CONV_W = 4
CONV_LEFT = 2
LRU_C = 8.0
POOL_WINDOWS = (2, 4, 8, 16)
POOL_GROUP = D_MODEL // len(POOL_WINDOWS)
PEER_HEADS = 8
N_KEYS = 128
PEER_DKEY = 256
PEER_TOPK = 16

SUBLANES = 8
LANES = 128
VMEM_BYTES = 64 * 1024 * 1024
VMEM_LIMIT = VMEM_BYTES - 4 * 1024 * 1024

TM = 256
QK_PAD = 256
HALO = 16
PEER_TM = 1024
PEER_EBLK = 512
PEER_SUB = 256
BF16_ROWS = 16
IN_EXT = 2048


def _rms(x, g):
    return x * lax.rsqrt(jnp.mean(x * x, axis=-1, keepdims=True) + EPS) * g


def _cparams(sem):
    return pltpu.CompilerParams(dimension_semantics=sem, vmem_limit_bytes=VMEM_LIMIT)


def _ada_kernel(c_ref, w_ref, b_ref, o_ref):
    c = c_ref[...]
    o_ref[...] = jnp.dot(c * jax.nn.sigmoid(c), w_ref[...], preferred_element_type=F32) + b_ref[...]


def _ada(cpad, w_mod, b_mod):
    n = w_mod.shape[1]
    bn = 768
    return pl.pallas_call(
        _ada_kernel,
        grid=(n // bn,),
        in_specs=[pl.BlockSpec(cpad.shape, lambda j: (0, 0)),
                  pl.BlockSpec((D_MODEL, bn), lambda j: (0, j)),
                  pl.BlockSpec((1, bn), lambda j: (0, j))],
        out_specs=pl.BlockSpec((cpad.shape[0], bn), lambda j: (0, j)),
        out_shape=jax.ShapeDtypeStruct((cpad.shape[0], n), F32),
        compiler_params=_cparams(("arbitrary",)),
    )(cpad, w_mod, b_mod.reshape(1, n))


def _inproj_kernel(x_ref, mod_ref, g_ref, win_ref, gq_ref, wuq_ref, gkv_ref, ck_ref, sk_ref,
                   q_ref, ckv_ref, kr_ref, krr_ref, ux_ref, ug_ref):
    mod = mod_ref[0]
    h = _rms(x_ref[...], g_ref[...]) * (1.0 + mod[1:2]) + mod[0:1]
    y = jnp.dot(h.astype(BF16), win_ref[...], preferred_element_type=F32)
    cq = y[:, 0:Q_LORA]
    ckv = y[:, Q_LORA:Q_LORA + KV_LORA]
    o = Q_LORA + KV_LORA
    kr = y[:, o:o + QK_ROPE]
    krp = y[:, o + 128:o + 128 + QK_ROPE]
    krs = y[:, o + 256:o + 256 + QK_ROPE]
    ux_ref[...] = y[:, o + 384:o + 384 + LRU_WIDTH]
    ug_ref[...] = y[:, o + 384 + LRU_WIDTH:o + 384 + 2 * LRU_WIDTH]
    ckv_ref[...] = _rms(ckv, gkv_ref[...])
    kr_ref[...] = kr
    ck = ck_ref[...]
    sk = sk_ref[...]
    krr_ref[...] = (krp * ck + krs * sk).astype(BF16)
    q = jnp.dot(_rms(cq, gq_ref[...]).astype(BF16), wuq_ref[...], preferred_element_type=F32)
    nw = MLA_HEADS * QK_NOPE
    rw = MLA_HEADS * QK_ROPE
    for hd in range(MLA_HEADS):
        qp = q[:, nw + hd * QK_ROPE:nw + (hd + 1) * QK_ROPE]
        qs = q[:, nw + rw + hd * QK_ROPE:nw + rw + (hd + 1) * QK_ROPE]
        q_ref[hd, :, 0:QK_NOPE] = q[:, hd * QK_NOPE:(hd + 1) * QK_NOPE].astype(BF16)
        q_ref[hd, :, QK_NOPE:QK_NOPE + QK_ROPE] = (qp * ck + qs * sk).astype(BF16)
        q_ref[hd, :, QK_NOPE + QK_ROPE:QK_PAD] = jnp.zeros((TM, QK_PAD - QK_NOPE - QK_ROPE), BF16)


def _attn_kernel(*refs, has_cache, s_new, n_cache):
    if has_cache:
        q_ref, ckv_ref, krr_ref, cckv_ref, ckr_ref, wukv_ref, o_ref, kcat_ref, vv_ref = refs
    else:
        q_ref, ckv_ref, krr_ref, wukv_ref, o_ref, kcat_ref, vv_ref = refs
    sk = n_cache + s_new
    kw = MLA_HEADS * QK_NOPE
    zpad = jnp.zeros((TM, QK_PAD - QK_NOPE - QK_ROPE), BF16)

    def put_keys(row0, ckv_rows, kr_rows):
        kv = jnp.dot(ckv_rows.astype(BF16), wukv_ref[...], preferred_element_type=F32)
        rows = pl.ds(row0, TM)
        for hd in range(MLA_HEADS):
            kcat_ref[hd, rows, 0:QK_NOPE] = kv[:, hd * QK_NOPE:(hd + 1) * QK_NOPE].astype(BF16)
            kcat_ref[hd, rows, QK_NOPE:QK_NOPE + QK_ROPE] = kr_rows.astype(BF16)
            kcat_ref[hd, rows, QK_NOPE + QK_ROPE:QK_PAD] = zpad
        vv_ref[rows, :] = kv[:, kw:].astype(BF16)

    @pl.when(pl.program_id(1) == 0)
    def _():
        if has_cache:
            for c in range(n_cache // TM):
                put_keys(c * TM, cckv_ref[0, c * TM:(c + 1) * TM, :], ckr_ref[0, c * TM:(c + 1) * TM, :])

        def body(c, carry):
            r0 = pl.multiple_of(c * TM, TM)
            put_keys(n_cache + r0, ckv_ref[pl.ds(r0, TM), :], krr_ref[pl.ds(r0, TM), :])
            return carry
        lax.fori_loop(0, s_new // TM, body, 0)

    scale = (QK_NOPE + QK_ROPE) ** -0.5
    for hd in range(MLA_HEADS):
        s = lax.dot_general(q_ref[hd], kcat_ref[hd], (((1,), (1,)), ((), ())),
                            preferred_element_type=F32) * scale
        m = jnp.max(s, axis=-1, keepdims=True)
        e = jnp.exp(s - m)
        l = jnp.sum(e, axis=-1, keepdims=True)
        o = jnp.dot(e.astype(BF16), vv_ref[:, hd * V_HEAD:(hd + 1) * V_HEAD], preferred_element_type=F32)
        o_ref[:, hd * V_HEAD:(hd + 1) * V_HEAD] = (o / l).astype(BF16)
    del sk


def _lru_dir(xp_ref, xc_ref, xn_ref, valid_prev, valid_next, d, reverse, cw_ref, cb_ref,
             wr_ref, br_ref, wi_ref, bi_ref, lam_ref, carry):
    xp = jnp.where(valid_prev, xp_ref[...], 0.0)
    xn = jnp.where(valid_next, xn_ref[...], 0.0)
    xx = jnp.concatenate([xp, xc_ref[...], xn], axis=0)
    n = TM + 2 * SUBLANES
    xc = cb_ref[...]
    for k in range(CONV_W):
        sh = (CONV_LEFT - k) % n
        xs = xx if sh == 0 else pltpu.roll(xx, sh, 0)
        xc = xc + xs[SUBLANES:SUBLANES + TM] * cw_ref[k:k + 1, :]
    rs, is_ = [], []
    for b in range(LRU_BLOCKS):
        xb = xc[:, b * LRU_BLOCK:(b + 1) * LRU_BLOCK]
        rs.append(jnp.dot(xb, wr_ref[d, b], preferred_element_type=F32))
        is_.append(jnp.dot(xb, wi_ref[d, b], preferred_element_type=F32))
    r = jax.nn.sigmoid(jnp.concatenate(rs, axis=-1) + br_ref[d:d + 1, :])
    i = jax.nn.sigmoid(jnp.concatenate(is_, axis=-1) + bi_ref[d:d + 1, :])
    nl = -lam_ref[d:d + 1, :]
    softplus = jnp.maximum(nl, 0.0) + jnp.log1p(jnp.exp(-jnp.abs(nl)))
    log_a = -LRU_C * r * softplus
    a = jnp.exp(log_a)
    bx = jnp.sqrt(jnp.tanh(-log_a) * (a * a + 1.0)) * (i * xc)
    t = lax.broadcasted_iota(jnp.int32, (TM, 1), 0)
    step = 1
    while step < TM:
        if reverse:
            keep = t < TM - step
            sh = TM - step
        else:
            keep = t >= step
            sh = step
        a_s = jnp.where(keep, pltpu.roll(a, sh, 0), 1.0)
        b_s = jnp.where(keep, pltpu.roll(bx, sh, 0), 0.0)
        bx = a * b_s + bx
        a = a * a_s
        step *= 2
    return a * carry + bx


def _lru_kernel(fxp, fxc, fxn, bxp, bxc, bxn, cw_ref, cb_ref, wr_ref, br_ref, wi_ref, bi_ref, lam_ref,
                h0_ref, hf_ref, hb_ref, st_ref, cf_ref, cbk_ref, *, nc):
    c = pl.program_id(1)

    @pl.when(c == 0)
    def _():
        cf_ref[...] = h0_ref[0, 0:1, :]
        cbk_ref[...] = h0_ref[0, 1:2, :]

    params = (cw_ref, cb_ref, wr_ref, br_ref, wi_ref, bi_ref, lam_ref)
    hf = _lru_dir(fxp, fxc, fxn, c > 0, c < nc - 1, 0, False, *params, cf_ref[...])
    hf_ref[...] = hf
    cf_ref[...] = hf[TM - 1:TM, :]
    hb = _lru_dir(bxp, bxc, bxn, c < nc - 1, c > 0, 1, True, *params, cbk_ref[...])
    hb_ref[...] = hb
    cbk_ref[...] = hb[0:1, :]
    st_ref[0, 0:1, :] = hf[TM - 1:TM, :]
    st_ref[0, 1:2, :] = hb[0:1, :]


def _oproj_kernel(x_ref, mod_ref, at_ref, hf_ref, hb_ref, ug_ref, woa_ref, wor_ref, o_ref):
    mod = mod_ref[0]
    rec = ((hf_ref[...] + hb_ref[...]) * jax.nn.gelu(ug_ref[...])).astype(BF16)
    out = (jnp.dot(at_ref[...], woa_ref[...], preferred_element_type=F32)
           + jnp.dot(rec, wor_ref[...], preferred_element_type=F32))
    o_ref[...] = x_ref[...] + mod[2:3] * out


def _pool_kernel(xp_ref, xc_ref, xn_ref, mod_ref, g_ref, wp_ref, sp_ref, o_ref, *, tiles_per_seq, n_prompt_tiles):
    i = pl.program_id(0)
    j = jnp.where(i < n_prompt_tiles, 0, (i - n_prompt_tiles) % tiles_per_seq)
    ntile = jnp.where(i < n_prompt_tiles, 1, tiles_per_seq)
    mod = mod_ref[0]
    g = g_ref[...]

    def hmod(x):
        return _rms(x, g) * (1.0 + mod[1:2]) + mod[0:1]

    x = xc_ref[...]
    hc = hmod(x)
    hp = jnp.where(j > 0, hmod(xp_ref[...]), 0.0)
    hn = jnp.where(j < ntile - 1, hmod(xn_ref[...]), 0.0)
    hh = jnp.concatenate([hp, hc, hn], axis=0)
    n = TM + 2 * HALO
    seq_len = ntile * TM
    t = j * TM + lax.broadcasted_iota(jnp.int32, (TM, 1), 0)
    ys = []
    for gi, w in enumerate(POOL_WINDOWS):
        cols = slice(gi * POOL_GROUP, (gi + 1) * POOL_GROUP)
        p = hh[:, cols]
        p = p + pltpu.roll(p, 1, 0)
        half = 1
        while 2 * half < w:
            p = pltpu.roll(p, half, 0) + pltpu.roll(p, n - half, 0)
            half *= 2
        lo = jnp.maximum(t - w // 2, 0)
        hi = jnp.minimum(t + (w - w // 2), seq_len)
        mean = p[HALO:HALO + TM] / (hi - lo).astype(F32)
        dg = (mean - hc[:, cols]).astype(BF16)
        ys.append(jnp.dot(dg, wp_ref[gi], preferred_element_type=F32))
    y = jnp.concatenate(ys, axis=-1) * sp_ref[...]
    o_ref[...] = x + mod[2:3] * y


def _sort_pairs(n):
    pairs = []

    def merge(lo, cnt, r):
        step = r * 2
        if step < cnt:
            merge(lo, cnt, step)
            merge(lo + r, cnt, step)
            for i in range(lo + r, lo + cnt - r, step):
                pairs.append((i, i + r))
        else:
            pairs.append((lo, lo + r))

    def sort(lo, cnt):
        if cnt > 1:
            m = cnt // 2
            sort(lo, m)
            sort(lo + m, m)
            merge(lo, cnt, 1)

    sort(0, n)
    return pairs


_SORT16 = _sort_pairs(PEER_TOPK)
_HYPER = [(a, b) for a in range(PEER_TOPK) for b in range(PEER_TOPK) if (a + 1) * (b + 1) <= PEER_TOPK]


def _top16_sorted(s):
    k = PEER_TOPK
    x = [s[SUBLANES * r:SUBLANES * (r + 1), :] for r in range(N_KEYS // SUBLANES)]
    for (i, j) in _SORT16:
        hi = jnp.maximum(x[i], x[j])
        lo = jnp.minimum(x[i], x[j])
        x[i], x[j] = hi, lo
    for shift in (4, 2, 1):
        y = [jnp.maximum(x[r], pltpu.roll(x[k - 1 - r], shift, 0)) for r in range(k)]
        stride = k // 2
        while stride >= 1:
            for i in range(k):
                if i & stride == 0:
                    hi = jnp.maximum(y[i], y[i + stride])
                    lo = jnp.minimum(y[i], y[i + stride])
                    y[i], y[i + stride] = hi, lo
            stride //= 2
        x = y
    return x


def _pair_counts(sv1, sv2):
    one = jnp.ones_like(sv1[0])
    zero = jnp.zeros_like(sv1[0])
    cand = [sv1[a] + sv2[b] for (a, b) in _HYPER]

    def ordered(i, j):
        (ai, bi), (aj, bj) = _HYPER[i], _HYPER[j]
        return (aj <= ai and bj <= bi) or (ai <= aj and bi <= bj)

    nh = len(_HYPER)
    cnt = [float((a + 1) * (b + 1) - 1 + sum(1 for j in range(i + 1, nh) if not ordered(i, j))) * one
           for i, (a, b) in enumerate(_HYPER)]
    for i in range(nh):
        for jx in range(i):
            if ordered(i, jx):
                continue
            ge = jnp.where(cand[jx] >= cand[i], one, zero)
            cnt[i] = cnt[i] + ge
            cnt[jx] = cnt[jx] - ge
    e1 = [jnp.exp(sv1[a] - sv1[0]) for a in range(PEER_TOPK)]
    e2 = [jnp.exp(sv2[b] - sv2[0]) for b in range(PEER_TOPK)]
    n = [zero for _ in range(PEER_TOPK)]
    z = zero
    for i, (a, b) in enumerate(_HYPER):
        sel = jnp.where(cnt[i] < float(PEER_TOPK), one, zero)
        n[a] = n[a] + sel
        z = z + sel * (e1[a] * e2[b])
    return n, 1.0 / z


def _head_tables(hd, g, lanes, sc_ref, sv_ref, nz_ref, ex_ref, cnt1_ref, e1n_ref, tab_ref, exact):
    k = PEER_TOPK

    def bc(ref, *idx):
        return jnp.broadcast_to(ref[idx + (slice(hd, hd + 1), lanes)], (SUBLANES, LANES))

    sv1b = [bc(sv_ref, 0, a) for a in range(k)]
    nb = [bc(nz_ref, a) for a in range(k)]
    izb = bc(nz_ref, k)
    c1 = jnp.zeros((SUBLANES, LANES), F32)
    for r in range(N_KEYS // SUBLANES):
        rows = slice(r * SUBLANES, (r + 1) * SUBLANES)
        s1 = sc_ref[pl.ds(2 * hd * N_KEYS + r * SUBLANES, SUBLANES), lanes]
        cnt1 = jnp.zeros_like(s1)
        if exact:
            rank1 = jnp.zeros_like(s1)
            for a in range(k):
                rank1 = jnp.where(sv1b[a] > s1, float(a + 1), rank1)
            rank1 = rank1 + ex_ref[0, rows, lanes]
            for a in range(k):
                cnt1 = jnp.where(rank1 == float(a), nb[a], cnt1)
        else:
            for a in range(k):
                cnt1 = jnp.where(s1 == sv1b[a], nb[a], cnt1)
            c1 = c1 + jnp.where(s1 >= sv1b[k - 1], 1.0, 0.0)
        cnt1_ref[hd, rows, lanes] = cnt1
        e1n_ref[hd, rows, lanes] = jnp.exp(s1 - sv1b[0]) * izb
    sv2b = [bc(sv_ref, 1, a) for a in range(k)]
    c2 = jnp.zeros((SUBLANES, LANES), F32)
    for r in range(N_KEYS // BF16_ROWS):
        rk, e2 = [], []
        for q in range(BF16_ROWS // SUBLANES):
            r0 = r * BF16_ROWS + q * SUBLANES
            rows = slice(r0, r0 + SUBLANES)
            s2 = sc_ref[pl.ds((2 * hd + 1) * N_KEYS + r0, SUBLANES), lanes]
            rank2 = jnp.zeros_like(s2)
            for a in range(k):
                rank2 = jnp.where(sv2b[a] > s2, float(a + 1), rank2)
            if exact:
                rank2 = jnp.minimum(rank2 + ex_ref[1, rows, lanes], float(k))
            else:
                c2 = c2 + jnp.where(rank2 < float(k), 1.0, 0.0)
            rk.append(rank2)
            e2.append(jnp.exp(s2 - sv2b[0]))
        tab_ref[r, g, hd, 0] = jnp.concatenate(rk, axis=0).astype(BF16)
        tab_ref[r, g, hd, 1] = jnp.concatenate(e2, axis=0).astype(BF16)
    if exact:
        return None
    ex_ref[0, 0:SUBLANES, lanes] = c1
    ex_ref[1, 0:SUBLANES, lanes] = c2
    b = jnp.zeros((SUBLANES, LANES), F32)
    for svb in (sv1b, sv2b):
        for a in range(k - 1):
            b = b + jnp.where(svb[a] == svb[a + 1], 1.0, 0.0)
    return b


def _tie_offsets(hd, sc_ref, ex_ref):
    tm = sc_ref.shape[-1]
    nidx = lax.broadcasted_iota(jnp.int32, (N_KEYS, tm), 0)
    for p in range(2):
        s = sc_ref[pl.ds((2 * hd + p) * N_KEYS, N_KEYS), :]

        def body(m, e, s=s, p=p):
            row = sc_ref[pl.ds((2 * hd + p) * N_KEYS + m, 1), :]
            return e + jnp.where((s == row) & (nidx > m), 1.0, 0.0)

        ex_ref[p] = lax.fori_loop(0, N_KEYS, body, jnp.zeros((N_KEYS, tm), F32))


def _peer_prologue(x_ref, mod_ref, g_ref, wqt_ref, keys_ref, h2t_ref, tab_ref, sc_ref, sv_ref, nz_ref,
                   ex_ref, cnt1_ref, e1n_ref, y_ref):
    tm = PEER_TM
    half = PEER_DKEY // 2
    ngroup = tm // LANES
    mod = mod_ref[0]
    h2 = _rms(x_ref[...], g_ref[...]) * (1.0 + mod[4:5]) + mod[3:4]
    h2t_ref[...] = h2.T.astype(BF16)

    def lane_group(g):
        return pl.ds(pl.multiple_of(g * LANES, LANES), LANES)

    nslot, qrows = y_ref.shape[0], y_ref.shape[1]
    per_pass = nslot * qrows // half
    for i in range(2 * PEER_HEADS):
        if i % per_pass == 0:
            qt = jnp.dot(wqt_ref[i * half:(i + per_pass) * half, :], h2t_ref[...],
                         preferred_element_type=F32).astype(BF16)
            for s in range(nslot):
                y_ref[s, :, 0:tm] = qt[s * qrows:(s + 1) * qrows, :]
        j = (i % per_pass) * half
        sc_ref[i * N_KEYS:(i + 1) * N_KEYS, :] = jnp.dot(
            keys_ref[i], y_ref[j // qrows, j % qrows:j % qrows + half, 0:tm],
            preferred_element_type=F32)

        def group(g, c, i=i):
            lanes = lane_group(g)
            top = _top16_sorted(sc_ref[i * N_KEYS:(i + 1) * N_KEYS, lanes])
            for a in range(PEER_TOPK):
                sv_ref[i % 2, a, i // 2:i // 2 + 1, lanes] = top[a][0:1, :]
            return c
        lax.fori_loop(0, ngroup, group, 0)

    def counts(g, carry):
        lanes = lane_group(g)
        sv1 = [sv_ref[0, a, :, lanes] for a in range(PEER_TOPK)]
        sv2 = [sv_ref[1, a, :, lanes] for a in range(PEER_TOPK)]
        n, inv_z = _pair_counts(sv1, sv2)
        for a in range(PEER_TOPK):
            nz_ref[a, :, lanes] = n[a]
        nz_ref[PEER_TOPK, :, lanes] = inv_z
        return carry
    lax.fori_loop(0, ngroup, counts, 0)

    tabs = (sc_ref, sv_ref, nz_ref, ex_ref, cnt1_ref, e1n_ref, tab_ref)

    for hd in range(PEER_HEADS):
        def fast(g, bad, hd=hd):
            return bad + _head_tables(hd, g, lane_group(g), *tabs, exact=False)
        bad = lax.fori_loop(0, ngroup, fast, jnp.zeros((SUBLANES, LANES), F32))
        totals = jnp.sum(ex_ref[:, 0:SUBLANES, :], axis=1)
        ties = jnp.max(bad) + jnp.max(jnp.abs(totals - float(PEER_TOPK)))

        @pl.when(ties > 0.0)
        def _(hd=hd):
            _tie_offsets(hd, sc_ref, ex_ref)

            def slow(g, c):
                _head_tables(hd, g, lane_group(g), *tabs, exact=True)
                return c
            lax.fori_loop(0, ngroup, slow, 0)
    sc_ref[0:D_MODEL, :] = jnp.zeros((D_MODEL, tm), F32)


def _peer_kernel(x_ref, mod_ref, g_ref, wqt_ref, keys_ref, u_ref, vt_ref, gfin_ref, o_ref,
                 h2t_ref, tab_ref, sc_ref, sv_ref, nz_ref, ex_ref, cnt1_ref, e1n_ref, y_ref, *, final_norm):
    k = pl.program_id(1)
    tm = PEER_TM
    acc_ref = sc_ref.at[0:D_MODEL]

    @pl.when(k == 0)
    def _():
        _peer_prologue(x_ref, mod_ref, g_ref, wqt_ref, keys_ref, h2t_ref, tab_ref, sc_ref, sv_ref, nz_ref,
                       ex_ref, cnt1_ref, e1n_ref, y_ref)

    zero = jnp.zeros((BF16_ROWS, LANES), BF16)
    npair = PEER_EBLK // (2 * N_KEYS)
    ngroup = tm // LANES

    def up_dot(p):
        return jnp.dot(u_ref[p * N_KEYS:(p + 1) * N_KEYS, :], h2t_ref[...], preferred_element_type=F32)

    def gate_pair(q, raws):
        i1 = (k * npair + q) * 2
        crow = [[cnt1_ref[hd, pl.ds(i1 + j, 1), :] for hd in range(PEER_HEADS)] for j in range(2)]
        erow = [[e1n_ref[hd, pl.ds(i1 + j, 1), :] for hd in range(PEER_HEADS)] for j in range(2)]
        out = [[[None] * ngroup for _ in range(N_KEYS // BF16_ROWS)] for _ in range(2)]
        for g in range(ngroup):
            lanes = slice(g * LANES, (g + 1) * LANES)
            cbs = [[jnp.broadcast_to(crow[j][hd][:, lanes], (BF16_ROWS, LANES)).astype(BF16)
                    for hd in range(PEER_HEADS)] for j in range(2)]
            ebs = [[jnp.broadcast_to(erow[j][hd][:, lanes], (BF16_ROWS, LANES)).astype(BF16)
                    for hd in range(PEER_HEADS)] for j in range(2)]
            for r in range(N_KEYS // BF16_ROWS):
                w = [None, None]
                for hd in range(PEER_HEADS):
                    rk = tab_ref[r, g, hd, 0]
                    ev = tab_ref[r, g, hd, 1]
                    for j in range(2):
                        t = jnp.where(rk < cbs[j][hd], ev, zero) * ebs[j][hd]
                        w[j] = t if w[j] is None else w[j] + t
                for j in range(2):
                    act = jax.nn.gelu(raws[j][r * BF16_ROWS:(r + 1) * BF16_ROWS, lanes].astype(BF16))
                    out[j][r][g] = act * w[j]
        return jnp.concatenate([jnp.concatenate(out[j][r], axis=1)
                                for j in range(2) for r in range(N_KEYS // BF16_ROWS)], axis=0)

    raws = [up_dot(0), up_dot(1)]
    for q in range(npair):
        nxt = [up_dot(2 * q + 2), up_dot(2 * q + 3)] if q + 1 < npair else None
        y = gate_pair(q, raws)
        e0 = 2 * q * N_KEYS
        acc_ref[...] += jnp.dot(vt_ref[0, :, e0:e0 + 2 * N_KEYS], y, preferred_element_type=F32)
        raws = nxt

    @pl.when(k == pl.num_programs(1) - 1)
    def _():
        out = x_ref[...] + mod_ref[0][5:6] * acc_ref[...].T
        if final_norm:
            out = _rms(out, gfin_ref[...])
        o_ref[...] = out


def _rope_tables(s_prompt_tile, s_sample):
    n_rows = s_sample // GRID_W
    rows = jnp.repeat(jnp.arange(n_rows, dtype=F32), GRID_W)
    cols = jnp.tile(jnp.arange(GRID_W, dtype=F32), n_rows)
    axis_dim = QK_ROPE // 2
    inv_freq = ROPE_BASE ** (-jnp.arange(0, axis_dim, 2, dtype=F32) / axis_dim)
    ang = jnp.concatenate([rows[:, None] * inv_freq, cols[:, None] * inv_freq], axis=-1)
    cos, sin = jnp.cos(ang), jnp.sin(ang)
    ck = jnp.concatenate([cos, cos], axis=-1)
    sk = jnp.concatenate([-sin, sin], axis=-1)
    ident_c = jnp.ones((s_prompt_tile, QK_ROPE), F32)
    ident_s = jnp.zeros((s_prompt_tile, QK_ROPE), F32)
    return jnp.concatenate([ident_c, ck], axis=0), jnp.concatenate([ident_s, sk], axis=0)


def kernel(x_prompt, x_sample, cache_ckv_l0, cache_krope_l0, state_lru_l0, c, c_ctx, w_mod_l0, b_mod_l0, w_mod_l1, b_mod_l1, g_mix_l0, g_ffn_l0, g_mix_l1, g_ffn_l1, w_in_l0, g_q_l0, w_uq_l0, g_kv_l0, w_ukv_l0, conv_w_l0, conv_b_l0, w_rg_l0, b_rg_l0, w_ig_l0, b_ig_l0, lam_l0, w_o_l0, w_pool_l1, s_pool_l1, peer_wq_l0, peer_keys_l0, peer_u_l0, peer_v_l0, peer_wq_l1, peer_keys_l1, peer_u_l1, peer_v_l1, g_final):
    nb_p, s_p, d = x_prompt.shape
    nb_s, s_s, _ = x_sample.shape
    n_cache = cache_ckv_l0.shape[1]
    assert d == D_MODEL and s_p == TM and s_s % TM == 0 and n_cache % TM == 0
    t_p = nb_p * s_p
    t_s = nb_s * s_s
    t_all = t_p + t_s
    npt = t_p // TM
    tps = s_s // TM
    ntile = t_all // TM
    assert t_all % PEER_TM == 0 and t_p % PEER_TM == 0 and s_s % PEER_TM == 0

    x0 = jnp.concatenate([x_prompt.reshape(t_p, d), x_sample.reshape(t_s, d)], axis=0)

    ncond = 1 + nb_s
    cpad = jnp.zeros((2 * SUBLANES, d), F32).at[0].set(c_ctx).at[1:ncond].set(c)
    mod0 = _ada(cpad, w_mod_l0, b_mod_l0).reshape(2 * SUBLANES, 6, d)
    mod1 = _ada(cpad, w_mod_l1, b_mod_l1).reshape(2 * SUBLANES, 6, d)

    def cond_row(i):
        return jnp.where(i < npt, 0, 1 + (i - npt) // tps)

    mod_spec = pl.BlockSpec((1, 6, d), lambda i: (cond_row(i), 0, 0))
    row = lambda a: a.reshape(1, -1)

    perm = np.concatenate([np.arange(0, QK_ROPE, 2), np.arange(1, QK_ROPE, 2)])
    perm_sw = np.concatenate([np.arange(1, QK_ROPE, 2), np.arange(0, QK_ROPE, 2)])
    o1 = Q_LORA + KV_LORA
    w_kr = w_in_l0[:, o1:o1 + QK_ROPE]
    z64 = jnp.zeros((d, 128 - QK_ROPE), F32)
    w_in_ext = jnp.concatenate(
        [w_in_l0[:, :o1], w_kr, z64, w_kr[:, perm], z64, w_kr[:, perm_sw], z64, w_in_l0[:, o1 + QK_ROPE:]],
        axis=1).astype(BF16)
    assert w_in_ext.shape[1] == IN_EXT
    wq3 = w_uq_l0.reshape(Q_LORA, MLA_HEADS, QK_NOPE + QK_ROPE)
    w_uq_ext = jnp.concatenate(
        [wq3[:, :, :QK_NOPE].reshape(Q_LORA, -1),
         wq3[:, :, QK_NOPE:][:, :, perm].reshape(Q_LORA, -1),
         wq3[:, :, QK_NOPE:][:, :, perm_sw].reshape(Q_LORA, -1)], axis=1).astype(BF16)
    ck_tab, sk_tab = _rope_tables(TM, s_s)

    def rope_blk(i):
        return jnp.where(i < npt, 0, 1 + (i - npt) % tps)

    tok = lambda w: pl.BlockSpec((TM, w), lambda i: (i, 0))
    full = lambda a: pl.BlockSpec(a.shape, lambda *_: (0,) * a.ndim)
    q, ckv, kr, krr, ux, ug = pl.pallas_call(
        _inproj_kernel,
        grid=(ntile,),
        in_specs=[tok(d), mod_spec, full(row(g_mix_l0)), full(w_in_ext), full(row(g_q_l0)), full(w_uq_ext),
                  full(row(g_kv_l0)),
                  pl.BlockSpec((TM, QK_ROPE), lambda i: (rope_blk(i), 0)),
                  pl.BlockSpec((TM, QK_ROPE), lambda i: (rope_blk(i), 0))],
        out_specs=[pl.BlockSpec((MLA_HEADS, TM, QK_PAD), lambda i: (0, i, 0)),
                   tok(KV_LORA), tok(QK_ROPE), tok(QK_ROPE), tok(LRU_WIDTH), tok(LRU_WIDTH)],
        out_shape=[jax.ShapeDtypeStruct((MLA_HEADS, t_all, QK_PAD), BF16),
                   jax.ShapeDtypeStruct((t_all, KV_LORA), F32),
                   jax.ShapeDtypeStruct((t_all, QK_ROPE), F32),
                   jax.ShapeDtypeStruct((t_all, QK_ROPE), BF16),
                   jax.ShapeDtypeStruct((t_all, LRU_WIDTH), F32),
                   jax.ShapeDtypeStruct((t_all, LRU_WIDTH), F32)],
        compiler_params=_cparams(("arbitrary",)),
    )(x0, mod0, row(g_mix_l0), w_in_ext, row(g_q_l0), w_uq_ext, row(g_kv_l0), ck_tab, sk_tab)

    wkv3 = w_ukv_l0.reshape(KV_LORA, MLA_HEADS, QK_NOPE + V_HEAD)
    w_ukv_ext = jnp.concatenate([wkv3[:, :, :QK_NOPE].reshape(KV_LORA, -1),
                                 wkv3[:, :, QK_NOPE:].reshape(KV_LORA, -1)], axis=1).astype(BF16)

    def attn_call(nb, s_new, tile0, has_cache):
        nq = s_new // TM
        blk0 = tile0 * TM // s_new
        n_c = n_cache if has_cache else 0
        in_specs = [pl.BlockSpec((MLA_HEADS, TM, QK_PAD), lambda b, qi: (0, tile0 + b * nq + qi, 0)),
                    pl.BlockSpec((s_new, KV_LORA), lambda b, qi: (blk0 + b, 0)),
                    pl.BlockSpec((s_new, QK_ROPE), lambda b, qi: (blk0 + b, 0))]
        args = [q, ckv, krr]
        if has_cache:
            in_specs += [pl.BlockSpec((1, n_cache, KV_LORA), lambda b, qi: (b, 0, 0)),
                         pl.BlockSpec((1, n_cache, QK_ROPE), lambda b, qi: (b, 0, 0))]
            args += [cache_ckv_l0, cache_krope_l0[:, :, perm]]
        in_specs.append(pl.BlockSpec(w_ukv_ext.shape, lambda b, qi: (0, 0)))
        args.append(w_ukv_ext)
        return pl.pallas_call(
            functools.partial(_attn_kernel, has_cache=has_cache, s_new=s_new, n_cache=n_c),
            grid=(nb, nq),
            in_specs=in_specs,
            out_specs=pl.BlockSpec((TM, MLA_WIDTH), lambda b, qi: (b * nq + qi, 0)),
            out_shape=jax.ShapeDtypeStruct((nb * s_new, MLA_WIDTH), BF16),
            scratch_shapes=[pltpu.VMEM((MLA_HEADS, n_c + s_new, QK_PAD), BF16),
                            pltpu.VMEM((n_c + s_new, MLA_WIDTH), BF16)],
            compiler_params=_cparams(("arbitrary", "arbitrary")),
        )(*args)

    attn = jnp.concatenate([attn_call(nb_p, s_p, 0, False), attn_call(nb_s, s_s, npt, True)], axis=0)

    def lru_call(nb, s_new, tile0, h0):
        nc = s_new // TM
        r8 = TM // SUBLANES
        last8 = t_all // SUBLANES - 1

        def cur(rev):
            return pl.BlockSpec((TM, LRU_WIDTH),
                                lambda b, cc: (tile0 + b * nc + (nc - 1 - cc if rev else cc), 0))

        def prev(rev):
            return pl.BlockSpec((SUBLANES, LRU_WIDTH), lambda b, cc: (
                jnp.maximum((tile0 + b * nc + (nc - 1 - cc if rev else cc)) * r8 - 1, 0), 0))

        def nxt(rev):
            return pl.BlockSpec((SUBLANES, LRU_WIDTH), lambda b, cc: (
                jnp.minimum((tile0 + b * nc + (nc - 1 - cc if rev else cc) + 1) * r8, last8), 0))

        small = [conv_w_l0, row(conv_b_l0), w_rg_l0, b_rg_l0, w_ig_l0, b_ig_l0, lam_l0]
        return pl.pallas_call(
            functools.partial(_lru_kernel, nc=nc),
            grid=(nb, nc),
            in_specs=[prev(False), cur(False), nxt(False), prev(True), cur(True), nxt(True)]
                     + [full(a) for a in small]
                     + [pl.BlockSpec((1, 2, LRU_WIDTH), lambda b, cc: (b, 0, 0))],
            out_specs=[pl.BlockSpec((TM, LRU_WIDTH), lambda b, cc: (b * nc + cc, 0)),
                       pl.BlockSpec((TM, LRU_WIDTH), lambda b, cc: (b * nc + nc - 1 - cc, 0)),
                       pl.BlockSpec((1, 2, LRU_WIDTH), lambda b, cc: (b, 0, 0))],
            out_shape=[jax.ShapeDtypeStruct((nb * s_new, LRU_WIDTH), F32),
                       jax.ShapeDtypeStruct((nb * s_new, LRU_WIDTH), F32),
                       jax.ShapeDtypeStruct((nb, 2, LRU_WIDTH), F32)],
            scratch_shapes=[pltpu.VMEM((1, LRU_WIDTH), F32), pltpu.VMEM((1, LRU_WIDTH), F32)],
            compiler_params=_cparams(("arbitrary", "arbitrary")),
        )(ux, ux, ux, ux, ux, ux, *small, h0)

    hf_p, hb_p, new_lru = lru_call(nb_p, s_p, 0, jnp.zeros((nb_p, 2, LRU_WIDTH), F32))
    hf_s, hb_s, _ = lru_call(nb_s, s_s, npt, state_lru_l0.astype(F32))
    hf = jnp.concatenate([hf_p, hf_s], axis=0)
    hb = jnp.concatenate([hb_p, hb_s], axis=0)

    w_o = w_o_l0.astype(BF16)
    x1 = pl.pallas_call(
        _oproj_kernel,
        grid=(ntile,),
        in_specs=[tok(d), mod_spec, tok(MLA_WIDTH), tok(LRU_WIDTH), tok(LRU_WIDTH), tok(LRU_WIDTH),
                  pl.BlockSpec((MLA_WIDTH, d), lambda i: (0, 0)), pl.BlockSpec((LRU_WIDTH, d), lambda i: (1, 0))],
        out_specs=tok(d),
        out_shape=jax.ShapeDtypeStruct((t_all, d), F32),
        compiler_params=_cparams(("arbitrary",)),
    )(x0, mod0, attn, hf, hb, ug, w_o, w_o)

    def peer_call(x, mod, g_ffn, w_q, sub_keys, u, v, final_norm):
        n_exp = u.shape[0]
        assert n_exp == N_KEYS * N_KEYS and n_exp % PEER_EBLK == 0
        wqt = w_q.T.astype(BF16)
        keys = sub_keys.astype(BF16).reshape(2 * PEER_HEADS, N_KEYS, PEER_DKEY // 2)
        ub = u.astype(BF16)
        vt = v.reshape(n_exp // PEER_EBLK, PEER_EBLK, d).transpose(0, 2, 1).astype(BF16)
        tpp = PEER_TM // TM

        nblk = n_exp // PEER_EBLK

        def cond_row_p(i):
            return cond_row(i * tpp)

        big = lambda dt: pltpu.VMEM((PEER_HEADS, N_KEYS, PEER_TM), dt)
        return pl.pallas_call(
            functools.partial(_peer_kernel, final_norm=final_norm),
            grid=(t_all // PEER_TM, nblk),
            in_specs=[pl.BlockSpec((PEER_TM, d), lambda i, k: (i, 0), pipeline_mode=pl.Buffered(1)),
                      pl.BlockSpec((1, 6, d), lambda i, k: (cond_row_p(i), 0, 0)),
                      pl.BlockSpec((1, d), lambda i, k: (0, 0)),
                      pl.BlockSpec(wqt.shape, lambda i, k: (0, 0)),
                      pl.BlockSpec(keys.shape, lambda i, k: (0, 0, 0)),
                      pl.BlockSpec((PEER_EBLK, d), lambda i, k: (k, 0)),
                      pl.BlockSpec((1, d, PEER_EBLK), lambda i, k: (k, 0, 0)),
                      pl.BlockSpec((1, d), lambda i, k: (0, 0))],
            out_specs=pl.BlockSpec((PEER_TM, d), lambda i, k: (i, 0), pipeline_mode=pl.Buffered(1)),
            out_shape=jax.ShapeDtypeStruct((t_all, d), F32),
            scratch_shapes=[pltpu.VMEM((d, PEER_TM), BF16),
                            pltpu.VMEM((N_KEYS // BF16_ROWS, PEER_TM // LANES, PEER_HEADS, 2, BF16_ROWS, LANES),
                                       BF16),
                            pltpu.VMEM((2 * PEER_HEADS * N_KEYS, PEER_TM), F32),
                            pltpu.VMEM((2, PEER_TOPK, PEER_HEADS, PEER_TM), F32),
                            pltpu.VMEM((PEER_TOPK + 1, PEER_HEADS, PEER_TM), F32),
                            pltpu.VMEM((2, N_KEYS, PEER_TM), F32),
                            big(F32), big(F32),
                            pltpu.VMEM((2, PEER_EBLK, PEER_TM), BF16)],
            compiler_params=_cparams(("arbitrary", "arbitrary")),
        )(x, mod, row(g_ffn), wqt, keys, ub, vt, row(g_final))

    x2 = peer_call(x1, mod0, g_ffn_l0, peer_wq_l0, peer_keys_l0, peer_u_l0, peer_v_l0, False)

    rh = TM // HALO
    lasth = t_all // HALO - 1
    x3 = pl.pallas_call(
        functools.partial(_pool_kernel, tiles_per_seq=tps, n_prompt_tiles=npt),
        grid=(ntile,),
        in_specs=[pl.BlockSpec((HALO, d), lambda i: (jnp.maximum(i * rh - 1, 0), 0)),
                  tok(d),
                  pl.BlockSpec((HALO, d), lambda i: (jnp.minimum((i + 1) * rh, lasth), 0)),
                  mod_spec, full(row(g_mix_l1)),
                  pl.BlockSpec(w_pool_l1.shape, lambda i: (0, 0, 0)), full(row(s_pool_l1))],
        out_specs=tok(d),
        out_shape=jax.ShapeDtypeStruct((t_all, d), F32),
        compiler_params=_cparams(("arbitrary",)),
    )(x2, x2, x2, mod1, row(g_mix_l1), w_pool_l1.astype(BF16), row(s_pool_l1))

    y = peer_call(x3, mod1, g_ffn_l1, peer_wq_l1, peer_keys_l1, peer_u_l1, peer_v_l1, True)

    y_prompt = y[:t_p].reshape(nb_p, s_p, d)
    y_sample = y[t_p:].reshape(nb_s, s_s, d)
    new_ckv = ckv[:t_p].reshape(nb_p, s_p, KV_LORA)
    new_krope = kr[:t_p].reshape(nb_p, s_p, QK_ROPE)
    return (y_prompt, y_sample, new_ckv, new_krope, new_lru)
```

```python
import functools

import numpy as np
import jax
import jax.numpy as jnp
from jax import lax
from jax.experimental import pallas as pl
from jax.experimental.pallas import tpu as pltpu

F32 = jnp.float32
BF16 = jnp.bfloat16

D_MODEL = 1024
EPS = 1e-6
GRID_W = 64
MLA_HEADS = 4
Q_LORA = 384
KV_LORA = 256
QK_NOPE = 128
QK_ROPE = 64
V_HEAD = 128
MLA_WIDTH = MLA_HEADS * V_HEAD
ROPE_BASE = 10000.0
LRU_WIDTH = 512
LRU_BLOCKS = 4
LRU_BLOCK = LRU_WIDTH // LRU_BLOCKS
CONV_W = 4
CONV_LEFT = 2
LRU_C = 8.0
POOL_WINDOWS = (2, 4, 8, 16)
POOL_GROUP = D_MODEL // len(POOL_WINDOWS)
PEER_HEADS = 8
N_KEYS = 128
PEER_DKEY = 256
PEER_TOPK = 16

SUBLANES = 8
LANES = 128
VMEM_BYTES = 64 * 1024 * 1024
VMEM_LIMIT = VMEM_BYTES - 4 * 1024 * 1024

TM = 256
QK_PAD = 256
HALO = 16
PEER_TM = 1024
PEER_EBLK = 512
PEER_SUB = 256
BF16_ROWS = 16
IN_EXT = 2048


def _rms(x, g):
    return x * lax.rsqrt(jnp.mean(x * x, axis=-1, keepdims=True) + EPS) * g


def _cparams(sem):
    return pltpu.CompilerParams(dimension_semantics=sem, vmem_limit_bytes=VMEM_LIMIT)


def _ada_kernel(c_ref, w_ref, b_ref, o_ref):
    c = c_ref[...]
    o_ref[...] = jnp.dot(c * jax.nn.sigmoid(c), w_ref[...], preferred_element_type=F32) + b_ref[...]


def _ada(cpad, w_mod, b_mod):
    n = w_mod.shape[1]
    bn = 768
    return pl.pallas_call(
        _ada_kernel,
        grid=(n // bn,),
        in_specs=[pl.BlockSpec(cpad.shape, lambda j: (0, 0)),
                  pl.BlockSpec((D_MODEL, bn), lambda j: (0, j)),
                  pl.BlockSpec((1, bn), lambda j: (0, j))],
        out_specs=pl.BlockSpec((cpad.shape[0], bn), lambda j: (0, j)),
        out_shape=jax.ShapeDtypeStruct((cpad.shape[0], n), F32),
        compiler_params=_cparams(("arbitrary",)),
    )(cpad, w_mod, b_mod.reshape(1, n))


def _inproj_kernel(x_ref, mod_ref, g_ref, win_ref, gq_ref, wuq_ref, gkv_ref, ck_ref, sk_ref,
                   q_ref, ckv_ref, kr_ref, krr_ref, ux_ref, ug_ref):
    mod = mod_ref[0]
    h = _rms(x_ref[...], g_ref[...]) * (1.0 + mod[1:2]) + mod[0:1]
    y = jnp.dot(h.astype(BF16), win_ref[...], preferred_element_type=F32)
    cq = y[:, 0:Q_LORA]
    ckv = y[:, Q_LORA:Q_LORA + KV_LORA]
    o = Q_LORA + KV_LORA
    kr = y[:, o:o + QK_ROPE]
    krp = y[:, o + 128:o + 128 + QK_ROPE]
    krs = y[:, o + 256:o + 256 + QK_ROPE]
    ux_ref[...] = y[:, o + 384:o + 384 + LRU_WIDTH]
    ug_ref[...] = y[:, o + 384 + LRU_WIDTH:o + 384 + 2 * LRU_WIDTH]
    ckv_ref[...] = _rms(ckv, gkv_ref[...])
    kr_ref[...] = kr
    ck = ck_ref[...]
    sk = sk_ref[...]
    krr_ref[...] = (krp * ck + krs * sk).astype(BF16)
    q = jnp.dot(_rms(cq, gq_ref[...]).astype(BF16), wuq_ref[...], preferred_element_type=F32)
    nw = MLA_HEADS * QK_NOPE
    rw = MLA_HEADS * QK_ROPE
    for hd in range(MLA_HEADS):
        qp = q[:, nw + hd * QK_ROPE:nw + (hd + 1) * QK_ROPE]
        qs = q[:, nw + rw + hd * QK_ROPE:nw + rw + (hd + 1) * QK_ROPE]
        q_ref[hd, :, 0:QK_NOPE] = q[:, hd * QK_NOPE:(hd + 1) * QK_NOPE].astype(BF16)
        q_ref[hd, :, QK_NOPE:QK_NOPE + QK_ROPE] = (qp * ck + qs * sk).astype(BF16)
        q_ref[hd, :, QK_NOPE + QK_ROPE:QK_PAD] = jnp.zeros((TM, QK_PAD - QK_NOPE - QK_ROPE), BF16)


def _attn_kernel(*refs, has_cache, s_new, n_cache):
    if has_cache:
        q_ref, ckv_ref, krr_ref, cckv_ref, ckr_ref, wukv_ref, o_ref, kcat_ref, vv_ref = refs
    else:
        q_ref, ckv_ref, krr_ref, wukv_ref, o_ref, kcat_ref, vv_ref = refs
    sk = n_cache + s_new
    kw = MLA_HEADS * QK_NOPE
    zpad = jnp.zeros((TM, QK_PAD - QK_NOPE - QK_ROPE), BF16)

    def put_keys(row0, ckv_rows, kr_rows):
        kv = jnp.dot(ckv_rows.astype(BF16), wukv_ref[...], preferred_element_type=F32)
        rows = pl.ds(row0, TM)
        for hd in range(MLA_HEADS):
            kcat_ref[hd, rows, 0:QK_NOPE] = kv[:, hd * QK_NOPE:(hd + 1) * QK_NOPE].astype(BF16)
            kcat_ref[hd, rows, QK_NOPE:QK_NOPE + QK_ROPE] = kr_rows.astype(BF16)
            kcat_ref[hd, rows, QK_NOPE + QK_ROPE:QK_PAD] = zpad
        vv_ref[rows, :] = kv[:, kw:].astype(BF16)

    @pl.when(pl.program_id(1) == 0)
    def _():
        if has_cache:
            for c in range(n_cache // TM):
                put_keys(c * TM, cckv_ref[0, c * TM:(c + 1) * TM, :], ckr_ref[0, c * TM:(c + 1) * TM, :])

        def body(c, carry):
            r0 = pl.multiple_of(c * TM, TM)
            put_keys(n_cache + r0, ckv_ref[pl.ds(r0, TM), :], krr_ref[pl.ds(r0, TM), :])
            return carry
        lax.fori_loop(0, s_new // TM, body, 0)

    scale = (QK_NOPE + QK_ROPE) ** -0.5
    for hd in range(MLA_HEADS):
        s = lax.dot_general(q_ref[hd], kcat_ref[hd], (((1,), (1,)), ((), ())),
                            preferred_element_type=F32) * scale
        m = jnp.max(s, axis=-1, keepdims=True)
        e = jnp.exp(s - m)
        l = jnp.sum(e, axis=-1, keepdims=True)
        o = jnp.dot(e.astype(BF16), vv_ref[:, hd * V_HEAD:(hd + 1) * V_HEAD], preferred_element_type=F32)
        o_ref[:, hd * V_HEAD:(hd + 1) * V_HEAD] = (o / l).astype(BF16)
    del sk


def _lru_dir(xp_ref, xc_ref, xn_ref, valid_prev, valid_next, d, reverse, cw_ref, cb_ref,
             wr_ref, br_ref, wi_ref, bi_ref, lam_ref, carry):
    xp = jnp.where(valid_prev, xp_ref[...], 0.0)
    xn = jnp.where(valid_next, xn_ref[...], 0.0)
    xx = jnp.concatenate([xp, xc_ref[...], xn], axis=0)
    n = TM + 2 * SUBLANES
    xc = cb_ref[...]
    for k in range(CONV_W):
        sh = (CONV_LEFT - k) % n
        xs = xx if sh == 0 else pltpu.roll(xx, sh, 0)
        xc = xc + xs[SUBLANES:SUBLANES + TM] * cw_ref[k:k + 1, :]
    rs, is_ = [], []
    for b in range(LRU_BLOCKS):
        xb = xc[:, b * LRU_BLOCK:(b + 1) * LRU_BLOCK]
        rs.append(jnp.dot(xb, wr_ref[d, b], preferred_element_type=F32))
        is_.append(jnp.dot(xb, wi_ref[d, b], preferred_element_type=F32))
    r = jax.nn.sigmoid(jnp.concatenate(rs, axis=-1) + br_ref[d:d + 1, :])
    i = jax.nn.sigmoid(jnp.concatenate(is_, axis=-1) + bi_ref[d:d + 1, :])
    nl = -lam_ref[d:d + 1, :]
    softplus = jnp.maximum(nl, 0.0) + jnp.log1p(jnp.exp(-jnp.abs(nl)))
    log_a = -LRU_C * r * softplus
    a = jnp.exp(log_a)
    bx = jnp.sqrt(jnp.tanh(-log_a) * (a * a + 1.0)) * (i * xc)
    t = lax.broadcasted_iota(jnp.int32, (TM, 1), 0)
    step = 1
    while step < TM:
        if reverse:
            keep = t < TM - step
            sh = TM - step
        else:
            keep = t >= step
            sh = step
        a_s = jnp.where(keep, pltpu.roll(a, sh, 0), 1.0)
        b_s = jnp.where(keep, pltpu.roll(bx, sh, 0), 0.0)
        bx = a * b_s + bx
        a = a * a_s
        step *= 2
    return a * carry + bx


def _lru_kernel(fxp, fxc, fxn, bxp, bxc, bxn, cw_ref, cb_ref, wr_ref, br_ref, wi_ref, bi_ref, lam_ref,
                h0_ref, hf_ref, hb_ref, st_ref, cf_ref, cbk_ref, *, nc):
    c = pl.program_id(1)

    @pl.when(c == 0)
    def _():
        cf_ref[...] = h0_ref[0, 0:1, :]
        cbk_ref[...] = h0_ref[0, 1:2, :]

    params = (cw_ref, cb_ref, wr_ref, br_ref, wi_ref, bi_ref, lam_ref)
    hf = _lru_dir(fxp, fxc, fxn, c > 0, c < nc - 1, 0, False, *params, cf_ref[...])
    hf_ref[...] = hf
    cf_ref[...] = hf[TM - 1:TM, :]
    hb = _lru_dir(bxp, bxc, bxn, c < nc - 1, c > 0, 1, True, *params, cbk_ref[...])
    hb_ref[...] = hb
    cbk_ref[...] = hb[0:1, :]
    st_ref[0, 0:1, :] = hf[TM - 1:TM, :]
    st_ref[0, 1:2, :] = hb[0:1, :]


def _oproj_kernel(x_ref, mod_ref, at_ref, hf_ref, hb_ref, ug_ref, woa_ref, wor_ref, o_ref):
    mod = mod_ref[0]
    rec = ((hf_ref[...] + hb_ref[...]) * jax.nn.gelu(ug_ref[...])).astype(BF16)
    out = (jnp.dot(at_ref[...], woa_ref[...], preferred_element_type=F32)
           + jnp.dot(rec, wor_ref[...], preferred_element_type=F32))
    o_ref[...] = x_ref[...] + mod[2:3] * out


def _pool_kernel(xp_ref, xc_ref, xn_ref, mod_ref, g_ref, wp_ref, sp_ref, o_ref, *, tiles_per_seq, n_prompt_tiles):
    i = pl.program_id(0)
    j = jnp.where(i < n_prompt_tiles, 0, (i - n_prompt_tiles) % tiles_per_seq)
    ntile = jnp.where(i < n_prompt_tiles, 1, tiles_per_seq)
    mod = mod_ref[0]
    g = g_ref[...]

    def hmod(x):
        return _rms(x, g) * (1.0 + mod[1:2]) + mod[0:1]

    x = xc_ref[...]
    hc = hmod(x)
    hp = jnp.where(j > 0, hmod(xp_ref[...]), 0.0)
    hn = jnp.where(j < ntile - 1, hmod(xn_ref[...]), 0.0)
    hh = jnp.concatenate([hp, hc, hn], axis=0)
    n = TM + 2 * HALO
    seq_len = ntile * TM
    t = j * TM + lax.broadcasted_iota(jnp.int32, (TM, 1), 0)
    ys = []
    for gi, w in enumerate(POOL_WINDOWS):
        cols = slice(gi * POOL_GROUP, (gi + 1) * POOL_GROUP)
        p = hh[:, cols]
        p = p + pltpu.roll(p, 1, 0)
        half = 1
        while 2 * half < w:
            p = pltpu.roll(p, half, 0) + pltpu.roll(p, n - half, 0)
            half *= 2
        lo = jnp.maximum(t - w // 2, 0)
        hi = jnp.minimum(t + (w - w // 2), seq_len)
        mean = p[HALO:HALO + TM] / (hi - lo).astype(F32)
        dg = (mean - hc[:, cols]).astype(BF16)
        ys.append(jnp.dot(dg, wp_ref[gi], preferred_element_type=F32))
    y = jnp.concatenate(ys, axis=-1) * sp_ref[...]
    o_ref[...] = x + mod[2:3] * y


def _sort_pairs(n):
    pairs = []

    def merge(lo, cnt, r):
        step = r * 2
        if step < cnt:
            merge(lo, cnt, step)
            merge(lo + r, cnt, step)
            for i in range(lo + r, lo + cnt - r, step):
                pairs.append((i, i + r))
        else:
            pairs.append((lo, lo + r))

    def sort(lo, cnt):
        if cnt > 1:
            m = cnt // 2
            sort(lo, m)
            sort(lo + m, m)
            merge(lo, cnt, 1)

    sort(0, n)
    return pairs


_SORT16 = _sort_pairs(PEER_TOPK)
_HYPER = [(a, b) for a in range(PEER_TOPK) for b in range(PEER_TOPK) if (a + 1) * (b + 1) <= PEER_TOPK]


def _top16_sorted(s):
    k = PEER_TOPK
    x = [s[SUBLANES * r:SUBLANES * (r + 1), :] for r in range(N_KEYS // SUBLANES)]
    for (i, j) in _SORT16:
        hi = jnp.maximum(x[i], x[j])
        lo = jnp.minimum(x[i], x[j])
        x[i], x[j] = hi, lo
    for shift in (4, 2, 1):
        y = [jnp.maximum(x[r], pltpu.roll(x[k - 1 - r], shift, 0)) for r in range(k)]
        stride = k // 2
        while stride >= 1:
            for i in range(k):
                if i & stride == 0:
                    hi = jnp.maximum(y[i], y[i + stride])
                    lo = jnp.minimum(y[i], y[i + stride])
                    y[i], y[i + stride] = hi, lo
            stride //= 2
        x = y
    return x


def _pair_counts(sv1, sv2):
    one = jnp.ones_like(sv1[0])
    zero = jnp.zeros_like(sv1[0])
    cand = [sv1[a] + sv2[b] for (a, b) in _HYPER]

    def ordered(i, j):
        (ai, bi), (aj, bj) = _HYPER[i], _HYPER[j]
        return (aj <= ai and bj <= bi) or (ai <= aj and bi <= bj)

    nh = len(_HYPER)
    cnt = [float((a + 1) * (b + 1) - 1 + sum(1 for j in range(i + 1, nh) if not ordered(i, j))) * one
           for i, (a, b) in enumerate(_HYPER)]
    for i in range(nh):
        for jx in range(i):
            if ordered(i, jx):
                continue
            ge = jnp.where(cand[jx] >= cand[i], one, zero)
            cnt[i] = cnt[i] + ge
            cnt[jx] = cnt[jx] - ge
    e1 = [jnp.exp(sv1[a] - sv1[0]) for a in range(PEER_TOPK)]
    e2 = [jnp.exp(sv2[b] - sv2[0]) for b in range(PEER_TOPK)]
    n = [zero for _ in range(PEER_TOPK)]
    z = zero
    for i, (a, b) in enumerate(_HYPER):
        sel = jnp.where(cnt[i] < float(PEER_TOPK), one, zero)
        n[a] = n[a] + sel
        z = z + sel * (e1[a] * e2[b])
    return n, 1.0 / z


def _head_tables(hd, g, lanes, sc_ref, sv_ref, nz_ref, ex_ref, cnt1_ref, e1n_ref, tab_ref, exact):
    k = PEER_TOPK

    def bc(ref, *idx):
        return jnp.broadcast_to(ref[idx + (slice(hd, hd + 1), lanes)], (SUBLANES, LANES))

    sv1b = [bc(sv_ref, 0, a) for a in range(k)]
    nb = [bc(nz_ref, a) for a in range(k)]
    izb = bc(nz_ref, k)
    c1 = jnp.zeros((SUBLANES, LANES), F32)
    for r in range(N_KEYS // SUBLANES):
        rows = slice(r * SUBLANES, (r + 1) * SUBLANES)
        s1 = sc_ref[pl.ds(2 * hd * N_KEYS + r * SUBLANES, SUBLANES), lanes]
        cnt1 = jnp.zeros_like(s1)
        if exact:
            rank1 = jnp.zeros_like(s1)
            for a in range(k):
                rank1 = jnp.where(sv1b[a] > s1, float(a + 1), rank1)
            rank1 = rank1 + ex_ref[0, rows, lanes]
            for a in range(k):
                cnt1 = jnp.where(rank1 == float(a), nb[a], cnt1)
        else:
            for a in range(k):
                cnt1 = jnp.where(s1 == sv1b[a], nb[a], cnt1)
            c1 = c1 + jnp.where(s1 >= sv1b[k - 1], 1.0, 0.0)
        cnt1_ref[hd, rows, lanes] = cnt1
        e1n_ref[hd, rows, lanes] = jnp.exp(s1 - sv1b[0]) * izb
    sv2b = [bc(sv_ref, 1, a) for a in range(k)]
    c2 = jnp.zeros((SUBLANES, LANES), F32)
    for r in range(N_KEYS // BF16_ROWS):
        rk, e2 = [], []
        for q in range(BF16_ROWS // SUBLANES):
            r0 = r * BF16_ROWS + q * SUBLANES
            rows = slice(r0, r0 + SUBLANES)
            s2 = sc_ref[pl.ds((2 * hd + 1) * N_KEYS + r0, SUBLANES), lanes]
            rank2 = jnp.zeros_like(s2)
            for a in range(k):
                rank2 = jnp.where(sv2b[a] > s2, float(a + 1), rank2)
            if exact:
                rank2 = jnp.minimum(rank2 + ex_ref[1, rows, lanes], float(k))
            else:
                c2 = c2 + jnp.where(rank2 < float(k), 1.0, 0.0)
            rk.append(rank2)
            e2.append(jnp.exp(s2 - sv2b[0]))
        tab_ref[r, g, hd, 0] = jnp.concatenate(rk, axis=0).astype(BF16)
        tab_ref[r, g, hd, 1] = jnp.concatenate(e2, axis=0).astype(BF16)
    if exact:
        return None
    ex_ref[0, 0:SUBLANES, lanes] = c1
    ex_ref[1, 0:SUBLANES, lanes] = c2
    b = jnp.zeros((SUBLANES, LANES), F32)
    for svb in (sv1b, sv2b):
        for a in range(k - 1):
            b = b + jnp.where(svb[a] == svb[a + 1], 1.0, 0.0)
    return b


def _tie_offsets(hd, sc_ref, ex_ref):
    tm = sc_ref.shape[-1]
    nidx = lax.broadcasted_iota(jnp.int32, (N_KEYS, tm), 0)
    for p in range(2):
        s = sc_ref[pl.ds((2 * hd + p) * N_KEYS, N_KEYS), :]

        def body(m, e, s=s, p=p):
            row = sc_ref[pl.ds((2 * hd + p) * N_KEYS + m, 1), :]
            return e + jnp.where((s == row) & (nidx > m), 1.0, 0.0)

        ex_ref[p] = lax.fori_loop(0, N_KEYS, body, jnp.zeros((N_KEYS, tm), F32))


def _peer_prologue(x_ref, mod_ref, g_ref, wqt_ref, keys_ref, h2t_ref, tab_ref, sc_ref, sv_ref, nz_ref,
                   ex_ref, cnt1_ref, e1n_ref, y_ref):
    tm = PEER_TM
    half = PEER_DKEY // 2
    ngroup = tm // LANES
    mod = mod_ref[0]
    h2 = _rms(x_ref[...], g_ref[...]) * (1.0 + mod[4:5]) + mod[3:4]
    h2t_ref[...] = h2.T.astype(BF16)

    def lane_group(g):
        return pl.ds(pl.multiple_of(g * LANES, LANES), LANES)

    nslot, qrows = y_ref.shape[0], y_ref.shape[1]
    per_pass = nslot * qrows // half
    for i in range(2 * PEER_HEADS):
        if i % per_pass == 0:
            qt = jnp.dot(wqt_ref[i * half:(i + per_pass) * half, :], h2t_ref[...],
                         preferred_element_type=F32).astype(BF16)
            for s in range(nslot):
                y_ref[s, :, 0:tm] = qt[s * qrows:(s + 1) * qrows, :]
        j = (i % per_pass) * half
        sc_ref[i * N_KEYS:(i + 1) * N_KEYS, :] = jnp.dot(
            keys_ref[i], y_ref[j // qrows, j % qrows:j % qrows + half, 0:tm],
            preferred_element_type=F32)

        def group(g, c, i=i):
            lanes = lane_group(g)
            top = _top16_sorted(sc_ref[i * N_KEYS:(i + 1) * N_KEYS, lanes])
            for a in range(PEER_TOPK):
                sv_ref[i % 2, a, i // 2:i // 2 + 1, lanes] = top[a][0:1, :]
            return c
        lax.fori_loop(0, ngroup, group, 0)

    def counts(g, carry):
        lanes = lane_group(g)
        sv1 = [sv_ref[0, a, :, lanes] for a in range(PEER_TOPK)]
        sv2 = [sv_ref[1, a, :, lanes] for a in range(PEER_TOPK)]
        n, inv_z = _pair_counts(sv1, sv2)
        for a in range(PEER_TOPK):
            nz_ref[a, :, lanes] = n[a]
        nz_ref[PEER_TOPK, :, lanes] = inv_z
        return carry
    lax.fori_loop(0, ngroup, counts, 0)

    tabs = (sc_ref, sv_ref, nz_ref, ex_ref, cnt1_ref, e1n_ref, tab_ref)

    for hd in range(PEER_HEADS):
        def fast(g, bad, hd=hd):
            return bad + _head_tables(hd, g, lane_group(g), *tabs, exact=False)
        bad = lax.fori_loop(0, ngroup, fast, jnp.zeros((SUBLANES, LANES), F32))
        totals = jnp.sum(ex_ref[:, 0:SUBLANES, :], axis=1)
        ties = jnp.max(bad) + jnp.max(jnp.abs(totals - float(PEER_TOPK)))

        @pl.when(ties > 0.0)
        def _(hd=hd):
            _tie_offsets(hd, sc_ref, ex_ref)

            def slow(g, c):
                _head_tables(hd, g, lane_group(g), *tabs, exact=True)
                return c
            lax.fori_loop(0, ngroup, slow, 0)
    sc_ref[0:D_MODEL, :] = jnp.zeros((D_MODEL, tm), F32)


def _peer_kernel(x_ref, mod_ref, g_ref, wqt_ref, keys_ref, u_ref, vt_ref, gfin_ref, o_ref,
                 h2t_ref, tab_ref, sc_ref, sv_ref, nz_ref, ex_ref, cnt1_ref, e1n_ref, y_ref, *, final_norm):
    k = pl.program_id(1)
    tm = PEER_TM
    acc_ref = sc_ref.at[0:D_MODEL]

    @pl.when(k == 0)
    def _():
        _peer_prologue(x_ref, mod_ref, g_ref, wqt_ref, keys_ref, h2t_ref, tab_ref, sc_ref, sv_ref, nz_ref,
                       ex_ref, cnt1_ref, e1n_ref, y_ref)

    zero = jnp.zeros((BF16_ROWS, LANES), BF16)
    npair = PEER_EBLK // (2 * N_KEYS)
    ngroup = tm // LANES

    def up_dot(q):
        return jnp.dot(u_ref[2 * q * N_KEYS:2 * (q + 1) * N_KEYS, :], h2t_ref[...], preferred_element_type=F32)

    def gate_pair(q, raws):
        i1 = (k * npair + q) * 2
        crow = [[cnt1_ref[hd, pl.ds(i1 + j, 1), :] for hd in range(PEER_HEADS)] for j in range(2)]
        erow = [[e1n_ref[hd, pl.ds(i1 + j, 1), :] for hd in range(PEER_HEADS)] for j in range(2)]
        out = [[[None] * ngroup for _ in range(N_KEYS // BF16_ROWS)] for _ in range(2)]
        for g in range(ngroup):
            lanes = slice(g * LANES, (g + 1) * LANES)
            cbs = [[jnp.broadcast_to(crow[j][hd][:, lanes], (BF16_ROWS, LANES)).astype(BF16)
                    for hd in range(PEER_HEADS)] for j in range(2)]
            ebs = [[jnp.broadcast_to(erow[j][hd][:, lanes], (BF16_ROWS, LANES)).astype(BF16)
                    for hd in range(PEER_HEADS)] for j in range(2)]
            for r in range(N_KEYS // BF16_ROWS):
                w = [None, None]
                for hd in range(PEER_HEADS):
                    rk = tab_ref[r, g, hd, 0]
                    ev = tab_ref[r, g, hd, 1]
                    for j in range(2):
                        t = jnp.where(rk < cbs[j][hd], ev, zero) * ebs[j][hd]
                        w[j] = t if w[j] is None else w[j] + t
                for j in range(2):
                    r0 = j * N_KEYS + r * BF16_ROWS
                    act = jax.nn.gelu(raws[r0:r0 + BF16_ROWS, lanes].astype(BF16))
                    out[j][r][g] = act * w[j]
        return jnp.concatenate([jnp.concatenate(out[j][r], axis=1)
                                for j in range(2) for r in range(N_KEYS // BF16_ROWS)], axis=0)

    raws = up_dot(0)
    for q in range(npair):
        nxt = up_dot(q + 1) if q + 1 < npair else None
        y = gate_pair(q, raws)
        e0 = 2 * q * N_KEYS
        acc_ref[...] += jnp.dot(vt_ref[0, :, e0:e0 + 2 * N_KEYS], y, preferred_element_type=F32)
        raws = nxt

    @pl.when(k == pl.num_programs(1) - 1)
    def _():
        out = x_ref[...] + mod_ref[0][5:6] * acc_ref[...].T
        if final_norm:
            out = _rms(out, gfin_ref[...])
        o_ref[...] = out


def _rope_tables(s_prompt_tile, s_sample):
    n_rows = s_sample // GRID_W
    rows = jnp.repeat(jnp.arange(n_rows, dtype=F32), GRID_W)
    cols = jnp.tile(jnp.arange(GRID_W, dtype=F32), n_rows)
    axis_dim = QK_ROPE // 2
    inv_freq = ROPE_BASE ** (-jnp.arange(0, axis_dim, 2, dtype=F32) / axis_dim)
    ang = jnp.concatenate([rows[:, None] * inv_freq, cols[:, None] * inv_freq], axis=-1)
    cos, sin = jnp.cos(ang), jnp.sin(ang)
    ck = jnp.concatenate([cos, cos], axis=-1)
    sk = jnp.concatenate([-sin, sin], axis=-1)
    ident_c = jnp.ones((s_prompt_tile, QK_ROPE), F32)
    ident_s = jnp.zeros((s_prompt_tile, QK_ROPE), F32)
    return jnp.concatenate([ident_c, ck], axis=0), jnp.concatenate([ident_s, sk], axis=0)


def kernel(x_prompt, x_sample, cache_ckv_l0, cache_krope_l0, state_lru_l0, c, c_ctx, w_mod_l0, b_mod_l0, w_mod_l1, b_mod_l1, g_mix_l0, g_ffn_l0, g_mix_l1, g_ffn_l1, w_in_l0, g_q_l0, w_uq_l0, g_kv_l0, w_ukv_l0, conv_w_l0, conv_b_l0, w_rg_l0, b_rg_l0, w_ig_l0, b_ig_l0, lam_l0, w_o_l0, w_pool_l1, s_pool_l1, peer_wq_l0, peer_keys_l0, peer_u_l0, peer_v_l0, peer_wq_l1, peer_keys_l1, peer_u_l1, peer_v_l1, g_final):
    nb_p, s_p, d = x_prompt.shape
    nb_s, s_s, _ = x_sample.shape
    n_cache = cache_ckv_l0.shape[1]
    assert d == D_MODEL and s_p == TM and s_s % TM == 0 and n_cache % TM == 0
    t_p = nb_p * s_p
    t_s = nb_s * s_s
    t_all = t_p + t_s
    npt = t_p // TM
    tps = s_s // TM
    ntile = t_all // TM
    assert t_all % PEER_TM == 0 and t_p % PEER_TM == 0 and s_s % PEER_TM == 0

    x0 = jnp.concatenate([x_prompt.reshape(t_p, d), x_sample.reshape(t_s, d)], axis=0)

    ncond = 1 + nb_s
    cpad = jnp.zeros((2 * SUBLANES, d), F32).at[0].set(c_ctx).at[1:ncond].set(c)
    mod0 = _ada(cpad, w_mod_l0, b_mod_l0).reshape(2 * SUBLANES, 6, d)
    mod1 = _ada(cpad, w_mod_l1, b_mod_l1).reshape(2 * SUBLANES, 6, d)

    def cond_row(i):
        return jnp.where(i < npt, 0, 1 + (i - npt) // tps)

    mod_spec = pl.BlockSpec((1, 6, d), lambda i: (cond_row(i), 0, 0))
    row = lambda a: a.reshape(1, -1)

    perm = np.concatenate([np.arange(0, QK_ROPE, 2), np.arange(1, QK_ROPE, 2)])
    perm_sw = np.concatenate([np.arange(1, QK_ROPE, 2), np.arange(0, QK_ROPE, 2)])
    o1 = Q_LORA + KV_LORA
    w_kr = w_in_l0[:, o1:o1 + QK_ROPE]
    z64 = jnp.zeros((d, 128 - QK_ROPE), F32)
    w_in_ext = jnp.concatenate(
        [w_in_l0[:, :o1], w_kr, z64, w_kr[:, perm], z64, w_kr[:, perm_sw], z64, w_in_l0[:, o1 + QK_ROPE:]],
        axis=1).astype(BF16)
    assert w_in_ext.shape[1] == IN_EXT
    wq3 = w_uq_l0.reshape(Q_LORA, MLA_HEADS, QK_NOPE + QK_ROPE)
    w_uq_ext = jnp.concatenate(
        [wq3[:, :, :QK_NOPE].reshape(Q_LORA, -1),
         wq3[:, :, QK_NOPE:][:, :, perm].reshape(Q_LORA, -1),
         wq3[:, :, QK_NOPE:][:, :, perm_sw].reshape(Q_LORA, -1)], axis=1).astype(BF16)
    ck_tab, sk_tab = _rope_tables(TM, s_s)

    def rope_blk(i):
        return jnp.where(i < npt, 0, 1 + (i - npt) % tps)

    tok = lambda w: pl.BlockSpec((TM, w), lambda i: (i, 0))
    full = lambda a: pl.BlockSpec(a.shape, lambda *_: (0,) * a.ndim)
    q, ckv, kr, krr, ux, ug = pl.pallas_call(
        _inproj_kernel,
        grid=(ntile,),
        in_specs=[tok(d), mod_spec, full(row(g_mix_l0)), full(w_in_ext), full(row(g_q_l0)), full(w_uq_ext),
                  full(row(g_kv_l0)),
                  pl.BlockSpec((TM, QK_ROPE), lambda i: (rope_blk(i), 0)),
                  pl.BlockSpec((TM, QK_ROPE), lambda i: (rope_blk(i), 0))],
        out_specs=[pl.BlockSpec((MLA_HEADS, TM, QK_PAD), lambda i: (0, i, 0)),
                   tok(KV_LORA), tok(QK_ROPE), tok(QK_ROPE), tok(LRU_WIDTH), tok(LRU_WIDTH)],
        out_shape=[jax.ShapeDtypeStruct((MLA_HEADS, t_all, QK_PAD), BF16),
                   jax.ShapeDtypeStruct((t_all, KV_LORA), F32),
                   jax.ShapeDtypeStruct((t_all, QK_ROPE), F32),
                   jax.ShapeDtypeStruct((t_all, QK_ROPE), BF16),
                   jax.ShapeDtypeStruct((t_all, LRU_WIDTH), F32),
                   jax.ShapeDtypeStruct((t_all, LRU_WIDTH), F32)],
        compiler_params=_cparams(("arbitrary",)),
    )(x0, mod0, row(g_mix_l0), w_in_ext, row(g_q_l0), w_uq_ext, row(g_kv_l0), ck_tab, sk_tab)

    wkv3 = w_ukv_l0.reshape(KV_LORA, MLA_HEADS, QK_NOPE + V_HEAD)
    w_ukv_ext = jnp.concatenate([wkv3[:, :, :QK_NOPE].reshape(KV_LORA, -1),
                                 wkv3[:, :, QK_NOPE:].reshape(KV_LORA, -1)], axis=1).astype(BF16)

    def attn_call(nb, s_new, tile0, has_cache):
        nq = s_new // TM
        blk0 = tile0 * TM // s_new
        n_c = n_cache if has_cache else 0
        in_specs = [pl.BlockSpec((MLA_HEADS, TM, QK_PAD), lambda b, qi: (0, tile0 + b * nq + qi, 0)),
                    pl.BlockSpec((s_new, KV_LORA), lambda b, qi: (blk0 + b, 0)),
                    pl.BlockSpec((s_new, QK_ROPE), lambda b, qi: (blk0 + b, 0))]
        args = [q, ckv, krr]
        if has_cache:
            in_specs += [pl.BlockSpec((1, n_cache, KV_LORA), lambda b, qi: (b, 0, 0)),
                         pl.BlockSpec((1, n_cache, QK_ROPE), lambda b, qi: (b, 0, 0))]
            args += [cache_ckv_l0, cache_krope_l0[:, :, perm]]
        in_specs.append(pl.BlockSpec(w_ukv_ext.shape, lambda b, qi: (0, 0)))
        args.append(w_ukv_ext)
        return pl.pallas_call(
            functools.partial(_attn_kernel, has_cache=has_cache, s_new=s_new, n_cache=n_c),
            grid=(nb, nq),
            in_specs=in_specs,
            out_specs=pl.BlockSpec((TM, MLA_WIDTH), lambda b, qi: (b * nq + qi, 0)),
            out_shape=jax.ShapeDtypeStruct((nb * s_new, MLA_WIDTH), BF16),
            scratch_shapes=[pltpu.VMEM((MLA_HEADS, n_c + s_new, QK_PAD), BF16),
                            pltpu.VMEM((n_c + s_new, MLA_WIDTH), BF16)],
            compiler_params=_cparams(("arbitrary", "arbitrary")),
        )(*args)

    attn = jnp.concatenate([attn_call(nb_p, s_p, 0, False), attn_call(nb_s, s_s, npt, True)], axis=0)

    def lru_call(nb, s_new, tile0, h0):
        nc = s_new // TM
        r8 = TM // SUBLANES
        last8 = t_all // SUBLANES - 1

        def cur(rev):
            return pl.BlockSpec((TM, LRU_WIDTH),
                                lambda b, cc: (tile0 + b * nc + (nc - 1 - cc if rev else cc), 0))

        def prev(rev):
            return pl.BlockSpec((SUBLANES, LRU_WIDTH), lambda b, cc: (
                jnp.maximum((tile0 + b * nc + (nc - 1 - cc if rev else cc)) * r8 - 1, 0), 0))

        def nxt(rev):
            return pl.BlockSpec((SUBLANES, LRU_WIDTH), lambda b, cc: (
                jnp.minimum((tile0 + b * nc + (nc - 1 - cc if rev else cc) + 1) * r8, last8), 0))

        small = [conv_w_l0, row(conv_b_l0), w_rg_l0, b_rg_l0, w_ig_l0, b_ig_l0, lam_l0]
        return pl.pallas_call(
            functools.partial(_lru_kernel, nc=nc),
            grid=(nb, nc),
            in_specs=[prev(False), cur(False), nxt(False), prev(True), cur(True), nxt(True)]
                     + [full(a) for a in small]
                     + [pl.BlockSpec((1, 2, LRU_WIDTH), lambda b, cc: (b, 0, 0))],
            out_specs=[pl.BlockSpec((TM, LRU_WIDTH), lambda b, cc: (b * nc + cc, 0)),
                       pl.BlockSpec((TM, LRU_WIDTH), lambda b, cc: (b * nc + nc - 1 - cc, 0)),
                       pl.BlockSpec((1, 2, LRU_WIDTH), lambda b, cc: (b, 0, 0))],
            out_shape=[jax.ShapeDtypeStruct((nb * s_new, LRU_WIDTH), F32),
                       jax.ShapeDtypeStruct((nb * s_new, LRU_WIDTH), F32),
                       jax.ShapeDtypeStruct((nb, 2, LRU_WIDTH), F32)],
            scratch_shapes=[pltpu.VMEM((1, LRU_WIDTH), F32), pltpu.VMEM((1, LRU_WIDTH), F32)],
            compiler_params=_cparams(("arbitrary", "arbitrary")),
        )(ux, ux, ux, ux, ux, ux, *small, h0)

    hf_p, hb_p, new_lru = lru_call(nb_p, s_p, 0, jnp.zeros((nb_p, 2, LRU_WIDTH), F32))
    hf_s, hb_s, _ = lru_call(nb_s, s_s, npt, state_lru_l0.astype(F32))
    hf = jnp.concatenate([hf_p, hf_s], axis=0)
    hb = jnp.concatenate([hb_p, hb_s], axis=0)

    w_o = w_o_l0.astype(BF16)
    x1 = pl.pallas_call(
        _oproj_kernel,
        grid=(ntile,),
        in_specs=[tok(d), mod_spec, tok(MLA_WIDTH), tok(LRU_WIDTH), tok(LRU_WIDTH), tok(LRU_WIDTH),
                  pl.BlockSpec((MLA_WIDTH, d), lambda i: (0, 0)), pl.BlockSpec((LRU_WIDTH, d), lambda i: (1, 0))],
        out_specs=tok(d),
        out_shape=jax.ShapeDtypeStruct((t_all, d), F32),
        compiler_params=_cparams(("arbitrary",)),
    )(x0, mod0, attn, hf, hb, ug, w_o, w_o)

    def peer_call(x, mod, g_ffn, w_q, sub_keys, u, v, final_norm):
        n_exp = u.shape[0]
        assert n_exp == N_KEYS * N_KEYS and n_exp % PEER_EBLK == 0
        wqt = w_q.T.astype(BF16)
        keys = sub_keys.astype(BF16).reshape(2 * PEER_HEADS, N_KEYS, PEER_DKEY // 2)
        ub = u.astype(BF16)
        vt = v.reshape(n_exp // PEER_EBLK, PEER_EBLK, d).transpose(0, 2, 1).astype(BF16)
        tpp = PEER_TM // TM

        nblk = n_exp // PEER_EBLK

        def cond_row_p(i):
            return cond_row(i * tpp)

        big = lambda dt: pltpu.VMEM((PEER_HEADS, N_KEYS, PEER_TM), dt)
        return pl.pallas_call(
            functools.partial(_peer_kernel, final_norm=final_norm),
            grid=(t_all // PEER_TM, nblk),
            in_specs=[pl.BlockSpec((PEER_TM, d), lambda i, k: (i, 0), pipeline_mode=pl.Buffered(1)),
                      pl.BlockSpec((1, 6, d), lambda i, k: (cond_row_p(i), 0, 0)),
                      pl.BlockSpec((1, d), lambda i, k: (0, 0)),
                      pl.BlockSpec(wqt.shape, lambda i, k: (0, 0)),
                      pl.BlockSpec(keys.shape, lambda i, k: (0, 0, 0)),
                      pl.BlockSpec((PEER_EBLK, d), lambda i, k: (k, 0)),
                      pl.BlockSpec((1, d, PEER_EBLK), lambda i, k: (k, 0, 0)),
                      pl.BlockSpec((1, d), lambda i, k: (0, 0))],
            out_specs=pl.BlockSpec((PEER_TM, d), lambda i, k: (i, 0), pipeline_mode=pl.Buffered(1)),
            out_shape=jax.ShapeDtypeStruct((t_all, d), F32),
            scratch_shapes=[pltpu.VMEM((d, PEER_TM), BF16),
                            pltpu.VMEM((N_KEYS // BF16_ROWS, PEER_TM // LANES, PEER_HEADS, 2, BF16_ROWS, LANES),
                                       BF16),
                            pltpu.VMEM((2 * PEER_HEADS * N_KEYS, PEER_TM), F32),
                            pltpu.VMEM((2, PEER_TOPK, PEER_HEADS, PEER_TM), F32),
                            pltpu.VMEM((PEER_TOPK + 1, PEER_HEADS, PEER_TM), F32),
                            pltpu.VMEM((2, N_KEYS, PEER_TM), F32),
                            big(F32), big(F32),
                            pltpu.VMEM((2, PEER_EBLK, PEER_TM), BF16)],
            compiler_params=_cparams(("arbitrary", "arbitrary")),
        )(x, mod, row(g_ffn), wqt, keys, ub, vt, row(g_final))

    x2 = peer_call(x1, mod0, g_ffn_l0, peer_wq_l0, peer_keys_l0, peer_u_l0, peer_v_l0, False)

    rh = TM // HALO
    lasth = t_all // HALO - 1
    x3 = pl.pallas_call(
        functools.partial(_pool_kernel, tiles_per_seq=tps, n_prompt_tiles=npt),
        grid=(ntile,),
        in_specs=[pl.BlockSpec((HALO, d), lambda i: (jnp.maximum(i * rh - 1, 0), 0)),
                  tok(d),
                  pl.BlockSpec((HALO, d), lambda i: (jnp.minimum((i + 1) * rh, lasth), 0)),
                  mod_spec, full(row(g_mix_l1)),
                  pl.BlockSpec(w_pool_l1.shape, lambda i: (0, 0, 0)), full(row(s_pool_l1))],
        out_specs=tok(d),
        out_shape=jax.ShapeDtypeStruct((t_all, d), F32),
        compiler_params=_cparams(("arbitrary",)),
    )(x2, x2, x2, mod1, row(g_mix_l1), w_pool_l1.astype(BF16), row(s_pool_l1))

    y = peer_call(x3, mod1, g_ffn_l1, peer_wq_l1, peer_keys_l1, peer_u_l1, peer_v_l1, True)

    y_prompt = y[:t_p].reshape(nb_p, s_p, d)
    y_sample = y[t_p:].reshape(nb_s, s_s, d)
    new_ckv = ckv[:t_p].reshape(nb_p, s_p, KV_LORA)
    new_krope = kr[:t_p].reshape(nb_p, s_p, QK_ROPE)
    return (y_prompt, y_sample, new_ckv, new_krope, new_lru)
```

```python
import functools

import numpy as np
import jax
import jax.numpy as jnp
from jax import lax
from jax.experimental import pallas as pl
from jax.experimental.pallas import tpu as pltpu

F32 = jnp.float32
BF16 = jnp.bfloat16

D_MODEL = 1024
EPS = 1e-6
GRID_W = 64
MLA_HEADS = 4
Q_LORA = 384
KV_LORA = 256
QK_NOPE = 128
QK_ROPE = 64
V_HEAD = 128
MLA_WIDTH = MLA_HEADS * V_HEAD
ROPE_BASE = 10000.0
LRU_WIDTH = 512
LRU_BLOCKS = 4
LRU_BLOCK = LRU_WIDTH // LRU_BLOCKS
CONV_W = 4
CONV_LEFT = 2
LRU_C = 8.0
POOL_WINDOWS = (2, 4, 8, 16)
POOL_GROUP = D_MODEL // len(POOL_WINDOWS)
PEER_HEADS = 8
N_KEYS = 128
PEER_DKEY = 256
PEER_TOPK = 16

SUBLANES = 8
LANES = 128
VMEM_BYTES = 64 * 1024 * 1024
VMEM_LIMIT = VMEM_BYTES - 4 * 1024 * 1024

TM = 256
QK_PAD = 256
HALO = 16
PEER_TM = 1024
PEER_EBLK = 512
PEER_SUB = 256
BF16_ROWS = 16
ROW_PAD = LANES
IN_EXT = 2048


def _rms(x, g):
    return x * lax.rsqrt(jnp.mean(x * x, axis=-1, keepdims=True) + EPS) * g


def _cparams(sem):
    return pltpu.CompilerParams(dimension_semantics=sem, vmem_limit_bytes=VMEM_LIMIT)


def _ada_kernel(c_ref, w_ref, b_ref, o_ref):
    c = c_ref[...]
    o_ref[...] = jnp.dot(c * jax.nn.sigmoid(c), w_ref[...], preferred_element_type=F32) + b_ref[...]


def _ada(cpad, w_mod, b_mod):
    n = w_mod.shape[1]
    bn = 768
    return pl.pallas_call(
        _ada_kernel,
        grid=(n // bn,),
        in_specs=[pl.BlockSpec(cpad.shape, lambda j: (0, 0)),
                  pl.BlockSpec((D_MODEL, bn), lambda j: (0, j)),
                  pl.BlockSpec((1, bn), lambda j: (0, j))],
        out_specs=pl.BlockSpec((cpad.shape[0], bn), lambda j: (0, j)),
        out_shape=jax.ShapeDtypeStruct((cpad.shape[0], n), F32),
        compiler_params=_cparams(("arbitrary",)),
    )(cpad, w_mod, b_mod.reshape(1, n))


def _inproj_kernel(x_ref, mod_ref, g_ref, win_ref, gq_ref, wuq_ref, gkv_ref, ck_ref, sk_ref,
                   q_ref, ckv_ref, kr_ref, krr_ref, ux_ref, ug_ref):
    mod = mod_ref[0]
    h = _rms(x_ref[...], g_ref[...]) * (1.0 + mod[1:2]) + mod[0:1]
    y = jnp.dot(h.astype(BF16), win_ref[...], preferred_element_type=F32)
    cq = y[:, 0:Q_LORA]
    ckv = y[:, Q_LORA:Q_LORA + KV_LORA]
    o = Q_LORA + KV_LORA
    kr = y[:, o:o + QK_ROPE]
    krp = y[:, o + 128:o + 128 + QK_ROPE]
    krs = y[:, o + 256:o + 256 + QK_ROPE]
    ux_ref[...] = y[:, o + 384:o + 384 + LRU_WIDTH]
    ug_ref[...] = y[:, o + 384 + LRU_WIDTH:o + 384 + 2 * LRU_WIDTH]
    ckv_ref[...] = _rms(ckv, gkv_ref[...])
    kr_ref[...] = kr
    ck = ck_ref[...]
    sk = sk_ref[...]
    krr_ref[...] = (krp * ck + krs * sk).astype(BF16)
    q = jnp.dot(_rms(cq, gq_ref[...]).astype(BF16), wuq_ref[...], preferred_element_type=F32)
    nw = MLA_HEADS * QK_NOPE
    rw = MLA_HEADS * QK_ROPE
    for hd in range(MLA_HEADS):
        qp = q[:, nw + hd * QK_ROPE:nw + (hd + 1) * QK_ROPE]
        qs = q[:, nw + rw + hd * QK_ROPE:nw + rw + (hd + 1) * QK_ROPE]
        q_ref[hd, :, 0:QK_NOPE] = q[:, hd * QK_NOPE:(hd + 1) * QK_NOPE].astype(BF16)
        q_ref[hd, :, QK_NOPE:QK_NOPE + QK_ROPE] = (qp * ck + qs * sk).astype(BF16)
        q_ref[hd, :, QK_NOPE + QK_ROPE:QK_PAD] = jnp.zeros((TM, QK_PAD - QK_NOPE - QK_ROPE), BF16)


def _attn_kernel(*refs, has_cache, s_new, n_cache):
    if has_cache:
        q_ref, ckv_ref, krr_ref, cckv_ref, ckr_ref, wukv_ref, o_ref, kcat_ref, vv_ref = refs
    else:
        q_ref, ckv_ref, krr_ref, wukv_ref, o_ref, kcat_ref, vv_ref = refs
    sk = n_cache + s_new
    kw = MLA_HEADS * QK_NOPE
    zpad = jnp.zeros((TM, QK_PAD - QK_NOPE - QK_ROPE), BF16)

    def put_keys(row0, ckv_rows, kr_rows):
        kv = jnp.dot(ckv_rows.astype(BF16), wukv_ref[...], preferred_element_type=F32)
        rows = pl.ds(row0, TM)
        for hd in range(MLA_HEADS):
            kcat_ref[hd, rows, 0:QK_NOPE] = kv[:, hd * QK_NOPE:(hd + 1) * QK_NOPE].astype(BF16)
            kcat_ref[hd, rows, QK_NOPE:QK_NOPE + QK_ROPE] = kr_rows.astype(BF16)
            kcat_ref[hd, rows, QK_NOPE + QK_ROPE:QK_PAD] = zpad
        vv_ref[rows, :] = kv[:, kw:].astype(BF16)

    @pl.when(pl.program_id(1) == 0)
    def _():
        if has_cache:
            for c in range(n_cache // TM):
                put_keys(c * TM, cckv_ref[0, c * TM:(c + 1) * TM, :], ckr_ref[0, c * TM:(c + 1) * TM, :])

        def body(c, carry):
            r0 = pl.multiple_of(c * TM, TM)
            put_keys(n_cache + r0, ckv_ref[pl.ds(r0, TM), :], krr_ref[pl.ds(r0, TM), :])
            return carry
        lax.fori_loop(0, s_new // TM, body, 0)

    scale = (QK_NOPE + QK_ROPE) ** -0.5
    for hd in range(MLA_HEADS):
        s = lax.dot_general(q_ref[hd], kcat_ref[hd], (((1,), (1,)), ((), ())),
                            preferred_element_type=F32) * scale
        m = jnp.max(s, axis=-1, keepdims=True)
        e = jnp.exp(s - m)
        l = jnp.sum(e, axis=-1, keepdims=True)
        o = jnp.dot(e.astype(BF16), vv_ref[:, hd * V_HEAD:(hd + 1) * V_HEAD], preferred_element_type=F32)
        o_ref[:, hd * V_HEAD:(hd + 1) * V_HEAD] = (o / l).astype(BF16)
    del sk


def _lru_dir(xp_ref, xc_ref, xn_ref, valid_prev, valid_next, d, reverse, cw_ref, cb_ref,
             wr_ref, br_ref, wi_ref, bi_ref, lam_ref, carry):
    xp = jnp.where(valid_prev, xp_ref[...], 0.0)
    xn = jnp.where(valid_next, xn_ref[...], 0.0)
    xx = jnp.concatenate([xp, xc_ref[...], xn], axis=0)
    n = TM + 2 * SUBLANES
    xc = cb_ref[...]
    for k in range(CONV_W):
        sh = (CONV_LEFT - k) % n
        xs = xx if sh == 0 else pltpu.roll(xx, sh, 0)
        xc = xc + xs[SUBLANES:SUBLANES + TM] * cw_ref[k:k + 1, :]
    rs, is_ = [], []
    for b in range(LRU_BLOCKS):
        xb = xc[:, b * LRU_BLOCK:(b + 1) * LRU_BLOCK]
        rs.append(jnp.dot(xb, wr_ref[d, b], preferred_element_type=F32))
        is_.append(jnp.dot(xb, wi_ref[d, b], preferred_element_type=F32))
    r = jax.nn.sigmoid(jnp.concatenate(rs, axis=-1) + br_ref[d:d + 1, :])
    i = jax.nn.sigmoid(jnp.concatenate(is_, axis=-1) + bi_ref[d:d + 1, :])
    nl = -lam_ref[d:d + 1, :]
    softplus = jnp.maximum(nl, 0.0) + jnp.log1p(jnp.exp(-jnp.abs(nl)))
    log_a = -LRU_C * r * softplus
    a = jnp.exp(log_a)
    bx = jnp.sqrt(jnp.tanh(-log_a) * (a * a + 1.0)) * (i * xc)
    t = lax.broadcasted_iota(jnp.int32, (TM, 1), 0)
    step = 1
    while step < TM:
        if reverse:
            keep = t < TM - step
            sh = TM - step
        else:
            keep = t >= step
            sh = step
        a_s = jnp.where(keep, pltpu.roll(a, sh, 0), 1.0)
        b_s = jnp.where(keep, pltpu.roll(bx, sh, 0), 0.0)
        bx = a * b_s + bx
        a = a * a_s
        step *= 2
    return a * carry + bx


def _lru_kernel(fxp, fxc, fxn, bxp, bxc, bxn, cw_ref, cb_ref, wr_ref, br_ref, wi_ref, bi_ref, lam_ref,
                h0_ref, hf_ref, hb_ref, st_ref, cf_ref, cbk_ref, *, nc):
    c = pl.program_id(1)

    @pl.when(c == 0)
    def _():
        cf_ref[...] = h0_ref[0, 0:1, :]
        cbk_ref[...] = h0_ref[0, 1:2, :]

    params = (cw_ref, cb_ref, wr_ref, br_ref, wi_ref, bi_ref, lam_ref)
    hf = _lru_dir(fxp, fxc, fxn, c > 0, c < nc - 1, 0, False, *params, cf_ref[...])
    hf_ref[...] = hf
    cf_ref[...] = hf[TM - 1:TM, :]
    hb = _lru_dir(bxp, bxc, bxn, c < nc - 1, c > 0, 1, True, *params, cbk_ref[...])
    hb_ref[...] = hb
    cbk_ref[...] = hb[0:1, :]
    st_ref[0, 0:1, :] = hf[TM - 1:TM, :]
    st_ref[0, 1:2, :] = hb[0:1, :]


def _oproj_kernel(x_ref, mod_ref, at_ref, hf_ref, hb_ref, ug_ref, woa_ref, wor_ref, o_ref):
    mod = mod_ref[0]
    rec = ((hf_ref[...] + hb_ref[...]) * jax.nn.gelu(ug_ref[...])).astype(BF16)
    out = (jnp.dot(at_ref[...], woa_ref[...], preferred_element_type=F32)
           + jnp.dot(rec, wor_ref[...], preferred_element_type=F32))
    o_ref[...] = x_ref[...] + mod[2:3] * out


def _pool_kernel(xp_ref, xc_ref, xn_ref, mod_ref, g_ref, wp_ref, sp_ref, o_ref, *, tiles_per_seq, n_prompt_tiles):
    i = pl.program_id(0)
    j = jnp.where(i < n_prompt_tiles, 0, (i - n_prompt_tiles) % tiles_per_seq)
    ntile = jnp.where(i < n_prompt_tiles, 1, tiles_per_seq)
    mod = mod_ref[0]
    g = g_ref[...]

    def hmod(x):
        return _rms(x, g) * (1.0 + mod[1:2]) + mod[0:1]

    x = xc_ref[...]
    hc = hmod(x)
    hp = jnp.where(j > 0, hmod(xp_ref[...]), 0.0)
    hn = jnp.where(j < ntile - 1, hmod(xn_ref[...]), 0.0)
    hh = jnp.concatenate([hp, hc, hn], axis=0)
    n = TM + 2 * HALO
    seq_len = ntile * TM
    t = j * TM + lax.broadcasted_iota(jnp.int32, (TM, 1), 0)
    ys = []
    for gi, w in enumerate(POOL_WINDOWS):
        cols = slice(gi * POOL_GROUP, (gi + 1) * POOL_GROUP)
        p = hh[:, cols]
        p = p + pltpu.roll(p, 1, 0)
        half = 1
        while 2 * half < w:
            p = pltpu.roll(p, half, 0) + pltpu.roll(p, n - half, 0)
            half *= 2
        lo = jnp.maximum(t - w // 2, 0)
        hi = jnp.minimum(t + (w - w // 2), seq_len)
        mean = p[HALO:HALO + TM] / (hi - lo).astype(F32)
        dg = (mean - hc[:, cols]).astype(BF16)
        ys.append(jnp.dot(dg, wp_ref[gi], preferred_element_type=F32))
    y = jnp.concatenate(ys, axis=-1) * sp_ref[...]
    o_ref[...] = x + mod[2:3] * y


def _sort_pairs(n):
    pairs = []

    def merge(lo, cnt, r):
        step = r * 2
        if step < cnt:
            merge(lo, cnt, step)
            merge(lo + r, cnt, step)
            for i in range(lo + r, lo + cnt - r, step):
                pairs.append((i, i + r))
        else:
            pairs.append((lo, lo + r))

    def sort(lo, cnt):
        if cnt > 1:
            m = cnt // 2
            sort(lo, m)
            sort(lo + m, m)
            merge(lo, cnt, 1)

    sort(0, n)
    return pairs


_SORT16 = _sort_pairs(PEER_TOPK)
_HYPER = [(a, b) for a in range(PEER_TOPK) for b in range(PEER_TOPK) if (a + 1) * (b + 1) <= PEER_TOPK]


def _top16_sorted(s):
    k = PEER_TOPK
    x = [s[SUBLANES * r:SUBLANES * (r + 1), :] for r in range(N_KEYS // SUBLANES)]
    for (i, j) in _SORT16:
        hi = jnp.maximum(x[i], x[j])
        lo = jnp.minimum(x[i], x[j])
        x[i], x[j] = hi, lo
    for shift in (4, 2, 1):
        y = [jnp.maximum(x[r], pltpu.roll(x[k - 1 - r], shift, 0)) for r in range(k)]
        stride = k // 2
        while stride >= 1:
            for i in range(k):
                if i & stride == 0:
                    hi = jnp.maximum(y[i], y[i + stride])
                    lo = jnp.minimum(y[i], y[i + stride])
                    y[i], y[i + stride] = hi, lo
            stride //= 2
        x = y
    return x


def _pair_counts(sv1, sv2):
    one = jnp.ones_like(sv1[0])
    zero = jnp.zeros_like(sv1[0])
    cand = [sv1[a] + sv2[b] for (a, b) in _HYPER]

    def ordered(i, j):
        (ai, bi), (aj, bj) = _HYPER[i], _HYPER[j]
        return (aj <= ai and bj <= bi) or (ai <= aj and bi <= bj)

    nh = len(_HYPER)
    cnt = [float((a + 1) * (b + 1) - 1 + sum(1 for j in range(i + 1, nh) if not ordered(i, j))) * one
           for i, (a, b) in enumerate(_HYPER)]
    for i in range(nh):
        for jx in range(i):
            if ordered(i, jx):
                continue
            ge = jnp.where(cand[jx] >= cand[i], one, zero)
            cnt[i] = cnt[i] + ge
            cnt[jx] = cnt[jx] - ge
    e1 = [jnp.exp(sv1[a] - sv1[0]) for a in range(PEER_TOPK)]
    e2 = [jnp.exp(sv2[b] - sv2[0]) for b in range(PEER_TOPK)]
    n = [zero for _ in range(PEER_TOPK)]
    z = zero
    for i, (a, b) in enumerate(_HYPER):
        sel = jnp.where(cnt[i] < float(PEER_TOPK), one, zero)
        n[a] = n[a] + sel
        z = z + sel * (e1[a] * e2[b])
    return n, 1.0 / z


def _head_tables(hd, g, lanes, sc_ref, sv_ref, nz_ref, ex_ref, cnt1_ref, e1n_ref, tab_ref, exact):
    k = PEER_TOPK

    def bc(ref, *idx):
        return jnp.broadcast_to(ref[idx + (slice(hd, hd + 1), lanes)], (SUBLANES, LANES))

    sv1b = [bc(sv_ref, 0, a) for a in range(k)]
    nb = [bc(nz_ref, a) for a in range(k)]
    izb = bc(nz_ref, k)
    c1 = jnp.zeros((SUBLANES, LANES), F32)
    for r in range(N_KEYS // SUBLANES):
        rows = slice(r * SUBLANES, (r + 1) * SUBLANES)
        s1 = sc_ref[pl.ds(2 * hd * N_KEYS + r * SUBLANES, SUBLANES), lanes]
        cnt1 = jnp.zeros_like(s1)
        if exact:
            rank1 = jnp.zeros_like(s1)
            for a in range(k):
                rank1 = jnp.where(sv1b[a] > s1, float(a + 1), rank1)
            rank1 = rank1 + ex_ref[0, rows, lanes]
            for a in range(k):
                cnt1 = jnp.where(rank1 == float(a), nb[a], cnt1)
        else:
            for a in range(k):
                cnt1 = jnp.where(s1 == sv1b[a], nb[a], cnt1)
            c1 = c1 + jnp.where(s1 >= sv1b[k - 1], 1.0, 0.0)
        cnt1_ref[hd, rows, lanes] = cnt1
        e1n_ref[hd, rows, lanes] = jnp.exp(s1 - sv1b[0]) * izb
    sv2b = [bc(sv_ref, 1, a) for a in range(k)]
    c2 = jnp.zeros((SUBLANES, LANES), F32)
    for r in range(N_KEYS // BF16_ROWS):
        rk, e2 = [], []
        for q in range(BF16_ROWS // SUBLANES):
            r0 = r * BF16_ROWS + q * SUBLANES
            rows = slice(r0, r0 + SUBLANES)
            s2 = sc_ref[pl.ds((2 * hd + 1) * N_KEYS + r0, SUBLANES), lanes]
            rank2 = jnp.zeros_like(s2)
            for a in range(k):
                rank2 = jnp.where(sv2b[a] > s2, float(a + 1), rank2)
            if exact:
                rank2 = jnp.minimum(rank2 + ex_ref[1, rows, lanes], float(k))
            else:
                c2 = c2 + jnp.where(rank2 < float(k), 1.0, 0.0)
            rk.append(rank2)
            e2.append(jnp.exp(s2 - sv2b[0]))
        tab_ref[r, g, hd, 0] = jnp.concatenate(rk, axis=0).astype(BF16)
        tab_ref[r, g, hd, 1] = jnp.concatenate(e2, axis=0).astype(BF16)
    if exact:
        return None
    ex_ref[0, 0:SUBLANES, lanes] = c1
    ex_ref[1, 0:SUBLANES, lanes] = c2
    b = jnp.zeros((SUBLANES, LANES), F32)
    for svb in (sv1b, sv2b):
        for a in range(k - 1):
            b = b + jnp.where(svb[a] == svb[a + 1], 1.0, 0.0)
    return b


def _tie_offsets(hd, sc_ref, ex_ref):
    tm = PEER_TM
    nidx = lax.broadcasted_iota(jnp.int32, (N_KEYS, tm), 0)
    for p in range(2):
        s = sc_ref[pl.ds((2 * hd + p) * N_KEYS, N_KEYS), 0:tm]

        def body(m, e, s=s, p=p):
            row = sc_ref[pl.ds((2 * hd + p) * N_KEYS + m, 1), 0:tm]
            return e + jnp.where((s == row) & (nidx > m), 1.0, 0.0)

        ex_ref[p] = lax.fori_loop(0, N_KEYS, body, jnp.zeros((N_KEYS, tm), F32))


def _peer_prologue(x_ref, mod_ref, g_ref, wqt_ref, keys_ref, h2t_ref, tab_ref, sc_ref, sv_ref, nz_ref,
                   ex_ref, cnt1_ref, e1n_ref, y_ref):
    tm = PEER_TM
    half = PEER_DKEY // 2
    ngroup = tm // LANES
    mod = mod_ref[0]
    h2 = _rms(x_ref[...], g_ref[...]) * (1.0 + mod[4:5]) + mod[3:4]
    h2t_ref[:, 0:tm] = h2.T.astype(BF16)

    def lane_group(g):
        return pl.ds(pl.multiple_of(g * LANES, LANES), LANES)

    nslot, qrows = y_ref.shape[0], y_ref.shape[1]
    per_pass = nslot * qrows // half
    for i in range(2 * PEER_HEADS):
        if i % per_pass == 0:
            qt = jnp.dot(wqt_ref[i * half:(i + per_pass) * half, :], h2t_ref[:, 0:tm],
                         preferred_element_type=F32).astype(BF16)
            for s in range(nslot):
                y_ref[s, :, 0:tm] = qt[s * qrows:(s + 1) * qrows, :]
        j = (i % per_pass) * half
        sc_ref[i * N_KEYS:(i + 1) * N_KEYS, 0:tm] = jnp.dot(
            keys_ref[i], y_ref[j // qrows, j % qrows:j % qrows + half, 0:tm],
            preferred_element_type=F32)

        def group(g, c, i=i):
            lanes = lane_group(g)
            top = _top16_sorted(sc_ref[i * N_KEYS:(i + 1) * N_KEYS, lanes])
            for a in range(PEER_TOPK):
                sv_ref[i % 2, a, i // 2:i // 2 + 1, lanes] = top[a][0:1, :]
            return c
        lax.fori_loop(0, ngroup, group, 0)

    def counts(g, carry):
        lanes = lane_group(g)
        sv1 = [sv_ref[0, a, :, lanes] for a in range(PEER_TOPK)]
        sv2 = [sv_ref[1, a, :, lanes] for a in range(PEER_TOPK)]
        n, inv_z = _pair_counts(sv1, sv2)
        for a in range(PEER_TOPK):
            nz_ref[a, :, lanes] = n[a]
        nz_ref[PEER_TOPK, :, lanes] = inv_z
        return carry
    lax.fori_loop(0, ngroup, counts, 0)

    tabs = (sc_ref, sv_ref, nz_ref, ex_ref, cnt1_ref, e1n_ref, tab_ref)

    for hd in range(PEER_HEADS):
        def fast(g, bad, hd=hd):
            return bad + _head_tables(hd, g, lane_group(g), *tabs, exact=False)
        bad = lax.fori_loop(0, ngroup, fast, jnp.zeros((SUBLANES, LANES), F32))
        totals = jnp.sum(ex_ref[:, 0:SUBLANES, :], axis=1)
        ties = jnp.max(bad) + jnp.max(jnp.abs(totals - float(PEER_TOPK)))

        @pl.when(ties > 0.0)
        def _(hd=hd):
            _tie_offsets(hd, sc_ref, ex_ref)

            def slow(g, c):
                _head_tables(hd, g, lane_group(g), *tabs, exact=True)
                return c
            lax.fori_loop(0, ngroup, slow, 0)
    sc_ref[0:D_MODEL, 0:tm] = jnp.zeros((D_MODEL, tm), F32)


def _peer_kernel(x_ref, mod_ref, g_ref, wqt_ref, keys_ref, u0_ref, u_ref, vt_ref, gfin_ref, o_ref,
                 h2t_ref, tab_ref, sc_ref, sv_ref, nz_ref, ex_ref, cnt1_ref, e1n_ref, raw_ref, y_ref,
                 *, final_norm):
    k = pl.program_id(1)
    tm = PEER_TM
    d = D_MODEL
    acc_ref = sc_ref.at[0:d]

    nstep = pl.num_programs(1) - 1
    zero = jnp.zeros((BF16_ROWS, LANES), BF16)
    npair = PEER_EBLK // (2 * N_KEYS)
    cur = k % 2
    oth = (k + 1) % 2

    def up_next():
        raw_ref[oth, :, 0:tm] = jnp.dot(u_ref[:, 0:d], h2t_ref[:, 0:tm], preferred_element_type=F32)

    def down_prev():
        acc_ref[:, 0:tm] += jnp.dot(vt_ref[0, :, 0:PEER_EBLK], y_ref[oth, :, 0:tm],
                                    preferred_element_type=F32)

    def gate_block():
        for q in range(npair):
            i1 = (k * npair + q) * 2
            crow = [[cnt1_ref[hd, pl.ds(i1 + j, 1), :] for hd in range(PEER_HEADS)] for j in range(2)]
            erow = [[e1n_ref[hd, pl.ds(i1 + j, 1), :] for hd in range(PEER_HEADS)] for j in range(2)]
            for g in range(tm // LANES):
                lanes = slice(g * LANES, (g + 1) * LANES)
                cbs = [[jnp.broadcast_to(crow[j][hd][:, lanes], (BF16_ROWS, LANES)).astype(BF16)
                        for hd in range(PEER_HEADS)] for j in range(2)]
                ebs = [[jnp.broadcast_to(erow[j][hd][:, lanes], (BF16_ROWS, LANES)).astype(BF16)
                        for hd in range(PEER_HEADS)] for j in range(2)]
                for r in range(N_KEYS // BF16_ROWS):
                    w = [None, None]
                    for hd in range(PEER_HEADS):
                        rk = tab_ref[r, g, hd, 0]
                        ev = tab_ref[r, g, hd, 1]
                        for j in range(2):
                            t = jnp.where(rk < cbs[j][hd], ev, zero) * ebs[j][hd]
                            w[j] = t if w[j] is None else w[j] + t
                    for j in range(2):
                        rows = pl.ds((2 * q + j) * N_KEYS + r * BF16_ROWS, BF16_ROWS)
                        act = jax.nn.gelu(raw_ref[cur, rows, lanes].astype(BF16))
                        y_ref[cur, rows, lanes] = act * w[j]

    @pl.when(k == 0)
    def _():
        _peer_prologue(x_ref, mod_ref, g_ref, wqt_ref, keys_ref, h2t_ref, tab_ref, sc_ref, sv_ref, nz_ref,
                       ex_ref, cnt1_ref, e1n_ref, y_ref)
        raw_ref[0, :, 0:tm] = jnp.dot(u0_ref[:, 0:d], h2t_ref[:, 0:tm], preferred_element_type=F32)
        up_next()
        gate_block()

    @pl.when((k > 0) & (k < nstep))
    def _():
        up_next()
        down_prev()
        gate_block()

    @pl.when(k == nstep)
    def _():
        down_prev()
        out = x_ref[...] + mod_ref[0][5:6] * acc_ref[:, 0:tm].T
        if final_norm:
            out = _rms(out, gfin_ref[...])
        o_ref[...] = out


def _rope_tables(s_prompt_tile, s_sample):
    n_rows = s_sample // GRID_W
    rows = jnp.repeat(jnp.arange(n_rows, dtype=F32), GRID_W)
    cols = jnp.tile(jnp.arange(GRID_W, dtype=F32), n_rows)
    axis_dim = QK_ROPE // 2
    inv_freq = ROPE_BASE ** (-jnp.arange(0, axis_dim, 2, dtype=F32) / axis_dim)
    ang = jnp.concatenate([rows[:, None] * inv_freq, cols[:, None] * inv_freq], axis=-1)
    cos, sin = jnp.cos(ang), jnp.sin(ang)
    ck = jnp.concatenate([cos, cos], axis=-1)
    sk = jnp.concatenate([-sin, sin], axis=-1)
    ident_c = jnp.ones((s_prompt_tile, QK_ROPE), F32)
    ident_s = jnp.zeros((s_prompt_tile, QK_ROPE), F32)
    return jnp.concatenate([ident_c, ck], axis=0), jnp.concatenate([ident_s, sk], axis=0)


def kernel(x_prompt, x_sample, cache_ckv_l0, cache_krope_l0, state_lru_l0, c, c_ctx, w_mod_l0, b_mod_l0, w_mod_l1, b_mod_l1, g_mix_l0, g_ffn_l0, g_mix_l1, g_ffn_l1, w_in_l0, g_q_l0, w_uq_l0, g_kv_l0, w_ukv_l0, conv_w_l0, conv_b_l0, w_rg_l0, b_rg_l0, w_ig_l0, b_ig_l0, lam_l0, w_o_l0, w_pool_l1, s_pool_l1, peer_wq_l0, peer_keys_l0, peer_u_l0, peer_v_l0, peer_wq_l1, peer_keys_l1, peer_u_l1, peer_v_l1, g_final):
    nb_p, s_p, d = x_prompt.shape
    nb_s, s_s, _ = x_sample.shape
    n_cache = cache_ckv_l0.shape[1]
    assert d == D_MODEL and s_p == TM and s_s % TM == 0 and n_cache % TM == 0
    t_p = nb_p * s_p
    t_s = nb_s * s_s
    t_all = t_p + t_s
    npt = t_p // TM
    tps = s_s // TM
    ntile = t_all // TM
    assert t_all % PEER_TM == 0 and t_p % PEER_TM == 0 and s_s % PEER_TM == 0

    x0 = jnp.concatenate([x_prompt.reshape(t_p, d), x_sample.reshape(t_s, d)], axis=0)

    ncond = 1 + nb_s
    cpad = jnp.zeros((2 * SUBLANES, d), F32).at[0].set(c_ctx).at[1:ncond].set(c)
    mod0 = _ada(cpad, w_mod_l0, b_mod_l0).reshape(2 * SUBLANES, 6, d)
    mod1 = _ada(cpad, w_mod_l1, b_mod_l1).reshape(2 * SUBLANES, 6, d)

    def cond_row(i):
        return jnp.where(i < npt, 0, 1 + (i - npt) // tps)

    mod_spec = pl.BlockSpec((1, 6, d), lambda i: (cond_row(i), 0, 0))
    row = lambda a: a.reshape(1, -1)

    perm = np.concatenate([np.arange(0, QK_ROPE, 2), np.arange(1, QK_ROPE, 2)])
    perm_sw = np.concatenate([np.arange(1, QK_ROPE, 2), np.arange(0, QK_ROPE, 2)])
    o1 = Q_LORA + KV_LORA
    w_kr = w_in_l0[:, o1:o1 + QK_ROPE]
    z64 = jnp.zeros((d, 128 - QK_ROPE), F32)
    w_in_ext = jnp.concatenate(
        [w_in_l0[:, :o1], w_kr, z64, w_kr[:, perm], z64, w_kr[:, perm_sw], z64, w_in_l0[:, o1 + QK_ROPE:]],
        axis=1).astype(BF16)
    assert w_in_ext.shape[1] == IN_EXT
    wq3 = w_uq_l0.reshape(Q_LORA, MLA_HEADS, QK_NOPE + QK_ROPE)
    w_uq_ext = jnp.concatenate(
        [wq3[:, :, :QK_NOPE].reshape(Q_LORA, -1),
         wq3[:, :, QK_NOPE:][:, :, perm].reshape(Q_LORA, -1),
         wq3[:, :, QK_NOPE:][:, :, perm_sw].reshape(Q_LORA, -1)], axis=1).astype(BF16)
    ck_tab, sk_tab = _rope_tables(TM, s_s)

    def rope_blk(i):
        return jnp.where(i < npt, 0, 1 + (i - npt) % tps)

    tok = lambda w: pl.BlockSpec((TM, w), lambda i: (i, 0))
    full = lambda a: pl.BlockSpec(a.shape, lambda *_: (0,) * a.ndim)
    q, ckv, kr, krr, ux, ug = pl.pallas_call(
        _inproj_kernel,
        grid=(ntile,),
        in_specs=[tok(d), mod_spec, full(row(g_mix_l0)), full(w_in_ext), full(row(g_q_l0)), full(w_uq_ext),
                  full(row(g_kv_l0)),
                  pl.BlockSpec((TM, QK_ROPE), lambda i: (rope_blk(i), 0)),
                  pl.BlockSpec((TM, QK_ROPE), lambda i: (rope_blk(i), 0))],
        out_specs=[pl.BlockSpec((MLA_HEADS, TM, QK_PAD), lambda i: (0, i, 0)),
                   tok(KV_LORA), tok(QK_ROPE), tok(QK_ROPE), tok(LRU_WIDTH), tok(LRU_WIDTH)],
        out_shape=[jax.ShapeDtypeStruct((MLA_HEADS, t_all, QK_PAD), BF16),
                   jax.ShapeDtypeStruct((t_all, KV_LORA), F32),
                   jax.ShapeDtypeStruct((t_all, QK_ROPE), F32),
                   jax.ShapeDtypeStruct((t_all, QK_ROPE), BF16),
                   jax.ShapeDtypeStruct((t_all, LRU_WIDTH), F32),
                   jax.ShapeDtypeStruct((t_all, LRU_WIDTH), F32)],
        compiler_params=_cparams(("arbitrary",)),
    )(x0, mod0, row(g_mix_l0), w_in_ext, row(g_q_l0), w_uq_ext, row(g_kv_l0), ck_tab, sk_tab)

    wkv3 = w_ukv_l0.reshape(KV_LORA, MLA_HEADS, QK_NOPE + V_HEAD)
    w_ukv_ext = jnp.concatenate([wkv3[:, :, :QK_NOPE].reshape(KV_LORA, -1),
                                 wkv3[:, :, QK_NOPE:].reshape(KV_LORA, -1)], axis=1).astype(BF16)

    def attn_call(nb, s_new, tile0, has_cache):
        nq = s_new // TM
        blk0 = tile0 * TM // s_new
        n_c = n_cache if has_cache else 0
        in_specs = [pl.BlockSpec((MLA_HEADS, TM, QK_PAD), lambda b, qi: (0, tile0 + b * nq + qi, 0)),
                    pl.BlockSpec((s_new, KV_LORA), lambda b, qi: (blk0 + b, 0)),
                    pl.BlockSpec((s_new, QK_ROPE), lambda b, qi: (blk0 + b, 0))]
        args = [q, ckv, krr]
        if has_cache:
            in_specs += [pl.BlockSpec((1, n_cache, KV_LORA), lambda b, qi: (b, 0, 0)),
                         pl.BlockSpec((1, n_cache, QK_ROPE), lambda b, qi: (b, 0, 0))]
            args += [cache_ckv_l0, cache_krope_l0[:, :, perm]]
        in_specs.append(pl.BlockSpec(w_ukv_ext.shape, lambda b, qi: (0, 0)))
        args.append(w_ukv_ext)
        return pl.pallas_call(
            functools.partial(_attn_kernel, has_cache=has_cache, s_new=s_new, n_cache=n_c),
            grid=(nb, nq),
            in_specs=in_specs,
            out_specs=pl.BlockSpec((TM, MLA_WIDTH), lambda b, qi: (b * nq + qi, 0)),
            out_shape=jax.ShapeDtypeStruct((nb * s_new, MLA_WIDTH), BF16),
            scratch_shapes=[pltpu.VMEM((MLA_HEADS, n_c + s_new, QK_PAD), BF16),
                            pltpu.VMEM((n_c + s_new, MLA_WIDTH), BF16)],
            compiler_params=_cparams(("arbitrary", "arbitrary")),
        )(*args)

    attn = jnp.concatenate([attn_call(nb_p, s_p, 0, False), attn_call(nb_s, s_s, npt, True)], axis=0)

    def lru_call(nb, s_new, tile0, h0):
        nc = s_new // TM
        r8 = TM // SUBLANES
        last8 = t_all // SUBLANES - 1

        def cur(rev):
            return pl.BlockSpec((TM, LRU_WIDTH),
                                lambda b, cc: (tile0 + b * nc + (nc - 1 - cc if rev else cc), 0))

        def prev(rev):
            return pl.BlockSpec((SUBLANES, LRU_WIDTH), lambda b, cc: (
                jnp.maximum((tile0 + b * nc + (nc - 1 - cc if rev else cc)) * r8 - 1, 0), 0))

        def nxt(rev):
            return pl.BlockSpec((SUBLANES, LRU_WIDTH), lambda b, cc: (
                jnp.minimum((tile0 + b * nc + (nc - 1 - cc if rev else cc) + 1) * r8, last8), 0))

        small = [conv_w_l0, row(conv_b_l0), w_rg_l0, b_rg_l0, w_ig_l0, b_ig_l0, lam_l0]
        return pl.pallas_call(
            functools.partial(_lru_kernel, nc=nc),
            grid=(nb, nc),
            in_specs=[prev(False), cur(False), nxt(False), prev(True), cur(True), nxt(True)]
                     + [full(a) for a in small]
                     + [pl.BlockSpec((1, 2, LRU_WIDTH), lambda b, cc: (b, 0, 0))],
            out_specs=[pl.BlockSpec((TM, LRU_WIDTH), lambda b, cc: (b * nc + cc, 0)),
                       pl.BlockSpec((TM, LRU_WIDTH), lambda b, cc: (b * nc + nc - 1 - cc, 0)),
                       pl.BlockSpec((1, 2, LRU_WIDTH), lambda b, cc: (b, 0, 0))],
            out_shape=[jax.ShapeDtypeStruct((nb * s_new, LRU_WIDTH), F32),
                       jax.ShapeDtypeStruct((nb * s_new, LRU_WIDTH), F32),
                       jax.ShapeDtypeStruct((nb, 2, LRU_WIDTH), F32)],
            scratch_shapes=[pltpu.VMEM((1, LRU_WIDTH), F32), pltpu.VMEM((1, LRU_WIDTH), F32)],
            compiler_params=_cparams(("arbitrary", "arbitrary")),
        )(ux, ux, ux, ux, ux, ux, *small, h0)

    hf_p, hb_p, new_lru = lru_call(nb_p, s_p, 0, jnp.zeros((nb_p, 2, LRU_WIDTH), F32))
    hf_s, hb_s, _ = lru_call(nb_s, s_s, npt, state_lru_l0.astype(F32))
    hf = jnp.concatenate([hf_p, hf_s], axis=0)
    hb = jnp.concatenate([hb_p, hb_s], axis=0)

    w_o = w_o_l0.astype(BF16)
    x1 = pl.pallas_call(
        _oproj_kernel,
        grid=(ntile,),
        in_specs=[tok(d), mod_spec, tok(MLA_WIDTH), tok(LRU_WIDTH), tok(LRU_WIDTH), tok(LRU_WIDTH),
                  pl.BlockSpec((MLA_WIDTH, d), lambda i: (0, 0)), pl.BlockSpec((LRU_WIDTH, d), lambda i: (1, 0))],
        out_specs=tok(d),
        out_shape=jax.ShapeDtypeStruct((t_all, d), F32),
        compiler_params=_cparams(("arbitrary",)),
    )(x0, mod0, attn, hf, hb, ug, w_o, w_o)

    def peer_call(x, mod, g_ffn, w_q, sub_keys, u, v, final_norm):
        n_exp = u.shape[0]
        assert n_exp == N_KEYS * N_KEYS and n_exp % PEER_EBLK == 0
        wqt = w_q.T.astype(BF16)
        keys = sub_keys.astype(BF16).reshape(2 * PEER_HEADS, N_KEYS, PEER_DKEY // 2)
        ub = jnp.pad(u.astype(BF16), ((0, 0), (0, ROW_PAD)))
        vt = jnp.pad(v.reshape(n_exp // PEER_EBLK, PEER_EBLK, d).transpose(0, 2, 1).astype(BF16),
                     ((0, 0), (0, 0), (0, ROW_PAD)))
        tpp = PEER_TM // TM

        nblk = n_exp // PEER_EBLK

        def cond_row_p(i):
            return cond_row(i * tpp)

        big = lambda dt: pltpu.VMEM((PEER_HEADS, N_KEYS, PEER_TM), dt)
        return pl.pallas_call(
            functools.partial(_peer_kernel, final_norm=final_norm),
            grid=(t_all // PEER_TM, nblk + 1),
            in_specs=[pl.BlockSpec((PEER_TM, d), lambda i, k: (i, 0), pipeline_mode=pl.Buffered(1)),
                      pl.BlockSpec((1, 6, d), lambda i, k: (cond_row_p(i), 0, 0)),
                      pl.BlockSpec((1, d), lambda i, k: (0, 0)),
                      pl.BlockSpec(wqt.shape, lambda i, k: (0, 0)),
                      pl.BlockSpec(keys.shape, lambda i, k: (0, 0, 0)),
                      pl.BlockSpec((PEER_EBLK, d + ROW_PAD), lambda i, k: (0, 0), pipeline_mode=pl.Buffered(1)),
                      pl.BlockSpec((PEER_EBLK, d + ROW_PAD), lambda i, k: (jnp.minimum(k + 1, nblk - 1), 0)),
                      pl.BlockSpec((1, d, PEER_EBLK + ROW_PAD), lambda i, k: (jnp.maximum(k - 1, 0), 0, 0)),
                      pl.BlockSpec((1, d), lambda i, k: (0, 0))],
            out_specs=pl.BlockSpec((PEER_TM, d), lambda i, k: (i, 0), pipeline_mode=pl.Buffered(1)),
            out_shape=jax.ShapeDtypeStruct((t_all, d), F32),
            scratch_shapes=[pltpu.VMEM((d, PEER_TM + ROW_PAD), BF16),
                            pltpu.VMEM((N_KEYS // BF16_ROWS, PEER_TM // LANES, PEER_HEADS, 2, BF16_ROWS, LANES),
                                       BF16),
                            pltpu.VMEM((2 * PEER_HEADS * N_KEYS, PEER_TM + ROW_PAD), F32),
                            pltpu.VMEM((2, PEER_TOPK, PEER_HEADS, PEER_TM), F32),
                            pltpu.VMEM((PEER_TOPK + 1, PEER_HEADS, PEER_TM), F32),
                            pltpu.VMEM((2, N_KEYS, PEER_TM), F32),
                            big(F32), big(F32),
                            pltpu.VMEM((2, PEER_EBLK, PEER_TM + ROW_PAD), F32),
                            pltpu.VMEM((2, PEER_EBLK, PEER_TM + ROW_PAD), BF16)],
            compiler_params=_cparams(("arbitrary", "arbitrary")),
        )(x, mod, row(g_ffn), wqt, keys, ub, ub, vt, row(g_final))

    x2 = peer_call(x1, mod0, g_ffn_l0, peer_wq_l0, peer_keys_l0, peer_u_l0, peer_v_l0, False)

    rh = TM // HALO
    lasth = t_all // HALO - 1
    x3 = pl.pallas_call(
        functools.partial(_pool_kernel, tiles_per_seq=tps, n_prompt_tiles=npt),
        grid=(ntile,),
        in_specs=[pl.BlockSpec((HALO, d), lambda i: (jnp.maximum(i * rh - 1, 0), 0)),
                  tok(d),
                  pl.BlockSpec((HALO, d), lambda i: (jnp.minimum((i + 1) * rh, lasth), 0)),
                  mod_spec, full(row(g_mix_l1)),
                  pl.BlockSpec(w_pool_l1.shape, lambda i: (0, 0, 0)), full(row(s_pool_l1))],
        out_specs=tok(d),
        out_shape=jax.ShapeDtypeStruct((t_all, d), F32),
        compiler_params=_cparams(("arbitrary",)),
    )(x2, x2, x2, mod1, row(g_mix_l1), w_pool_l1.astype(BF16), row(s_pool_l1))

    y = peer_call(x3, mod1, g_ffn_l1, peer_wq_l1, peer_keys_l1, peer_u_l1, peer_v_l1, True)

    y_prompt = y[:t_p].reshape(nb_p, s_p, d)
    y_sample = y[t_p:].reshape(nb_s, s_s, d)
    new_ckv = ckv[:t_p].reshape(nb_p, s_p, KV_LORA)
    new_krope = kr[:t_p].reshape(nb_p, s_p, QK_ROPE)
    return (y_prompt, y_sample, new_ckv, new_krope, new_lru)
```

```python
import functools

import numpy as np
import jax
import jax.numpy as jnp
from jax import lax
from jax.experimental import pallas as pl
from jax.experimental.pallas import tpu as pltpu

F32 = jnp.float32
BF16 = jnp.bfloat16

D_MODEL = 1024
EPS = 1e-6
GRID_W = 64
MLA_HEADS = 4
Q_LORA = 384
KV_LORA = 256
QK_NOPE = 128
QK_ROPE = 64
V_HEAD = 128
MLA_WIDTH = MLA_HEADS * V_HEAD
ROPE_BASE = 10000.0
LRU_WIDTH = 512
LRU_BLOCKS = 4
LRU_BLOCK = LRU_WIDTH // LRU_BLOCKS
CONV_W = 4
CONV_LEFT = 2
LRU_C = 8.0
POOL_WINDOWS = (2, 4, 8, 16)
POOL_GROUP = D_MODEL // len(POOL_WINDOWS)
PEER_HEADS = 8
N_KEYS = 128
PEER_DKEY = 256
PEER_TOPK = 16

SUBLANES = 8
LANES = 128
VMEM_BYTES = 64 * 1024 * 1024
VMEM_LIMIT = VMEM_BYTES - 4 * 1024 * 1024

TM = 256
QK_PAD = 256
HALO = 16
PEER_TM = 1024
PEER_EBLK = 512
PEER_COLS = 256
BF16_ROWS = 16
ROW_PAD = LANES
IN_EXT = 2048


def _rms(x, g):
    return x * lax.rsqrt(jnp.mean(x * x, axis=-1, keepdims=True) + EPS) * g


def _cparams(sem):
    return pltpu.CompilerParams(dimension_semantics=sem, vmem_limit_bytes=VMEM_LIMIT)


def _ada_kernel(c_ref, w_ref, b_ref, o_ref):
    c = c_ref[...]
    o_ref[...] = jnp.dot(c * jax.nn.sigmoid(c), w_ref[...], preferred_element_type=F32) + b_ref[...]


def _ada(cpad, w_mod, b_mod):
    n = w_mod.shape[1]
    bn = 768
    return pl.pallas_call(
        _ada_kernel,
        grid=(n // bn,),
        in_specs=[pl.BlockSpec(cpad.shape, lambda j: (0, 0)),
                  pl.BlockSpec((D_MODEL, bn), lambda j: (0, j)),
                  pl.BlockSpec((1, bn), lambda j: (0, j))],
        out_specs=pl.BlockSpec((cpad.shape[0], bn), lambda j: (0, j)),
        out_shape=jax.ShapeDtypeStruct((cpad.shape[0], n), F32),
        compiler_params=_cparams(("arbitrary",)),
    )(cpad, w_mod, b_mod.reshape(1, n))


def _inproj_kernel(x_ref, mod_ref, g_ref, win_ref, gq_ref, wuq_ref, gkv_ref, ck_ref, sk_ref,
                   q_ref, ckv_ref, kr_ref, krr_ref, ux_ref, ug_ref):
    mod = mod_ref[0]
    h = _rms(x_ref[...], g_ref[...]) * (1.0 + mod[1:2]) + mod[0:1]
    y = jnp.dot(h.astype(BF16), win_ref[...], preferred_element_type=F32)
    cq = y[:, 0:Q_LORA]
    ckv = y[:, Q_LORA:Q_LORA + KV_LORA]
    o = Q_LORA + KV_LORA
    kr = y[:, o:o + QK_ROPE]
    krp = y[:, o + 128:o + 128 + QK_ROPE]
    krs = y[:, o + 256:o + 256 + QK_ROPE]
    ux_ref[...] = y[:, o + 384:o + 384 + LRU_WIDTH]
    ug_ref[...] = y[:, o + 384 + LRU_WIDTH:o + 384 + 2 * LRU_WIDTH]
    ckv_ref[...] = _rms(ckv, gkv_ref[...])
    kr_ref[...] = kr
    ck = ck_ref[...]
    sk = sk_ref[...]
    krr_ref[...] = (krp * ck + krs * sk).astype(BF16)
    q = jnp.dot(_rms(cq, gq_ref[...]).astype(BF16), wuq_ref[...], preferred_element_type=F32)
    nw = MLA_HEADS * QK_NOPE
    rw = MLA_HEADS * QK_ROPE
    for hd in range(MLA_HEADS):
        qp = q[:, nw + hd * QK_ROPE:nw + (hd + 1) * QK_ROPE]
        qs = q[:, nw + rw + hd * QK_ROPE:nw + rw + (hd + 1) * QK_ROPE]
        q_ref[hd, :, 0:QK_NOPE] = q[:, hd * QK_NOPE:(hd + 1) * QK_NOPE].astype(BF16)
        q_ref[hd, :, QK_NOPE:QK_NOPE + QK_ROPE] = (qp * ck + qs * sk).astype(BF16)
        q_ref[hd, :, QK_NOPE + QK_ROPE:QK_PAD] = jnp.zeros((TM, QK_PAD - QK_NOPE - QK_ROPE), BF16)


def _attn_kernel(*refs, has_cache, s_new, n_cache):
    if has_cache:
        q_ref, ckv_ref, krr_ref, cckv_ref, ckr_ref, wukv_ref, o_ref, kcat_ref, vv_ref = refs
    else:
        q_ref, ckv_ref, krr_ref, wukv_ref, o_ref, kcat_ref, vv_ref = refs
    sk = n_cache + s_new
    kw = MLA_HEADS * QK_NOPE
    zpad = jnp.zeros((TM, QK_PAD - QK_NOPE - QK_ROPE), BF16)

    def put_keys(row0, ckv_rows, kr_rows):
        kv = jnp.dot(ckv_rows.astype(BF16), wukv_ref[...], preferred_element_type=F32)
        rows = pl.ds(row0, TM)
        for hd in range(MLA_HEADS):
            kcat_ref[hd, rows, 0:QK_NOPE] = kv[:, hd * QK_NOPE:(hd + 1) * QK_NOPE].astype(BF16)
            kcat_ref[hd, rows, QK_NOPE:QK_NOPE + QK_ROPE] = kr_rows.astype(BF16)
            kcat_ref[hd, rows, QK_NOPE + QK_ROPE:QK_PAD] = zpad
        vv_ref[rows, :] = kv[:, kw:].astype(BF16)

    @pl.when(pl.program_id(1) == 0)
    def _():
        if has_cache:
            for c in range(n_cache // TM):
                put_keys(c * TM, cckv_ref[0, c * TM:(c + 1) * TM, :], ckr_ref[0, c * TM:(c + 1) * TM, :])

        def body(c, carry):
            r0 = pl.multiple_of(c * TM, TM)
            put_keys(n_cache + r0, ckv_ref[pl.ds(r0, TM), :], krr_ref[pl.ds(r0, TM), :])
            return carry
        lax.fori_loop(0, s_new // TM, body, 0)

    scale = (QK_NOPE + QK_ROPE) ** -0.5
    for hd in range(MLA_HEADS):
        s = lax.dot_general(q_ref[hd], kcat_ref[hd], (((1,), (1,)), ((), ())),
                            preferred_element_type=F32) * scale
        m = jnp.max(s, axis=-1, keepdims=True)
        e = jnp.exp(s - m)
        l = jnp.sum(e, axis=-1, keepdims=True)
        o = jnp.dot(e.astype(BF16), vv_ref[:, hd * V_HEAD:(hd + 1) * V_HEAD], preferred_element_type=F32)
        o_ref[:, hd * V_HEAD:(hd + 1) * V_HEAD] = (o / l).astype(BF16)
    del sk


def _lru_dir(xp_ref, xc_ref, xn_ref, valid_prev, valid_next, d, reverse, cw_ref, cb_ref,
             wr_ref, br_ref, wi_ref, bi_ref, lam_ref, carry):
    xp = jnp.where(valid_prev, xp_ref[...], 0.0)
    xn = jnp.where(valid_next, xn_ref[...], 0.0)
    xx = jnp.concatenate([xp, xc_ref[...], xn], axis=0)
    n = TM + 2 * SUBLANES
    xc = cb_ref[...]
    for k in range(CONV_W):
        sh = (CONV_LEFT - k) % n
        xs = xx if sh == 0 else pltpu.roll(xx, sh, 0)
        xc = xc + xs[SUBLANES:SUBLANES + TM] * cw_ref[k:k + 1, :]
    rs, is_ = [], []
    for b in range(LRU_BLOCKS):
        xb = xc[:, b * LRU_BLOCK:(b + 1) * LRU_BLOCK]
        rs.append(jnp.dot(xb, wr_ref[d, b], preferred_element_type=F32))
        is_.append(jnp.dot(xb, wi_ref[d, b], preferred_element_type=F32))
    r = jax.nn.sigmoid(jnp.concatenate(rs, axis=-1) + br_ref[d:d + 1, :])
    i = jax.nn.sigmoid(jnp.concatenate(is_, axis=-1) + bi_ref[d:d + 1, :])
    nl = -lam_ref[d:d + 1, :]
    softplus = jnp.maximum(nl, 0.0) + jnp.log1p(jnp.exp(-jnp.abs(nl)))
    log_a = -LRU_C * r * softplus
    a = jnp.exp(log_a)
    bx = jnp.sqrt(jnp.tanh(-log_a) * (a * a + 1.0)) * (i * xc)
    t = lax.broadcasted_iota(jnp.int32, (TM, 1), 0)
    step = 1
    while step < TM:
        if reverse:
            keep = t < TM - step
            sh = TM - step
        else:
            keep = t >= step
            sh = step
        a_s = jnp.where(keep, pltpu.roll(a, sh, 0), 1.0)
        b_s = jnp.where(keep, pltpu.roll(bx, sh, 0), 0.0)
        bx = a * b_s + bx
        a = a * a_s
        step *= 2
    return a * carry + bx


def _lru_kernel(fxp, fxc, fxn, bxp, bxc, bxn, cw_ref, cb_ref, wr_ref, br_ref, wi_ref, bi_ref, lam_ref,
                h0_ref, hf_ref, hb_ref, st_ref, cf_ref, cbk_ref, *, nc):
    c = pl.program_id(1)

    @pl.when(c == 0)
    def _():
        cf_ref[...] = h0_ref[0, 0:1, :]
        cbk_ref[...] = h0_ref[0, 1:2, :]

    params = (cw_ref, cb_ref, wr_ref, br_ref, wi_ref, bi_ref, lam_ref)
    hf = _lru_dir(fxp, fxc, fxn, c > 0, c < nc - 1, 0, False, *params, cf_ref[...])
    hf_ref[...] = hf
    cf_ref[...] = hf[TM - 1:TM, :]
    hb = _lru_dir(bxp, bxc, bxn, c < nc - 1, c > 0, 1, True, *params, cbk_ref[...])
    hb_ref[...] = hb
    cbk_ref[...] = hb[0:1, :]
    st_ref[0, 0:1, :] = hf[TM - 1:TM, :]
    st_ref[0, 1:2, :] = hb[0:1, :]


def _oproj_kernel(x_ref, mod_ref, at_ref, hf_ref, hb_ref, ug_ref, woa_ref, wor_ref, o_ref):
    mod = mod_ref[0]
    rec = ((hf_ref[...] + hb_ref[...]) * jax.nn.gelu(ug_ref[...])).astype(BF16)
    out = (jnp.dot(at_ref[...], woa_ref[...], preferred_element_type=F32)
           + jnp.dot(rec, wor_ref[...], preferred_element_type=F32))
    o_ref[...] = x_ref[...] + mod[2:3] * out


def _pool_kernel(xp_ref, xc_ref, xn_ref, mod_ref, g_ref, wp_ref, sp_ref, o_ref, *, tiles_per_seq, n_prompt_tiles):
    i = pl.program_id(0)
    j = jnp.where(i < n_prompt_tiles, 0, (i - n_prompt_tiles) % tiles_per_seq)
    ntile = jnp.where(i < n_prompt_tiles, 1, tiles_per_seq)
    mod = mod_ref[0]
    g = g_ref[...]

    def hmod(x):
        return _rms(x, g) * (1.0 + mod[1:2]) + mod[0:1]

    x = xc_ref[...]
    hc = hmod(x)
    hp = jnp.where(j > 0, hmod(xp_ref[...]), 0.0)
    hn = jnp.where(j < ntile - 1, hmod(xn_ref[...]), 0.0)
    hh = jnp.concatenate([hp, hc, hn], axis=0)
    n = TM + 2 * HALO
    seq_len = ntile * TM
    t = j * TM + lax.broadcasted_iota(jnp.int32, (TM, 1), 0)
    ys = []
    for gi, w in enumerate(POOL_WINDOWS):
        cols = slice(gi * POOL_GROUP, (gi + 1) * POOL_GROUP)
        p = hh[:, cols]
        p = p + pltpu.roll(p, 1, 0)
        half = 1
        while 2 * half < w:
            p = pltpu.roll(p, half, 0) + pltpu.roll(p, n - half, 0)
            half *= 2
        lo = jnp.maximum(t - w // 2, 0)
        hi = jnp.minimum(t + (w - w // 2), seq_len)
        mean = p[HALO:HALO + TM] / (hi - lo).astype(F32)
        dg = (mean - hc[:, cols]).astype(BF16)
        ys.append(jnp.dot(dg, wp_ref[gi], preferred_element_type=F32))
    y = jnp.concatenate(ys, axis=-1) * sp_ref[...]
    o_ref[...] = x + mod[2:3] * y


def _sort_pairs(n):
    pairs = []

    def merge(lo, cnt, r):
        step = r * 2
        if step < cnt:
            merge(lo, cnt, step)
            merge(lo + r, cnt, step)
            for i in range(lo + r, lo + cnt - r, step):
                pairs.append((i, i + r))
        else:
            pairs.append((lo, lo + r))

    def sort(lo, cnt):
        if cnt > 1:
            m = cnt // 2
            sort(lo, m)
            sort(lo + m, m)
            merge(lo, cnt, 1)

    sort(0, n)
    return pairs


_SORT16 = _sort_pairs(PEER_TOPK)
_HYPER = [(a, b) for a in range(PEER_TOPK) for b in range(PEER_TOPK) if (a + 1) * (b + 1) <= PEER_TOPK]


def _top16_sorted(s):
    k = PEER_TOPK
    x = [s[SUBLANES * r:SUBLANES * (r + 1), :] for r in range(N_KEYS // SUBLANES)]
    for (i, j) in _SORT16:
        hi = jnp.maximum(x[i], x[j])
        lo = jnp.minimum(x[i], x[j])
        x[i], x[j] = hi, lo
    for shift in (4, 2, 1):
        y = [jnp.maximum(x[r], pltpu.roll(x[k - 1 - r], shift, 0)) for r in range(k)]
        stride = k // 2
        while stride >= 1:
            for i in range(k):
                if i & stride == 0:
                    hi = jnp.maximum(y[i], y[i + stride])
                    lo = jnp.minimum(y[i], y[i + stride])
                    y[i], y[i + stride] = hi, lo
            stride //= 2
        x = y
    return x


def _pair_counts(sv1, sv2):
    one = jnp.ones_like(sv1[0])
    zero = jnp.zeros_like(sv1[0])
    cand = [sv1[a] + sv2[b] for (a, b) in _HYPER]

    def ordered(i, j):
        (ai, bi), (aj, bj) = _HYPER[i], _HYPER[j]
        return (aj <= ai and bj <= bi) or (ai <= aj and bi <= bj)

    nh = len(_HYPER)
    cnt = [float((a + 1) * (b + 1) - 1 + sum(1 for j in range(i + 1, nh) if not ordered(i, j))) * one
           for i, (a, b) in enumerate(_HYPER)]
    for i in range(nh):
        for jx in range(i):
            if ordered(i, jx):
                continue
            ge = jnp.where(cand[jx] >= cand[i], one, zero)
            cnt[i] = cnt[i] + ge
            cnt[jx] = cnt[jx] - ge
    e1 = [jnp.exp(sv1[a] - sv1[0]) for a in range(PEER_TOPK)]
    e2 = [jnp.exp(sv2[b] - sv2[0]) for b in range(PEER_TOPK)]
    n = [zero for _ in range(PEER_TOPK)]
    z = zero
    for i, (a, b) in enumerate(_HYPER):
        sel = jnp.where(cnt[i] < float(PEER_TOPK), one, zero)
        n[a] = n[a] + sel
        z = z + sel * (e1[a] * e2[b])
    return n, 1.0 / z


def _head_tables(hd, g, lanes, sc_ref, sv_ref, nz_ref, ex_ref, cnt1_ref, e1n_ref, tab_ref, exact):
    k = PEER_TOPK

    def bc(ref, *idx):
        return jnp.broadcast_to(ref[idx + (slice(hd, hd + 1), lanes)], (SUBLANES, LANES))

    sv1b = [bc(sv_ref, 0, a) for a in range(k)]
    nb = [bc(nz_ref, a) for a in range(k)]
    izb = bc(nz_ref, k)
    c1 = jnp.zeros((SUBLANES, LANES), F32)
    for r in range(N_KEYS // SUBLANES):
        rows = slice(r * SUBLANES, (r + 1) * SUBLANES)
        s1 = sc_ref[pl.ds(2 * hd * N_KEYS + r * SUBLANES, SUBLANES), lanes]
        cnt1 = jnp.zeros_like(s1)
        if exact:
            rank1 = jnp.zeros_like(s1)
            for a in range(k):
                rank1 = jnp.where(sv1b[a] > s1, float(a + 1), rank1)
            rank1 = rank1 + ex_ref[0, rows, lanes]
            for a in range(k):
                cnt1 = jnp.where(rank1 == float(a), nb[a], cnt1)
        else:
            for a in range(k):
                cnt1 = jnp.where(s1 == sv1b[a], nb[a], cnt1)
            c1 = c1 + jnp.where(s1 >= sv1b[k - 1], 1.0, 0.0)
        cnt1_ref[hd, rows, lanes] = cnt1
        e1n_ref[hd, rows, lanes] = jnp.exp(s1 - sv1b[0]) * izb
    sv2b = [bc(sv_ref, 1, a) for a in range(k)]
    c2 = jnp.zeros((SUBLANES, LANES), F32)
    for r in range(N_KEYS // BF16_ROWS):
        rk, e2 = [], []
        for q in range(BF16_ROWS // SUBLANES):
            r0 = r * BF16_ROWS + q * SUBLANES
            rows = slice(r0, r0 + SUBLANES)
            s2 = sc_ref[pl.ds((2 * hd + 1) * N_KEYS + r0, SUBLANES), lanes]
            rank2 = jnp.zeros_like(s2)
            for a in range(k):
                rank2 = jnp.where(sv2b[a] > s2, float(a + 1), rank2)
            if exact:
                rank2 = jnp.minimum(rank2 + ex_ref[1, rows, lanes], float(k))
            else:
                c2 = c2 + jnp.where(rank2 < float(k), 1.0, 0.0)
            rk.append(rank2)
            e2.append(jnp.exp(s2 - sv2b[0]))
        tab_ref[r, g, hd, 0] = jnp.concatenate(rk, axis=0).astype(BF16)
        tab_ref[r, g, hd, 1] = jnp.concatenate(e2, axis=0).astype(BF16)
    if exact:
        return None
    ex_ref[0, 0:SUBLANES, lanes] = c1
    ex_ref[1, 0:SUBLANES, lanes] = c2
    b = jnp.zeros((SUBLANES, LANES), F32)
    for svb in (sv1b, sv2b):
        for a in range(k - 1):
            b = b + jnp.where(svb[a] == svb[a + 1], 1.0, 0.0)
    return b


def _tie_offsets(hd, sc_ref, ex_ref):
    tm = PEER_TM
    nidx = lax.broadcasted_iota(jnp.int32, (N_KEYS, tm), 0)
    for p in range(2):
        s = sc_ref[pl.ds((2 * hd + p) * N_KEYS, N_KEYS), 0:tm]

        def body(m, e, s=s, p=p):
            row = sc_ref[pl.ds((2 * hd + p) * N_KEYS + m, 1), 0:tm]
            return e + jnp.where((s == row) & (nidx > m), 1.0, 0.0)

        ex_ref[p] = lax.fori_loop(0, N_KEYS, body, jnp.zeros((N_KEYS, tm), F32))


def _peer_prologue(x_ref, mod_ref, g_ref, wqt_ref, keys_ref, h2t_ref, tab_ref, sc_ref, sv_ref, nz_ref,
                   ex_ref, cnt1_ref, e1n_ref, y_ref):
    tm = PEER_TM
    half = PEER_DKEY // 2
    ngroup = tm // LANES
    mod = mod_ref[0]
    h2 = _rms(x_ref[...], g_ref[...]) * (1.0 + mod[4:5]) + mod[3:4]
    h2t_ref[:, 0:tm] = h2.T.astype(BF16)

    def lane_group(g):
        return pl.ds(pl.multiple_of(g * LANES, LANES), LANES)

    nslot, qrows = y_ref.shape[0], y_ref.shape[1]
    per_pass = nslot * qrows // half
    for i in range(2 * PEER_HEADS):
        if i % per_pass == 0:
            qt = jnp.dot(wqt_ref[i * half:(i + per_pass) * half, :], h2t_ref[:, 0:tm],
                         preferred_element_type=F32).astype(BF16)
            for s in range(nslot):
                y_ref[s, :, 0:tm] = qt[s * qrows:(s + 1) * qrows, :]
        j = (i % per_pass) * half
        sc_ref[i * N_KEYS:(i + 1) * N_KEYS, 0:tm] = jnp.dot(
            keys_ref[i], y_ref[j // qrows, j % qrows:j % qrows + half, 0:tm],
            preferred_element_type=F32)

        def group(g, c, i=i):
            lanes = lane_group(g)
            top = _top16_sorted(sc_ref[i * N_KEYS:(i + 1) * N_KEYS, lanes])
            for a in range(PEER_TOPK):
                sv_ref[i % 2, a, i // 2:i // 2 + 1, lanes] = top[a][0:1, :]
            return c
        lax.fori_loop(0, ngroup, group, 0)

    def counts(g, carry):
        lanes = lane_group(g)
        sv1 = [sv_ref[0, a, :, lanes] for a in range(PEER_TOPK)]
        sv2 = [sv_ref[1, a, :, lanes] for a in range(PEER_TOPK)]
        n, inv_z = _pair_counts(sv1, sv2)
        for a in range(PEER_TOPK):
            nz_ref[a, :, lanes] = n[a]
        nz_ref[PEER_TOPK, :, lanes] = inv_z
        return carry
    lax.fori_loop(0, ngroup, counts, 0)

    tabs = (sc_ref, sv_ref, nz_ref, ex_ref, cnt1_ref, e1n_ref, tab_ref)

    for hd in range(PEER_HEADS):
        def fast(g, bad, hd=hd):
            return bad + _head_tables(hd, g, lane_group(g), *tabs, exact=False)
        bad = lax.fori_loop(0, ngroup, fast, jnp.zeros((SUBLANES, LANES), F32))
        totals = jnp.sum(ex_ref[:, 0:SUBLANES, :], axis=1)
        ties = jnp.max(bad) + jnp.max(jnp.abs(totals - float(PEER_TOPK)))

        @pl.when(ties > 0.0)
        def _(hd=hd):
            _tie_offsets(hd, sc_ref, ex_ref)

            def slow(g, c):
                _head_tables(hd, g, lane_group(g), *tabs, exact=True)
                return c
            lax.fori_loop(0, ngroup, slow, 0)
    sc_ref[0:D_MODEL, 0:tm] = jnp.zeros((D_MODEL, tm), F32)


def _peer_kernel(x_ref, mod_ref, g_ref, wqt_ref, keys_ref, u0_ref, u_ref, vt_ref, gfin_ref, o_ref,
                 h2t_ref, tab_ref, sc_ref, sv_ref, nz_ref, ex_ref, cnt1_ref, e1n_ref, raw_ref, y_ref, row_ref,
                 *, final_norm):
    k = pl.program_id(1)
    tm = PEER_TM
    d = D_MODEL
    acc_ref = sc_ref.at[0:d]

    nstep = pl.num_programs(1) - 1
    zero = jnp.zeros((BF16_ROWS, LANES), BF16)
    npair = PEER_EBLK // (2 * N_KEYS)
    nh = PEER_HEADS
    cur = k % 2
    oth = (k + 1) % 2
    nchunk = tm // PEER_COLS
    gpc = PEER_COLS // LANES

    def park_rows():
        for q in range(npair):
            i1 = (k * npair + q) * 2
            for j in range(2):
                for hd in range(nh):
                    r0 = (q * 4 + j) * nh + hd
                    r1 = (q * 4 + 2 + j) * nh + hd
                    row_ref[r0:r0 + 1, :] = cnt1_ref[hd, pl.ds(i1 + j, 1), :]
                    row_ref[r1:r1 + 1, :] = e1n_ref[hd, pl.ds(i1 + j, 1), :]

    def up_cols(src_ref, slot, cols):
        raw_ref[slot, :, cols] = jnp.dot(src_ref[:, 0:d], h2t_ref[:, cols], preferred_element_type=F32)

    def down_cols(cols):
        acc_ref[:, cols] += jnp.dot(vt_ref[0, :, 0:PEER_EBLK], y_ref[oth, :, cols], preferred_element_type=F32)

    def gate_cols(g):
        lanes = pl.ds(pl.multiple_of(g * LANES, LANES), LANES)

        def bc(idx):
            return jnp.broadcast_to(row_ref[idx:idx + 1, lanes], (BF16_ROWS, LANES)).astype(BF16)

        for q in range(npair):
            cbs = [[bc((q * 4 + j) * nh + hd) for hd in range(nh)] for j in range(2)]
            ebs = [[bc((q * 4 + 2 + j) * nh + hd) for hd in range(nh)] for j in range(2)]
            for r in range(N_KEYS // BF16_ROWS):
                w = [None, None]
                for hd in range(nh):
                    rk = tab_ref[r, g, hd, 0]
                    ev = tab_ref[r, g, hd, 1]
                    for j in range(2):
                        t = jnp.where(rk < cbs[j][hd], ev, zero) * ebs[j][hd]
                        w[j] = t if w[j] is None else w[j] + t
                for j in range(2):
                    rows = pl.ds((2 * q + j) * N_KEYS + r * BF16_ROWS, BF16_ROWS)
                    act = jax.nn.gelu(raw_ref[cur, rows, lanes].astype(BF16))
                    y_ref[cur, rows, lanes] = act * w[j]

    def sweep(first, with_up, with_gate, with_down):
        def chunk(c, carry):
            cols = pl.ds(pl.multiple_of(c * PEER_COLS, PEER_COLS), PEER_COLS)
            if first:
                up_cols(u0_ref, 0, cols)
            if with_up:
                up_cols(u_ref, oth, cols)
            if with_down:
                down_cols(cols)
            if with_gate:
                for gi in range(gpc):
                    gate_cols(c * gpc + gi)
            return carry
        lax.fori_loop(0, nchunk, chunk, 0)

    @pl.when(k == 0)
    def _():
        _peer_prologue(x_ref, mod_ref, g_ref, wqt_ref, keys_ref, h2t_ref, tab_ref, sc_ref, sv_ref, nz_ref,
                       ex_ref, cnt1_ref, e1n_ref, y_ref)
        park_rows()
        sweep(True, True, True, False)

    @pl.when((k > 0) & (k < nstep))
    def _():
        park_rows()
        sweep(False, True, True, True)

    @pl.when(k == nstep)
    def _():
        sweep(False, False, False, True)
        out = x_ref[...] + mod_ref[0][5:6] * acc_ref[:, 0:tm].T
        if final_norm:
            out = _rms(out, gfin_ref[...])
        o_ref[...] = out


def _rope_tables(s_prompt_tile, s_sample):
    n_rows = s_sample // GRID_W
    rows = jnp.repeat(jnp.arange(n_rows, dtype=F32), GRID_W)
    cols = jnp.tile(jnp.arange(GRID_W, dtype=F32), n_rows)
    axis_dim = QK_ROPE // 2
    inv_freq = ROPE_BASE ** (-jnp.arange(0, axis_dim, 2, dtype=F32) / axis_dim)
    ang = jnp.concatenate([rows[:, None] * inv_freq, cols[:, None] * inv_freq], axis=-1)
    cos, sin = jnp.cos(ang), jnp.sin(ang)
    ck = jnp.concatenate([cos, cos], axis=-1)
    sk = jnp.concatenate([-sin, sin], axis=-1)
    ident_c = jnp.ones((s_prompt_tile, QK_ROPE), F32)
    ident_s = jnp.zeros((s_prompt_tile, QK_ROPE), F32)
    return jnp.concatenate([ident_c, ck], axis=0), jnp.concatenate([ident_s, sk], axis=0)


def kernel(x_prompt, x_sample, cache_ckv_l0, cache_krope_l0, state_lru_l0, c, c_ctx, w_mod_l0, b_mod_l0, w_mod_l1, b_mod_l1, g_mix_l0, g_ffn_l0, g_mix_l1, g_ffn_l1, w_in_l0, g_q_l0, w_uq_l0, g_kv_l0, w_ukv_l0, conv_w_l0, conv_b_l0, w_rg_l0, b_rg_l0, w_ig_l0, b_ig_l0, lam_l0, w_o_l0, w_pool_l1, s_pool_l1, peer_wq_l0, peer_keys_l0, peer_u_l0, peer_v_l0, peer_wq_l1, peer_keys_l1, peer_u_l1, peer_v_l1, g_final):
    nb_p, s_p, d = x_prompt.shape
    nb_s, s_s, _ = x_sample.shape
    n_cache = cache_ckv_l0.shape[1]
    assert d == D_MODEL and s_p == TM and s_s % TM == 0 and n_cache % TM == 0
    t_p = nb_p * s_p
    t_s = nb_s * s_s
    t_all = t_p + t_s
    npt = t_p // TM
    tps = s_s // TM
    ntile = t_all // TM
    assert t_all % PEER_TM == 0 and t_p % PEER_TM == 0 and s_s % PEER_TM == 0

    x0 = jnp.concatenate([x_prompt.reshape(t_p, d), x_sample.reshape(t_s, d)], axis=0)

    ncond = 1 + nb_s
    cpad = jnp.zeros((2 * SUBLANES, d), F32).at[0].set(c_ctx).at[1:ncond].set(c)
    mod0 = _ada(cpad, w_mod_l0, b_mod_l0).reshape(2 * SUBLANES, 6, d)
    mod1 = _ada(cpad, w_mod_l1, b_mod_l1).reshape(2 * SUBLANES, 6, d)

    def cond_row(i):
        return jnp.where(i < npt, 0, 1 + (i - npt) // tps)

    mod_spec = pl.BlockSpec((1, 6, d), lambda i: (cond_row(i), 0, 0))
    row = lambda a: a.reshape(1, -1)

    perm = np.concatenate([np.arange(0, QK_ROPE, 2), np.arange(1, QK_ROPE, 2)])
    perm_sw = np.concatenate([np.arange(1, QK_ROPE, 2), np.arange(0, QK_ROPE, 2)])
    o1 = Q_LORA + KV_LORA
    w_kr = w_in_l0[:, o1:o1 + QK_ROPE]
    z64 = jnp.zeros((d, 128 - QK_ROPE), F32)
    w_in_ext = jnp.concatenate(
        [w_in_l0[:, :o1], w_kr, z64, w_kr[:, perm], z64, w_kr[:, perm_sw], z64, w_in_l0[:, o1 + QK_ROPE:]],
        axis=1).astype(BF16)
    assert w_in_ext.shape[1] == IN_EXT
    wq3 = w_uq_l0.reshape(Q_LORA, MLA_HEADS, QK_NOPE + QK_ROPE)
    w_uq_ext = jnp.concatenate(
        [wq3[:, :, :QK_NOPE].reshape(Q_LORA, -1),
         wq3[:, :, QK_NOPE:][:, :, perm].reshape(Q_LORA, -1),
         wq3[:, :, QK_NOPE:][:, :, perm_sw].reshape(Q_LORA, -1)], axis=1).astype(BF16)
    ck_tab, sk_tab = _rope_tables(TM, s_s)

    def rope_blk(i):
        return jnp.where(i < npt, 0, 1 + (i - npt) % tps)

    tok = lambda w: pl.BlockSpec((TM, w), lambda i: (i, 0))
    full = lambda a: pl.BlockSpec(a.shape, lambda *_: (0,) * a.ndim)
    q, ckv, kr, krr, ux, ug = pl.pallas_call(
        _inproj_kernel,
        grid=(ntile,),
        in_specs=[tok(d), mod_spec, full(row(g_mix_l0)), full(w_in_ext), full(row(g_q_l0)), full(w_uq_ext),
                  full(row(g_kv_l0)),
                  pl.BlockSpec((TM, QK_ROPE), lambda i: (rope_blk(i), 0)),
                  pl.BlockSpec((TM, QK_ROPE), lambda i: (rope_blk(i), 0))],
        out_specs=[pl.BlockSpec((MLA_HEADS, TM, QK_PAD), lambda i: (0, i, 0)),
                   tok(KV_LORA), tok(QK_ROPE), tok(QK_ROPE), tok(LRU_WIDTH), tok(LRU_WIDTH)],
        out_shape=[jax.ShapeDtypeStruct((MLA_HEADS, t_all, QK_PAD), BF16),
                   jax.ShapeDtypeStruct((t_all, KV_LORA), F32),
                   jax.ShapeDtypeStruct((t_all, QK_ROPE), F32),
                   jax.ShapeDtypeStruct((t_all, QK_ROPE), BF16),
                   jax.ShapeDtypeStruct((t_all, LRU_WIDTH), F32),
                   jax.ShapeDtypeStruct((t_all, LRU_WIDTH), F32)],
        compiler_params=_cparams(("arbitrary",)),
    )(x0, mod0, row(g_mix_l0), w_in_ext, row(g_q_l0), w_uq_ext, row(g_kv_l0), ck_tab, sk_tab)

    wkv3 = w_ukv_l0.reshape(KV_LORA, MLA_HEADS, QK_NOPE + V_HEAD)
    w_ukv_ext = jnp.concatenate([wkv3[:, :, :QK_NOPE].reshape(KV_LORA, -1),
                                 wkv3[:, :, QK_NOPE:].reshape(KV_LORA, -1)], axis=1).astype(BF16)

    def attn_call(nb, s_new, tile0, has_cache):
        nq = s_new // TM
        blk0 = tile0 * TM // s_new
        n_c = n_cache if has_cache else 0
        in_specs = [pl.BlockSpec((MLA_HEADS, TM, QK_PAD), lambda b, qi: (0, tile0 + b * nq + qi, 0)),
                    pl.BlockSpec((s_new, KV_LORA), lambda b, qi: (blk0 + b, 0)),
                    pl.BlockSpec((s_new, QK_ROPE), lambda b, qi: (blk0 + b, 0))]
        args = [q, ckv, krr]
        if has_cache:
            in_specs += [pl.BlockSpec((1, n_cache, KV_LORA), lambda b, qi: (b, 0, 0)),
                         pl.BlockSpec((1, n_cache, QK_ROPE), lambda b, qi: (b, 0, 0))]
            args += [cache_ckv_l0, cache_krope_l0[:, :, perm]]
        in_specs.append(pl.BlockSpec(w_ukv_ext.shape, lambda b, qi: (0, 0)))
        args.append(w_ukv_ext)
        return pl.pallas_call(
            functools.partial(_attn_kernel, has_cache=has_cache, s_new=s_new, n_cache=n_c),
            grid=(nb, nq),
            in_specs=in_specs,
            out_specs=pl.BlockSpec((TM, MLA_WIDTH), lambda b, qi: (b * nq + qi, 0)),
            out_shape=jax.ShapeDtypeStruct((nb * s_new, MLA_WIDTH), BF16),
            scratch_shapes=[pltpu.VMEM((MLA_HEADS, n_c + s_new, QK_PAD), BF16),
                            pltpu.VMEM((n_c + s_new, MLA_WIDTH), BF16)],
            compiler_params=_cparams(("arbitrary", "arbitrary")),
        )(*args)

    attn = jnp.concatenate([attn_call(nb_p, s_p, 0, False), attn_call(nb_s, s_s, npt, True)], axis=0)

    def lru_call(nb, s_new, tile0, h0):
        nc = s_new // TM
        r8 = TM // SUBLANES
        last8 = t_all // SUBLANES - 1

        def cur(rev):
            return pl.BlockSpec((TM, LRU_WIDTH),
                                lambda b, cc: (tile0 + b * nc + (nc - 1 - cc if rev else cc), 0))

        def prev(rev):
            return pl.BlockSpec((SUBLANES, LRU_WIDTH), lambda b, cc: (
                jnp.maximum((tile0 + b * nc + (nc - 1 - cc if rev else cc)) * r8 - 1, 0), 0))

        def nxt(rev):
            return pl.BlockSpec((SUBLANES, LRU_WIDTH), lambda b, cc: (
                jnp.minimum((tile0 + b * nc + (nc - 1 - cc if rev else cc) + 1) * r8, last8), 0))

        small = [conv_w_l0, row(conv_b_l0), w_rg_l0, b_rg_l0, w_ig_l0, b_ig_l0, lam_l0]
        return pl.pallas_call(
            functools.partial(_lru_kernel, nc=nc),
            grid=(nb, nc),
            in_specs=[prev(False), cur(False), nxt(False), prev(True), cur(True), nxt(True)]
                     + [full(a) for a in small]
                     + [pl.BlockSpec((1, 2, LRU_WIDTH), lambda b, cc: (b, 0, 0))],
            out_specs=[pl.BlockSpec((TM, LRU_WIDTH), lambda b, cc: (b * nc + cc, 0)),
                       pl.BlockSpec((TM, LRU_WIDTH), lambda b, cc: (b * nc + nc - 1 - cc, 0)),
                       pl.BlockSpec((1, 2, LRU_WIDTH), lambda b, cc: (b, 0, 0))],
            out_shape=[jax.ShapeDtypeStruct((nb * s_new, LRU_WIDTH), F32),
                       jax.ShapeDtypeStruct((nb * s_new, LRU_WIDTH), F32),
                       jax.ShapeDtypeStruct((nb, 2, LRU_WIDTH), F32)],
            scratch_shapes=[pltpu.VMEM((1, LRU_WIDTH), F32), pltpu.VMEM((1, LRU_WIDTH), F32)],
            compiler_params=_cparams(("arbitrary", "arbitrary")),
        )(ux, ux, ux, ux, ux, ux, *small, h0)

    hf_p, hb_p, new_lru = lru_call(nb_p, s_p, 0, jnp.zeros((nb_p, 2, LRU_WIDTH), F32))
    hf_s, hb_s, _ = lru_call(nb_s, s_s, npt, state_lru_l0.astype(F32))
    hf = jnp.concatenate([hf_p, hf_s], axis=0)
    hb = jnp.concatenate([hb_p, hb_s], axis=0)

    w_o = w_o_l0.astype(BF16)
    x1 = pl.pallas_call(
        _oproj_kernel,
        grid=(ntile,),
        in_specs=[tok(d), mod_spec, tok(MLA_WIDTH), tok(LRU_WIDTH), tok(LRU_WIDTH), tok(LRU_WIDTH),
                  pl.BlockSpec((MLA_WIDTH, d), lambda i: (0, 0)), pl.BlockSpec((LRU_WIDTH, d), lambda i: (1, 0))],
        out_specs=tok(d),
        out_shape=jax.ShapeDtypeStruct((t_all, d), F32),
        compiler_params=_cparams(("arbitrary",)),
    )(x0, mod0, attn, hf, hb, ug, w_o, w_o)

    def peer_call(x, mod, g_ffn, w_q, sub_keys, u, v, final_norm):
        n_exp = u.shape[0]
        assert n_exp == N_KEYS * N_KEYS and n_exp % PEER_EBLK == 0
        wqt = w_q.T.astype(BF16)
        keys = sub_keys.astype(BF16).reshape(2 * PEER_HEADS, N_KEYS, PEER_DKEY // 2)
        ub = jnp.pad(u.astype(BF16), ((0, 0), (0, ROW_PAD)))
        vt = jnp.pad(v.reshape(n_exp // PEER_EBLK, PEER_EBLK, d).transpose(0, 2, 1).astype(BF16),
                     ((0, 0), (0, 0), (0, ROW_PAD)))
        tpp = PEER_TM // TM

        nblk = n_exp // PEER_EBLK

        def cond_row_p(i):
            return cond_row(i * tpp)

        big = lambda dt: pltpu.VMEM((PEER_HEADS, N_KEYS, PEER_TM), dt)
        return pl.pallas_call(
            functools.partial(_peer_kernel, final_norm=final_norm),
            grid=(t_all // PEER_TM, nblk + 1),
            in_specs=[pl.BlockSpec((PEER_TM, d), lambda i, k: (i, 0), pipeline_mode=pl.Buffered(1)),
                      pl.BlockSpec((1, 6, d), lambda i, k: (cond_row_p(i), 0, 0)),
                      pl.BlockSpec((1, d), lambda i, k: (0, 0)),
                      pl.BlockSpec(wqt.shape, lambda i, k: (0, 0)),
                      pl.BlockSpec(keys.shape, lambda i, k: (0, 0, 0)),
                      pl.BlockSpec((PEER_EBLK, d + ROW_PAD), lambda i, k: (0, 0), pipeline_mode=pl.Buffered(1)),
                      pl.BlockSpec((PEER_EBLK, d + ROW_PAD), lambda i, k: (jnp.minimum(k + 1, nblk - 1), 0)),
                      pl.BlockSpec((1, d, PEER_EBLK + ROW_PAD), lambda i, k: (jnp.maximum(k - 1, 0), 0, 0)),
                      pl.BlockSpec((1, d), lambda i, k: (0, 0))],
            out_specs=pl.BlockSpec((PEER_TM, d), lambda i, k: (i, 0), pipeline_mode=pl.Buffered(1)),
            out_shape=jax.ShapeDtypeStruct((t_all, d), F32),
            scratch_shapes=[pltpu.VMEM((d, PEER_TM + ROW_PAD), BF16),
                            pltpu.VMEM((N_KEYS // BF16_ROWS, PEER_TM // LANES, PEER_HEADS, 2, BF16_ROWS, LANES),
                                       BF16),
                            pltpu.VMEM((2 * PEER_HEADS * N_KEYS, PEER_TM + ROW_PAD), F32),
                            pltpu.VMEM((2, PEER_TOPK, PEER_HEADS, PEER_TM), F32),
                            pltpu.VMEM((PEER_TOPK + 1, PEER_HEADS, PEER_TM), F32),
                            pltpu.VMEM((2, N_KEYS, PEER_TM), F32),
                            big(F32), big(F32),
                            pltpu.VMEM((2, PEER_EBLK, PEER_TM + ROW_PAD), F32),
                            pltpu.VMEM((2, PEER_EBLK, PEER_TM + ROW_PAD), BF16),
                            pltpu.VMEM((PEER_EBLK // N_KEYS * 2 * PEER_HEADS, PEER_TM), F32)],
            compiler_params=_cparams(("arbitrary", "arbitrary")),
        )(x, mod, row(g_ffn), wqt, keys, ub, ub, vt, row(g_final))

    x2 = peer_call(x1, mod0, g_ffn_l0, peer_wq_l0, peer_keys_l0, peer_u_l0, peer_v_l0, False)

    rh = TM // HALO
    lasth = t_all // HALO - 1
    x3 = pl.pallas_call(
        functools.partial(_pool_kernel, tiles_per_seq=tps, n_prompt_tiles=npt),
        grid=(ntile,),
        in_specs=[pl.BlockSpec((HALO, d), lambda i: (jnp.maximum(i * rh - 1, 0), 0)),
                  tok(d),
                  pl.BlockSpec((HALO, d), lambda i: (jnp.minimum((i + 1) * rh, lasth), 0)),
                  mod_spec, full(row(g_mix_l1)),
                  pl.BlockSpec(w_pool_l1.shape, lambda i: (0, 0, 0)), full(row(s_pool_l1))],
        out_specs=tok(d),
        out_shape=jax.ShapeDtypeStruct((t_all, d), F32),
        compiler_params=_cparams(("arbitrary",)),
    )(x2, x2, x2, mod1, row(g_mix_l1), w_pool_l1.astype(BF16), row(s_pool_l1))

    y = peer_call(x3, mod1, g_ffn_l1, peer_wq_l1, peer_keys_l1, peer_u_l1, peer_v_l1, True)

    y_prompt = y[:t_p].reshape(nb_p, s_p, d)
    y_sample = y[t_p:].reshape(nb_s, s_s, d)
    new_ckv = ckv[:t_p].reshape(nb_p, s_p, KV_LORA)
    new_krope = kr[:t_p].reshape(nb_p, s_p, QK_ROPE)
    return (y_prompt, y_sample, new_ckv, new_krope, new_lru)
```

```python
import functools

import numpy as np
import jax
import jax.numpy as jnp
from jax import lax
from jax.experimental import pallas as pl
from jax.experimental.pallas import tpu as pltpu

F32 = jnp.float32
BF16 = jnp.bfloat16

D_MODEL = 1024
EPS = 1e-6
GRID_W = 64
MLA_HEADS = 4
Q_LORA = 384
KV_LORA = 256
QK_NOPE = 128
QK_ROPE = 64
V_HEAD = 128
MLA_WIDTH = MLA_HEADS * V_HEAD
ROPE_BASE = 10000.0
LRU_WIDTH = 512
LRU_BLOCKS = 4
LRU_BLOCK = LRU_WIDTH // LRU_BLOCKS
CONV_W = 4
CONV_LEFT = 2
LRU_C = 8.0
POOL_WINDOWS = (2, 4, 8, 16)
POOL_GROUP = D_MODEL // len(POOL_WINDOWS)
PEER_HEADS = 8
N_KEYS = 128
PEER_DKEY = 256
PEER_TOPK = 16

SUBLANES = 8
LANES = 128
VMEM_LIMIT = 56 * 1024 * 1024

TM = 256
QK_PAD = 256
HALO = 16
PEER_TM = 512
PEER_EBLK = 1024
PEER_SUB = 256
BF16_ROWS = 16
IN_EXT = 2048


def _rms(x, g):
    return x * lax.rsqrt(jnp.mean(x * x, axis=-1, keepdims=True) + EPS) * g


def _cparams(sem):
    return pltpu.CompilerParams(dimension_semantics=sem, vmem_limit_bytes=VMEM_LIMIT)


def _ada_kernel(c_ref, w_ref, b_ref, o_ref):
    c = c_ref[...]
    o_ref[...] = jnp.dot(c * jax.nn.sigmoid(c), w_ref[...], preferred_element_type=F32) + b_ref[...]


def _ada(cpad, w_mod, b_mod):
    n = w_mod.shape[1]
    bn = 768
    return pl.pallas_call(
        _ada_kernel,
        grid=(n // bn,),
        in_specs=[pl.BlockSpec(cpad.shape, lambda j: (0, 0)),
                  pl.BlockSpec((D_MODEL, bn), lambda j: (0, j)),
                  pl.BlockSpec((1, bn), lambda j: (0, j))],
        out_specs=pl.BlockSpec((cpad.shape[0], bn), lambda j: (0, j)),
        out_shape=jax.ShapeDtypeStruct((cpad.shape[0], n), F32),
        compiler_params=_cparams(("arbitrary",)),
    )(cpad, w_mod, b_mod.reshape(1, n))


def _inproj_kernel(x_ref, mod_ref, g_ref, win_ref, gq_ref, wuq_ref, gkv_ref, ck_ref, sk_ref,
                   q_ref, ckv_ref, kr_ref, krr_ref, ux_ref, ug_ref):
    mod = mod_ref[0]
    h = _rms(x_ref[...], g_ref[...]) * (1.0 + mod[1:2]) + mod[0:1]
    y = jnp.dot(h.astype(BF16), win_ref[...], preferred_element_type=F32)
    cq = y[:, 0:Q_LORA]
    ckv = y[:, Q_LORA:Q_LORA + KV_LORA]
    o = Q_LORA + KV_LORA
    kr = y[:, o:o + QK_ROPE]
    krp = y[:, o + 128:o + 128 + QK_ROPE]
    krs = y[:, o + 256:o + 256 + QK_ROPE]
    ux_ref[...] = y[:, o + 384:o + 384 + LRU_WIDTH]
    ug_ref[...] = y[:, o + 384 + LRU_WIDTH:o + 384 + 2 * LRU_WIDTH]
    ckv_ref[...] = _rms(ckv, gkv_ref[...])
    kr_ref[...] = kr
    ck = ck_ref[...]
    sk = sk_ref[...]
    krr_ref[...] = (krp * ck + krs * sk).astype(BF16)
    q = jnp.dot(_rms(cq, gq_ref[...]).astype(BF16), wuq_ref[...], preferred_element_type=F32)
    nw = MLA_HEADS * QK_NOPE
    rw = MLA_HEADS * QK_ROPE
    for hd in range(MLA_HEADS):
        qp = q[:, nw + hd * QK_ROPE:nw + (hd + 1) * QK_ROPE]
        qs = q[:, nw + rw + hd * QK_ROPE:nw + rw + (hd + 1) * QK_ROPE]
        q_ref[hd, :, 0:QK_NOPE] = q[:, hd * QK_NOPE:(hd + 1) * QK_NOPE].astype(BF16)
        q_ref[hd, :, QK_NOPE:QK_NOPE + QK_ROPE] = (qp * ck + qs * sk).astype(BF16)
        q_ref[hd, :, QK_NOPE + QK_ROPE:QK_PAD] = jnp.zeros((TM, QK_PAD - QK_NOPE - QK_ROPE), BF16)


def _attn_kernel(*refs, has_cache, s_new, n_cache):
    if has_cache:
        q_ref, ckv_ref, krr_ref, cckv_ref, ckr_ref, wukv_ref, o_ref, kcat_ref, vv_ref = refs
    else:
        q_ref, ckv_ref, krr_ref, wukv_ref, o_ref, kcat_ref, vv_ref = refs
    sk = n_cache + s_new
    kw = MLA_HEADS * QK_NOPE
    zpad = jnp.zeros((TM, QK_PAD - QK_NOPE - QK_ROPE), BF16)

    def put_keys(row0, ckv_rows, kr_rows):
        kv = jnp.dot(ckv_rows.astype(BF16), wukv_ref[...], preferred_element_type=F32)
        rows = pl.ds(row0, TM)
        for hd in range(MLA_HEADS):
            kcat_ref[hd, rows, 0:QK_NOPE] = kv[:, hd * QK_NOPE:(hd + 1) * QK_NOPE].astype(BF16)
            kcat_ref[hd, rows, QK_NOPE:QK_NOPE + QK_ROPE] = kr_rows.astype(BF16)
            kcat_ref[hd, rows, QK_NOPE + QK_ROPE:QK_PAD] = zpad
        vv_ref[rows, :] = kv[:, kw:].astype(BF16)

    @pl.when(pl.program_id(1) == 0)
    def _():
        if has_cache:
            for c in range(n_cache // TM):
                put_keys(c * TM, cckv_ref[0, c * TM:(c + 1) * TM, :], ckr_ref[0, c * TM:(c + 1) * TM, :])

        def body(c, carry):
            r0 = pl.multiple_of(c * TM, TM)
            put_keys(n_cache + r0, ckv_ref[pl.ds(r0, TM), :], krr_ref[pl.ds(r0, TM), :])
            return carry
        lax.fori_loop(0, s_new // TM, body, 0)

    scale = (QK_NOPE + QK_ROPE) ** -0.5
    for hd in range(MLA_HEADS):
        s = lax.dot_general(q_ref[hd], kcat_ref[hd], (((1,), (1,)), ((), ())),
                            preferred_element_type=F32) * scale
        m = jnp.max(s, axis=-1, keepdims=True)
        e = jnp.exp(s - m)
        l = jnp.sum(e, axis=-1, keepdims=True)
        o = jnp.dot(e.astype(BF16), vv_ref[:, hd * V_HEAD:(hd + 1) * V_HEAD], preferred_element_type=F32)
        o_ref[:, hd * V_HEAD:(hd + 1) * V_HEAD] = (o / l).astype(BF16)
    del sk


def _lru_dir(xp_ref, xc_ref, xn_ref, valid_prev, valid_next, d, reverse, cw_ref, cb_ref,
             wr_ref, br_ref, wi_ref, bi_ref, lam_ref, carry):
    xp = jnp.where(valid_prev, xp_ref[...], 0.0)
    xn = jnp.where(valid_next, xn_ref[...], 0.0)
    xx = jnp.concatenate([xp, xc_ref[...], xn], axis=0)
    n = TM + 2 * SUBLANES
    xc = cb_ref[...]
    for k in range(CONV_W):
        sh = (CONV_LEFT - k) % n
        xs = xx if sh == 0 else pltpu.roll(xx, sh, 0)
        xc = xc + xs[SUBLANES:SUBLANES + TM] * cw_ref[k:k + 1, :]
    rs, is_ = [], []
    for b in range(LRU_BLOCKS):
        xb = xc[:, b * LRU_BLOCK:(b + 1) * LRU_BLOCK]
        rs.append(jnp.dot(xb, wr_ref[d, b], preferred_element_type=F32))
        is_.append(jnp.dot(xb, wi_ref[d, b], preferred_element_type=F32))
    r = jax.nn.sigmoid(jnp.concatenate(rs, axis=-1) + br_ref[d:d + 1, :])
    i = jax.nn.sigmoid(jnp.concatenate(is_, axis=-1) + bi_ref[d:d + 1, :])
    nl = -lam_ref[d:d + 1, :]
    softplus = jnp.maximum(nl, 0.0) + jnp.log1p(jnp.exp(-jnp.abs(nl)))
    log_a = -LRU_C * r * softplus
    a = jnp.exp(log_a)
    bx = jnp.sqrt(jnp.tanh(-log_a) * (a * a + 1.0)) * (i * xc)
    t = lax.broadcasted_iota(jnp.int32, (TM, 1), 0)
    step = 1
    while step < TM:
        if reverse:
            keep = t < TM - step
            sh = TM - step
        else:
            keep = t >= step
            sh = step
        a_s = jnp.where(keep, pltpu.roll(a, sh, 0), 1.0)
        b_s = jnp.where(keep, pltpu.roll(bx, sh, 0), 0.0)
        bx = a * b_s + bx
        a = a * a_s
        step *= 2
    return a * carry + bx


def _lru_kernel(fxp, fxc, fxn, bxp, bxc, bxn, cw_ref, cb_ref, wr_ref, br_ref, wi_ref, bi_ref, lam_ref,
                h0_ref, hf_ref, hb_ref, st_ref, cf_ref, cbk_ref, *, nc):
    c = pl.program_id(1)

    @pl.when(c == 0)
    def _():
        cf_ref[...] = h0_ref[0, 0:1, :]
        cbk_ref[...] = h0_ref[0, 1:2, :]

    params = (cw_ref, cb_ref, wr_ref, br_ref, wi_ref, bi_ref, lam_ref)
    hf = _lru_dir(fxp, fxc, fxn, c > 0, c < nc - 1, 0, False, *params, cf_ref[...])
    hf_ref[...] = hf
    cf_ref[...] = hf[TM - 1:TM, :]
    hb = _lru_dir(bxp, bxc, bxn, c < nc - 1, c > 0, 1, True, *params, cbk_ref[...])
    hb_ref[...] = hb
    cbk_ref[...] = hb[0:1, :]
    st_ref[0, 0:1, :] = hf[TM - 1:TM, :]
    st_ref[0, 1:2, :] = hb[0:1, :]


def _oproj_kernel(x_ref, mod_ref, at_ref, hf_ref, hb_ref, ug_ref, woa_ref, wor_ref, o_ref):
    mod = mod_ref[0]
    rec = ((hf_ref[...] + hb_ref[...]) * jax.nn.gelu(ug_ref[...])).astype(BF16)
    out = (jnp.dot(at_ref[...], woa_ref[...], preferred_element_type=F32)
           + jnp.dot(rec, wor_ref[...], preferred_element_type=F32))
    o_ref[...] = x_ref[...] + mod[2:3] * out


def _pool_kernel(xp_ref, xc_ref, xn_ref, mod_ref, g_ref, wp_ref, sp_ref, o_ref, *, tiles_per_seq, n_prompt_tiles):
    i = pl.program_id(0)
    j = jnp.where(i < n_prompt_tiles, 0, (i - n_prompt_tiles) % tiles_per_seq)
    ntile = jnp.where(i < n_prompt_tiles, 1, tiles_per_seq)
    mod = mod_ref[0]
    g = g_ref[...]

    def hmod(x):
        return _rms(x, g) * (1.0 + mod[1:2]) + mod[0:1]

    x = xc_ref[...]
    hc = hmod(x)
    hp = jnp.where(j > 0, hmod(xp_ref[...]), 0.0)
    hn = jnp.where(j < ntile - 1, hmod(xn_ref[...]), 0.0)
    hh = jnp.concatenate([hp, hc, hn], axis=0)
    n = TM + 2 * HALO
    seq_len = ntile * TM
    t = j * TM + lax.broadcasted_iota(jnp.int32, (TM, 1), 0)
    ys = []
    for gi, w in enumerate(POOL_WINDOWS):
        cols = slice(gi * POOL_GROUP, (gi + 1) * POOL_GROUP)
        p = hh[:, cols]
        p = p + pltpu.roll(p, 1, 0)
        half = 1
        while 2 * half < w:
            p = pltpu.roll(p, half, 0) + pltpu.roll(p, n - half, 0)
            half *= 2
        lo = jnp.maximum(t - w // 2, 0)
        hi = jnp.minimum(t + (w - w // 2), seq_len)
        mean = p[HALO:HALO + TM] / (hi - lo).astype(F32)
        dg = (mean - hc[:, cols]).astype(BF16)
        ys.append(jnp.dot(dg, wp_ref[gi], preferred_element_type=F32))
    y = jnp.concatenate(ys, axis=-1) * sp_ref[...]
    o_ref[...] = x + mod[2:3] * y


def _sort_pairs(n):
    pairs = []

    def merge(lo, cnt, r):
        step = r * 2
        if step < cnt:
            merge(lo, cnt, step)
            merge(lo + r, cnt, step)
            for i in range(lo + r, lo + cnt - r, step):
                pairs.append((i, i + r))
        else:
            pairs.append((lo, lo + r))

    def sort(lo, cnt):
        if cnt > 1:
            m = cnt // 2
            sort(lo, m)
            sort(lo + m, m)
            merge(lo, cnt, 1)

    sort(0, n)
    return pairs


_SORT16 = _sort_pairs(PEER_TOPK)
_HYPER = [(a, b) for a in range(PEER_TOPK) for b in range(PEER_TOPK) if (a + 1) * (b + 1) <= PEER_TOPK]


def _top16_sorted(s):
    k = PEER_TOPK
    x = [s[SUBLANES * r:SUBLANES * (r + 1), :] for r in range(N_KEYS // SUBLANES)]
    for (i, j) in _SORT16:
        hi = jnp.maximum(x[i], x[j])
        lo = jnp.minimum(x[i], x[j])
        x[i], x[j] = hi, lo
    for shift in (4, 2, 1):
        y = [jnp.maximum(x[r], pltpu.roll(x[k - 1 - r], shift, 0)) for r in range(k)]
        stride = k // 2
        while stride >= 1:
            for i in range(k):
                if i & stride == 0:
                    hi = jnp.maximum(y[i], y[i + stride])
                    lo = jnp.minimum(y[i], y[i + stride])
                    y[i], y[i + stride] = hi, lo
            stride //= 2
        x = y
    return x


def _pair_counts(sv1, sv2):
    one = jnp.ones_like(sv1[0])
    zero = jnp.zeros_like(sv1[0])
    cand = [sv1[a] + sv2[b] for (a, b) in _HYPER]

    def ordered(i, j):
        (ai, bi), (aj, bj) = _HYPER[i], _HYPER[j]
        return (aj <= ai and bj <= bi) or (ai <= aj and bi <= bj)

    nh = len(_HYPER)
    cnt = [float((a + 1) * (b + 1) - 1 + sum(1 for j in range(i + 1, nh) if not ordered(i, j))) * one
           for i, (a, b) in enumerate(_HYPER)]
    for i in range(nh):
        for jx in range(i):
            if ordered(i, jx):
                continue
            ge = jnp.where(cand[jx] >= cand[i], one, zero)
            cnt[i] = cnt[i] + ge
            cnt[jx] = cnt[jx] - ge
    e1 = [jnp.exp(sv1[a] - sv1[0]) for a in range(PEER_TOPK)]
    e2 = [jnp.exp(sv2[b] - sv2[0]) for b in range(PEER_TOPK)]
    n = [zero for _ in range(PEER_TOPK)]
    z = zero
    for i, (a, b) in enumerate(_HYPER):
        sel = jnp.where(cnt[i] < float(PEER_TOPK), one, zero)
        n[a] = n[a] + sel
        z = z + sel * (e1[a] * e2[b])
    return n, 1.0 / z


def _head_tables(hd, sc_ref, sv_ref, nz_ref, ex_ref, cnt1_ref, e1n_ref, rank2_ref, e2_ref, exact):
    k = PEER_TOPK
    tm = sc_ref.shape[-1]
    nslab = N_KEYS // SUBLANES
    bad = []
    for g in range(tm // LANES):
        lanes = slice(g * LANES, (g + 1) * LANES)

        def bc(ref, *idx):
            return jnp.broadcast_to(ref[idx + (slice(hd, hd + 1), lanes)], (SUBLANES, LANES))

        sv1b = [bc(sv_ref, 0, a) for a in range(k)]
        nb = [bc(nz_ref, a) for a in range(k)]
        izb = bc(nz_ref, k)
        c1 = jnp.zeros((SUBLANES, LANES), F32)
        for r in range(nslab):
            rows = slice(r * SUBLANES, (r + 1) * SUBLANES)
            s1 = sc_ref[0, hd, rows, lanes]
            cnt1 = jnp.zeros_like(s1)
            if exact:
                rank1 = jnp.zeros_like(s1)
                for a in range(k):
                    rank1 = jnp.where(sv1b[a] > s1, float(a + 1), rank1)
                rank1 = rank1 + ex_ref[0, rows, lanes]
                for a in range(k):
                    cnt1 = jnp.where(rank1 == float(a), nb[a], cnt1)
            else:
                for a in range(k):
                    cnt1 = jnp.where(s1 == sv1b[a], nb[a], cnt1)
                c1 = c1 + jnp.where(s1 >= sv1b[k - 1], 1.0, 0.0)
            cnt1_ref[hd, rows, lanes] = cnt1
            e1n_ref[hd, rows, lanes] = jnp.exp(s1 - sv1b[0]) * izb
        sv2b = [bc(sv_ref, 1, a) for a in range(k)]
        c2 = jnp.zeros((SUBLANES, LANES), F32)
        for r in range(N_KEYS // BF16_ROWS):
            rk, e2 = [], []
            for q in range(BF16_ROWS // SUBLANES):
                r0 = r * BF16_ROWS + q * SUBLANES
                rows = slice(r0, r0 + SUBLANES)
                s2 = sc_ref[1, hd, rows, lanes]
                rank2 = jnp.zeros_like(s2)
                for a in range(k):
                    rank2 = jnp.where(sv2b[a] > s2, float(a + 1), rank2)
                if exact:
                    rank2 = jnp.minimum(rank2 + ex_ref[1, rows, lanes], float(k))
                else:
                    c2 = c2 + jnp.where(rank2 < float(k), 1.0, 0.0)
                rk.append(rank2)
                e2.append(jnp.exp(s2 - sv2b[0]))
            rows16 = slice(r * BF16_ROWS, (r + 1) * BF16_ROWS)
            rank2_ref[hd, rows16, lanes] = jnp.concatenate(rk, axis=0).astype(BF16)
            e2_ref[hd, rows16, lanes] = jnp.concatenate(e2, axis=0).astype(BF16)
        if not exact:
            n1 = jnp.sum(c1, axis=0, keepdims=True)
            n2 = jnp.sum(c2, axis=0, keepdims=True)
            b = jnp.where(n1 != float(k), 1.0, 0.0) + jnp.where(n2 != float(k), 1.0, 0.0)
            for svb in (sv1b, sv2b):
                for a in range(k - 1):
                    b = b + jnp.where(svb[a][0:1] == svb[a + 1][0:1], 1.0, 0.0)
            bad.append(b)
    return jnp.concatenate(bad, axis=1) if bad else None


def _tie_offsets(hd, sc_ref, ex_ref):
    tm = sc_ref.shape[-1]
    nidx = lax.broadcasted_iota(jnp.int32, (N_KEYS, tm), 0)
    for p in range(2):
        s = sc_ref[p, hd]

        def body(m, e, s=s, p=p):
            row = sc_ref[p, hd, pl.ds(m, 1), :]
            return e + jnp.where((s == row) & (nidx > m), 1.0, 0.0)

        ex_ref[p] = lax.fori_loop(0, N_KEYS, body, jnp.zeros((N_KEYS, tm), F32))


def _peer_kernel(x_ref, mod_ref, g_ref, wqt_ref, keys_ref, u_ref, vt_ref, gfin_ref, o_ref,
                 h2t_ref, acc_ref, sc_ref, sv_ref, nz_ref, ex_ref, cnt1_ref, e1n_ref, rank2_ref, e2_ref,
                 *, final_norm):
    k = pl.program_id(1)
    tm = PEER_TM
    half = PEER_DKEY // 2

    @pl.when(k == 0)
    def _():
        mod = mod_ref[0]
        h2 = _rms(x_ref[...], g_ref[...]) * (1.0 + mod[4:5]) + mod[3:4]
        h2t_ref[...] = h2.T.astype(BF16)
        acc_ref[...] = jnp.zeros_like(acc_ref)
        qt = jnp.dot(wqt_ref[...], h2t_ref[...], preferred_element_type=F32).astype(BF16)
        for hd in range(PEER_HEADS):
            for p in range(2):
                r0 = (hd * 2 + p) * half
                sc_ref[p, hd] = jnp.dot(keys_ref[hd, p], qt[r0:r0 + half, :],
                                        preferred_element_type=F32)
                for g in range(tm // LANES):
                    lanes = slice(g * LANES, (g + 1) * LANES)
                    top = _top16_sorted(sc_ref[p, hd, :, lanes])
                    for a in range(PEER_TOPK):
                        sv_ref[p, a, hd:hd + 1, lanes] = top[a][0:1, :]
        for g in range(tm // LANES):
            lanes = slice(g * LANES, (g + 1) * LANES)
            sv1 = [sv_ref[0, a, :, lanes] for a in range(PEER_TOPK)]
            sv2 = [sv_ref[1, a, :, lanes] for a in range(PEER_TOPK)]
            n, inv_z = _pair_counts(sv1, sv2)
            for a in range(PEER_TOPK):
                nz_ref[a, :, lanes] = n[a]
            nz_ref[PEER_TOPK, :, lanes] = inv_z
        tabs = (sc_ref, sv_ref, nz_ref, ex_ref, cnt1_ref, e1n_ref, rank2_ref, e2_ref)
        for hd in range(PEER_HEADS):
            bad = _head_tables(hd, *tabs, exact=False)

            @pl.when(jnp.max(bad) > 0.0)
            def _(hd=hd):
                _tie_offsets(hd, sc_ref, ex_ref)
                _head_tables(hd, *tabs, exact=True)

    zero = jnp.zeros((BF16_ROWS, tm), BF16)
    npiece = PEER_EBLK // N_KEYS
    per_sub = PEER_SUB // N_KEYS

    def up_dot(p):
        return jnp.dot(u_ref[p * N_KEYS:(p + 1) * N_KEYS, :], h2t_ref[...], preferred_element_type=F32)

    raw = up_dot(0)
    ys = []
    for p in range(npiece):
        nxt = up_dot(p + 1) if p + 1 < npiece else None
        i1 = k * npiece + p
        cbs = [jnp.broadcast_to(cnt1_ref[hd, pl.ds(i1, 1), :], (BF16_ROWS, tm)).astype(BF16)
               for hd in range(PEER_HEADS)]
        ebs = [jnp.broadcast_to(e1n_ref[hd, pl.ds(i1, 1), :], (BF16_ROWS, tm)).astype(BF16)
               for hd in range(PEER_HEADS)]
        act = jax.nn.gelu(raw).astype(BF16)
        for r in range(N_KEYS // BF16_ROWS):
            rows = slice(r * BF16_ROWS, (r + 1) * BF16_ROWS)
            w = None
            for hd in range(PEER_HEADS):
                t = jnp.where(rank2_ref[hd, rows, :] < cbs[hd], e2_ref[hd, rows, :], zero) * ebs[hd]
                w = t if w is None else w + t
            ys.append(act[rows, :] * w)
        raw = nxt
        if (p + 1) % per_sub == 0:
            e0 = (p + 1 - per_sub) * N_KEYS
            y = jnp.concatenate(ys, axis=0)
            ys = []
            acc_ref[...] += jnp.dot(vt_ref[:, e0:e0 + PEER_SUB], y, preferred_element_type=F32)

    @pl.when(k == pl.num_programs(1) - 1)
    def _():
        out = x_ref[...] + mod_ref[0][5:6] * acc_ref[...].T
        if final_norm:
            out = _rms(out, gfin_ref[...])
        o_ref[...] = out


def _rope_tables(s_prompt_tile, s_sample):
    n_rows = s_sample // GRID_W
    rows = jnp.repeat(jnp.arange(n_rows, dtype=F32), GRID_W)
    cols = jnp.tile(jnp.arange(GRID_W, dtype=F32), n_rows)
    axis_dim = QK_ROPE // 2
    inv_freq = ROPE_BASE ** (-jnp.arange(0, axis_dim, 2, dtype=F32) / axis_dim)
    ang = jnp.concatenate([rows[:, None] * inv_freq, cols[:, None] * inv_freq], axis=-1)
    cos, sin = jnp.cos(ang), jnp.sin(ang)
    ck = jnp.concatenate([cos, cos], axis=-1)
    sk = jnp.concatenate([-sin, sin], axis=-1)
    ident_c = jnp.ones((s_prompt_tile, QK_ROPE), F32)
    ident_s = jnp.zeros((s_prompt_tile, QK_ROPE), F32)
    return jnp.concatenate([ident_c, ck], axis=0), jnp.concatenate([ident_s, sk], axis=0)


def kernel(x_prompt, x_sample, cache_ckv_l0, cache_krope_l0, state_lru_l0, c, c_ctx, w_mod_l0, b_mod_l0, w_mod_l1, b_mod_l1, g_mix_l0, g_ffn_l0, g_mix_l1, g_ffn_l1, w_in_l0, g_q_l0, w_uq_l0, g_kv_l0, w_ukv_l0, conv_w_l0, conv_b_l0, w_rg_l0, b_rg_l0, w_ig_l0, b_ig_l0, lam_l0, w_o_l0, w_pool_l1, s_pool_l1, peer_wq_l0, peer_keys_l0, peer_u_l0, peer_v_l0, peer_wq_l1, peer_keys_l1, peer_u_l1, peer_v_l1, g_final):
    nb_p, s_p, d = x_prompt.shape
    nb_s, s_s, _ = x_sample.shape
    n_cache = cache_ckv_l0.shape[1]
    assert d == D_MODEL and s_p == TM and s_s % TM == 0 and n_cache % TM == 0
    t_p = nb_p * s_p
    t_s = nb_s * s_s
    t_all = t_p + t_s
    npt = t_p // TM
    tps = s_s // TM
    ntile = t_all // TM
    assert t_all % PEER_TM == 0 and t_p % PEER_TM == 0 and s_s % PEER_TM == 0

    x0 = jnp.concatenate([x_prompt.reshape(t_p, d), x_sample.reshape(t_s, d)], axis=0)

    ncond = 1 + nb_s
    cpad = jnp.zeros((2 * SUBLANES, d), F32).at[0].set(c_ctx).at[1:ncond].set(c)
    mod0 = _ada(cpad, w_mod_l0, b_mod_l0).reshape(2 * SUBLANES, 6, d)
    mod1 = _ada(cpad, w_mod_l1, b_mod_l1).reshape(2 * SUBLANES, 6, d)

    def cond_row(i):
        return jnp.where(i < npt, 0, 1 + (i - npt) // tps)

    mod_spec = pl.BlockSpec((1, 6, d), lambda i: (cond_row(i), 0, 0))
    row = lambda a: a.reshape(1, -1)

    perm = np.concatenate([np.arange(0, QK_ROPE, 2), np.arange(1, QK_ROPE, 2)])
    perm_sw = np.concatenate([np.arange(1, QK_ROPE, 2), np.arange(0, QK_ROPE, 2)])
    o1 = Q_LORA + KV_LORA
    w_kr = w_in_l0[:, o1:o1 + QK_ROPE]
    z64 = jnp.zeros((d, 128 - QK_ROPE), F32)
    w_in_ext = jnp.concatenate(
        [w_in_l0[:, :o1], w_kr, z64, w_kr[:, perm], z64, w_kr[:, perm_sw], z64, w_in_l0[:, o1 + QK_ROPE:]],
        axis=1).astype(BF16)
    assert w_in_ext.shape[1] == IN_EXT
    wq3 = w_uq_l0.reshape(Q_LORA, MLA_HEADS, QK_NOPE + QK_ROPE)
    w_uq_ext = jnp.concatenate(
        [wq3[:, :, :QK_NOPE].reshape(Q_LORA, -1),
         wq3[:, :, QK_NOPE:][:, :, perm].reshape(Q_LORA, -1),
         wq3[:, :, QK_NOPE:][:, :, perm_sw].reshape(Q_LORA, -1)], axis=1).astype(BF16)
    ck_tab, sk_tab = _rope_tables(TM, s_s)

    def rope_blk(i):
        return jnp.where(i < npt, 0, 1 + (i - npt) % tps)

    tok = lambda w: pl.BlockSpec((TM, w), lambda i: (i, 0))
    full = lambda a: pl.BlockSpec(a.shape, lambda *_: (0,) * a.ndim)
    q, ckv, kr, krr, ux, ug = pl.pallas_call(
        _inproj_kernel,
        grid=(ntile,),
        in_specs=[tok(d), mod_spec, full(row(g_mix_l0)), full(w_in_ext), full(row(g_q_l0)), full(w_uq_ext),
                  full(row(g_kv_l0)),
                  pl.BlockSpec((TM, QK_ROPE), lambda i: (rope_blk(i), 0)),
                  pl.BlockSpec((TM, QK_ROPE), lambda i: (rope_blk(i), 0))],
        out_specs=[pl.BlockSpec((MLA_HEADS, TM, QK_PAD), lambda i: (0, i, 0)),
                   tok(KV_LORA), tok(QK_ROPE), tok(QK_ROPE), tok(LRU_WIDTH), tok(LRU_WIDTH)],
        out_shape=[jax.ShapeDtypeStruct((MLA_HEADS, t_all, QK_PAD), BF16),
                   jax.ShapeDtypeStruct((t_all, KV_LORA), F32),
                   jax.ShapeDtypeStruct((t_all, QK_ROPE), F32),
                   jax.ShapeDtypeStruct((t_all, QK_ROPE), BF16),
                   jax.ShapeDtypeStruct((t_all, LRU_WIDTH), F32),
                   jax.ShapeDtypeStruct((t_all, LRU_WIDTH), F32)],
        compiler_params=_cparams(("arbitrary",)),
    )(x0, mod0, row(g_mix_l0), w_in_ext, row(g_q_l0), w_uq_ext, row(g_kv_l0), ck_tab, sk_tab)

    wkv3 = w_ukv_l0.reshape(KV_LORA, MLA_HEADS, QK_NOPE + V_HEAD)
    w_ukv_ext = jnp.concatenate([wkv3[:, :, :QK_NOPE].reshape(KV_LORA, -1),
                                 wkv3[:, :, QK_NOPE:].reshape(KV_LORA, -1)], axis=1).astype(BF16)

    def attn_call(nb, s_new, tile0, has_cache):
        nq = s_new // TM
        blk0 = tile0 * TM // s_new
        n_c = n_cache if has_cache else 0
        in_specs = [pl.BlockSpec((MLA_HEADS, TM, QK_PAD), lambda b, qi: (0, tile0 + b * nq + qi, 0)),
                    pl.BlockSpec((s_new, KV_LORA), lambda b, qi: (blk0 + b, 0)),
                    pl.BlockSpec((s_new, QK_ROPE), lambda b, qi: (blk0 + b, 0))]
        args = [q, ckv, krr]
        if has_cache:
            in_specs += [pl.BlockSpec((1, n_cache, KV_LORA), lambda b, qi: (b, 0, 0)),
                         pl.BlockSpec((1, n_cache, QK_ROPE), lambda b, qi: (b, 0, 0))]
            args += [cache_ckv_l0, cache_krope_l0[:, :, perm]]
        in_specs.append(pl.BlockSpec(w_ukv_ext.shape, lambda b, qi: (0, 0)))
        args.append(w_ukv_ext)
        return pl.pallas_call(
            functools.partial(_attn_kernel, has_cache=has_cache, s_new=s_new, n_cache=n_c),
            grid=(nb, nq),
            in_specs=in_specs,
            out_specs=pl.BlockSpec((TM, MLA_WIDTH), lambda b, qi: (b * nq + qi, 0)),
            out_shape=jax.ShapeDtypeStruct((nb * s_new, MLA_WIDTH), BF16),
            scratch_shapes=[pltpu.VMEM((MLA_HEADS, n_c + s_new, QK_PAD), BF16),
                            pltpu.VMEM((n_c + s_new, MLA_WIDTH), BF16)],
            compiler_params=_cparams(("arbitrary", "arbitrary")),
        )(*args)

    attn = jnp.concatenate([attn_call(nb_p, s_p, 0, False), attn_call(nb_s, s_s, npt, True)], axis=0)

    def lru_call(nb, s_new, tile0, h0):
        nc = s_new // TM
        r8 = TM // SUBLANES
        last8 = t_all // SUBLANES - 1

        def cur(rev):
            return pl.BlockSpec((TM, LRU_WIDTH),
                                lambda b, cc: (tile0 + b * nc + (nc - 1 - cc if rev else cc), 0))

        def prev(rev):
            return pl.BlockSpec((SUBLANES, LRU_WIDTH), lambda b, cc: (
                jnp.maximum((tile0 + b * nc + (nc - 1 - cc if rev else cc)) * r8 - 1, 0), 0))

        def nxt(rev):
            return pl.BlockSpec((SUBLANES, LRU_WIDTH), lambda b, cc: (
                jnp.minimum((tile0 + b * nc + (nc - 1 - cc if rev else cc) + 1) * r8, last8), 0))

        small = [conv_w_l0, row(conv_b_l0), w_rg_l0, b_rg_l0, w_ig_l0, b_ig_l0, lam_l0]
        return pl.pallas_call(
            functools.partial(_lru_kernel, nc=nc),
            grid=(nb, nc),
            in_specs=[prev(False), cur(False), nxt(False), prev(True), cur(True), nxt(True)]
                     + [full(a) for a in small]
                     + [pl.BlockSpec((1, 2, LRU_WIDTH), lambda b, cc: (b, 0, 0))],
            out_specs=[pl.BlockSpec((TM, LRU_WIDTH), lambda b, cc: (b * nc + cc, 0)),
                       pl.BlockSpec((TM, LRU_WIDTH), lambda b, cc: (b * nc + nc - 1 - cc, 0)),
                       pl.BlockSpec((1, 2, LRU_WIDTH), lambda b, cc: (b, 0, 0))],
            out_shape=[jax.ShapeDtypeStruct((nb * s_new, LRU_WIDTH), F32),
                       jax.ShapeDtypeStruct((nb * s_new, LRU_WIDTH), F32),
                       jax.ShapeDtypeStruct((nb, 2, LRU_WIDTH), F32)],
            scratch_shapes=[pltpu.VMEM((1, LRU_WIDTH), F32), pltpu.VMEM((1, LRU_WIDTH), F32)],
            compiler_params=_cparams(("arbitrary", "arbitrary")),
        )(ux, ux, ux, ux, ux, ux, *small, h0)

    hf_p, hb_p, new_lru = lru_call(nb_p, s_p, 0, jnp.zeros((nb_p, 2, LRU_WIDTH), F32))
    hf_s, hb_s, _ = lru_call(nb_s, s_s, npt, state_lru_l0.astype(F32))
    hf = jnp.concatenate([hf_p, hf_s], axis=0)
    hb = jnp.concatenate([hb_p, hb_s], axis=0)

    w_o = w_o_l0.astype(BF16)
    x1 = pl.pallas_call(
        _oproj_kernel,
        grid=(ntile,),
        in_specs=[tok(d), mod_spec, tok(MLA_WIDTH), tok(LRU_WIDTH), tok(LRU_WIDTH), tok(LRU_WIDTH),
                  pl.BlockSpec((MLA_WIDTH, d), lambda i: (0, 0)), pl.BlockSpec((LRU_WIDTH, d), lambda i: (1, 0))],
        out_specs=tok(d),
        out_shape=jax.ShapeDtypeStruct((t_all, d), F32),
        compiler_params=_cparams(("arbitrary",)),
    )(x0, mod0, attn, hf, hb, ug, w_o, w_o)

    def peer_call(x, mod, g_ffn, w_q, sub_keys, u, v, final_norm):
        n_exp = u.shape[0]
        assert n_exp == N_KEYS * N_KEYS and n_exp % PEER_EBLK == 0
        wqt = w_q.T.astype(BF16)
        keys = sub_keys.astype(BF16)
        ub = u.astype(BF16)
        vt = v.T.astype(BF16)
        tpp = PEER_TM // TM

        def cond_row_p(i):
            return cond_row(i * tpp)

        big = lambda dt: pltpu.VMEM((PEER_HEADS, N_KEYS, PEER_TM), dt)
        return pl.pallas_call(
            functools.partial(_peer_kernel, final_norm=final_norm),
            grid=(t_all // PEER_TM, n_exp // PEER_EBLK),
            in_specs=[pl.BlockSpec((PEER_TM, d), lambda i, k: (i, 0)),
                      pl.BlockSpec((1, 6, d), lambda i, k: (cond_row_p(i), 0, 0)),
                      pl.BlockSpec((1, d), lambda i, k: (0, 0)),
                      pl.BlockSpec(wqt.shape, lambda i, k: (0, 0)),
                      pl.BlockSpec(keys.shape, lambda i, k: (0, 0, 0, 0)),
                      pl.BlockSpec((PEER_EBLK, d), lambda i, k: (k, 0)),
                      pl.BlockSpec((d, PEER_EBLK), lambda i, k: (0, k)),
                      pl.BlockSpec((1, d), lambda i, k: (0, 0))],
            out_specs=pl.BlockSpec((PEER_TM, d), lambda i, k: (i, 0)),
            out_shape=jax.ShapeDtypeStruct((t_all, d), F32),
            scratch_shapes=[pltpu.VMEM((d, PEER_TM), BF16),
                            pltpu.VMEM((d, PEER_TM), F32),
                            pltpu.VMEM((2, PEER_HEADS, N_KEYS, PEER_TM), F32),
                            pltpu.VMEM((2, PEER_TOPK, PEER_HEADS, PEER_TM), F32),
                            pltpu.VMEM((PEER_TOPK + 1, PEER_HEADS, PEER_TM), F32),
                            pltpu.VMEM((2, N_KEYS, PEER_TM), F32),
                            big(F32), big(F32), big(BF16), big(BF16)],
            compiler_params=_cparams(("arbitrary", "arbitrary")),
        )(x, mod, row(g_ffn), wqt, keys, ub, vt, row(g_final))

    x2 = peer_call(x1, mod0, g_ffn_l0, peer_wq_l0, peer_keys_l0, peer_u_l0, peer_v_l0, False)

    rh = TM // HALO
    lasth = t_all // HALO - 1
    x3 = pl.pallas_call(
        functools.partial(_pool_kernel, tiles_per_seq=tps, n_prompt_tiles=npt),
        grid=(ntile,),
        in_specs=[pl.BlockSpec((HALO, d), lambda i: (jnp.maximum(i * rh - 1, 0), 0)),
                  tok(d),
                  pl.BlockSpec((HALO, d), lambda i: (jnp.minimum((i + 1) * rh, lasth), 0)),
                  mod_spec, full(row(g_mix_l1)),
                  pl.BlockSpec(w_pool_l1.shape, lambda i: (0, 0, 0)), full(row(s_pool_l1))],
        out_specs=tok(d),
        out_shape=jax.ShapeDtypeStruct((t_all, d), F32),
        compiler_params=_cparams(("arbitrary",)),
    )(x2, x2, x2, mod1, row(g_mix_l1), w_pool_l1.astype(BF16), row(s_pool_l1))

    y = peer_call(x3, mod1, g_ffn_l1, peer_wq_l1, peer_keys_l1, peer_u_l1, peer_v_l1, True)

    y_prompt = y[:t_p].reshape(nb_p, s_p, d)
    y_sample = y[t_p:].reshape(nb_s, s_s, d)
    new_ckv = ckv[:t_p].reshape(nb_p, s_p, KV_LORA)
    new_krope = kr[:t_p].reshape(nb_p, s_p, QK_ROPE)
    return (y_prompt, y_sample, new_ckv, new_krope, new_lru)
```

```python
import functools

import numpy as np
import jax
import jax.numpy as jnp
from jax import lax
from jax.experimental import pallas as pl
from jax.experimental.pallas import tpu as pltpu

F32 = jnp.float32
BF16 = jnp.bfloat16

D_MODEL = 1024
EPS = 1e-6
GRID_W = 64
MLA_HEADS = 4
Q_LORA = 384
KV_LORA = 256
QK_NOPE = 128
QK_ROPE = 64
V_HEAD = 128
MLA_WIDTH = MLA_HEADS * V_HEAD
ROPE_BASE = 10000.0
LRU_WIDTH = 512
LRU_BLOCKS = 4
LRU_BLOCK = LRU_WIDTH // LRU_BLOCKS
CONV_W = 4
CONV_LEFT = 2
LRU_C = 8.0
POOL_WINDOWS = (2, 4, 8, 16)
POOL_GROUP = D_MODEL // len(POOL_WINDOWS)
PEER_HEADS = 8
N_KEYS = 128
PEER_DKEY = 256
PEER_TOPK = 16

SUBLANES = 8
LANES = 128
VMEM_LIMIT = 56 * 1024 * 1024

TM = 256
QK_PAD = 256
HALO = 16
PEER_TM = 512
PEER_EBLK = 2048
PEER_SUB = 256
BF16_ROWS = 16
IN_EXT = 2048


def _rms(x, g):
    return x * lax.rsqrt(jnp.mean(x * x, axis=-1, keepdims=True) + EPS) * g


def _cparams(sem):
    return pltpu.CompilerParams(dimension_semantics=sem, vmem_limit_bytes=VMEM_LIMIT)


def _ada_kernel(c_ref, w_ref, b_ref, o_ref):
    c = c_ref[...]
    o_ref[...] = jnp.dot(c * jax.nn.sigmoid(c), w_ref[...], preferred_element_type=F32) + b_ref[...]


def _ada(cpad, w_mod, b_mod):
    n = w_mod.shape[1]
    bn = 768
    return pl.pallas_call(
        _ada_kernel,
        grid=(n // bn,),
        in_specs=[pl.BlockSpec(cpad.shape, lambda j: (0, 0)),
                  pl.BlockSpec((D_MODEL, bn), lambda j: (0, j)),
                  pl.BlockSpec((1, bn), lambda j: (0, j))],
        out_specs=pl.BlockSpec((cpad.shape[0], bn), lambda j: (0, j)),
        out_shape=jax.ShapeDtypeStruct((cpad.shape[0], n), F32),
        compiler_params=_cparams(("arbitrary",)),
    )(cpad, w_mod, b_mod.reshape(1, n))


def _inproj_kernel(x_ref, mod_ref, g_ref, win_ref, gq_ref, wuq_ref, gkv_ref, ck_ref, sk_ref,
                   q_ref, ckv_ref, kr_ref, krr_ref, ux_ref, ug_ref):
    mod = mod_ref[0]
    h = _rms(x_ref[...], g_ref[...]) * (1.0 + mod[1:2]) + mod[0:1]
    y = jnp.dot(h.astype(BF16), win_ref[...], preferred_element_type=F32)
    cq = y[:, 0:Q_LORA]
    ckv = y[:, Q_LORA:Q_LORA + KV_LORA]
    o = Q_LORA + KV_LORA
    kr = y[:, o:o + QK_ROPE]
    krp = y[:, o + 128:o + 128 + QK_ROPE]
    krs = y[:, o + 256:o + 256 + QK_ROPE]
    ux_ref[...] = y[:, o + 384:o + 384 + LRU_WIDTH]
    ug_ref[...] = y[:, o + 384 + LRU_WIDTH:o + 384 + 2 * LRU_WIDTH]
    ckv_ref[...] = _rms(ckv, gkv_ref[...])
    kr_ref[...] = kr
    ck = ck_ref[...]
    sk = sk_ref[...]
    krr_ref[...] = (krp * ck + krs * sk).astype(BF16)
    q = jnp.dot(_rms(cq, gq_ref[...]).astype(BF16), wuq_ref[...], preferred_element_type=F32)
    nw = MLA_HEADS * QK_NOPE
    rw = MLA_HEADS * QK_ROPE
    for hd in range(MLA_HEADS):
        qp = q[:, nw + hd * QK_ROPE:nw + (hd + 1) * QK_ROPE]
        qs = q[:, nw + rw + hd * QK_ROPE:nw + rw + (hd + 1) * QK_ROPE]
        q_ref[hd, :, 0:QK_NOPE] = q[:, hd * QK_NOPE:(hd + 1) * QK_NOPE].astype(BF16)
        q_ref[hd, :, QK_NOPE:QK_NOPE + QK_ROPE] = (qp * ck + qs * sk).astype(BF16)
        q_ref[hd, :, QK_NOPE + QK_ROPE:QK_PAD] = jnp.zeros((TM, QK_PAD - QK_NOPE - QK_ROPE), BF16)


def _attn_kernel(*refs, has_cache, s_new, n_cache):
    if has_cache:
        q_ref, ckv_ref, krr_ref, cckv_ref, ckr_ref, wukv_ref, o_ref, kcat_ref, vv_ref = refs
    else:
        q_ref, ckv_ref, krr_ref, wukv_ref, o_ref, kcat_ref, vv_ref = refs
    sk = n_cache + s_new
    kw = MLA_HEADS * QK_NOPE
    zpad = jnp.zeros((TM, QK_PAD - QK_NOPE - QK_ROPE), BF16)

    def put_keys(row0, ckv_rows, kr_rows):
        kv = jnp.dot(ckv_rows.astype(BF16), wukv_ref[...], preferred_element_type=F32)
        rows = pl.ds(row0, TM)
        for hd in range(MLA_HEADS):
            kcat_ref[hd, rows, 0:QK_NOPE] = kv[:, hd * QK_NOPE:(hd + 1) * QK_NOPE].astype(BF16)
            kcat_ref[hd, rows, QK_NOPE:QK_NOPE + QK_ROPE] = kr_rows.astype(BF16)
            kcat_ref[hd, rows, QK_NOPE + QK_ROPE:QK_PAD] = zpad
        vv_ref[rows, :] = kv[:, kw:].astype(BF16)

    @pl.when(pl.program_id(1) == 0)
    def _():
        if has_cache:
            for c in range(n_cache // TM):
                put_keys(c * TM, cckv_ref[0, c * TM:(c + 1) * TM, :], ckr_ref[0, c * TM:(c + 1) * TM, :])

        def body(c, carry):
            r0 = pl.multiple_of(c * TM, TM)
            put_keys(n_cache + r0, ckv_ref[pl.ds(r0, TM), :], krr_ref[pl.ds(r0, TM), :])
            return carry
        lax.fori_loop(0, s_new // TM, body, 0)

    scale = (QK_NOPE + QK_ROPE) ** -0.5
    for hd in range(MLA_HEADS):
        s = lax.dot_general(q_ref[hd], kcat_ref[hd], (((1,), (1,)), ((), ())),
                            preferred_element_type=F32) * scale
        m = jnp.max(s, axis=-1, keepdims=True)
        e = jnp.exp(s - m)
        l = jnp.sum(e, axis=-1, keepdims=True)
        o = jnp.dot(e.astype(BF16), vv_ref[:, hd * V_HEAD:(hd + 1) * V_HEAD], preferred_element_type=F32)
        o_ref[:, hd * V_HEAD:(hd + 1) * V_HEAD] = (o / l).astype(BF16)
    del sk


def _lru_dir(xp_ref, xc_ref, xn_ref, valid_prev, valid_next, d, reverse, cw_ref, cb_ref,
             wr_ref, br_ref, wi_ref, bi_ref, lam_ref, carry):
    xp = jnp.where(valid_prev, xp_ref[...], 0.0)
    xn = jnp.where(valid_next, xn_ref[...], 0.0)
    xx = jnp.concatenate([xp, xc_ref[...], xn], axis=0)
    n = TM + 2 * SUBLANES
    xc = cb_ref[...]
    for k in range(CONV_W):
        sh = (CONV_LEFT - k) % n
        xs = xx if sh == 0 else pltpu.roll(xx, sh, 0)
        xc = xc + xs[SUBLANES:SUBLANES + TM] * cw_ref[k:k + 1, :]
    rs, is_ = [], []
    for b in range(LRU_BLOCKS):
        xb = xc[:, b * LRU_BLOCK:(b + 1) * LRU_BLOCK]
        rs.append(jnp.dot(xb, wr_ref[d, b], preferred_element_type=F32))
        is_.append(jnp.dot(xb, wi_ref[d, b], preferred_element_type=F32))
    r = jax.nn.sigmoid(jnp.concatenate(rs, axis=-1) + br_ref[d:d + 1, :])
    i = jax.nn.sigmoid(jnp.concatenate(is_, axis=-1) + bi_ref[d:d + 1, :])
    nl = -lam_ref[d:d + 1, :]
    softplus = jnp.maximum(nl, 0.0) + jnp.log1p(jnp.exp(-jnp.abs(nl)))
    log_a = -LRU_C * r * softplus
    a = jnp.exp(log_a)
    bx = jnp.sqrt(jnp.tanh(-log_a) * (a * a + 1.0)) * (i * xc)
    t = lax.broadcasted_iota(jnp.int32, (TM, 1), 0)
    step = 1
    while step < TM:
        if reverse:
            keep = t < TM - step
            sh = TM - step
        else:
            keep = t >= step
            sh = step
        a_s = jnp.where(keep, pltpu.roll(a, sh, 0), 1.0)
        b_s = jnp.where(keep, pltpu.roll(bx, sh, 0), 0.0)
        bx = a * b_s + bx
        a = a * a_s
        step *= 2
    return a * carry + bx


def _lru_kernel(fxp, fxc, fxn, bxp, bxc, bxn, cw_ref, cb_ref, wr_ref, br_ref, wi_ref, bi_ref, lam_ref,
                h0_ref, hf_ref, hb_ref, st_ref, cf_ref, cbk_ref, *, nc):
    c = pl.program_id(1)

    @pl.when(c == 0)
    def _():
        cf_ref[...] = h0_ref[0, 0:1, :]
        cbk_ref[...] = h0_ref[0, 1:2, :]

    params = (cw_ref, cb_ref, wr_ref, br_ref, wi_ref, bi_ref, lam_ref)
    hf = _lru_dir(fxp, fxc, fxn, c > 0, c < nc - 1, 0, False, *params, cf_ref[...])
    hf_ref[...] = hf
    cf_ref[...] = hf[TM - 1:TM, :]
    hb = _lru_dir(bxp, bxc, bxn, c < nc - 1, c > 0, 1, True, *params, cbk_ref[...])
    hb_ref[...] = hb
    cbk_ref[...] = hb[0:1, :]
    st_ref[0, 0:1, :] = hf[TM - 1:TM, :]
    st_ref[0, 1:2, :] = hb[0:1, :]


def _oproj_kernel(x_ref, mod_ref, at_ref, hf_ref, hb_ref, ug_ref, woa_ref, wor_ref, o_ref):
    mod = mod_ref[0]
    rec = ((hf_ref[...] + hb_ref[...]) * jax.nn.gelu(ug_ref[...])).astype(BF16)
    out = (jnp.dot(at_ref[...], woa_ref[...], preferred_element_type=F32)
           + jnp.dot(rec, wor_ref[...], preferred_element_type=F32))
    o_ref[...] = x_ref[...] + mod[2:3] * out


def _pool_kernel(xp_ref, xc_ref, xn_ref, mod_ref, g_ref, wp_ref, sp_ref, o_ref, *, tiles_per_seq, n_prompt_tiles):
    i = pl.program_id(0)
    j = jnp.where(i < n_prompt_tiles, 0, (i - n_prompt_tiles) % tiles_per_seq)
    ntile = jnp.where(i < n_prompt_tiles, 1, tiles_per_seq)
    mod = mod_ref[0]
    g = g_ref[...]

    def hmod(x):
        return _rms(x, g) * (1.0 + mod[1:2]) + mod[0:1]

    x = xc_ref[...]
    hc = hmod(x)
    hp = jnp.where(j > 0, hmod(xp_ref[...]), 0.0)
    hn = jnp.where(j < ntile - 1, hmod(xn_ref[...]), 0.0)
    hh = jnp.concatenate([hp, hc, hn], axis=0)
    n = TM + 2 * HALO
    seq_len = ntile * TM
    t = j * TM + lax.broadcasted_iota(jnp.int32, (TM, 1), 0)
    ys = []
    for gi, w in enumerate(POOL_WINDOWS):
        cols = slice(gi * POOL_GROUP, (gi + 1) * POOL_GROUP)
        p = hh[:, cols]
        p = p + pltpu.roll(p, 1, 0)
        half = 1
        while 2 * half < w:
            p = pltpu.roll(p, half, 0) + pltpu.roll(p, n - half, 0)
            half *= 2
        lo = jnp.maximum(t - w // 2, 0)
        hi = jnp.minimum(t + (w - w // 2), seq_len)
        mean = p[HALO:HALO + TM] / (hi - lo).astype(F32)
        dg = (mean - hc[:, cols]).astype(BF16)
        ys.append(jnp.dot(dg, wp_ref[gi], preferred_element_type=F32))
    y = jnp.concatenate(ys, axis=-1) * sp_ref[...]
    o_ref[...] = x + mod[2:3] * y


def _sort_pairs(n):
    pairs = []

    def merge(lo, cnt, r):
        step = r * 2
        if step < cnt:
            merge(lo, cnt, step)
            merge(lo + r, cnt, step)
            for i in range(lo + r, lo + cnt - r, step):
                pairs.append((i, i + r))
        else:
            pairs.append((lo, lo + r))

    def sort(lo, cnt):
        if cnt > 1:
            m = cnt // 2
            sort(lo, m)
            sort(lo + m, m)
            merge(lo, cnt, 1)

    sort(0, n)
    return pairs


_SORT16 = _sort_pairs(PEER_TOPK)
_HYPER = [(a, b) for a in range(PEER_TOPK) for b in range(PEER_TOPK) if (a + 1) * (b + 1) <= PEER_TOPK]


def _top16_sorted(s):
    k = PEER_TOPK
    x = [s[SUBLANES * r:SUBLANES * (r + 1), :] for r in range(N_KEYS // SUBLANES)]
    for (i, j) in _SORT16:
        hi = jnp.maximum(x[i], x[j])
        lo = jnp.minimum(x[i], x[j])
        x[i], x[j] = hi, lo
    for shift in (4, 2, 1):
        y = [jnp.maximum(x[r], pltpu.roll(x[k - 1 - r], shift, 0)) for r in range(k)]
        stride = k // 2
        while stride >= 1:
            for i in range(k):
                if i & stride == 0:
                    hi = jnp.maximum(y[i], y[i + stride])
                    lo = jnp.minimum(y[i], y[i + stride])
                    y[i], y[i + stride] = hi, lo
            stride //= 2
        x = y
    return x


def _pair_counts(sv1, sv2):
    one = jnp.ones_like(sv1[0])
    zero = jnp.zeros_like(sv1[0])
    cand = [sv1[a] + sv2[b] for (a, b) in _HYPER]

    def ordered(i, j):
        (ai, bi), (aj, bj) = _HYPER[i], _HYPER[j]
        return (aj <= ai and bj <= bi) or (ai <= aj and bi <= bj)

    nh = len(_HYPER)
    cnt = [float((a + 1) * (b + 1) - 1 + sum(1 for j in range(i + 1, nh) if not ordered(i, j))) * one
           for i, (a, b) in enumerate(_HYPER)]
    for i in range(nh):
        for jx in range(i):
            if ordered(i, jx):
                continue
            ge = jnp.where(cand[jx] >= cand[i], one, zero)
            cnt[i] = cnt[i] + ge
            cnt[jx] = cnt[jx] - ge
    e1 = [jnp.exp(sv1[a] - sv1[0]) for a in range(PEER_TOPK)]
    e2 = [jnp.exp(sv2[b] - sv2[0]) for b in range(PEER_TOPK)]
    n = [zero for _ in range(PEER_TOPK)]
    z = zero
    for i, (a, b) in enumerate(_HYPER):
        sel = jnp.where(cnt[i] < float(PEER_TOPK), one, zero)
        n[a] = n[a] + sel
        z = z + sel * (e1[a] * e2[b])
    return n, 1.0 / z


def _head_tables(hd, sc_ref, sv_ref, nz_ref, ex_ref, cnt1_ref, e1n_ref, rank2_ref, e2_ref, exact):
    k = PEER_TOPK
    tm = sc_ref.shape[-1]
    nslab = N_KEYS // SUBLANES
    bad = []
    for g in range(tm // LANES):
        lanes = slice(g * LANES, (g + 1) * LANES)

        def bc(ref, *idx):
            return jnp.broadcast_to(ref[idx + (slice(hd, hd + 1), lanes)], (SUBLANES, LANES))

        sv1b = [bc(sv_ref, 0, a) for a in range(k)]
        nb = [bc(nz_ref, a) for a in range(k)]
        izb = bc(nz_ref, k)
        c1 = jnp.zeros((SUBLANES, LANES), F32)
        for r in range(nslab):
            rows = slice(r * SUBLANES, (r + 1) * SUBLANES)
            s1 = sc_ref[0, hd, rows, lanes]
            cnt1 = jnp.zeros_like(s1)
            if exact:
                rank1 = jnp.zeros_like(s1)
                for a in range(k):
                    rank1 = jnp.where(sv1b[a] > s1, float(a + 1), rank1)
                rank1 = rank1 + ex_ref[0, rows, lanes]
                for a in range(k):
                    cnt1 = jnp.where(rank1 == float(a), nb[a], cnt1)
            else:
                for a in range(k):
                    cnt1 = jnp.where(s1 == sv1b[a], nb[a], cnt1)
                c1 = c1 + jnp.where(s1 >= sv1b[k - 1], 1.0, 0.0)
            cnt1_ref[hd, rows, lanes] = cnt1
            e1n_ref[hd, rows, lanes] = jnp.exp(s1 - sv1b[0]) * izb
        sv2b = [bc(sv_ref, 1, a) for a in range(k)]
        c2 = jnp.zeros((SUBLANES, LANES), F32)
        for r in range(N_KEYS // BF16_ROWS):
            rk, e2 = [], []
            for q in range(BF16_ROWS // SUBLANES):
                r0 = r * BF16_ROWS + q * SUBLANES
                rows = slice(r0, r0 + SUBLANES)
                s2 = sc_ref[1, hd, rows, lanes]
                rank2 = jnp.zeros_like(s2)
                for a in range(k):
                    rank2 = jnp.where(sv2b[a] > s2, float(a + 1), rank2)
                if exact:
                    rank2 = jnp.minimum(rank2 + ex_ref[1, rows, lanes], float(k))
                else:
                    c2 = c2 + jnp.where(rank2 < float(k), 1.0, 0.0)
                rk.append(rank2)
                e2.append(jnp.exp(s2 - sv2b[0]))
            rows16 = slice(r * BF16_ROWS, (r + 1) * BF16_ROWS)
            rank2_ref[hd, rows16, lanes] = jnp.concatenate(rk, axis=0).astype(BF16)
            e2_ref[hd, rows16, lanes] = jnp.concatenate(e2, axis=0).astype(BF16)
        if not exact:
            n1 = jnp.sum(c1, axis=0, keepdims=True)
            n2 = jnp.sum(c2, axis=0, keepdims=True)
            b = jnp.where(n1 != float(k), 1.0, 0.0) + jnp.where(n2 != float(k), 1.0, 0.0)
            for svb in (sv1b, sv2b):
                for a in range(k - 1):
                    b = b + jnp.where(svb[a][0:1] == svb[a + 1][0:1], 1.0, 0.0)
            bad.append(b)
    return jnp.concatenate(bad, axis=1) if bad else None


def _tie_offsets(hd, sc_ref, ex_ref):
    tm = sc_ref.shape[-1]
    nidx = lax.broadcasted_iota(jnp.int32, (N_KEYS, tm), 0)
    for p in range(2):
        s = sc_ref[p, hd]

        def body(m, e, s=s, p=p):
            row = sc_ref[p, hd, pl.ds(m, 1), :]
            return e + jnp.where((s == row) & (nidx > m), 1.0, 0.0)

        ex_ref[p] = lax.fori_loop(0, N_KEYS, body, jnp.zeros((N_KEYS, tm), F32))


def _peer_kernel(x_ref, mod_ref, g_ref, wqt_ref, keys_ref, u_ref, vt_ref, gfin_ref, o_ref,
                 h2t_ref, acc_ref, sc_ref, sv_ref, nz_ref, ex_ref, cnt1_ref, e1n_ref, rank2_ref, e2_ref,
                 *, final_norm):
    k = pl.program_id(1)
    tm = PEER_TM
    half = PEER_DKEY // 2

    @pl.when(k == 0)
    def _():
        mod = mod_ref[0]
        h2 = _rms(x_ref[...], g_ref[...]) * (1.0 + mod[4:5]) + mod[3:4]
        h2t_ref[...] = h2.T.astype(BF16)
        acc_ref[...] = jnp.zeros_like(acc_ref)
        qt = jnp.dot(wqt_ref[...], h2t_ref[...], preferred_element_type=F32).astype(BF16)
        for hd in range(PEER_HEADS):
            for p in range(2):
                r0 = (hd * 2 + p) * half
                sc_ref[p, hd] = jnp.dot(keys_ref[hd, p], qt[r0:r0 + half, :],
                                        preferred_element_type=F32)
                for g in range(tm // LANES):
                    lanes = slice(g * LANES, (g + 1) * LANES)
                    top = _top16_sorted(sc_ref[p, hd, :, lanes])
                    for a in range(PEER_TOPK):
                        sv_ref[p, a, hd:hd + 1, lanes] = top[a][0:1, :]
        for g in range(tm // LANES):
            lanes = slice(g * LANES, (g + 1) * LANES)
            sv1 = [sv_ref[0, a, :, lanes] for a in range(PEER_TOPK)]
            sv2 = [sv_ref[1, a, :, lanes] for a in range(PEER_TOPK)]
            n, inv_z = _pair_counts(sv1, sv2)
            for a in range(PEER_TOPK):
                nz_ref[a, :, lanes] = n[a]
            nz_ref[PEER_TOPK, :, lanes] = inv_z
        tabs = (sc_ref, sv_ref, nz_ref, ex_ref, cnt1_ref, e1n_ref, rank2_ref, e2_ref)
        for hd in range(PEER_HEADS):
            bad = _head_tables(hd, *tabs, exact=False)

            @pl.when(jnp.max(bad) > 0.0)
            def _(hd=hd):
                _tie_offsets(hd, sc_ref, ex_ref)
                _head_tables(hd, *tabs, exact=True)

    zero = jnp.zeros((BF16_ROWS, tm), BF16)
    npiece = PEER_EBLK // N_KEYS
    per_sub = PEER_SUB // N_KEYS

    def up_dot(p):
        return jnp.dot(u_ref[p * N_KEYS:(p + 1) * N_KEYS, :], h2t_ref[...], preferred_element_type=F32)

    raw = up_dot(0)
    ys = []
    for p in range(npiece):
        nxt = up_dot(p + 1) if p + 1 < npiece else None
        i1 = k * npiece + p
        cbs = [jnp.broadcast_to(cnt1_ref[hd, pl.ds(i1, 1), :], (BF16_ROWS, tm)).astype(BF16)
               for hd in range(PEER_HEADS)]
        ebs = [jnp.broadcast_to(e1n_ref[hd, pl.ds(i1, 1), :], (BF16_ROWS, tm)).astype(BF16)
               for hd in range(PEER_HEADS)]
        act = jax.nn.gelu(raw).astype(BF16)
        for r in range(N_KEYS // BF16_ROWS):
            rows = slice(r * BF16_ROWS, (r + 1) * BF16_ROWS)
            w = None
            for hd in range(PEER_HEADS):
                t = jnp.where(rank2_ref[hd, rows, :] < cbs[hd], e2_ref[hd, rows, :], zero) * ebs[hd]
                w = t if w is None else w + t
            ys.append(act[rows, :] * w)
        raw = nxt
        if (p + 1) % per_sub == 0:
            e0 = (p + 1 - per_sub) * N_KEYS
            y = jnp.concatenate(ys, axis=0)
            ys = []
            acc_ref[...] += jnp.dot(vt_ref[:, e0:e0 + PEER_SUB], y, preferred_element_type=F32)

    @pl.when(k == pl.num_programs(1) - 1)
    def _():
        out = x_ref[...] + mod_ref[0][5:6] * acc_ref[...].T
        if final_norm:
            out = _rms(out, gfin_ref[...])
        o_ref[...] = out


def _rope_tables(s_prompt_tile, s_sample):
    n_rows = s_sample // GRID_W
    rows = jnp.repeat(jnp.arange(n_rows, dtype=F32), GRID_W)
    cols = jnp.tile(jnp.arange(GRID_W, dtype=F32), n_rows)
    axis_dim = QK_ROPE // 2
    inv_freq = ROPE_BASE ** (-jnp.arange(0, axis_dim, 2, dtype=F32) / axis_dim)
    ang = jnp.concatenate([rows[:, None] * inv_freq, cols[:, None] * inv_freq], axis=-1)
    cos, sin = jnp.cos(ang), jnp.sin(ang)
    ck = jnp.concatenate([cos, cos], axis=-1)
    sk = jnp.concatenate([-sin, sin], axis=-1)
    ident_c = jnp.ones((s_prompt_tile, QK_ROPE), F32)
    ident_s = jnp.zeros((s_prompt_tile, QK_ROPE), F32)
    return jnp.concatenate([ident_c, ck], axis=0), jnp.concatenate([ident_s, sk], axis=0)


def kernel(x_prompt, x_sample, cache_ckv_l0, cache_krope_l0, state_lru_l0, c, c_ctx, w_mod_l0, b_mod_l0, w_mod_l1, b_mod_l1, g_mix_l0, g_ffn_l0, g_mix_l1, g_ffn_l1, w_in_l0, g_q_l0, w_uq_l0, g_kv_l0, w_ukv_l0, conv_w_l0, conv_b_l0, w_rg_l0, b_rg_l0, w_ig_l0, b_ig_l0, lam_l0, w_o_l0, w_pool_l1, s_pool_l1, peer_wq_l0, peer_keys_l0, peer_u_l0, peer_v_l0, peer_wq_l1, peer_keys_l1, peer_u_l1, peer_v_l1, g_final):
    nb_p, s_p, d = x_prompt.shape
    nb_s, s_s, _ = x_sample.shape
    n_cache = cache_ckv_l0.shape[1]
    assert d == D_MODEL and s_p == TM and s_s % TM == 0 and n_cache % TM == 0
    t_p = nb_p * s_p
    t_s = nb_s * s_s
    t_all = t_p + t_s
    npt = t_p // TM
    tps = s_s // TM
    ntile = t_all // TM
    assert t_all % PEER_TM == 0 and t_p % PEER_TM == 0 and s_s % PEER_TM == 0

    x0 = jnp.concatenate([x_prompt.reshape(t_p, d), x_sample.reshape(t_s, d)], axis=0)

    ncond = 1 + nb_s
    cpad = jnp.zeros((2 * SUBLANES, d), F32).at[0].set(c_ctx).at[1:ncond].set(c)
    mod0 = _ada(cpad, w_mod_l0, b_mod_l0).reshape(2 * SUBLANES, 6, d)
    mod1 = _ada(cpad, w_mod_l1, b_mod_l1).reshape(2 * SUBLANES, 6, d)

    def cond_row(i):
        return jnp.where(i < npt, 0, 1 + (i - npt) // tps)

    mod_spec = pl.BlockSpec((1, 6, d), lambda i: (cond_row(i), 0, 0))
    row = lambda a: a.reshape(1, -1)

    perm = np.concatenate([np.arange(0, QK_ROPE, 2), np.arange(1, QK_ROPE, 2)])
    perm_sw = np.concatenate([np.arange(1, QK_ROPE, 2), np.arange(0, QK_ROPE, 2)])
    o1 = Q_LORA + KV_LORA
    w_kr = w_in_l0[:, o1:o1 + QK_ROPE]
    z64 = jnp.zeros((d, 128 - QK_ROPE), F32)
    w_in_ext = jnp.concatenate(
        [w_in_l0[:, :o1], w_kr, z64, w_kr[:, perm], z64, w_kr[:, perm_sw], z64, w_in_l0[:, o1 + QK_ROPE:]],
        axis=1).astype(BF16)
    assert w_in_ext.shape[1] == IN_EXT
    wq3 = w_uq_l0.reshape(Q_LORA, MLA_HEADS, QK_NOPE + QK_ROPE)
    w_uq_ext = jnp.concatenate(
        [wq3[:, :, :QK_NOPE].reshape(Q_LORA, -1),
         wq3[:, :, QK_NOPE:][:, :, perm].reshape(Q_LORA, -1),
         wq3[:, :, QK_NOPE:][:, :, perm_sw].reshape(Q_LORA, -1)], axis=1).astype(BF16)
    ck_tab, sk_tab = _rope_tables(TM, s_s)

    def rope_blk(i):
        return jnp.where(i < npt, 0, 1 + (i - npt) % tps)

    tok = lambda w: pl.BlockSpec((TM, w), lambda i: (i, 0))
    full = lambda a: pl.BlockSpec(a.shape, lambda *_: (0,) * a.ndim)
    q, ckv, kr, krr, ux, ug = pl.pallas_call(
        _inproj_kernel,
        grid=(ntile,),
        in_specs=[tok(d), mod_spec, full(row(g_mix_l0)), full(w_in_ext), full(row(g_q_l0)), full(w_uq_ext),
                  full(row(g_kv_l0)),
                  pl.BlockSpec((TM, QK_ROPE), lambda i: (rope_blk(i), 0)),
                  pl.BlockSpec((TM, QK_ROPE), lambda i: (rope_blk(i), 0))],
        out_specs=[pl.BlockSpec((MLA_HEADS, TM, QK_PAD), lambda i: (0, i, 0)),
                   tok(KV_LORA), tok(QK_ROPE), tok(QK_ROPE), tok(LRU_WIDTH), tok(LRU_WIDTH)],
        out_shape=[jax.ShapeDtypeStruct((MLA_HEADS, t_all, QK_PAD), BF16),
                   jax.ShapeDtypeStruct((t_all, KV_LORA), F32),
                   jax.ShapeDtypeStruct((t_all, QK_ROPE), F32),
                   jax.ShapeDtypeStruct((t_all, QK_ROPE), BF16),
                   jax.ShapeDtypeStruct((t_all, LRU_WIDTH), F32),
                   jax.ShapeDtypeStruct((t_all, LRU_WIDTH), F32)],
        compiler_params=_cparams(("arbitrary",)),
    )(x0, mod0, row(g_mix_l0), w_in_ext, row(g_q_l0), w_uq_ext, row(g_kv_l0), ck_tab, sk_tab)

    wkv3 = w_ukv_l0.reshape(KV_LORA, MLA_HEADS, QK_NOPE + V_HEAD)
    w_ukv_ext = jnp.concatenate([wkv3[:, :, :QK_NOPE].reshape(KV_LORA, -1),
                                 wkv3[:, :, QK_NOPE:].reshape(KV_LORA, -1)], axis=1).astype(BF16)

    def attn_call(nb, s_new, tile0, has_cache):
        nq = s_new // TM
        blk0 = tile0 * TM // s_new
        n_c = n_cache if has_cache else 0
        in_specs = [pl.BlockSpec((MLA_HEADS, TM, QK_PAD), lambda b, qi: (0, tile0 + b * nq + qi, 0)),
                    pl.BlockSpec((s_new, KV_LORA), lambda b, qi: (blk0 + b, 0)),
                    pl.BlockSpec((s_new, QK_ROPE), lambda b, qi: (blk0 + b, 0))]
        args = [q, ckv, krr]
        if has_cache:
            in_specs += [pl.BlockSpec((1, n_cache, KV_LORA), lambda b, qi: (b, 0, 0)),
                         pl.BlockSpec((1, n_cache, QK_ROPE), lambda b, qi: (b, 0, 0))]
            args += [cache_ckv_l0, cache_krope_l0[:, :, perm]]
        in_specs.append(pl.BlockSpec(w_ukv_ext.shape, lambda b, qi: (0, 0)))
        args.append(w_ukv_ext)
        return pl.pallas_call(
            functools.partial(_attn_kernel, has_cache=has_cache, s_new=s_new, n_cache=n_c),
            grid=(nb, nq),
            in_specs=in_specs,
            out_specs=pl.BlockSpec((TM, MLA_WIDTH), lambda b, qi: (b * nq + qi, 0)),
            out_shape=jax.ShapeDtypeStruct((nb * s_new, MLA_WIDTH), BF16),
            scratch_shapes=[pltpu.VMEM((MLA_HEADS, n_c + s_new, QK_PAD), BF16),
                            pltpu.VMEM((n_c + s_new, MLA_WIDTH), BF16)],
            compiler_params=_cparams(("arbitrary", "arbitrary")),
        )(*args)

    attn = jnp.concatenate([attn_call(nb_p, s_p, 0, False), attn_call(nb_s, s_s, npt, True)], axis=0)

    def lru_call(nb, s_new, tile0, h0):
        nc = s_new // TM
        r8 = TM // SUBLANES
        last8 = t_all // SUBLANES - 1

        def cur(rev):
            return pl.BlockSpec((TM, LRU_WIDTH),
                                lambda b, cc: (tile0 + b * nc + (nc - 1 - cc if rev else cc), 0))

        def prev(rev):
            return pl.BlockSpec((SUBLANES, LRU_WIDTH), lambda b, cc: (
                jnp.maximum((tile0 + b * nc + (nc - 1 - cc if rev else cc)) * r8 - 1, 0), 0))

        def nxt(rev):
            return pl.BlockSpec((SUBLANES, LRU_WIDTH), lambda b, cc: (
                jnp.minimum((tile0 + b * nc + (nc - 1 - cc if rev else cc) + 1) * r8, last8), 0))

        small = [conv_w_l0, row(conv_b_l0), w_rg_l0, b_rg_l0, w_ig_l0, b_ig_l0, lam_l0]
        return pl.pallas_call(
            functools.partial(_lru_kernel, nc=nc),
            grid=(nb, nc),
            in_specs=[prev(False), cur(False), nxt(False), prev(True), cur(True), nxt(True)]
                     + [full(a) for a in small]
                     + [pl.BlockSpec((1, 2, LRU_WIDTH), lambda b, cc: (b, 0, 0))],
            out_specs=[pl.BlockSpec((TM, LRU_WIDTH), lambda b, cc: (b * nc + cc, 0)),
                       pl.BlockSpec((TM, LRU_WIDTH), lambda b, cc: (b * nc + nc - 1 - cc, 0)),
                       pl.BlockSpec((1, 2, LRU_WIDTH), lambda b, cc: (b, 0, 0))],
            out_shape=[jax.ShapeDtypeStruct((nb * s_new, LRU_WIDTH), F32),
                       jax.ShapeDtypeStruct((nb * s_new, LRU_WIDTH), F32),
                       jax.ShapeDtypeStruct((nb, 2, LRU_WIDTH), F32)],
            scratch_shapes=[pltpu.VMEM((1, LRU_WIDTH), F32), pltpu.VMEM((1, LRU_WIDTH), F32)],
            compiler_params=_cparams(("arbitrary", "arbitrary")),
        )(ux, ux, ux, ux, ux, ux, *small, h0)

    hf_p, hb_p, new_lru = lru_call(nb_p, s_p, 0, jnp.zeros((nb_p, 2, LRU_WIDTH), F32))
    hf_s, hb_s, _ = lru_call(nb_s, s_s, npt, state_lru_l0.astype(F32))
    hf = jnp.concatenate([hf_p, hf_s], axis=0)
    hb = jnp.concatenate([hb_p, hb_s], axis=0)

    w_o = w_o_l0.astype(BF16)
    x1 = pl.pallas_call(
        _oproj_kernel,
        grid=(ntile,),
        in_specs=[tok(d), mod_spec, tok(MLA_WIDTH), tok(LRU_WIDTH), tok(LRU_WIDTH), tok(LRU_WIDTH),
                  pl.BlockSpec((MLA_WIDTH, d), lambda i: (0, 0)), pl.BlockSpec((LRU_WIDTH, d), lambda i: (1, 0))],
        out_specs=tok(d),
        out_shape=jax.ShapeDtypeStruct((t_all, d), F32),
        compiler_params=_cparams(("arbitrary",)),
    )(x0, mod0, attn, hf, hb, ug, w_o, w_o)

    def peer_call(x, mod, g_ffn, w_q, sub_keys, u, v, final_norm):
        n_exp = u.shape[0]
        assert n_exp == N_KEYS * N_KEYS and n_exp % PEER_EBLK == 0
        wqt = w_q.T.astype(BF16)
        keys = sub_keys.astype(BF16)
        ub = u.astype(BF16)
        vt = v.T.astype(BF16)
        tpp = PEER_TM // TM

        def cond_row_p(i):
            return cond_row(i * tpp)

        big = lambda dt: pltpu.VMEM((PEER_HEADS, N_KEYS, PEER_TM), dt)
        return pl.pallas_call(
            functools.partial(_peer_kernel, final_norm=final_norm),
            grid=(t_all // PEER_TM, n_exp // PEER_EBLK),
            in_specs=[pl.BlockSpec((PEER_TM, d), lambda i, k: (i, 0)),
                      pl.BlockSpec((1, 6, d), lambda i, k: (cond_row_p(i), 0, 0)),
                      pl.BlockSpec((1, d), lambda i, k: (0, 0)),
                      pl.BlockSpec(wqt.shape, lambda i, k: (0, 0)),
                      pl.BlockSpec(keys.shape, lambda i, k: (0, 0, 0, 0)),
                      pl.BlockSpec((PEER_EBLK, d), lambda i, k: (k, 0)),
                      pl.BlockSpec((d, PEER_EBLK), lambda i, k: (0, k)),
                      pl.BlockSpec((1, d), lambda i, k: (0, 0))],
            out_specs=pl.BlockSpec((PEER_TM, d), lambda i, k: (i, 0)),
            out_shape=jax.ShapeDtypeStruct((t_all, d), F32),
            scratch_shapes=[pltpu.VMEM((d, PEER_TM), BF16),
                            pltpu.VMEM((d, PEER_TM), F32),
                            pltpu.VMEM((2, PEER_HEADS, N_KEYS, PEER_TM), F32),
                            pltpu.VMEM((2, PEER_TOPK, PEER_HEADS, PEER_TM), F32),
                            pltpu.VMEM((PEER_TOPK + 1, PEER_HEADS, PEER_TM), F32),
                            pltpu.VMEM((2, N_KEYS, PEER_TM), F32),
                            big(F32), big(F32), big(BF16), big(BF16)],
            compiler_params=_cparams(("arbitrary", "arbitrary")),
        )(x, mod, row(g_ffn), wqt, keys, ub, vt, row(g_final))

    x2 = peer_call(x1, mod0, g_ffn_l0, peer_wq_l0, peer_keys_l0, peer_u_l0, peer_v_l0, False)

    rh = TM // HALO
    lasth = t_all // HALO - 1
    x3 = pl.pallas_call(
        functools.partial(_pool_kernel, tiles_per_seq=tps, n_prompt_tiles=npt),
        grid=(ntile,),
        in_specs=[pl.BlockSpec((HALO, d), lambda i: (jnp.maximum(i * rh - 1, 0), 0)),
                  tok(d),
                  pl.BlockSpec((HALO, d), lambda i: (jnp.minimum((i + 1) * rh, lasth), 0)),
                  mod_spec, full(row(g_mix_l1)),
                  pl.BlockSpec(w_pool_l1.shape, lambda i: (0, 0, 0)), full(row(s_pool_l1))],
        out_specs=tok(d),
        out_shape=jax.ShapeDtypeStruct((t_all, d), F32),
        compiler_params=_cparams(("arbitrary",)),
    )(x2, x2, x2, mod1, row(g_mix_l1), w_pool_l1.astype(BF16), row(s_pool_l1))

    y = peer_call(x3, mod1, g_ffn_l1, peer_wq_l1, peer_keys_l1, peer_u_l1, peer_v_l1, True)

    y_prompt = y[:t_p].reshape(nb_p, s_p, d)
    y_sample = y[t_p:].reshape(nb_s, s_s, d)
    new_ckv = ckv[:t_p].reshape(nb_p, s_p, KV_LORA)
    new_krope = kr[:t_p].reshape(nb_p, s_p, QK_ROPE)
    return (y_prompt, y_sample, new_ckv, new_krope, new_lru)
```

```python
import functools

import numpy as np
import jax
import jax.numpy as jnp
from jax import lax
from jax.experimental import pallas as pl
from jax.experimental.pallas import tpu as pltpu

F32 = jnp.float32
BF16 = jnp.bfloat16

D_MODEL = 1024
EPS = 1e-6
GRID_W = 64
MLA_HEADS = 4
Q_LORA = 384
KV_LORA = 256
QK_NOPE = 128
QK_ROPE = 64
V_HEAD = 128
MLA_WIDTH = MLA_HEADS * V_HEAD
ROPE_BASE = 10000.0
LRU_WIDTH = 512
LRU_BLOCKS = 4
LRU_BLOCK = LRU_WIDTH // LRU_BLOCKS
CONV_W = 4
CONV_LEFT = 2
LRU_C = 8.0
POOL_WINDOWS = (2, 4, 8, 16)
POOL_GROUP = D_MODEL // len(POOL_WINDOWS)
PEER_HEADS = 8
N_KEYS = 128
PEER_DKEY = 256
PEER_TOPK = 16

SUBLANES = 8
LANES = 128
VMEM_LIMIT = 56 * 1024 * 1024

TM = 256
QK_PAD = 256
HALO = 16
PEER_TM = 512
PEER_EBLK = 2048
PEER_SUB = 512
BF16_ROWS = 16
IN_EXT = 2048


def _rms(x, g):
    return x * lax.rsqrt(jnp.mean(x * x, axis=-1, keepdims=True) + EPS) * g


def _cparams(sem):
    return pltpu.CompilerParams(dimension_semantics=sem, vmem_limit_bytes=VMEM_LIMIT)


def _ada_kernel(c_ref, w_ref, b_ref, o_ref):
    c = c_ref[...]
    o_ref[...] = jnp.dot(c * jax.nn.sigmoid(c), w_ref[...], preferred_element_type=F32) + b_ref[...]


def _ada(cpad, w_mod, b_mod):
    n = w_mod.shape[1]
    bn = 768
    return pl.pallas_call(
        _ada_kernel,
        grid=(n // bn,),
        in_specs=[pl.BlockSpec(cpad.shape, lambda j: (0, 0)),
                  pl.BlockSpec((D_MODEL, bn), lambda j: (0, j)),
                  pl.BlockSpec((1, bn), lambda j: (0, j))],
        out_specs=pl.BlockSpec((cpad.shape[0], bn), lambda j: (0, j)),
        out_shape=jax.ShapeDtypeStruct((cpad.shape[0], n), F32),
        compiler_params=_cparams(("arbitrary",)),
    )(cpad, w_mod, b_mod.reshape(1, n))


def _inproj_kernel(x_ref, mod_ref, g_ref, win_ref, gq_ref, wuq_ref, gkv_ref, ck_ref, sk_ref,
                   q_ref, ckv_ref, kr_ref, krr_ref, ux_ref, ug_ref):
    mod = mod_ref[0]
    h = _rms(x_ref[...], g_ref[...]) * (1.0 + mod[1:2]) + mod[0:1]
    y = jnp.dot(h.astype(BF16), win_ref[...], preferred_element_type=F32)
    cq = y[:, 0:Q_LORA]
    ckv = y[:, Q_LORA:Q_LORA + KV_LORA]
    o = Q_LORA + KV_LORA
    kr = y[:, o:o + QK_ROPE]
    krp = y[:, o + 128:o + 128 + QK_ROPE]
    krs = y[:, o + 256:o + 256 + QK_ROPE]
    ux_ref[...] = y[:, o + 384:o + 384 + LRU_WIDTH]
    ug_ref[...] = y[:, o + 384 + LRU_WIDTH:o + 384 + 2 * LRU_WIDTH]
    ckv_ref[...] = _rms(ckv, gkv_ref[...])
    kr_ref[...] = kr
    ck = ck_ref[...]
    sk = sk_ref[...]
    krr_ref[...] = (krp * ck + krs * sk).astype(BF16)
    q = jnp.dot(_rms(cq, gq_ref[...]).astype(BF16), wuq_ref[...], preferred_element_type=F32)
    nw = MLA_HEADS * QK_NOPE
    rw = MLA_HEADS * QK_ROPE
    for hd in range(MLA_HEADS):
        qp = q[:, nw + hd * QK_ROPE:nw + (hd + 1) * QK_ROPE]
        qs = q[:, nw + rw + hd * QK_ROPE:nw + rw + (hd + 1) * QK_ROPE]
        q_ref[hd, :, 0:QK_NOPE] = q[:, hd * QK_NOPE:(hd + 1) * QK_NOPE].astype(BF16)
        q_ref[hd, :, QK_NOPE:QK_NOPE + QK_ROPE] = (qp * ck + qs * sk).astype(BF16)
        q_ref[hd, :, QK_NOPE + QK_ROPE:QK_PAD] = jnp.zeros((TM, QK_PAD - QK_NOPE - QK_ROPE), BF16)


def _attn_kernel(*refs, has_cache, s_new, n_cache):
    if has_cache:
        q_ref, ckv_ref, krr_ref, cckv_ref, ckr_ref, wukv_ref, o_ref, kcat_ref, vv_ref = refs
    else:
        q_ref, ckv_ref, krr_ref, wukv_ref, o_ref, kcat_ref, vv_ref = refs
    sk = n_cache + s_new
    kw = MLA_HEADS * QK_NOPE
    zpad = jnp.zeros((TM, QK_PAD - QK_NOPE - QK_ROPE), BF16)

    def put_keys(row0, ckv_rows, kr_rows):
        kv = jnp.dot(ckv_rows.astype(BF16), wukv_ref[...], preferred_element_type=F32)
        rows = pl.ds(row0, TM)
        for hd in range(MLA_HEADS):
            kcat_ref[hd, rows, 0:QK_NOPE] = kv[:, hd * QK_NOPE:(hd + 1) * QK_NOPE].astype(BF16)
            kcat_ref[hd, rows, QK_NOPE:QK_NOPE + QK_ROPE] = kr_rows.astype(BF16)
            kcat_ref[hd, rows, QK_NOPE + QK_ROPE:QK_PAD] = zpad
        vv_ref[rows, :] = kv[:, kw:].astype(BF16)

    @pl.when(pl.program_id(1) == 0)
    def _():
        if has_cache:
            for c in range(n_cache // TM):
                put_keys(c * TM, cckv_ref[0, c * TM:(c + 1) * TM, :], ckr_ref[0, c * TM:(c + 1) * TM, :])

        def body(c, carry):
            r0 = pl.multiple_of(c * TM, TM)
            put_keys(n_cache + r0, ckv_ref[pl.ds(r0, TM), :], krr_ref[pl.ds(r0, TM), :])
            return carry
        lax.fori_loop(0, s_new // TM, body, 0)

    scale = (QK_NOPE + QK_ROPE) ** -0.5
    for hd in range(MLA_HEADS):
        s = lax.dot_general(q_ref[hd], kcat_ref[hd], (((1,), (1,)), ((), ())),
                            preferred_element_type=F32) * scale
        m = jnp.max(s, axis=-1, keepdims=True)
        e = jnp.exp(s - m)
        l = jnp.sum(e, axis=-1, keepdims=True)
        o = jnp.dot(e.astype(BF16), vv_ref[:, hd * V_HEAD:(hd + 1) * V_HEAD], preferred_element_type=F32)
        o_ref[:, hd * V_HEAD:(hd + 1) * V_HEAD] = (o / l).astype(BF16)
    del sk


def _lru_dir(xp_ref, xc_ref, xn_ref, valid_prev, valid_next, d, reverse, cw_ref, cb_ref,
             wr_ref, br_ref, wi_ref, bi_ref, lam_ref, carry):
    xp = jnp.where(valid_prev, xp_ref[...], 0.0)
    xn = jnp.where(valid_next, xn_ref[...], 0.0)
    xx = jnp.concatenate([xp, xc_ref[...], xn], axis=0)
    n = TM + 2 * SUBLANES
    xc = cb_ref[...]
    for k in range(CONV_W):
        sh = (CONV_LEFT - k) % n
        xs = xx if sh == 0 else pltpu.roll(xx, sh, 0)
        xc = xc + xs[SUBLANES:SUBLANES + TM] * cw_ref[k:k + 1, :]
    rs, is_ = [], []
    for b in range(LRU_BLOCKS):
        xb = xc[:, b * LRU_BLOCK:(b + 1) * LRU_BLOCK]
        rs.append(jnp.dot(xb, wr_ref[d, b], preferred_element_type=F32))
        is_.append(jnp.dot(xb, wi_ref[d, b], preferred_element_type=F32))
    r = jax.nn.sigmoid(jnp.concatenate(rs, axis=-1) + br_ref[d:d + 1, :])
    i = jax.nn.sigmoid(jnp.concatenate(is_, axis=-1) + bi_ref[d:d + 1, :])
    nl = -lam_ref[d:d + 1, :]
    softplus = jnp.maximum(nl, 0.0) + jnp.log1p(jnp.exp(-jnp.abs(nl)))
    log_a = -LRU_C * r * softplus
    a = jnp.exp(log_a)
    bx = jnp.sqrt(jnp.tanh(-log_a) * (a * a + 1.0)) * (i * xc)
    t = lax.broadcasted_iota(jnp.int32, (TM, 1), 0)
    step = 1
    while step < TM:
        if reverse:
            keep = t < TM - step
            sh = TM - step
        else:
            keep = t >= step
            sh = step
        a_s = jnp.where(keep, pltpu.roll(a, sh, 0), 1.0)
        b_s = jnp.where(keep, pltpu.roll(bx, sh, 0), 0.0)
        bx = a * b_s + bx
        a = a * a_s
        step *= 2
    return a * carry + bx


def _lru_kernel(fxp, fxc, fxn, bxp, bxc, bxn, cw_ref, cb_ref, wr_ref, br_ref, wi_ref, bi_ref, lam_ref,
                h0_ref, hf_ref, hb_ref, st_ref, cf_ref, cbk_ref, *, nc):
    c = pl.program_id(1)

    @pl.when(c == 0)
    def _():
        cf_ref[...] = h0_ref[0, 0:1, :]
        cbk_ref[...] = h0_ref[0, 1:2, :]

    params = (cw_ref, cb_ref, wr_ref, br_ref, wi_ref, bi_ref, lam_ref)
    hf = _lru_dir(fxp, fxc, fxn, c > 0, c < nc - 1, 0, False, *params, cf_ref[...])
    hf_ref[...] = hf
    cf_ref[...] = hf[TM - 1:TM, :]
    hb = _lru_dir(bxp, bxc, bxn, c < nc - 1, c > 0, 1, True, *params, cbk_ref[...])
    hb_ref[...] = hb
    cbk_ref[...] = hb[0:1, :]
    st_ref[0, 0:1, :] = hf[TM - 1:TM, :]
    st_ref[0, 1:2, :] = hb[0:1, :]


def _oproj_kernel(x_ref, mod_ref, at_ref, hf_ref, hb_ref, ug_ref, woa_ref, wor_ref, o_ref):
    mod = mod_ref[0]
    rec = ((hf_ref[...] + hb_ref[...]) * jax.nn.gelu(ug_ref[...])).astype(BF16)
    out = (jnp.dot(at_ref[...], woa_ref[...], preferred_element_type=F32)
           + jnp.dot(rec, wor_ref[...], preferred_element_type=F32))
    o_ref[...] = x_ref[...] + mod[2:3] * out


def _pool_kernel(xp_ref, xc_ref, xn_ref, mod_ref, g_ref, wp_ref, sp_ref, o_ref, *, tiles_per_seq, n_prompt_tiles):
    i = pl.program_id(0)
    j = jnp.where(i < n_prompt_tiles, 0, (i - n_prompt_tiles) % tiles_per_seq)
    ntile = jnp.where(i < n_prompt_tiles, 1, tiles_per_seq)
    mod = mod_ref[0]
    g = g_ref[...]

    def hmod(x):
        return _rms(x, g) * (1.0 + mod[1:2]) + mod[0:1]

    x = xc_ref[...]
    hc = hmod(x)
    hp = jnp.where(j > 0, hmod(xp_ref[...]), 0.0)
    hn = jnp.where(j < ntile - 1, hmod(xn_ref[...]), 0.0)
    hh = jnp.concatenate([hp, hc, hn], axis=0)
    n = TM + 2 * HALO
    seq_len = ntile * TM
    t = j * TM + lax.broadcasted_iota(jnp.int32, (TM, 1), 0)
    ys = []
    for gi, w in enumerate(POOL_WINDOWS):
        cols = slice(gi * POOL_GROUP, (gi + 1) * POOL_GROUP)
        p = hh[:, cols]
        p = p + pltpu.roll(p, 1, 0)
        half = 1
        while 2 * half < w:
            p = pltpu.roll(p, half, 0) + pltpu.roll(p, n - half, 0)
            half *= 2
        lo = jnp.maximum(t - w // 2, 0)
        hi = jnp.minimum(t + (w - w // 2), seq_len)
        mean = p[HALO:HALO + TM] / (hi - lo).astype(F32)
        dg = (mean - hc[:, cols]).astype(BF16)
        ys.append(jnp.dot(dg, wp_ref[gi], preferred_element_type=F32))
    y = jnp.concatenate(ys, axis=-1) * sp_ref[...]
    o_ref[...] = x + mod[2:3] * y


def _sort_pairs(n):
    pairs = []

    def merge(lo, cnt, r):
        step = r * 2
        if step < cnt:
            merge(lo, cnt, step)
            merge(lo + r, cnt, step)
            for i in range(lo + r, lo + cnt - r, step):
                pairs.append((i, i + r))
        else:
            pairs.append((lo, lo + r))

    def sort(lo, cnt):
        if cnt > 1:
            m = cnt // 2
            sort(lo, m)
            sort(lo + m, m)
            merge(lo, cnt, 1)

    sort(0, n)
    return pairs


_SORT16 = _sort_pairs(PEER_TOPK)
_HYPER = [(a, b) for a in range(PEER_TOPK) for b in range(PEER_TOPK) if (a + 1) * (b + 1) <= PEER_TOPK]


def _top16_sorted(s):
    k = PEER_TOPK
    x = [s[SUBLANES * r:SUBLANES * (r + 1), :] for r in range(N_KEYS // SUBLANES)]
    for (i, j) in _SORT16:
        hi = jnp.maximum(x[i], x[j])
        lo = jnp.minimum(x[i], x[j])
        x[i], x[j] = hi, lo
    for shift in (4, 2, 1):
        y = [jnp.maximum(x[r], pltpu.roll(x[k - 1 - r], shift, 0)) for r in range(k)]
        stride = k // 2
        while stride >= 1:
            for i in range(k):
                if i & stride == 0:
                    hi = jnp.maximum(y[i], y[i + stride])
                    lo = jnp.minimum(y[i], y[i + stride])
                    y[i], y[i + stride] = hi, lo
            stride //= 2
        x = y
    return x


def _pair_counts(sv1, sv2):
    one = jnp.ones_like(sv1[0])
    zero = jnp.zeros_like(sv1[0])
    cand = [sv1[a] + sv2[b] for (a, b) in _HYPER]

    def ordered(i, j):
        (ai, bi), (aj, bj) = _HYPER[i], _HYPER[j]
        return (aj <= ai and bj <= bi) or (ai <= aj and bi <= bj)

    nh = len(_HYPER)
    cnt = [float((a + 1) * (b + 1) - 1 + sum(1 for j in range(i + 1, nh) if not ordered(i, j))) * one
           for i, (a, b) in enumerate(_HYPER)]
    for i in range(nh):
        for jx in range(i):
            if ordered(i, jx):
                continue
            ge = jnp.where(cand[jx] >= cand[i], one, zero)
            cnt[i] = cnt[i] + ge
            cnt[jx] = cnt[jx] - ge
    e1 = [jnp.exp(sv1[a] - sv1[0]) for a in range(PEER_TOPK)]
    e2 = [jnp.exp(sv2[b] - sv2[0]) for b in range(PEER_TOPK)]
    n = [zero for _ in range(PEER_TOPK)]
    z = zero
    for i, (a, b) in enumerate(_HYPER):
        sel = jnp.where(cnt[i] < float(PEER_TOPK), one, zero)
        n[a] = n[a] + sel
        z = z + sel * (e1[a] * e2[b])
    return n, 1.0 / z


def _head_tables(hd, sc_ref, sv_ref, nz_ref, ex_ref, cnt1_ref, e1n_ref, rank2_ref, e2_ref, exact):
    k = PEER_TOPK
    tm = sc_ref.shape[-1]
    nslab = N_KEYS // SUBLANES
    bad = []
    for g in range(tm // LANES):
        lanes = slice(g * LANES, (g + 1) * LANES)

        def bc(ref, *idx):
            return jnp.broadcast_to(ref[idx + (slice(hd, hd + 1), lanes)], (SUBLANES, LANES))

        sv1b = [bc(sv_ref, 0, a) for a in range(k)]
        nb = [bc(nz_ref, a) for a in range(k)]
        izb = bc(nz_ref, k)
        c1 = jnp.zeros((SUBLANES, LANES), F32)
        for r in range(nslab):
            rows = slice(r * SUBLANES, (r + 1) * SUBLANES)
            s1 = sc_ref[0, hd, rows, lanes]
            cnt1 = jnp.zeros_like(s1)
            if exact:
                rank1 = jnp.zeros_like(s1)
                for a in range(k):
                    rank1 = jnp.where(sv1b[a] > s1, float(a + 1), rank1)
                rank1 = rank1 + ex_ref[0, rows, lanes]
                for a in range(k):
                    cnt1 = jnp.where(rank1 == float(a), nb[a], cnt1)
            else:
                for a in range(k):
                    cnt1 = jnp.where(s1 == sv1b[a], nb[a], cnt1)
                c1 = c1 + jnp.where(s1 >= sv1b[k - 1], 1.0, 0.0)
            cnt1_ref[hd, rows, lanes] = cnt1
            e1n_ref[hd, rows, lanes] = jnp.exp(s1 - sv1b[0]) * izb
        sv2b = [bc(sv_ref, 1, a) for a in range(k)]
        c2 = jnp.zeros((SUBLANES, LANES), F32)
        for r in range(N_KEYS // BF16_ROWS):
            rk, e2 = [], []
            for q in range(BF16_ROWS // SUBLANES):
                r0 = r * BF16_ROWS + q * SUBLANES
                rows = slice(r0, r0 + SUBLANES)
                s2 = sc_ref[1, hd, rows, lanes]
                rank2 = jnp.zeros_like(s2)
                for a in range(k):
                    rank2 = jnp.where(sv2b[a] > s2, float(a + 1), rank2)
                if exact:
                    rank2 = jnp.minimum(rank2 + ex_ref[1, rows, lanes], float(k))
                else:
                    c2 = c2 + jnp.where(rank2 < float(k), 1.0, 0.0)
                rk.append(rank2)
                e2.append(jnp.exp(s2 - sv2b[0]))
            rows16 = slice(r * BF16_ROWS, (r + 1) * BF16_ROWS)
            rank2_ref[hd, rows16, lanes] = jnp.concatenate(rk, axis=0).astype(BF16)
            e2_ref[hd, rows16, lanes] = jnp.concatenate(e2, axis=0).astype(BF16)
        if not exact:
            n1 = jnp.sum(c1, axis=0, keepdims=True)
            n2 = jnp.sum(c2, axis=0, keepdims=True)
            b = jnp.where(n1 != float(k), 1.0, 0.0) + jnp.where(n2 != float(k), 1.0, 0.0)
            for svb in (sv1b, sv2b):
                for a in range(k - 1):
                    b = b + jnp.where(svb[a][0:1] == svb[a + 1][0:1], 1.0, 0.0)
            bad.append(b)
    return jnp.concatenate(bad, axis=1) if bad else None


def _tie_offsets(hd, sc_ref, ex_ref):
    tm = sc_ref.shape[-1]
    nidx = lax.broadcasted_iota(jnp.int32, (N_KEYS, tm), 0)
    for p in range(2):
        s = sc_ref[p, hd]

        def body(m, e, s=s, p=p):
            row = sc_ref[p, hd, pl.ds(m, 1), :]
            return e + jnp.where((s == row) & (nidx > m), 1.0, 0.0)

        ex_ref[p] = lax.fori_loop(0, N_KEYS, body, jnp.zeros((N_KEYS, tm), F32))


def _peer_kernel(x_ref, mod_ref, g_ref, wqt_ref, keys_ref, u_ref, vt_ref, gfin_ref, o_ref,
                 h2t_ref, acc_ref, sc_ref, sv_ref, nz_ref, ex_ref, cnt1_ref, e1n_ref, rank2_ref, e2_ref,
                 *, final_norm):
    k = pl.program_id(1)
    tm = PEER_TM
    half = PEER_DKEY // 2

    @pl.when(k == 0)
    def _():
        mod = mod_ref[0]
        h2 = _rms(x_ref[...], g_ref[...]) * (1.0 + mod[4:5]) + mod[3:4]
        h2t_ref[...] = h2.T.astype(BF16)
        acc_ref[...] = jnp.zeros_like(acc_ref)
        qt = jnp.dot(wqt_ref[...], h2t_ref[...], preferred_element_type=F32).astype(BF16)
        for hd in range(PEER_HEADS):
            for p in range(2):
                r0 = (hd * 2 + p) * half
                sc_ref[p, hd] = jnp.dot(keys_ref[hd, p], qt[r0:r0 + half, :],
                                        preferred_element_type=F32)
                for g in range(tm // LANES):
                    lanes = slice(g * LANES, (g + 1) * LANES)
                    top = _top16_sorted(sc_ref[p, hd, :, lanes])
                    for a in range(PEER_TOPK):
                        sv_ref[p, a, hd:hd + 1, lanes] = top[a][0:1, :]
        for g in range(tm // LANES):
            lanes = slice(g * LANES, (g + 1) * LANES)
            sv1 = [sv_ref[0, a, :, lanes] for a in range(PEER_TOPK)]
            sv2 = [sv_ref[1, a, :, lanes] for a in range(PEER_TOPK)]
            n, inv_z = _pair_counts(sv1, sv2)
            for a in range(PEER_TOPK):
                nz_ref[a, :, lanes] = n[a]
            nz_ref[PEER_TOPK, :, lanes] = inv_z
        tabs = (sc_ref, sv_ref, nz_ref, ex_ref, cnt1_ref, e1n_ref, rank2_ref, e2_ref)
        for hd in range(PEER_HEADS):
            bad = _head_tables(hd, *tabs, exact=False)

            @pl.when(jnp.max(bad) > 0.0)
            def _(hd=hd):
                _tie_offsets(hd, sc_ref, ex_ref)
                _head_tables(hd, *tabs, exact=True)

    zero = jnp.zeros((BF16_ROWS, tm), BF16)
    npiece = PEER_EBLK // N_KEYS
    per_sub = PEER_SUB // N_KEYS

    def up_dot(p):
        return jnp.dot(u_ref[p * N_KEYS:(p + 1) * N_KEYS, :], h2t_ref[...], preferred_element_type=F32)

    raw = up_dot(0)
    ys = []
    for p in range(npiece):
        nxt = up_dot(p + 1) if p + 1 < npiece else None
        i1 = k * npiece + p
        cbs = [jnp.broadcast_to(cnt1_ref[hd, pl.ds(i1, 1), :], (BF16_ROWS, tm)).astype(BF16)
               for hd in range(PEER_HEADS)]
        ebs = [jnp.broadcast_to(e1n_ref[hd, pl.ds(i1, 1), :], (BF16_ROWS, tm)).astype(BF16)
               for hd in range(PEER_HEADS)]
        act = jax.nn.gelu(raw).astype(BF16)
        for r in range(N_KEYS // BF16_ROWS):
            rows = slice(r * BF16_ROWS, (r + 1) * BF16_ROWS)
            w = None
            for hd in range(PEER_HEADS):
                t = jnp.where(rank2_ref[hd, rows, :] < cbs[hd], e2_ref[hd, rows, :], zero) * ebs[hd]
                w = t if w is None else w + t
            ys.append(act[rows, :] * w)
        raw = nxt
        if (p + 1) % per_sub == 0:
            e0 = (p + 1 - per_sub) * N_KEYS
            y = jnp.concatenate(ys, axis=0)
            ys = []
            acc_ref[...] += jnp.dot(vt_ref[:, e0:e0 + PEER_SUB], y, preferred_element_type=F32)

    @pl.when(k == pl.num_programs(1) - 1)
    def _():
        out = x_ref[...] + mod_ref[0][5:6] * acc_ref[...].T
        if final_norm:
            out = _rms(out, gfin_ref[...])
        o_ref[...] = out


def _rope_tables(s_prompt_tile, s_sample):
    n_rows = s_sample // GRID_W
    rows = jnp.repeat(jnp.arange(n_rows, dtype=F32), GRID_W)
    cols = jnp.tile(jnp.arange(GRID_W, dtype=F32), n_rows)
    axis_dim = QK_ROPE // 2
    inv_freq = ROPE_BASE ** (-jnp.arange(0, axis_dim, 2, dtype=F32) / axis_dim)
    ang = jnp.concatenate([rows[:, None] * inv_freq, cols[:, None] * inv_freq], axis=-1)
    cos, sin = jnp.cos(ang), jnp.sin(ang)
    ck = jnp.concatenate([cos, cos], axis=-1)
    sk = jnp.concatenate([-sin, sin], axis=-1)
    ident_c = jnp.ones((s_prompt_tile, QK_ROPE), F32)
    ident_s = jnp.zeros((s_prompt_tile, QK_ROPE), F32)
    return jnp.concatenate([ident_c, ck], axis=0), jnp.concatenate([ident_s, sk], axis=0)


def kernel(x_prompt, x_sample, cache_ckv_l0, cache_krope_l0, state_lru_l0, c, c_ctx, w_mod_l0, b_mod_l0, w_mod_l1, b_mod_l1, g_mix_l0, g_ffn_l0, g_mix_l1, g_ffn_l1, w_in_l0, g_q_l0, w_uq_l0, g_kv_l0, w_ukv_l0, conv_w_l0, conv_b_l0, w_rg_l0, b_rg_l0, w_ig_l0, b_ig_l0, lam_l0, w_o_l0, w_pool_l1, s_pool_l1, peer_wq_l0, peer_keys_l0, peer_u_l0, peer_v_l0, peer_wq_l1, peer_keys_l1, peer_u_l1, peer_v_l1, g_final):
    nb_p, s_p, d = x_prompt.shape
    nb_s, s_s, _ = x_sample.shape
    n_cache = cache_ckv_l0.shape[1]
    assert d == D_MODEL and s_p == TM and s_s % TM == 0 and n_cache % TM == 0
    t_p = nb_p * s_p
    t_s = nb_s * s_s
    t_all = t_p + t_s
    npt = t_p // TM
    tps = s_s // TM
    ntile = t_all // TM
    assert t_all % PEER_TM == 0 and t_p % PEER_TM == 0 and s_s % PEER_TM == 0

    x0 = jnp.concatenate([x_prompt.reshape(t_p, d), x_sample.reshape(t_s, d)], axis=0)

    ncond = 1 + nb_s
    cpad = jnp.zeros((2 * SUBLANES, d), F32).at[0].set(c_ctx).at[1:ncond].set(c)
    mod0 = _ada(cpad, w_mod_l0, b_mod_l0).reshape(2 * SUBLANES, 6, d)
    mod1 = _ada(cpad, w_mod_l1, b_mod_l1).reshape(2 * SUBLANES, 6, d)

    def cond_row(i):
        return jnp.where(i < npt, 0, 1 + (i - npt) // tps)

    mod_spec = pl.BlockSpec((1, 6, d), lambda i: (cond_row(i), 0, 0))
    row = lambda a: a.reshape(1, -1)

    perm = np.concatenate([np.arange(0, QK_ROPE, 2), np.arange(1, QK_ROPE, 2)])
    perm_sw = np.concatenate([np.arange(1, QK_ROPE, 2), np.arange(0, QK_ROPE, 2)])
    o1 = Q_LORA + KV_LORA
    w_kr = w_in_l0[:, o1:o1 + QK_ROPE]
    z64 = jnp.zeros((d, 128 - QK_ROPE), F32)
    w_in_ext = jnp.concatenate(
        [w_in_l0[:, :o1], w_kr, z64, w_kr[:, perm], z64, w_kr[:, perm_sw], z64, w_in_l0[:, o1 + QK_ROPE:]],
        axis=1).astype(BF16)
    assert w_in_ext.shape[1] == IN_EXT
    wq3 = w_uq_l0.reshape(Q_LORA, MLA_HEADS, QK_NOPE + QK_ROPE)
    w_uq_ext = jnp.concatenate(
        [wq3[:, :, :QK_NOPE].reshape(Q_LORA, -1),
         wq3[:, :, QK_NOPE:][:, :, perm].reshape(Q_LORA, -1),
         wq3[:, :, QK_NOPE:][:, :, perm_sw].reshape(Q_LORA, -1)], axis=1).astype(BF16)
    ck_tab, sk_tab = _rope_tables(TM, s_s)

    def rope_blk(i):
        return jnp.where(i < npt, 0, 1 + (i - npt) % tps)

    tok = lambda w: pl.BlockSpec((TM, w), lambda i: (i, 0))
    full = lambda a: pl.BlockSpec(a.shape, lambda *_: (0,) * a.ndim)
    q, ckv, kr, krr, ux, ug = pl.pallas_call(
        _inproj_kernel,
        grid=(ntile,),
        in_specs=[tok(d), mod_spec, full(row(g_mix_l0)), full(w_in_ext), full(row(g_q_l0)), full(w_uq_ext),
                  full(row(g_kv_l0)),
                  pl.BlockSpec((TM, QK_ROPE), lambda i: (rope_blk(i), 0)),
                  pl.BlockSpec((TM, QK_ROPE), lambda i: (rope_blk(i), 0))],
        out_specs=[pl.BlockSpec((MLA_HEADS, TM, QK_PAD), lambda i: (0, i, 0)),
                   tok(KV_LORA), tok(QK_ROPE), tok(QK_ROPE), tok(LRU_WIDTH), tok(LRU_WIDTH)],
        out_shape=[jax.ShapeDtypeStruct((MLA_HEADS, t_all, QK_PAD), BF16),
                   jax.ShapeDtypeStruct((t_all, KV_LORA), F32),
                   jax.ShapeDtypeStruct((t_all, QK_ROPE), F32),
                   jax.ShapeDtypeStruct((t_all, QK_ROPE), BF16),
                   jax.ShapeDtypeStruct((t_all, LRU_WIDTH), F32),
                   jax.ShapeDtypeStruct((t_all, LRU_WIDTH), F32)],
        compiler_params=_cparams(("arbitrary",)),
    )(x0, mod0, row(g_mix_l0), w_in_ext, row(g_q_l0), w_uq_ext, row(g_kv_l0), ck_tab, sk_tab)

    wkv3 = w_ukv_l0.reshape(KV_LORA, MLA_HEADS, QK_NOPE + V_HEAD)
    w_ukv_ext = jnp.concatenate([wkv3[:, :, :QK_NOPE].reshape(KV_LORA, -1),
                                 wkv3[:, :, QK_NOPE:].reshape(KV_LORA, -1)], axis=1).astype(BF16)

    def attn_call(nb, s_new, tile0, has_cache):
        nq = s_new // TM
        blk0 = tile0 * TM // s_new
        n_c = n_cache if has_cache else 0
        in_specs = [pl.BlockSpec((MLA_HEADS, TM, QK_PAD), lambda b, qi: (0, tile0 + b * nq + qi, 0)),
                    pl.BlockSpec((s_new, KV_LORA), lambda b, qi: (blk0 + b, 0)),
                    pl.BlockSpec((s_new, QK_ROPE), lambda b, qi: (blk0 + b, 0))]
        args = [q, ckv, krr]
        if has_cache:
            in_specs += [pl.BlockSpec((1, n_cache, KV_LORA), lambda b, qi: (b, 0, 0)),
                         pl.BlockSpec((1, n_cache, QK_ROPE), lambda b, qi: (b, 0, 0))]
            args += [cache_ckv_l0, cache_krope_l0[:, :, perm]]
        in_specs.append(pl.BlockSpec(w_ukv_ext.shape, lambda b, qi: (0, 0)))
        args.append(w_ukv_ext)
        return pl.pallas_call(
            functools.partial(_attn_kernel, has_cache=has_cache, s_new=s_new, n_cache=n_c),
            grid=(nb, nq),
            in_specs=in_specs,
            out_specs=pl.BlockSpec((TM, MLA_WIDTH), lambda b, qi: (b * nq + qi, 0)),
            out_shape=jax.ShapeDtypeStruct((nb * s_new, MLA_WIDTH), BF16),
            scratch_shapes=[pltpu.VMEM((MLA_HEADS, n_c + s_new, QK_PAD), BF16),
                            pltpu.VMEM((n_c + s_new, MLA_WIDTH), BF16)],
            compiler_params=_cparams(("arbitrary", "arbitrary")),
        )(*args)

    attn = jnp.concatenate([attn_call(nb_p, s_p, 0, False), attn_call(nb_s, s_s, npt, True)], axis=0)

    def lru_call(nb, s_new, tile0, h0):
        nc = s_new // TM
        r8 = TM // SUBLANES
        last8 = t_all // SUBLANES - 1

        def cur(rev):
            return pl.BlockSpec((TM, LRU_WIDTH),
                                lambda b, cc: (tile0 + b * nc + (nc - 1 - cc if rev else cc), 0))

        def prev(rev):
            return pl.BlockSpec((SUBLANES, LRU_WIDTH), lambda b, cc: (
                jnp.maximum((tile0 + b * nc + (nc - 1 - cc if rev else cc)) * r8 - 1, 0), 0))

        def nxt(rev):
            return pl.BlockSpec((SUBLANES, LRU_WIDTH), lambda b, cc: (
                jnp.minimum((tile0 + b * nc + (nc - 1 - cc if rev else cc) + 1) * r8, last8), 0))

        small = [conv_w_l0, row(conv_b_l0), w_rg_l0, b_rg_l0, w_ig_l0, b_ig_l0, lam_l0]
        return pl.pallas_call(
            functools.partial(_lru_kernel, nc=nc),
            grid=(nb, nc),
            in_specs=[prev(False), cur(False), nxt(False), prev(True), cur(True), nxt(True)]
                     + [full(a) for a in small]
                     + [pl.BlockSpec((1, 2, LRU_WIDTH), lambda b, cc: (b, 0, 0))],
            out_specs=[pl.BlockSpec((TM, LRU_WIDTH), lambda b, cc: (b * nc + cc, 0)),
                       pl.BlockSpec((TM, LRU_WIDTH), lambda b, cc: (b * nc + nc - 1 - cc, 0)),
                       pl.BlockSpec((1, 2, LRU_WIDTH), lambda b, cc: (b, 0, 0))],
            out_shape=[jax.ShapeDtypeStruct((nb * s_new, LRU_WIDTH), F32),
                       jax.ShapeDtypeStruct((nb * s_new, LRU_WIDTH), F32),
                       jax.ShapeDtypeStruct((nb, 2, LRU_WIDTH), F32)],
            scratch_shapes=[pltpu.VMEM((1, LRU_WIDTH), F32), pltpu.VMEM((1, LRU_WIDTH), F32)],
            compiler_params=_cparams(("arbitrary", "arbitrary")),
        )(ux, ux, ux, ux, ux, ux, *small, h0)

    hf_p, hb_p, new_lru = lru_call(nb_p, s_p, 0, jnp.zeros((nb_p, 2, LRU_WIDTH), F32))
    hf_s, hb_s, _ = lru_call(nb_s, s_s, npt, state_lru_l0.astype(F32))
    hf = jnp.concatenate([hf_p, hf_s], axis=0)
    hb = jnp.concatenate([hb_p, hb_s], axis=0)

    w_o = w_o_l0.astype(BF16)
    x1 = pl.pallas_call(
        _oproj_kernel,
        grid=(ntile,),
        in_specs=[tok(d), mod_spec, tok(MLA_WIDTH), tok(LRU_WIDTH), tok(LRU_WIDTH), tok(LRU_WIDTH),
                  pl.BlockSpec((MLA_WIDTH, d), lambda i: (0, 0)), pl.BlockSpec((LRU_WIDTH, d), lambda i: (1, 0))],
        out_specs=tok(d),
        out_shape=jax.ShapeDtypeStruct((t_all, d), F32),
        compiler_params=_cparams(("arbitrary",)),
    )(x0, mod0, attn, hf, hb, ug, w_o, w_o)

    def peer_call(x, mod, g_ffn, w_q, sub_keys, u, v, final_norm):
        n_exp = u.shape[0]
        assert n_exp == N_KEYS * N_KEYS and n_exp % PEER_EBLK == 0
        wqt = w_q.T.astype(BF16)
        keys = sub_keys.astype(BF16)
        ub = u.astype(BF16)
        vt = v.T.astype(BF16)
        tpp = PEER_TM // TM

        def cond_row_p(i):
            return cond_row(i * tpp)

        big = lambda dt: pltpu.VMEM((PEER_HEADS, N_KEYS, PEER_TM), dt)
        return pl.pallas_call(
            functools.partial(_peer_kernel, final_norm=final_norm),
            grid=(t_all // PEER_TM, n_exp // PEER_EBLK),
            in_specs=[pl.BlockSpec((PEER_TM, d), lambda i, k: (i, 0)),
                      pl.BlockSpec((1, 6, d), lambda i, k: (cond_row_p(i), 0, 0)),
                      pl.BlockSpec((1, d), lambda i, k: (0, 0)),
                      pl.BlockSpec(wqt.shape, lambda i, k: (0, 0)),
                      pl.BlockSpec(keys.shape, lambda i, k: (0, 0, 0, 0)),
                      pl.BlockSpec((PEER_EBLK, d), lambda i, k: (k, 0)),
                      pl.BlockSpec((d, PEER_EBLK), lambda i, k: (0, k)),
                      pl.BlockSpec((1, d), lambda i, k: (0, 0))],
            out_specs=pl.BlockSpec((PEER_TM, d), lambda i, k: (i, 0)),
            out_shape=jax.ShapeDtypeStruct((t_all, d), F32),
            scratch_shapes=[pltpu.VMEM((d, PEER_TM), BF16),
                            pltpu.VMEM((d, PEER_TM), F32),
                            pltpu.VMEM((2, PEER_HEADS, N_KEYS, PEER_TM), F32),
                            pltpu.VMEM((2, PEER_TOPK, PEER_HEADS, PEER_TM), F32),
                            pltpu.VMEM((PEER_TOPK + 1, PEER_HEADS, PEER_TM), F32),
                            pltpu.VMEM((2, N_KEYS, PEER_TM), F32),
                            big(F32), big(F32), big(BF16), big(BF16)],
            compiler_params=_cparams(("arbitrary", "arbitrary")),
        )(x, mod, row(g_ffn), wqt, keys, ub, vt, row(g_final))

    x2 = peer_call(x1, mod0, g_ffn_l0, peer_wq_l0, peer_keys_l0, peer_u_l0, peer_v_l0, False)

    rh = TM // HALO
    lasth = t_all // HALO - 1
    x3 = pl.pallas_call(
        functools.partial(_pool_kernel, tiles_per_seq=tps, n_prompt_tiles=npt),
        grid=(ntile,),
        in_specs=[pl.BlockSpec((HALO, d), lambda i: (jnp.maximum(i * rh - 1, 0), 0)),
                  tok(d),
                  pl.BlockSpec((HALO, d), lambda i: (jnp.minimum((i + 1) * rh, lasth), 0)),
                  mod_spec, full(row(g_mix_l1)),
                  pl.BlockSpec(w_pool_l1.shape, lambda i: (0, 0, 0)), full(row(s_pool_l1))],
        out_specs=tok(d),
        out_shape=jax.ShapeDtypeStruct((t_all, d), F32),
        compiler_params=_cparams(("arbitrary",)),
    )(x2, x2, x2, mod1, row(g_mix_l1), w_pool_l1.astype(BF16), row(s_pool_l1))

    y = peer_call(x3, mod1, g_ffn_l1, peer_wq_l1, peer_keys_l1, peer_u_l1, peer_v_l1, True)

    y_prompt = y[:t_p].reshape(nb_p, s_p, d)
    y_sample = y[t_p:].reshape(nb_s, s_s, d)
    new_ckv = ckv[:t_p].reshape(nb_p, s_p, KV_LORA)
    new_krope = kr[:t_p].reshape(nb_p, s_p, QK_ROPE)
    return (y_prompt, y_sample, new_ckv, new_krope, new_lru)
```

```python
import functools

import numpy as np
import jax
import jax.numpy as jnp
from jax import lax
from jax.experimental import pallas as pl
from jax.experimental.pallas import tpu as pltpu

F32 = jnp.float32
BF16 = jnp.bfloat16

D_MODEL = 1024
EPS = 1e-6
GRID_W = 64
MLA_HEADS = 4
Q_LORA = 384
KV_LORA = 256
QK_NOPE = 128
QK_ROPE = 64
V_HEAD = 128
MLA_WIDTH = MLA_HEADS * V_HEAD
ROPE_BASE = 10000.0
LRU_WIDTH = 512
LRU_BLOCKS = 4
LRU_BLOCK = LRU_WIDTH // LRU_BLOCKS
CONV_W = 4
CONV_LEFT = 2
LRU_C = 8.0
POOL_WINDOWS = (2, 4, 8, 16)
POOL_GROUP = D_MODEL // len(POOL_WINDOWS)
PEER_HEADS = 8
N_KEYS = 128
PEER_DKEY = 256
PEER_TOPK = 16

SUBLANES = 8
LANES = 128
VMEM_LIMIT = 56 * 1024 * 1024

TM = 256
QK_PAD = 256
HALO = 16
PEER_TM = 512
PEER_EBLK = 2048
PEER_SUB = 512
BF16_ROWS = 16
IN_EXT = 2048


def _rms(x, g):
    return x * lax.rsqrt(jnp.mean(x * x, axis=-1, keepdims=True) + EPS) * g


def _cparams(sem):
    return pltpu.CompilerParams(dimension_semantics=sem, vmem_limit_bytes=VMEM_LIMIT)


def _ada_kernel(c_ref, w_ref, b_ref, o_ref):
    c = c_ref[...]
    o_ref[...] = jnp.dot(c * jax.nn.sigmoid(c), w_ref[...], preferred_element_type=F32) + b_ref[...]


def _ada(cpad, w_mod, b_mod):
    n = w_mod.shape[1]
    bn = 768
    return pl.pallas_call(
        _ada_kernel,
        grid=(n // bn,),
        in_specs=[pl.BlockSpec(cpad.shape, lambda j: (0, 0)),
                  pl.BlockSpec((D_MODEL, bn), lambda j: (0, j)),
                  pl.BlockSpec((1, bn), lambda j: (0, j))],
        out_specs=pl.BlockSpec((cpad.shape[0], bn), lambda j: (0, j)),
        out_shape=jax.ShapeDtypeStruct((cpad.shape[0], n), F32),
        compiler_params=_cparams(("arbitrary",)),
    )(cpad, w_mod, b_mod.reshape(1, n))


def _inproj_kernel(x_ref, mod_ref, g_ref, win_ref, gq_ref, wuq_ref, gkv_ref, ck_ref, sk_ref,
                   q_ref, ckv_ref, kr_ref, krr_ref, ux_ref, ug_ref):
    mod = mod_ref[0]
    h = _rms(x_ref[...], g_ref[...]) * (1.0 + mod[1:2]) + mod[0:1]
    y = jnp.dot(h.astype(BF16), win_ref[...], preferred_element_type=F32)
    cq = y[:, 0:Q_LORA]
    ckv = y[:, Q_LORA:Q_LORA + KV_LORA]
    o = Q_LORA + KV_LORA
    kr = y[:, o:o + QK_ROPE]
    krp = y[:, o + 128:o + 128 + QK_ROPE]
    krs = y[:, o + 256:o + 256 + QK_ROPE]
    ux_ref[...] = y[:, o + 384:o + 384 + LRU_WIDTH]
    ug_ref[...] = y[:, o + 384 + LRU_WIDTH:o + 384 + 2 * LRU_WIDTH]
    ckv_ref[...] = _rms(ckv, gkv_ref[...])
    kr_ref[...] = kr
    ck = ck_ref[...]
    sk = sk_ref[...]
    krr_ref[...] = (krp * ck + krs * sk).astype(BF16)
    q = jnp.dot(_rms(cq, gq_ref[...]).astype(BF16), wuq_ref[...], preferred_element_type=F32)
    nw = MLA_HEADS * QK_NOPE
    rw = MLA_HEADS * QK_ROPE
    for hd in range(MLA_HEADS):
        qp = q[:, nw + hd * QK_ROPE:nw + (hd + 1) * QK_ROPE]
        qs = q[:, nw + rw + hd * QK_ROPE:nw + rw + (hd + 1) * QK_ROPE]
        q_ref[hd, :, 0:QK_NOPE] = q[:, hd * QK_NOPE:(hd + 1) * QK_NOPE].astype(BF16)
        q_ref[hd, :, QK_NOPE:QK_NOPE + QK_ROPE] = (qp * ck + qs * sk).astype(BF16)
        q_ref[hd, :, QK_NOPE + QK_ROPE:QK_PAD] = jnp.zeros((TM, QK_PAD - QK_NOPE - QK_ROPE), BF16)


def _attn_kernel(*refs, has_cache, s_new, n_cache):
    if has_cache:
        q_ref, ckv_ref, krr_ref, cckv_ref, ckr_ref, wukv_ref, o_ref, kcat_ref, vv_ref = refs
    else:
        q_ref, ckv_ref, krr_ref, wukv_ref, o_ref, kcat_ref, vv_ref = refs
    sk = n_cache + s_new
    kw = MLA_HEADS * QK_NOPE
    zpad = jnp.zeros((TM, QK_PAD - QK_NOPE - QK_ROPE), BF16)

    def put_keys(row0, ckv_rows, kr_rows):
        kv = jnp.dot(ckv_rows.astype(BF16), wukv_ref[...], preferred_element_type=F32)
        rows = pl.ds(row0, TM)
        for hd in range(MLA_HEADS):
            kcat_ref[hd, rows, 0:QK_NOPE] = kv[:, hd * QK_NOPE:(hd + 1) * QK_NOPE].astype(BF16)
            kcat_ref[hd, rows, QK_NOPE:QK_NOPE + QK_ROPE] = kr_rows.astype(BF16)
            kcat_ref[hd, rows, QK_NOPE + QK_ROPE:QK_PAD] = zpad
        vv_ref[rows, :] = kv[:, kw:].astype(BF16)

    @pl.when(pl.program_id(1) == 0)
    def _():
        if has_cache:
            for c in range(n_cache // TM):
                put_keys(c * TM, cckv_ref[0, c * TM:(c + 1) * TM, :], ckr_ref[0, c * TM:(c + 1) * TM, :])

        def body(c, carry):
            r0 = pl.multiple_of(c * TM, TM)
            put_keys(n_cache + r0, ckv_ref[pl.ds(r0, TM), :], krr_ref[pl.ds(r0, TM), :])
            return carry
        lax.fori_loop(0, s_new // TM, body, 0)

    scale = (QK_NOPE + QK_ROPE) ** -0.5
    for hd in range(MLA_HEADS):
        s = lax.dot_general(q_ref[hd], kcat_ref[hd], (((1,), (1,)), ((), ())),
                            preferred_element_type=F32) * scale
        m = jnp.max(s, axis=-1, keepdims=True)
        e = jnp.exp(s - m)
        l = jnp.sum(e, axis=-1, keepdims=True)
        o = jnp.dot(e.astype(BF16), vv_ref[:, hd * V_HEAD:(hd + 1) * V_HEAD], preferred_element_type=F32)
        o_ref[:, hd * V_HEAD:(hd + 1) * V_HEAD] = (o / l).astype(BF16)
    del sk


def _lru_dir(xp_ref, xc_ref, xn_ref, valid_prev, valid_next, d, reverse, cw_ref, cb_ref,
             wr_ref, br_ref, wi_ref, bi_ref, lam_ref, carry):
    xp = jnp.where(valid_prev, xp_ref[...], 0.0)
    xn = jnp.where(valid_next, xn_ref[...], 0.0)
    xx = jnp.concatenate([xp, xc_ref[...], xn], axis=0)
    n = TM + 2 * SUBLANES
    xc = cb_ref[...]
    for k in range(CONV_W):
        sh = (CONV_LEFT - k) % n
        xs = xx if sh == 0 else pltpu.roll(xx, sh, 0)
        xc = xc + xs[SUBLANES:SUBLANES + TM] * cw_ref[k:k + 1, :]
    rs, is_ = [], []
    for b in range(LRU_BLOCKS):
        xb = xc[:, b * LRU_BLOCK:(b + 1) * LRU_BLOCK]
        rs.append(jnp.dot(xb, wr_ref[d, b], preferred_element_type=F32))
        is_.append(jnp.dot(xb, wi_ref[d, b], preferred_element_type=F32))
    r = jax.nn.sigmoid(jnp.concatenate(rs, axis=-1) + br_ref[d:d + 1, :])
    i = jax.nn.sigmoid(jnp.concatenate(is_, axis=-1) + bi_ref[d:d + 1, :])
    nl = -lam_ref[d:d + 1, :]
    softplus = jnp.maximum(nl, 0.0) + jnp.log1p(jnp.exp(-jnp.abs(nl)))
    log_a = -LRU_C * r * softplus
    a = jnp.exp(log_a)
    bx = jnp.sqrt(jnp.tanh(-log_a) * (a * a + 1.0)) * (i * xc)
    t = lax.broadcasted_iota(jnp.int32, (TM, 1), 0)
    step = 1
    while step < TM:
        if reverse:
            keep = t < TM - step
            sh = TM - step
        else:
            keep = t >= step
            sh = step
        a_s = jnp.where(keep, pltpu.roll(a, sh, 0), 1.0)
        b_s = jnp.where(keep, pltpu.roll(bx, sh, 0), 0.0)
        bx = a * b_s + bx
        a = a * a_s
        step *= 2
    return a * carry + bx


def _lru_kernel(fxp, fxc, fxn, bxp, bxc, bxn, cw_ref, cb_ref, wr_ref, br_ref, wi_ref, bi_ref, lam_ref,
                h0_ref, hf_ref, hb_ref, st_ref, cf_ref, cbk_ref, *, nc):
    c = pl.program_id(1)

    @pl.when(c == 0)
    def _():
        cf_ref[...] = h0_ref[0, 0:1, :]
        cbk_ref[...] = h0_ref[0, 1:2, :]

    params = (cw_ref, cb_ref, wr_ref, br_ref, wi_ref, bi_ref, lam_ref)
    hf = _lru_dir(fxp, fxc, fxn, c > 0, c < nc - 1, 0, False, *params, cf_ref[...])
    hf_ref[...] = hf
    cf_ref[...] = hf[TM - 1:TM, :]
    hb = _lru_dir(bxp, bxc, bxn, c < nc - 1, c > 0, 1, True, *params, cbk_ref[...])
    hb_ref[...] = hb
    cbk_ref[...] = hb[0:1, :]
    st_ref[0, 0:1, :] = hf[TM - 1:TM, :]
    st_ref[0, 1:2, :] = hb[0:1, :]


def _oproj_kernel(x_ref, mod_ref, at_ref, hf_ref, hb_ref, ug_ref, woa_ref, wor_ref, o_ref):
    mod = mod_ref[0]
    rec = ((hf_ref[...] + hb_ref[...]) * jax.nn.gelu(ug_ref[...])).astype(BF16)
    out = (jnp.dot(at_ref[...], woa_ref[...], preferred_element_type=F32)
           + jnp.dot(rec, wor_ref[...], preferred_element_type=F32))
    o_ref[...] = x_ref[...] + mod[2:3] * out


def _pool_kernel(xp_ref, xc_ref, xn_ref, mod_ref, g_ref, wp_ref, sp_ref, o_ref, *, tiles_per_seq, n_prompt_tiles):
    i = pl.program_id(0)
    j = jnp.where(i < n_prompt_tiles, 0, (i - n_prompt_tiles) % tiles_per_seq)
    ntile = jnp.where(i < n_prompt_tiles, 1, tiles_per_seq)
    mod = mod_ref[0]
    g = g_ref[...]

    def hmod(x):
        return _rms(x, g) * (1.0 + mod[1:2]) + mod[0:1]

    x = xc_ref[...]
    hc = hmod(x)
    hp = jnp.where(j > 0, hmod(xp_ref[...]), 0.0)
    hn = jnp.where(j < ntile - 1, hmod(xn_ref[...]), 0.0)
    hh = jnp.concatenate([hp, hc, hn], axis=0)
    n = TM + 2 * HALO
    seq_len = ntile * TM
    t = j * TM + lax.broadcasted_iota(jnp.int32, (TM, 1), 0)
    ys = []
    for gi, w in enumerate(POOL_WINDOWS):
        cols = slice(gi * POOL_GROUP, (gi + 1) * POOL_GROUP)
        p = hh[:, cols]
        p = p + pltpu.roll(p, 1, 0)
        half = 1
        while 2 * half < w:
            p = pltpu.roll(p, half, 0) + pltpu.roll(p, n - half, 0)
            half *= 2
        lo = jnp.maximum(t - w // 2, 0)
        hi = jnp.minimum(t + (w - w // 2), seq_len)
        mean = p[HALO:HALO + TM] / (hi - lo).astype(F32)
        dg = (mean - hc[:, cols]).astype(BF16)
        ys.append(jnp.dot(dg, wp_ref[gi], preferred_element_type=F32))
    y = jnp.concatenate(ys, axis=-1) * sp_ref[...]
    o_ref[...] = x + mod[2:3] * y


def _sort_pairs(n):
    pairs = []

    def merge(lo, cnt, r):
        step = r * 2
        if step < cnt:
            merge(lo, cnt, step)
            merge(lo + r, cnt, step)
            for i in range(lo + r, lo + cnt - r, step):
                pairs.append((i, i + r))
        else:
            pairs.append((lo, lo + r))

    def sort(lo, cnt):
        if cnt > 1:
            m = cnt // 2
            sort(lo, m)
            sort(lo + m, m)
            merge(lo, cnt, 1)

    sort(0, n)
    return pairs


_SORT16 = _sort_pairs(PEER_TOPK)
_HYPER = [(a, b) for a in range(PEER_TOPK) for b in range(PEER_TOPK) if (a + 1) * (b + 1) <= PEER_TOPK]


def _top16_sorted(s):
    k = PEER_TOPK
    x = [s[SUBLANES * r:SUBLANES * (r + 1), :] for r in range(N_KEYS // SUBLANES)]
    for (i, j) in _SORT16:
        hi = jnp.maximum(x[i], x[j])
        lo = jnp.minimum(x[i], x[j])
        x[i], x[j] = hi, lo
    for shift in (4, 2, 1):
        y = [jnp.maximum(x[r], pltpu.roll(x[k - 1 - r], shift, 0)) for r in range(k)]
        stride = k // 2
        while stride >= 1:
            for i in range(k):
                if i & stride == 0:
                    hi = jnp.maximum(y[i], y[i + stride])
                    lo = jnp.minimum(y[i], y[i + stride])
                    y[i], y[i + stride] = hi, lo
            stride //= 2
        x = y
    return x


def _pair_counts(sv1, sv2):
    one = jnp.ones_like(sv1[0])
    zero = jnp.zeros_like(sv1[0])
    cand = [sv1[a] + sv2[b] for (a, b) in _HYPER]

    def ordered(i, j):
        (ai, bi), (aj, bj) = _HYPER[i], _HYPER[j]
        return (aj <= ai and bj <= bi) or (ai <= aj and bi <= bj)

    nh = len(_HYPER)
    cnt = [float((a + 1) * (b + 1) - 1 + sum(1 for j in range(i + 1, nh) if not ordered(i, j))) * one
           for i, (a, b) in enumerate(_HYPER)]
    for i in range(nh):
        for jx in range(i):
            if ordered(i, jx):
                continue
            ge = jnp.where(cand[jx] >= cand[i], one, zero)
            cnt[i] = cnt[i] + ge
            cnt[jx] = cnt[jx] - ge
    e1 = [jnp.exp(sv1[a] - sv1[0]) for a in range(PEER_TOPK)]
    e2 = [jnp.exp(sv2[b] - sv2[0]) for b in range(PEER_TOPK)]
    n = [zero for _ in range(PEER_TOPK)]
    z = zero
    for i, (a, b) in enumerate(_HYPER):
        sel = jnp.where(cnt[i] < float(PEER_TOPK), one, zero)
        n[a] = n[a] + sel
        z = z + sel * (e1[a] * e2[b])
    return n, 1.0 / z


def _head_tables(hd, sc_ref, sv_ref, nz_ref, ex_ref, cnt1_ref, e1n_ref, rank2_ref, e2_ref, exact):
    k = PEER_TOPK
    tm = sc_ref.shape[-1]
    nslab = N_KEYS // SUBLANES
    bad = []
    for g in range(tm // LANES):
        lanes = slice(g * LANES, (g + 1) * LANES)

        def bc(ref, *idx):
            return jnp.broadcast_to(ref[idx + (slice(hd, hd + 1), lanes)], (SUBLANES, LANES))

        sv1b = [bc(sv_ref, 0, a) for a in range(k)]
        nb = [bc(nz_ref, a) for a in range(k)]
        izb = bc(nz_ref, k)
        c1 = jnp.zeros((SUBLANES, LANES), F32)
        for r in range(nslab):
            rows = slice(r * SUBLANES, (r + 1) * SUBLANES)
            s1 = sc_ref[0, hd, rows, lanes]
            cnt1 = jnp.zeros_like(s1)
            if exact:
                rank1 = jnp.zeros_like(s1)
                for a in range(k):
                    rank1 = jnp.where(sv1b[a] > s1, float(a + 1), rank1)
                rank1 = rank1 + ex_ref[0, rows, lanes]
                for a in range(k):
                    cnt1 = jnp.where(rank1 == float(a), nb[a], cnt1)
            else:
                for a in range(k):
                    cnt1 = jnp.where(s1 == sv1b[a], nb[a], cnt1)
                c1 = c1 + jnp.where(s1 >= sv1b[k - 1], 1.0, 0.0)
            cnt1_ref[hd, rows, lanes] = cnt1
            e1n_ref[hd, rows, lanes] = jnp.exp(s1 - sv1b[0]) * izb
        sv2b = [bc(sv_ref, 1, a) for a in range(k)]
        c2 = jnp.zeros((SUBLANES, LANES), F32)
        for r in range(N_KEYS // BF16_ROWS):
            rk, e2 = [], []
            for q in range(BF16_ROWS // SUBLANES):
                r0 = r * BF16_ROWS + q * SUBLANES
                rows = slice(r0, r0 + SUBLANES)
                s2 = sc_ref[1, hd, rows, lanes]
                rank2 = jnp.zeros_like(s2)
                for a in range(k):
                    rank2 = jnp.where(sv2b[a] > s2, float(a + 1), rank2)
                if exact:
                    rank2 = jnp.minimum(rank2 + ex_ref[1, rows, lanes], float(k))
                else:
                    c2 = c2 + jnp.where(rank2 < float(k), 1.0, 0.0)
                rk.append(rank2)
                e2.append(jnp.exp(s2 - sv2b[0]))
            rows16 = slice(r * BF16_ROWS, (r + 1) * BF16_ROWS)
            rank2_ref[hd, rows16, lanes] = jnp.concatenate(rk, axis=0).astype(BF16)
            e2_ref[hd, rows16, lanes] = jnp.concatenate(e2, axis=0).astype(BF16)
        if not exact:
            n1 = jnp.sum(c1, axis=0, keepdims=True)
            n2 = jnp.sum(c2, axis=0, keepdims=True)
            b = jnp.where(n1 != float(k), 1.0, 0.0) + jnp.where(n2 != float(k), 1.0, 0.0)
            for svb in (sv1b, sv2b):
                for a in range(k - 1):
                    b = b + jnp.where(svb[a][0:1] == svb[a + 1][0:1], 1.0, 0.0)
            bad.append(b)
    return jnp.concatenate(bad, axis=1) if bad else None


def _tie_offsets(hd, sc_ref, ex_ref):
    tm = sc_ref.shape[-1]
    nidx = lax.broadcasted_iota(jnp.int32, (N_KEYS, tm), 0)
    for p in range(2):
        s = sc_ref[p, hd]

        def body(m, e, s=s, p=p):
            row = sc_ref[p, hd, pl.ds(m, 1), :]
            return e + jnp.where((s == row) & (nidx > m), 1.0, 0.0)

        ex_ref[p] = lax.fori_loop(0, N_KEYS, body, jnp.zeros((N_KEYS, tm), F32))


def _peer_kernel(x_ref, mod_ref, g_ref, wqt_ref, keys_ref, u_ref, vt_ref, gfin_ref, o_ref,
                 h2t_ref, acc_ref, sc_ref, sv_ref, nz_ref, ex_ref, cnt1_ref, e1n_ref, rank2_ref, e2_ref,
                 *, final_norm):
    k = pl.program_id(1)
    tm = PEER_TM
    half = PEER_DKEY // 2

    @pl.when(k == 0)
    def _():
        mod = mod_ref[0]
        h2 = _rms(x_ref[...], g_ref[...]) * (1.0 + mod[4:5]) + mod[3:4]
        h2t_ref[...] = h2.T.astype(BF16)
        acc_ref[...] = jnp.zeros_like(acc_ref)
        qt = jnp.dot(wqt_ref[...], h2t_ref[...], preferred_element_type=F32).astype(BF16)
        for hd in range(PEER_HEADS):
            for p in range(2):
                r0 = (hd * 2 + p) * half
                sc_ref[p, hd] = jnp.dot(keys_ref[hd, p], qt[r0:r0 + half, :],
                                        preferred_element_type=F32)
                for g in range(tm // LANES):
                    lanes = slice(g * LANES, (g + 1) * LANES)
                    top = _top16_sorted(sc_ref[p, hd, :, lanes])
                    for a in range(PEER_TOPK):
                        sv_ref[p, a, hd:hd + 1, lanes] = top[a][0:1, :]
        for g in range(tm // LANES):
            lanes = slice(g * LANES, (g + 1) * LANES)
            sv1 = [sv_ref[0, a, :, lanes] for a in range(PEER_TOPK)]
            sv2 = [sv_ref[1, a, :, lanes] for a in range(PEER_TOPK)]
            n, inv_z = _pair_counts(sv1, sv2)
            for a in range(PEER_TOPK):
                nz_ref[a, :, lanes] = n[a]
            nz_ref[PEER_TOPK, :, lanes] = inv_z
        tabs = (sc_ref, sv_ref, nz_ref, ex_ref, cnt1_ref, e1n_ref, rank2_ref, e2_ref)
        for hd in range(PEER_HEADS):
            bad = _head_tables(hd, *tabs, exact=False)

            @pl.when(jnp.max(bad) > 0.0)
            def _(hd=hd):
                _tie_offsets(hd, sc_ref, ex_ref)
                _head_tables(hd, *tabs, exact=True)

    zero = jnp.zeros((BF16_ROWS, tm), BF16)
    npiece = PEER_EBLK // N_KEYS
    per_sub = PEER_SUB // N_KEYS

    def up_dot(p):
        return jnp.dot(u_ref[p * N_KEYS:(p + 1) * N_KEYS, :], h2t_ref[...], preferred_element_type=F32)

    raw = up_dot(0)
    ys = []
    for p in range(npiece):
        nxt = up_dot(p + 1) if p + 1 < npiece else None
        i1 = k * npiece + p
        cbs = [jnp.broadcast_to(cnt1_ref[hd, pl.ds(i1, 1), :], (BF16_ROWS, tm)).astype(BF16)
               for hd in range(PEER_HEADS)]
        ebs = [jnp.broadcast_to(e1n_ref[hd, pl.ds(i1, 1), :], (BF16_ROWS, tm)).astype(BF16)
               for hd in range(PEER_HEADS)]
        act = jax.nn.gelu(raw.astype(BF16))
        for r in range(N_KEYS // BF16_ROWS):
            rows = slice(r * BF16_ROWS, (r + 1) * BF16_ROWS)
            w = None
            for hd in range(PEER_HEADS):
                t = jnp.where(rank2_ref[hd, rows, :] < cbs[hd], e2_ref[hd, rows, :], zero) * ebs[hd]
                w = t if w is None else w + t
            ys.append(act[rows, :] * w)
        raw = nxt
        if (p + 1) % per_sub == 0:
            e0 = (p + 1 - per_sub) * N_KEYS
            y = jnp.concatenate(ys, axis=0)
            ys = []
            acc_ref[...] += jnp.dot(vt_ref[:, e0:e0 + PEER_SUB], y, preferred_element_type=F32)

    @pl.when(k == pl.num_programs(1) - 1)
    def _():
        out = x_ref[...] + mod_ref[0][5:6] * acc_ref[...].T
        if final_norm:
            out = _rms(out, gfin_ref[...])
        o_ref[...] = out


def _rope_tables(s_prompt_tile, s_sample):
    n_rows = s_sample // GRID_W
    rows = jnp.repeat(jnp.arange(n_rows, dtype=F32), GRID_W)
    cols = jnp.tile(jnp.arange(GRID_W, dtype=F32), n_rows)
    axis_dim = QK_ROPE // 2
    inv_freq = ROPE_BASE ** (-jnp.arange(0, axis_dim, 2, dtype=F32) / axis_dim)
    ang = jnp.concatenate([rows[:, None] * inv_freq, cols[:, None] * inv_freq], axis=-1)
    cos, sin = jnp.cos(ang), jnp.sin(ang)
    ck = jnp.concatenate([cos, cos], axis=-1)
    sk = jnp.concatenate([-sin, sin], axis=-1)
    ident_c = jnp.ones((s_prompt_tile, QK_ROPE), F32)
    ident_s = jnp.zeros((s_prompt_tile, QK_ROPE), F32)
    return jnp.concatenate([ident_c, ck], axis=0), jnp.concatenate([ident_s, sk], axis=0)


def kernel(x_prompt, x_sample, cache_ckv_l0, cache_krope_l0, state_lru_l0, c, c_ctx, w_mod_l0, b_mod_l0, w_mod_l1, b_mod_l1, g_mix_l0, g_ffn_l0, g_mix_l1, g_ffn_l1, w_in_l0, g_q_l0, w_uq_l0, g_kv_l0, w_ukv_l0, conv_w_l0, conv_b_l0, w_rg_l0, b_rg_l0, w_ig_l0, b_ig_l0, lam_l0, w_o_l0, w_pool_l1, s_pool_l1, peer_wq_l0, peer_keys_l0, peer_u_l0, peer_v_l0, peer_wq_l1, peer_keys_l1, peer_u_l1, peer_v_l1, g_final):
    nb_p, s_p, d = x_prompt.shape
    nb_s, s_s, _ = x_sample.shape
    n_cache = cache_ckv_l0.shape[1]
    assert d == D_MODEL and s_p == TM and s_s % TM == 0 and n_cache % TM == 0
    t_p = nb_p * s_p
    t_s = nb_s * s_s
    t_all = t_p + t_s
    npt = t_p // TM
    tps = s_s // TM
    ntile = t_all // TM
    assert t_all % PEER_TM == 0 and t_p % PEER_TM == 0 and s_s % PEER_TM == 0

    x0 = jnp.concatenate([x_prompt.reshape(t_p, d), x_sample.reshape(t_s, d)], axis=0)

    ncond = 1 + nb_s
    cpad = jnp.zeros((2 * SUBLANES, d), F32).at[0].set(c_ctx).at[1:ncond].set(c)
    mod0 = _ada(cpad, w_mod_l0, b_mod_l0).reshape(2 * SUBLANES, 6, d)
    mod1 = _ada(cpad, w_mod_l1, b_mod_l1).reshape(2 * SUBLANES, 6, d)

    def cond_row(i):
        return jnp.where(i < npt, 0, 1 + (i - npt) // tps)

    mod_spec = pl.BlockSpec((1, 6, d), lambda i: (cond_row(i), 0, 0))
    row = lambda a: a.reshape(1, -1)

    perm = np.concatenate([np.arange(0, QK_ROPE, 2), np.arange(1, QK_ROPE, 2)])
    perm_sw = np.concatenate([np.arange(1, QK_ROPE, 2), np.arange(0, QK_ROPE, 2)])
    o1 = Q_LORA + KV_LORA
    w_kr = w_in_l0[:, o1:o1 + QK_ROPE]
    z64 = jnp.zeros((d, 128 - QK_ROPE), F32)
    w_in_ext = jnp.concatenate(
        [w_in_l0[:, :o1], w_kr, z64, w_kr[:, perm], z64, w_kr[:, perm_sw], z64, w_in_l0[:, o1 + QK_ROPE:]],
        axis=1).astype(BF16)
    assert w_in_ext.shape[1] == IN_EXT
    wq3 = w_uq_l0.reshape(Q_LORA, MLA_HEADS, QK_NOPE + QK_ROPE)
    w_uq_ext = jnp.concatenate(
        [wq3[:, :, :QK_NOPE].reshape(Q_LORA, -1),
         wq3[:, :, QK_NOPE:][:, :, perm].reshape(Q_LORA, -1),
         wq3[:, :, QK_NOPE:][:, :, perm_sw].reshape(Q_LORA, -1)], axis=1).astype(BF16)
    ck_tab, sk_tab = _rope_tables(TM, s_s)

    def rope_blk(i):
        return jnp.where(i < npt, 0, 1 + (i - npt) % tps)

    tok = lambda w: pl.BlockSpec((TM, w), lambda i: (i, 0))
    full = lambda a: pl.BlockSpec(a.shape, lambda *_: (0,) * a.ndim)
    q, ckv, kr, krr, ux, ug = pl.pallas_call(
        _inproj_kernel,
        grid=(ntile,),
        in_specs=[tok(d), mod_spec, full(row(g_mix_l0)), full(w_in_ext), full(row(g_q_l0)), full(w_uq_ext),
                  full(row(g_kv_l0)),
                  pl.BlockSpec((TM, QK_ROPE), lambda i: (rope_blk(i), 0)),
                  pl.BlockSpec((TM, QK_ROPE), lambda i: (rope_blk(i), 0))],
        out_specs=[pl.BlockSpec((MLA_HEADS, TM, QK_PAD), lambda i: (0, i, 0)),
                   tok(KV_LORA), tok(QK_ROPE), tok(QK_ROPE), tok(LRU_WIDTH), tok(LRU_WIDTH)],
        out_shape=[jax.ShapeDtypeStruct((MLA_HEADS, t_all, QK_PAD), BF16),
                   jax.ShapeDtypeStruct((t_all, KV_LORA), F32),
                   jax.ShapeDtypeStruct((t_all, QK_ROPE), F32),
                   jax.ShapeDtypeStruct((t_all, QK_ROPE), BF16),
                   jax.ShapeDtypeStruct((t_all, LRU_WIDTH), F32),
                   jax.ShapeDtypeStruct((t_all, LRU_WIDTH), F32)],
        compiler_params=_cparams(("arbitrary",)),
    )(x0, mod0, row(g_mix_l0), w_in_ext, row(g_q_l0), w_uq_ext, row(g_kv_l0), ck_tab, sk_tab)

    wkv3 = w_ukv_l0.reshape(KV_LORA, MLA_HEADS, QK_NOPE + V_HEAD)
    w_ukv_ext = jnp.concatenate([wkv3[:, :, :QK_NOPE].reshape(KV_LORA, -1),
                                 wkv3[:, :, QK_NOPE:].reshape(KV_LORA, -1)], axis=1).astype(BF16)

    def attn_call(nb, s_new, tile0, has_cache):
        nq = s_new // TM
        blk0 = tile0 * TM // s_new
        n_c = n_cache if has_cache else 0
        in_specs = [pl.BlockSpec((MLA_HEADS, TM, QK_PAD), lambda b, qi: (0, tile0 + b * nq + qi, 0)),
                    pl.BlockSpec((s_new, KV_LORA), lambda b, qi: (blk0 + b, 0)),
                    pl.BlockSpec((s_new, QK_ROPE), lambda b, qi: (blk0 + b, 0))]
        args = [q, ckv, krr]
        if has_cache:
            in_specs += [pl.BlockSpec((1, n_cache, KV_LORA), lambda b, qi: (b, 0, 0)),
                         pl.BlockSpec((1, n_cache, QK_ROPE), lambda b, qi: (b, 0, 0))]
            args += [cache_ckv_l0, cache_krope_l0[:, :, perm]]
        in_specs.append(pl.BlockSpec(w_ukv_ext.shape, lambda b, qi: (0, 0)))
        args.append(w_ukv_ext)
        return pl.pallas_call(
            functools.partial(_attn_kernel, has_cache=has_cache, s_new=s_new, n_cache=n_c),
            grid=(nb, nq),
            in_specs=in_specs,
            out_specs=pl.BlockSpec((TM, MLA_WIDTH), lambda b, qi: (b * nq + qi, 0)),
            out_shape=jax.ShapeDtypeStruct((nb * s_new, MLA_WIDTH), BF16),
            scratch_shapes=[pltpu.VMEM((MLA_HEADS, n_c + s_new, QK_PAD), BF16),
                            pltpu.VMEM((n_c + s_new, MLA_WIDTH), BF16)],
            compiler_params=_cparams(("arbitrary", "arbitrary")),
        )(*args)

    attn = jnp.concatenate([attn_call(nb_p, s_p, 0, False), attn_call(nb_s, s_s, npt, True)], axis=0)

    def lru_call(nb, s_new, tile0, h0):
        nc = s_new // TM
        r8 = TM // SUBLANES
        last8 = t_all // SUBLANES - 1

        def cur(rev):
            return pl.BlockSpec((TM, LRU_WIDTH),
                                lambda b, cc: (tile0 + b * nc + (nc - 1 - cc if rev else cc), 0))

        def prev(rev):
            return pl.BlockSpec((SUBLANES, LRU_WIDTH), lambda b, cc: (
                jnp.maximum((tile0 + b * nc + (nc - 1 - cc if rev else cc)) * r8 - 1, 0), 0))

        def nxt(rev):
            return pl.BlockSpec((SUBLANES, LRU_WIDTH), lambda b, cc: (
                jnp.minimum((tile0 + b * nc + (nc - 1 - cc if rev else cc) + 1) * r8, last8), 0))

        small = [conv_w_l0, row(conv_b_l0), w_rg_l0, b_rg_l0, w_ig_l0, b_ig_l0, lam_l0]
        return pl.pallas_call(
            functools.partial(_lru_kernel, nc=nc),
            grid=(nb, nc),
            in_specs=[prev(False), cur(False), nxt(False), prev(True), cur(True), nxt(True)]
                     + [full(a) for a in small]
                     + [pl.BlockSpec((1, 2, LRU_WIDTH), lambda b, cc: (b, 0, 0))],
            out_specs=[pl.BlockSpec((TM, LRU_WIDTH), lambda b, cc: (b * nc + cc, 0)),
                       pl.BlockSpec((TM, LRU_WIDTH), lambda b, cc: (b * nc + nc - 1 - cc, 0)),
                       pl.BlockSpec((1, 2, LRU_WIDTH), lambda b, cc: (b, 0, 0))],
            out_shape=[jax.ShapeDtypeStruct((nb * s_new, LRU_WIDTH), F32),
                       jax.ShapeDtypeStruct((nb * s_new, LRU_WIDTH), F32),
                       jax.ShapeDtypeStruct((nb, 2, LRU_WIDTH), F32)],
            scratch_shapes=[pltpu.VMEM((1, LRU_WIDTH), F32), pltpu.VMEM((1, LRU_WIDTH), F32)],
            compiler_params=_cparams(("arbitrary", "arbitrary")),
        )(ux, ux, ux, ux, ux, ux, *small, h0)

    hf_p, hb_p, new_lru = lru_call(nb_p, s_p, 0, jnp.zeros((nb_p, 2, LRU_WIDTH), F32))
    hf_s, hb_s, _ = lru_call(nb_s, s_s, npt, state_lru_l0.astype(F32))
    hf = jnp.concatenate([hf_p, hf_s], axis=0)
    hb = jnp.concatenate([hb_p, hb_s], axis=0)

    w_o = w_o_l0.astype(BF16)
    x1 = pl.pallas_call(
        _oproj_kernel,
        grid=(ntile,),
        in_specs=[tok(d), mod_spec, tok(MLA_WIDTH), tok(LRU_WIDTH), tok(LRU_WIDTH), tok(LRU_WIDTH),
                  pl.BlockSpec((MLA_WIDTH, d), lambda i: (0, 0)), pl.BlockSpec((LRU_WIDTH, d), lambda i: (1, 0))],
        out_specs=tok(d),
        out_shape=jax.ShapeDtypeStruct((t_all, d), F32),
        compiler_params=_cparams(("arbitrary",)),
    )(x0, mod0, attn, hf, hb, ug, w_o, w_o)

    def peer_call(x, mod, g_ffn, w_q, sub_keys, u, v, final_norm):
        n_exp = u.shape[0]
        assert n_exp == N_KEYS * N_KEYS and n_exp % PEER_EBLK == 0
        wqt = w_q.T.astype(BF16)
        keys = sub_keys.astype(BF16)
        ub = u.astype(BF16)
        vt = v.T.astype(BF16)
        tpp = PEER_TM // TM

        def cond_row_p(i):
            return cond_row(i * tpp)

        big = lambda dt: pltpu.VMEM((PEER_HEADS, N_KEYS, PEER_TM), dt)
        return pl.pallas_call(
            functools.partial(_peer_kernel, final_norm=final_norm),
            grid=(t_all // PEER_TM, n_exp // PEER_EBLK),
            in_specs=[pl.BlockSpec((PEER_TM, d), lambda i, k: (i, 0)),
                      pl.BlockSpec((1, 6, d), lambda i, k: (cond_row_p(i), 0, 0)),
                      pl.BlockSpec((1, d), lambda i, k: (0, 0)),
                      pl.BlockSpec(wqt.shape, lambda i, k: (0, 0)),
                      pl.BlockSpec(keys.shape, lambda i, k: (0, 0, 0, 0)),
                      pl.BlockSpec((PEER_EBLK, d), lambda i, k: (k, 0)),
                      pl.BlockSpec((d, PEER_EBLK), lambda i, k: (0, k)),
                      pl.BlockSpec((1, d), lambda i, k: (0, 0))],
            out_specs=pl.BlockSpec((PEER_TM, d), lambda i, k: (i, 0)),
            out_shape=jax.ShapeDtypeStruct((t_all, d), F32),
            scratch_shapes=[pltpu.VMEM((d, PEER_TM), BF16),
                            pltpu.VMEM((d, PEER_TM), F32),
                            pltpu.VMEM((2, PEER_HEADS, N_KEYS, PEER_TM), F32),
                            pltpu.VMEM((2, PEER_TOPK, PEER_HEADS, PEER_TM), F32),
                            pltpu.VMEM((PEER_TOPK + 1, PEER_HEADS, PEER_TM), F32),
                            pltpu.VMEM((2, N_KEYS, PEER_TM), F32),
                            big(F32), big(F32), big(BF16), big(BF16)],
            compiler_params=_cparams(("arbitrary", "arbitrary")),
        )(x, mod, row(g_ffn), wqt, keys, ub, vt, row(g_final))

    x2 = peer_call(x1, mod0, g_ffn_l0, peer_wq_l0, peer_keys_l0, peer_u_l0, peer_v_l0, False)

    rh = TM // HALO
    lasth = t_all // HALO - 1
    x3 = pl.pallas_call(
        functools.partial(_pool_kernel, tiles_per_seq=tps, n_prompt_tiles=npt),
        grid=(ntile,),
        in_specs=[pl.BlockSpec((HALO, d), lambda i: (jnp.maximum(i * rh - 1, 0), 0)),
                  tok(d),
                  pl.BlockSpec((HALO, d), lambda i: (jnp.minimum((i + 1) * rh, lasth), 0)),
                  mod_spec, full(row(g_mix_l1)),
                  pl.BlockSpec(w_pool_l1.shape, lambda i: (0, 0, 0)), full(row(s_pool_l1))],
        out_specs=tok(d),
        out_shape=jax.ShapeDtypeStruct((t_all, d), F32),
        compiler_params=_cparams(("arbitrary",)),
    )(x2, x2, x2, mod1, row(g_mix_l1), w_pool_l1.astype(BF16), row(s_pool_l1))

    y = peer_call(x3, mod1, g_ffn_l1, peer_wq_l1, peer_keys_l1, peer_u_l1, peer_v_l1, True)

    y_prompt = y[:t_p].reshape(nb_p, s_p, d)
    y_sample = y[t_p:].reshape(nb_s, s_s, d)
    new_ckv = ckv[:t_p].reshape(nb_p, s_p, KV_LORA)
    new_krope = kr[:t_p].reshape(nb_p, s_p, QK_ROPE)
    return (y_prompt, y_sample, new_ckv, new_krope, new_lru)
```

```python
import functools

import numpy as np
import jax
import jax.numpy as jnp
from jax import lax
from jax.experimental import pallas as pl
from jax.experimental.pallas import tpu as pltpu

F32 = jnp.float32
BF16 = jnp.bfloat16

D_MODEL = 1024
EPS = 1e-6
GRID_W = 64
MLA_HEADS = 4
Q_LORA = 384
KV_LORA = 256
QK_NOPE = 128
QK_ROPE = 64
V_HEAD = 128
MLA_WIDTH = MLA_HEADS * V_HEAD
ROPE_BASE = 10000.0
LRU_WIDTH = 512
LRU_BLOCKS = 4
LRU_BLOCK = LRU_WIDTH // LRU_BLOCKS
CONV_W = 4
CONV_LEFT = 2
LRU_C = 8.0
POOL_WINDOWS = (2, 4, 8, 16)
POOL_GROUP = D_MODEL // len(POOL_WINDOWS)
PEER_HEADS = 8
N_KEYS = 128
PEER_DKEY = 256
PEER_TOPK = 16

SUBLANES = 8
LANES = 128
VMEM_LIMIT = 56 * 1024 * 1024

TM = 256
QK_PAD = 256
HALO = 16
PEER_TM = 512
PEER_EBLK = 2048
PEER_SUB = 512
BF16_ROWS = 16
IN_EXT = 2048


def _rms(x, g):
    return x * lax.rsqrt(jnp.mean(x * x, axis=-1, keepdims=True) + EPS) * g


def _cparams(sem):
    return pltpu.CompilerParams(dimension_semantics=sem, vmem_limit_bytes=VMEM_LIMIT)


def _ada_kernel(c_ref, w_ref, b_ref, o_ref):
    c = c_ref[...]
    o_ref[...] = jnp.dot(c * jax.nn.sigmoid(c), w_ref[...], preferred_element_type=F32) + b_ref[...]


def _ada(cpad, w_mod, b_mod):
    n = w_mod.shape[1]
    bn = 768
    return pl.pallas_call(
        _ada_kernel,
        grid=(n // bn,),
        in_specs=[pl.BlockSpec(cpad.shape, lambda j: (0, 0)),
                  pl.BlockSpec((D_MODEL, bn), lambda j: (0, j)),
                  pl.BlockSpec((1, bn), lambda j: (0, j))],
        out_specs=pl.BlockSpec((cpad.shape[0], bn), lambda j: (0, j)),
        out_shape=jax.ShapeDtypeStruct((cpad.shape[0], n), F32),
        compiler_params=_cparams(("arbitrary",)),
    )(cpad, w_mod, b_mod.reshape(1, n))


def _inproj_kernel(x_ref, mod_ref, g_ref, win_ref, gq_ref, wuq_ref, gkv_ref, ck_ref, sk_ref,
                   q_ref, ckv_ref, kr_ref, krr_ref, ux_ref, ug_ref):
    mod = mod_ref[0]
    h = _rms(x_ref[...], g_ref[...]) * (1.0 + mod[1:2]) + mod[0:1]
    y = jnp.dot(h.astype(BF16), win_ref[...], preferred_element_type=F32)
    cq = y[:, 0:Q_LORA]
    ckv = y[:, Q_LORA:Q_LORA + KV_LORA]
    o = Q_LORA + KV_LORA
    kr = y[:, o:o + QK_ROPE]
    krp = y[:, o + 128:o + 128 + QK_ROPE]
    krs = y[:, o + 256:o + 256 + QK_ROPE]
    ux_ref[...] = y[:, o + 384:o + 384 + LRU_WIDTH]
    ug_ref[...] = y[:, o + 384 + LRU_WIDTH:o + 384 + 2 * LRU_WIDTH]
    ckv_ref[...] = _rms(ckv, gkv_ref[...])
    kr_ref[...] = kr
    ck = ck_ref[...]
    sk = sk_ref[...]
    krr_ref[...] = (krp * ck + krs * sk).astype(BF16)
    q = jnp.dot(_rms(cq, gq_ref[...]).astype(BF16), wuq_ref[...], preferred_element_type=F32)
    nw = MLA_HEADS * QK_NOPE
    rw = MLA_HEADS * QK_ROPE
    for hd in range(MLA_HEADS):
        qp = q[:, nw + hd * QK_ROPE:nw + (hd + 1) * QK_ROPE]
        qs = q[:, nw + rw + hd * QK_ROPE:nw + rw + (hd + 1) * QK_ROPE]
        q_ref[hd, :, 0:QK_NOPE] = q[:, hd * QK_NOPE:(hd + 1) * QK_NOPE].astype(BF16)
        q_ref[hd, :, QK_NOPE:QK_NOPE + QK_ROPE] = (qp * ck + qs * sk).astype(BF16)
        q_ref[hd, :, QK_NOPE + QK_ROPE:QK_PAD] = jnp.zeros((TM, QK_PAD - QK_NOPE - QK_ROPE), BF16)


def _attn_kernel(*refs, has_cache, s_new, n_cache):
    if has_cache:
        q_ref, ckv_ref, krr_ref, cckv_ref, ckr_ref, wukv_ref, o_ref, kcat_ref, vv_ref = refs
    else:
        q_ref, ckv_ref, krr_ref, wukv_ref, o_ref, kcat_ref, vv_ref = refs
    sk = n_cache + s_new
    kw = MLA_HEADS * QK_NOPE
    zpad = jnp.zeros((TM, QK_PAD - QK_NOPE - QK_ROPE), BF16)

    def put_keys(row0, ckv_rows, kr_rows):
        kv = jnp.dot(ckv_rows.astype(BF16), wukv_ref[...], preferred_element_type=F32)
        rows = pl.ds(row0, TM)
        for hd in range(MLA_HEADS):
            kcat_ref[hd, rows, 0:QK_NOPE] = kv[:, hd * QK_NOPE:(hd + 1) * QK_NOPE].astype(BF16)
            kcat_ref[hd, rows, QK_NOPE:QK_NOPE + QK_ROPE] = kr_rows.astype(BF16)
            kcat_ref[hd, rows, QK_NOPE + QK_ROPE:QK_PAD] = zpad
        vv_ref[rows, :] = kv[:, kw:].astype(BF16)

    @pl.when(pl.program_id(1) == 0)
    def _():
        if has_cache:
            for c in range(n_cache // TM):
                put_keys(c * TM, cckv_ref[0, c * TM:(c + 1) * TM, :], ckr_ref[0, c * TM:(c + 1) * TM, :])

        def body(c, carry):
            r0 = pl.multiple_of(c * TM, TM)
            put_keys(n_cache + r0, ckv_ref[pl.ds(r0, TM), :], krr_ref[pl.ds(r0, TM), :])
            return carry
        lax.fori_loop(0, s_new // TM, body, 0)

    scale = (QK_NOPE + QK_ROPE) ** -0.5
    for hd in range(MLA_HEADS):
        s = lax.dot_general(q_ref[hd], kcat_ref[hd], (((1,), (1,)), ((), ())),
                            preferred_element_type=F32) * scale
        m = jnp.max(s, axis=-1, keepdims=True)
        e = jnp.exp(s - m)
        l = jnp.sum(e, axis=-1, keepdims=True)
        o = jnp.dot(e.astype(BF16), vv_ref[:, hd * V_HEAD:(hd + 1) * V_HEAD], preferred_element_type=F32)
        o_ref[:, hd * V_HEAD:(hd + 1) * V_HEAD] = (o / l).astype(BF16)
    del sk


def _lru_dir(xp_ref, xc_ref, xn_ref, valid_prev, valid_next, d, reverse, cw_ref, cb_ref,
             wr_ref, br_ref, wi_ref, bi_ref, lam_ref, carry):
    xp = jnp.where(valid_prev, xp_ref[...], 0.0)
    xn = jnp.where(valid_next, xn_ref[...], 0.0)
    xx = jnp.concatenate([xp, xc_ref[...], xn], axis=0)
    n = TM + 2 * SUBLANES
    xc = cb_ref[...]
    for k in range(CONV_W):
        sh = (CONV_LEFT - k) % n
        xs = xx if sh == 0 else pltpu.roll(xx, sh, 0)
        xc = xc + xs[SUBLANES:SUBLANES + TM] * cw_ref[k:k + 1, :]
    rs, is_ = [], []
    for b in range(LRU_BLOCKS):
        xb = xc[:, b * LRU_BLOCK:(b + 1) * LRU_BLOCK]
        rs.append(jnp.dot(xb, wr_ref[d, b], preferred_element_type=F32))
        is_.append(jnp.dot(xb, wi_ref[d, b], preferred_element_type=F32))
    r = jax.nn.sigmoid(jnp.concatenate(rs, axis=-1) + br_ref[d:d + 1, :])
    i = jax.nn.sigmoid(jnp.concatenate(is_, axis=-1) + bi_ref[d:d + 1, :])
    nl = -lam_ref[d:d + 1, :]
    softplus = jnp.maximum(nl, 0.0) + jnp.log1p(jnp.exp(-jnp.abs(nl)))
    log_a = -LRU_C * r * softplus
    a = jnp.exp(log_a)
    bx = jnp.sqrt(jnp.tanh(-log_a) * (a * a + 1.0)) * (i * xc)
    t = lax.broadcasted_iota(jnp.int32, (TM, 1), 0)
    step = 1
    while step < TM:
        if reverse:
            keep = t < TM - step
            sh = TM - step
        else:
            keep = t >= step
            sh = step
        a_s = jnp.where(keep, pltpu.roll(a, sh, 0), 1.0)
        b_s = jnp.where(keep, pltpu.roll(bx, sh, 0), 0.0)
        bx = a * b_s + bx
        a = a * a_s
        step *= 2
    return a * carry + bx


def _lru_kernel(fxp, fxc, fxn, bxp, bxc, bxn, cw_ref, cb_ref, wr_ref, br_ref, wi_ref, bi_ref, lam_ref,
                h0_ref, hf_ref, hb_ref, st_ref, cf_ref, cbk_ref, *, nc):
    c = pl.program_id(1)

    @pl.when(c == 0)
    def _():
        cf_ref[...] = h0_ref[0, 0:1, :]
        cbk_ref[...] = h0_ref[0, 1:2, :]

    params = (cw_ref, cb_ref, wr_ref, br_ref, wi_ref, bi_ref, lam_ref)
    hf = _lru_dir(fxp, fxc, fxn, c > 0, c < nc - 1, 0, False, *params, cf_ref[...])
    hf_ref[...] = hf
    cf_ref[...] = hf[TM - 1:TM, :]
    hb = _lru_dir(bxp, bxc, bxn, c < nc - 1, c > 0, 1, True, *params, cbk_ref[...])
    hb_ref[...] = hb
    cbk_ref[...] = hb[0:1, :]
    st_ref[0, 0:1, :] = hf[TM - 1:TM, :]
    st_ref[0, 1:2, :] = hb[0:1, :]


def _oproj_kernel(x_ref, mod_ref, at_ref, hf_ref, hb_ref, ug_ref, woa_ref, wor_ref, o_ref):
    mod = mod_ref[0]
    rec = ((hf_ref[...] + hb_ref[...]) * jax.nn.gelu(ug_ref[...])).astype(BF16)
    out = (jnp.dot(at_ref[...], woa_ref[...], preferred_element_type=F32)
           + jnp.dot(rec, wor_ref[...], preferred_element_type=F32))
    o_ref[...] = x_ref[...] + mod[2:3] * out


def _pool_kernel(xp_ref, xc_ref, xn_ref, mod_ref, g_ref, wp_ref, sp_ref, o_ref, *, tiles_per_seq, n_prompt_tiles):
    i = pl.program_id(0)
    j = jnp.where(i < n_prompt_tiles, 0, (i - n_prompt_tiles) % tiles_per_seq)
    ntile = jnp.where(i < n_prompt_tiles, 1, tiles_per_seq)
    mod = mod_ref[0]
    g = g_ref[...]

    def hmod(x):
        return _rms(x, g) * (1.0 + mod[1:2]) + mod[0:1]

    x = xc_ref[...]
    hc = hmod(x)
    hp = jnp.where(j > 0, hmod(xp_ref[...]), 0.0)
    hn = jnp.where(j < ntile - 1, hmod(xn_ref[...]), 0.0)
    hh = jnp.concatenate([hp, hc, hn], axis=0)
    n = TM + 2 * HALO
    seq_len = ntile * TM
    t = j * TM + lax.broadcasted_iota(jnp.int32, (TM, 1), 0)
    ys = []
    for gi, w in enumerate(POOL_WINDOWS):
        cols = slice(gi * POOL_GROUP, (gi + 1) * POOL_GROUP)
        p = hh[:, cols]
        p = p + pltpu.roll(p, 1, 0)
        half = 1
        while 2 * half < w:
            p = pltpu.roll(p, half, 0) + pltpu.roll(p, n - half, 0)
            half *= 2
        lo = jnp.maximum(t - w // 2, 0)
        hi = jnp.minimum(t + (w - w // 2), seq_len)
        mean = p[HALO:HALO + TM] / (hi - lo).astype(F32)
        dg = (mean - hc[:, cols]).astype(BF16)
        ys.append(jnp.dot(dg, wp_ref[gi], preferred_element_type=F32))
    y = jnp.concatenate(ys, axis=-1) * sp_ref[...]
    o_ref[...] = x + mod[2:3] * y


def _sort_pairs(n):
    pairs = []

    def merge(lo, cnt, r):
        step = r * 2
        if step < cnt:
            merge(lo, cnt, step)
            merge(lo + r, cnt, step)
            for i in range(lo + r, lo + cnt - r, step):
                pairs.append((i, i + r))
        else:
            pairs.append((lo, lo + r))

    def sort(lo, cnt):
        if cnt > 1:
            m = cnt // 2
            sort(lo, m)
            sort(lo + m, m)
            merge(lo, cnt, 1)

    sort(0, n)
    return pairs


_SORT16 = _sort_pairs(PEER_TOPK)
_HYPER = [(a, b) for a in range(PEER_TOPK) for b in range(PEER_TOPK) if (a + 1) * (b + 1) <= PEER_TOPK]


def _top16_sorted(s):
    k = PEER_TOPK
    x = [s[SUBLANES * r:SUBLANES * (r + 1), :] for r in range(N_KEYS // SUBLANES)]
    for (i, j) in _SORT16:
        hi = jnp.maximum(x[i], x[j])
        lo = jnp.minimum(x[i], x[j])
        x[i], x[j] = hi, lo
    for shift in (4, 2, 1):
        y = [jnp.maximum(x[r], pltpu.roll(x[k - 1 - r], shift, 0)) for r in range(k)]
        stride = k // 2
        while stride >= 1:
            for i in range(k):
                if i & stride == 0:
                    hi = jnp.maximum(y[i], y[i + stride])
                    lo = jnp.minimum(y[i], y[i + stride])
                    y[i], y[i + stride] = hi, lo
            stride //= 2
        x = y
    return x


def _pair_counts(sv1, sv2):
    one = jnp.ones_like(sv1[0])
    zero = jnp.zeros_like(sv1[0])
    cand = [sv1[a] + sv2[b] for (a, b) in _HYPER]

    def ordered(i, j):
        (ai, bi), (aj, bj) = _HYPER[i], _HYPER[j]
        return (aj <= ai and bj <= bi) or (ai <= aj and bi <= bj)

    nh = len(_HYPER)
    cnt = [float((a + 1) * (b + 1) - 1 + sum(1 for j in range(i + 1, nh) if not ordered(i, j))) * one
           for i, (a, b) in enumerate(_HYPER)]
    for i in range(nh):
        for jx in range(i):
            if ordered(i, jx):
                continue
            ge = jnp.where(cand[jx] >= cand[i], one, zero)
            cnt[i] = cnt[i] + ge
            cnt[jx] = cnt[jx] - ge
    e1 = [jnp.exp(sv1[a] - sv1[0]) for a in range(PEER_TOPK)]
    e2 = [jnp.exp(sv2[b] - sv2[0]) for b in range(PEER_TOPK)]
    n = [zero for _ in range(PEER_TOPK)]
    z = zero
    for i, (a, b) in enumerate(_HYPER):
        sel = jnp.where(cnt[i] < float(PEER_TOPK), one, zero)
        n[a] = n[a] + sel
        z = z + sel * (e1[a] * e2[b])
    return n, 1.0 / z


def _head_tables(hd, sc_ref, sv_ref, nz_ref, ex_ref, cnt1_ref, e1n_ref, rank2_ref, e2_ref, exact):
    k = PEER_TOPK
    tm = sc_ref.shape[-1]
    nslab = N_KEYS // SUBLANES
    bad = []
    for g in range(tm // LANES):
        lanes = slice(g * LANES, (g + 1) * LANES)

        def bc(ref, *idx):
            return jnp.broadcast_to(ref[idx + (slice(hd, hd + 1), lanes)], (SUBLANES, LANES))

        sv1b = [bc(sv_ref, 0, a) for a in range(k)]
        nb = [bc(nz_ref, a) for a in range(k)]
        izb = bc(nz_ref, k)
        c1 = jnp.zeros((SUBLANES, LANES), F32)
        for r in range(nslab):
            rows = slice(r * SUBLANES, (r + 1) * SUBLANES)
            s1 = sc_ref[0, hd, rows, lanes]
            cnt1 = jnp.zeros_like(s1)
            if exact:
                rank1 = jnp.zeros_like(s1)
                for a in range(k):
                    rank1 = jnp.where(sv1b[a] > s1, float(a + 1), rank1)
                rank1 = rank1 + ex_ref[0, rows, lanes]
                for a in range(k):
                    cnt1 = jnp.where(rank1 == float(a), nb[a], cnt1)
            else:
                for a in range(k):
                    cnt1 = jnp.where(s1 == sv1b[a], nb[a], cnt1)
                c1 = c1 + jnp.where(s1 >= sv1b[k - 1], 1.0, 0.0)
            cnt1_ref[hd, rows, lanes] = cnt1
            e1n_ref[hd, rows, lanes] = jnp.exp(s1 - sv1b[0]) * izb
        sv2b = [bc(sv_ref, 1, a) for a in range(k)]
        c2 = jnp.zeros((SUBLANES, LANES), F32)
        for r in range(N_KEYS // BF16_ROWS):
            rk, e2 = [], []
            for q in range(BF16_ROWS // SUBLANES):
                r0 = r * BF16_ROWS + q * SUBLANES
                rows = slice(r0, r0 + SUBLANES)
                s2 = sc_ref[1, hd, rows, lanes]
                rank2 = jnp.zeros_like(s2)
                for a in range(k):
                    rank2 = jnp.where(sv2b[a] > s2, float(a + 1), rank2)
                if exact:
                    rank2 = jnp.minimum(rank2 + ex_ref[1, rows, lanes], float(k))
                else:
                    c2 = c2 + jnp.where(rank2 < float(k), 1.0, 0.0)
                rk.append(rank2)
                e2.append(jnp.exp(s2 - sv2b[0]))
            rows16 = slice(r * BF16_ROWS, (r + 1) * BF16_ROWS)
            rank2_ref[hd, rows16, lanes] = jnp.concatenate(rk, axis=0).astype(BF16)
            e2_ref[hd, rows16, lanes] = jnp.concatenate(e2, axis=0).astype(BF16)
        if not exact:
            n1 = jnp.sum(c1, axis=0, keepdims=True)
            n2 = jnp.sum(c2, axis=0, keepdims=True)
            b = jnp.where(n1 != float(k), 1.0, 0.0) + jnp.where(n2 != float(k), 1.0, 0.0)
            for svb in (sv1b, sv2b):
                for a in range(k - 1):
                    b = b + jnp.where(svb[a][0:1] == svb[a + 1][0:1], 1.0, 0.0)
            bad.append(b)
    return jnp.concatenate(bad, axis=1) if bad else None


def _tie_offsets(hd, sc_ref, ex_ref):
    tm = sc_ref.shape[-1]
    nidx = lax.broadcasted_iota(jnp.int32, (N_KEYS, tm), 0)
    for p in range(2):
        s = sc_ref[p, hd]

        def body(m, e, s=s, p=p):
            row = sc_ref[p, hd, pl.ds(m, 1), :]
            return e + jnp.where((s == row) & (nidx > m), 1.0, 0.0)

        ex_ref[p] = lax.fori_loop(0, N_KEYS, body, jnp.zeros((N_KEYS, tm), F32))


def _peer_kernel(x_ref, mod_ref, g_ref, wqt_ref, keys_ref, u_ref, vt_ref, gfin_ref, o_ref,
                 h2t_ref, acc_ref, sc_ref, sv_ref, nz_ref, ex_ref, cnt1_ref, e1n_ref, rank2_ref, e2_ref,
                 *, final_norm):
    k = pl.program_id(1)
    tm = PEER_TM
    half = PEER_DKEY // 2

    @pl.when(k == 0)
    def _():
        mod = mod_ref[0]
        h2 = _rms(x_ref[...], g_ref[...]) * (1.0 + mod[4:5]) + mod[3:4]
        h2t_ref[...] = h2.T.astype(BF16)
        acc_ref[...] = jnp.zeros_like(acc_ref)
        qt = jnp.dot(wqt_ref[...], h2t_ref[...], preferred_element_type=F32).astype(BF16)
        for hd in range(PEER_HEADS):
            for p in range(2):
                r0 = (hd * 2 + p) * half
                sc_ref[p, hd] = jnp.dot(keys_ref[hd, p], qt[r0:r0 + half, :],
                                        preferred_element_type=F32)
                for g in range(tm // LANES):
                    lanes = slice(g * LANES, (g + 1) * LANES)
                    top = _top16_sorted(sc_ref[p, hd, :, lanes])
                    for a in range(PEER_TOPK):
                        sv_ref[p, a, hd:hd + 1, lanes] = top[a][0:1, :]
        for g in range(tm // LANES):
            lanes = slice(g * LANES, (g + 1) * LANES)
            sv1 = [sv_ref[0, a, :, lanes] for a in range(PEER_TOPK)]
            sv2 = [sv_ref[1, a, :, lanes] for a in range(PEER_TOPK)]
            n, inv_z = _pair_counts(sv1, sv2)
            for a in range(PEER_TOPK):
                nz_ref[a, :, lanes] = n[a]
            nz_ref[PEER_TOPK, :, lanes] = inv_z
        tabs = (sc_ref, sv_ref, nz_ref, ex_ref, cnt1_ref, e1n_ref, rank2_ref, e2_ref)
        for hd in range(PEER_HEADS):
            bad = _head_tables(hd, *tabs, exact=False)

            @pl.when(jnp.max(bad) > 0.0)
            def _(hd=hd):
                _tie_offsets(hd, sc_ref, ex_ref)
                _head_tables(hd, *tabs, exact=True)

    zero = jnp.zeros((BF16_ROWS, tm), BF16)
    npiece = PEER_EBLK // N_KEYS
    per_sub = PEER_SUB // N_KEYS

    def up_dot(q):
        return jnp.dot(u_ref[2 * q * N_KEYS:2 * (q + 1) * N_KEYS, :], h2t_ref[...], preferred_element_type=F32)

    raw2 = up_dot(0)
    ys = []
    for p in range(npiece):
        if p % 2 == 0:
            nxt = up_dot(p // 2 + 1) if p + 2 < npiece else None
        raw = raw2[(p % 2) * N_KEYS:(p % 2 + 1) * N_KEYS, :]
        i1 = k * npiece + p
        cbs = [jnp.broadcast_to(cnt1_ref[hd, pl.ds(i1, 1), :], (BF16_ROWS, tm)).astype(BF16)
               for hd in range(PEER_HEADS)]
        ebs = [jnp.broadcast_to(e1n_ref[hd, pl.ds(i1, 1), :], (BF16_ROWS, tm)).astype(BF16)
               for hd in range(PEER_HEADS)]
        act = jax.nn.gelu(raw.astype(BF16))
        for r in range(N_KEYS // BF16_ROWS):
            rows = slice(r * BF16_ROWS, (r + 1) * BF16_ROWS)
            w = None
            for hd in range(PEER_HEADS):
                t = jnp.where(rank2_ref[hd, rows, :] < cbs[hd], e2_ref[hd, rows, :], zero) * ebs[hd]
                w = t if w is None else w + t
            ys.append(act[rows, :] * w)
        if p % 2 == 1:
            raw2 = nxt
        if (p + 1) % per_sub == 0:
            e0 = (p + 1 - per_sub) * N_KEYS
            y = jnp.concatenate(ys, axis=0)
            ys = []
            acc_ref[...] += jnp.dot(vt_ref[:, e0:e0 + PEER_SUB], y, preferred_element_type=F32)

    @pl.when(k == pl.num_programs(1) - 1)
    def _():
        out = x_ref[...] + mod_ref[0][5:6] * acc_ref[...].T
        if final_norm:
            out = _rms(out, gfin_ref[...])
        o_ref[...] = out


def _rope_tables(s_prompt_tile, s_sample):
    n_rows = s_sample // GRID_W
    rows = jnp.repeat(jnp.arange(n_rows, dtype=F32), GRID_W)
    cols = jnp.tile(jnp.arange(GRID_W, dtype=F32), n_rows)
    axis_dim = QK_ROPE // 2
    inv_freq = ROPE_BASE ** (-jnp.arange(0, axis_dim, 2, dtype=F32) / axis_dim)
    ang = jnp.concatenate([rows[:, None] * inv_freq, cols[:, None] * inv_freq], axis=-1)
    cos, sin = jnp.cos(ang), jnp.sin(ang)
    ck = jnp.concatenate([cos, cos], axis=-1)
    sk = jnp.concatenate([-sin, sin], axis=-1)
    ident_c = jnp.ones((s_prompt_tile, QK_ROPE), F32)
    ident_s = jnp.zeros((s_prompt_tile, QK_ROPE), F32)
    return jnp.concatenate([ident_c, ck], axis=0), jnp.concatenate([ident_s, sk], axis=0)


def kernel(x_prompt, x_sample, cache_ckv_l0, cache_krope_l0, state_lru_l0, c, c_ctx, w_mod_l0, b_mod_l0, w_mod_l1, b_mod_l1, g_mix_l0, g_ffn_l0, g_mix_l1, g_ffn_l1, w_in_l0, g_q_l0, w_uq_l0, g_kv_l0, w_ukv_l0, conv_w_l0, conv_b_l0, w_rg_l0, b_rg_l0, w_ig_l0, b_ig_l0, lam_l0, w_o_l0, w_pool_l1, s_pool_l1, peer_wq_l0, peer_keys_l0, peer_u_l0, peer_v_l0, peer_wq_l1, peer_keys_l1, peer_u_l1, peer_v_l1, g_final):
    nb_p, s_p, d = x_prompt.shape
    nb_s, s_s, _ = x_sample.shape
    n_cache = cache_ckv_l0.shape[1]
    assert d == D_MODEL and s_p == TM and s_s % TM == 0 and n_cache % TM == 0
    t_p = nb_p * s_p
    t_s = nb_s * s_s
    t_all = t_p + t_s
    npt = t_p // TM
    tps = s_s // TM
    ntile = t_all // TM
    assert t_all % PEER_TM == 0 and t_p % PEER_TM == 0 and s_s % PEER_TM == 0

    x0 = jnp.concatenate([x_prompt.reshape(t_p, d), x_sample.reshape(t_s, d)], axis=0)

    ncond = 1 + nb_s
    cpad = jnp.zeros((2 * SUBLANES, d), F32).at[0].set(c_ctx).at[1:ncond].set(c)
    mod0 = _ada(cpad, w_mod_l0, b_mod_l0).reshape(2 * SUBLANES, 6, d)
    mod1 = _ada(cpad, w_mod_l1, b_mod_l1).reshape(2 * SUBLANES, 6, d)

    def cond_row(i):
        return jnp.where(i < npt, 0, 1 + (i - npt) // tps)

    mod_spec = pl.BlockSpec((1, 6, d), lambda i: (cond_row(i), 0, 0))
    row = lambda a: a.reshape(1, -1)

    perm = np.concatenate([np.arange(0, QK_ROPE, 2), np.arange(1, QK_ROPE, 2)])
    perm_sw = np.concatenate([np.arange(1, QK_ROPE, 2), np.arange(0, QK_ROPE, 2)])
    o1 = Q_LORA + KV_LORA
    w_kr = w_in_l0[:, o1:o1 + QK_ROPE]
    z64 = jnp.zeros((d, 128 - QK_ROPE), F32)
    w_in_ext = jnp.concatenate(
        [w_in_l0[:, :o1], w_kr, z64, w_kr[:, perm], z64, w_kr[:, perm_sw], z64, w_in_l0[:, o1 + QK_ROPE:]],
        axis=1).astype(BF16)
    assert w_in_ext.shape[1] == IN_EXT
    wq3 = w_uq_l0.reshape(Q_LORA, MLA_HEADS, QK_NOPE + QK_ROPE)
    w_uq_ext = jnp.concatenate(
        [wq3[:, :, :QK_NOPE].reshape(Q_LORA, -1),
         wq3[:, :, QK_NOPE:][:, :, perm].reshape(Q_LORA, -1),
         wq3[:, :, QK_NOPE:][:, :, perm_sw].reshape(Q_LORA, -1)], axis=1).astype(BF16)
    ck_tab, sk_tab = _rope_tables(TM, s_s)

    def rope_blk(i):
        return jnp.where(i < npt, 0, 1 + (i - npt) % tps)

    tok = lambda w: pl.BlockSpec((TM, w), lambda i: (i, 0))
    full = lambda a: pl.BlockSpec(a.shape, lambda *_: (0,) * a.ndim)
    q, ckv, kr, krr, ux, ug = pl.pallas_call(
        _inproj_kernel,
        grid=(ntile,),
        in_specs=[tok(d), mod_spec, full(row(g_mix_l0)), full(w_in_ext), full(row(g_q_l0)), full(w_uq_ext),
                  full(row(g_kv_l0)),
                  pl.BlockSpec((TM, QK_ROPE), lambda i: (rope_blk(i), 0)),
                  pl.BlockSpec((TM, QK_ROPE), lambda i: (rope_blk(i), 0))],
        out_specs=[pl.BlockSpec((MLA_HEADS, TM, QK_PAD), lambda i: (0, i, 0)),
                   tok(KV_LORA), tok(QK_ROPE), tok(QK_ROPE), tok(LRU_WIDTH), tok(LRU_WIDTH)],
        out_shape=[jax.ShapeDtypeStruct((MLA_HEADS, t_all, QK_PAD), BF16),
                   jax.ShapeDtypeStruct((t_all, KV_LORA), F32),
                   jax.ShapeDtypeStruct((t_all, QK_ROPE), F32),
                   jax.ShapeDtypeStruct((t_all, QK_ROPE), BF16),
                   jax.ShapeDtypeStruct((t_all, LRU_WIDTH), F32),
                   jax.ShapeDtypeStruct((t_all, LRU_WIDTH), F32)],
        compiler_params=_cparams(("arbitrary",)),
    )(x0, mod0, row(g_mix_l0), w_in_ext, row(g_q_l0), w_uq_ext, row(g_kv_l0), ck_tab, sk_tab)

    wkv3 = w_ukv_l0.reshape(KV_LORA, MLA_HEADS, QK_NOPE + V_HEAD)
    w_ukv_ext = jnp.concatenate([wkv3[:, :, :QK_NOPE].reshape(KV_LORA, -1),
                                 wkv3[:, :, QK_NOPE:].reshape(KV_LORA, -1)], axis=1).astype(BF16)

    def attn_call(nb, s_new, tile0, has_cache):
        nq = s_new // TM
        blk0 = tile0 * TM // s_new
        n_c = n_cache if has_cache else 0
        in_specs = [pl.BlockSpec((MLA_HEADS, TM, QK_PAD), lambda b, qi: (0, tile0 + b * nq + qi, 0)),
                    pl.BlockSpec((s_new, KV_LORA), lambda b, qi: (blk0 + b, 0)),
                    pl.BlockSpec((s_new, QK_ROPE), lambda b, qi: (blk0 + b, 0))]
        args = [q, ckv, krr]
        if has_cache:
            in_specs += [pl.BlockSpec((1, n_cache, KV_LORA), lambda b, qi: (b, 0, 0)),
                         pl.BlockSpec((1, n_cache, QK_ROPE), lambda b, qi: (b, 0, 0))]
            args += [cache_ckv_l0, cache_krope_l0[:, :, perm]]
        in_specs.append(pl.BlockSpec(w_ukv_ext.shape, lambda b, qi: (0, 0)))
        args.append(w_ukv_ext)
        return pl.pallas_call(
            functools.partial(_attn_kernel, has_cache=has_cache, s_new=s_new, n_cache=n_c),
            grid=(nb, nq),
            in_specs=in_specs,
            out_specs=pl.BlockSpec((TM, MLA_WIDTH), lambda b, qi: (b * nq + qi, 0)),
            out_shape=jax.ShapeDtypeStruct((nb * s_new, MLA_WIDTH), BF16),
            scratch_shapes=[pltpu.VMEM((MLA_HEADS, n_c + s_new, QK_PAD), BF16),
                            pltpu.VMEM((n_c + s_new, MLA_WIDTH), BF16)],
            compiler_params=_cparams(("arbitrary", "arbitrary")),
        )(*args)

    attn = jnp.concatenate([attn_call(nb_p, s_p, 0, False), attn_call(nb_s, s_s, npt, True)], axis=0)

    def lru_call(nb, s_new, tile0, h0):
        nc = s_new // TM
        r8 = TM // SUBLANES
        last8 = t_all // SUBLANES - 1

        def cur(rev):
            return pl.BlockSpec((TM, LRU_WIDTH),
                                lambda b, cc: (tile0 + b * nc + (nc - 1 - cc if rev else cc), 0))

        def prev(rev):
            return pl.BlockSpec((SUBLANES, LRU_WIDTH), lambda b, cc: (
                jnp.maximum((tile0 + b * nc + (nc - 1 - cc if rev else cc)) * r8 - 1, 0), 0))

        def nxt(rev):
            return pl.BlockSpec((SUBLANES, LRU_WIDTH), lambda b, cc: (
                jnp.minimum((tile0 + b * nc + (nc - 1 - cc if rev else cc) + 1) * r8, last8), 0))

        small = [conv_w_l0, row(conv_b_l0), w_rg_l0, b_rg_l0, w_ig_l0, b_ig_l0, lam_l0]
        return pl.pallas_call(
            functools.partial(_lru_kernel, nc=nc),
            grid=(nb, nc),
            in_specs=[prev(False), cur(False), nxt(False), prev(True), cur(True), nxt(True)]
                     + [full(a) for a in small]
                     + [pl.BlockSpec((1, 2, LRU_WIDTH), lambda b, cc: (b, 0, 0))],
            out_specs=[pl.BlockSpec((TM, LRU_WIDTH), lambda b, cc: (b * nc + cc, 0)),
                       pl.BlockSpec((TM, LRU_WIDTH), lambda b, cc: (b * nc + nc - 1 - cc, 0)),
                       pl.BlockSpec((1, 2, LRU_WIDTH), lambda b, cc: (b, 0, 0))],
            out_shape=[jax.ShapeDtypeStruct((nb * s_new, LRU_WIDTH), F32),
                       jax.ShapeDtypeStruct((nb * s_new, LRU_WIDTH), F32),
                       jax.ShapeDtypeStruct((nb, 2, LRU_WIDTH), F32)],
            scratch_shapes=[pltpu.VMEM((1, LRU_WIDTH), F32), pltpu.VMEM((1, LRU_WIDTH), F32)],
            compiler_params=_cparams(("arbitrary", "arbitrary")),
        )(ux, ux, ux, ux, ux, ux, *small, h0)

    hf_p, hb_p, new_lru = lru_call(nb_p, s_p, 0, jnp.zeros((nb_p, 2, LRU_WIDTH), F32))
    hf_s, hb_s, _ = lru_call(nb_s, s_s, npt, state_lru_l0.astype(F32))
    hf = jnp.concatenate([hf_p, hf_s], axis=0)
    hb = jnp.concatenate([hb_p, hb_s], axis=0)

    w_o = w_o_l0.astype(BF16)
    x1 = pl.pallas_call(
        _oproj_kernel,
        grid=(ntile,),
        in_specs=[tok(d), mod_spec, tok(MLA_WIDTH), tok(LRU_WIDTH), tok(LRU_WIDTH), tok(LRU_WIDTH),
                  pl.BlockSpec((MLA_WIDTH, d), lambda i: (0, 0)), pl.BlockSpec((LRU_WIDTH, d), lambda i: (1, 0))],
        out_specs=tok(d),
        out_shape=jax.ShapeDtypeStruct((t_all, d), F32),
        compiler_params=_cparams(("arbitrary",)),
    )(x0, mod0, attn, hf, hb, ug, w_o, w_o)

    def peer_call(x, mod, g_ffn, w_q, sub_keys, u, v, final_norm):
        n_exp = u.shape[0]
        assert n_exp == N_KEYS * N_KEYS and n_exp % PEER_EBLK == 0
        wqt = w_q.T.astype(BF16)
        keys = sub_keys.astype(BF16)
        ub = u.astype(BF16)
        vt = v.T.astype(BF16)
        tpp = PEER_TM // TM

        def cond_row_p(i):
            return cond_row(i * tpp)

        big = lambda dt: pltpu.VMEM((PEER_HEADS, N_KEYS, PEER_TM), dt)
        return pl.pallas_call(
            functools.partial(_peer_kernel, final_norm=final_norm),
            grid=(t_all // PEER_TM, n_exp // PEER_EBLK),
            in_specs=[pl.BlockSpec((PEER_TM, d), lambda i, k: (i, 0)),
                      pl.BlockSpec((1, 6, d), lambda i, k: (cond_row_p(i), 0, 0)),
                      pl.BlockSpec((1, d), lambda i, k: (0, 0)),
                      pl.BlockSpec(wqt.shape, lambda i, k: (0, 0)),
                      pl.BlockSpec(keys.shape, lambda i, k: (0, 0, 0, 0)),
                      pl.BlockSpec((PEER_EBLK, d), lambda i, k: (k, 0)),
                      pl.BlockSpec((d, PEER_EBLK), lambda i, k: (0, k)),
                      pl.BlockSpec((1, d), lambda i, k: (0, 0))],
            out_specs=pl.BlockSpec((PEER_TM, d), lambda i, k: (i, 0)),
            out_shape=jax.ShapeDtypeStruct((t_all, d), F32),
            scratch_shapes=[pltpu.VMEM((d, PEER_TM), BF16),
                            pltpu.VMEM((d, PEER_TM), F32),
                            pltpu.VMEM((2, PEER_HEADS, N_KEYS, PEER_TM), F32),
                            pltpu.VMEM((2, PEER_TOPK, PEER_HEADS, PEER_TM), F32),
                            pltpu.VMEM((PEER_TOPK + 1, PEER_HEADS, PEER_TM), F32),
                            pltpu.VMEM((2, N_KEYS, PEER_TM), F32),
                            big(F32), big(F32), big(BF16), big(BF16)],
            compiler_params=_cparams(("arbitrary", "arbitrary")),
        )(x, mod, row(g_ffn), wqt, keys, ub, vt, row(g_final))

    x2 = peer_call(x1, mod0, g_ffn_l0, peer_wq_l0, peer_keys_l0, peer_u_l0, peer_v_l0, False)

    rh = TM // HALO
    lasth = t_all // HALO - 1
    x3 = pl.pallas_call(
        functools.partial(_pool_kernel, tiles_per_seq=tps, n_prompt_tiles=npt),
        grid=(ntile,),
        in_specs=[pl.BlockSpec((HALO, d), lambda i: (jnp.maximum(i * rh - 1, 0), 0)),
                  tok(d),
                  pl.BlockSpec((HALO, d), lambda i: (jnp.minimum((i + 1) * rh, lasth), 0)),
                  mod_spec, full(row(g_mix_l1)),
                  pl.BlockSpec(w_pool_l1.shape, lambda i: (0, 0, 0)), full(row(s_pool_l1))],
        out_specs=tok(d),
        out_shape=jax.ShapeDtypeStruct((t_all, d), F32),
        compiler_params=_cparams(("arbitrary",)),
    )(x2, x2, x2, mod1, row(g_mix_l1), w_pool_l1.astype(BF16), row(s_pool_l1))

    y = peer_call(x3, mod1, g_ffn_l1, peer_wq_l1, peer_keys_l1, peer_u_l1, peer_v_l1, True)

    y_prompt = y[:t_p].reshape(nb_p, s_p, d)
    y_sample = y[t_p:].reshape(nb_s, s_s, d)
    new_ckv = ckv[:t_p].reshape(nb_p, s_p, KV_LORA)
    new_krope = kr[:t_p].reshape(nb_p, s_p, QK_ROPE)
    return (y_prompt, y_sample, new_ckv, new_krope, new_lru)
```

```python
import functools

import numpy as np
import jax
import jax.numpy as jnp
from jax import lax
from jax.experimental import pallas as pl
from jax.experimental.pallas import tpu as pltpu

F32 = jnp.float32
BF16 = jnp.bfloat16

D_MODEL = 1024
EPS = 1e-6
GRID_W = 64
MLA_HEADS = 4
Q_LORA = 384
KV_LORA = 256
QK_NOPE = 128
QK_ROPE = 64
V_HEAD = 128
MLA_WIDTH = MLA_HEADS * V_HEAD
ROPE_BASE = 10000.0
LRU_WIDTH = 512
LRU_BLOCKS = 4
LRU_BLOCK = LRU_WIDTH // LRU_BLOCKS
CONV_W = 4
CONV_LEFT = 2
LRU_C = 8.0
POOL_WINDOWS = (2, 4, 8, 16)
POOL_GROUP = D_MODEL // len(POOL_WINDOWS)
PEER_HEADS = 8
N_KEYS = 128
PEER_DKEY = 256
PEER_TOPK = 16

SUBLANES = 8
LANES = 128
VMEM_LIMIT = 56 * 1024 * 1024

TM = 256
QK_PAD = 256
HALO = 16
PEER_TM = 512
PEER_EBLK = 2048
PEER_SUB = 512
PEER_UP = 512
BF16_ROWS = 16
IN_EXT = 2048


def _rms(x, g):
    return x * lax.rsqrt(jnp.mean(x * x, axis=-1, keepdims=True) + EPS) * g


def _cparams(sem):
    return pltpu.CompilerParams(dimension_semantics=sem, vmem_limit_bytes=VMEM_LIMIT)


def _ada_kernel(c_ref, w_ref, b_ref, o_ref):
    c = c_ref[...]
    o_ref[...] = jnp.dot(c * jax.nn.sigmoid(c), w_ref[...], preferred_element_type=F32) + b_ref[...]


def _ada(cpad, w_mod, b_mod):
    n = w_mod.shape[1]
    bn = 768
    return pl.pallas_call(
        _ada_kernel,
        grid=(n // bn,),
        in_specs=[pl.BlockSpec(cpad.shape, lambda j: (0, 0)),
                  pl.BlockSpec((D_MODEL, bn), lambda j: (0, j)),
                  pl.BlockSpec((1, bn), lambda j: (0, j))],
        out_specs=pl.BlockSpec((cpad.shape[0], bn), lambda j: (0, j)),
        out_shape=jax.ShapeDtypeStruct((cpad.shape[0], n), F32),
        compiler_params=_cparams(("arbitrary",)),
    )(cpad, w_mod, b_mod.reshape(1, n))


def _inproj_kernel(x_ref, mod_ref, g_ref, win_ref, gq_ref, wuq_ref, gkv_ref, ck_ref, sk_ref,
                   q_ref, ckv_ref, kr_ref, krr_ref, ux_ref, ug_ref):
    mod = mod_ref[0]
    h = _rms(x_ref[...], g_ref[...]) * (1.0 + mod[1:2]) + mod[0:1]
    y = jnp.dot(h.astype(BF16), win_ref[...], preferred_element_type=F32)
    cq = y[:, 0:Q_LORA]
    ckv = y[:, Q_LORA:Q_LORA + KV_LORA]
    o = Q_LORA + KV_LORA
    kr = y[:, o:o + QK_ROPE]
    krp = y[:, o + 128:o + 128 + QK_ROPE]
    krs = y[:, o + 256:o + 256 + QK_ROPE]
    ux_ref[...] = y[:, o + 384:o + 384 + LRU_WIDTH]
    ug_ref[...] = y[:, o + 384 + LRU_WIDTH:o + 384 + 2 * LRU_WIDTH]
    ckv_ref[...] = _rms(ckv, gkv_ref[...])
    kr_ref[...] = kr
    ck = ck_ref[...]
    sk = sk_ref[...]
    krr_ref[...] = (krp * ck + krs * sk).astype(BF16)
    q = jnp.dot(_rms(cq, gq_ref[...]).astype(BF16), wuq_ref[...], preferred_element_type=F32)
    nw = MLA_HEADS * QK_NOPE
    rw = MLA_HEADS * QK_ROPE
    for hd in range(MLA_HEADS):
        qp = q[:, nw + hd * QK_ROPE:nw + (hd + 1) * QK_ROPE]
        qs = q[:, nw + rw + hd * QK_ROPE:nw + rw + (hd + 1) * QK_ROPE]
        q_ref[hd, :, 0:QK_NOPE] = q[:, hd * QK_NOPE:(hd + 1) * QK_NOPE].astype(BF16)
        q_ref[hd, :, QK_NOPE:QK_NOPE + QK_ROPE] = (qp * ck + qs * sk).astype(BF16)
        q_ref[hd, :, QK_NOPE + QK_ROPE:QK_PAD] = jnp.zeros((TM, QK_PAD - QK_NOPE - QK_ROPE), BF16)


def _attn_kernel(*refs, has_cache, s_new, n_cache):
    if has_cache:
        q_ref, ckv_ref, krr_ref, cckv_ref, ckr_ref, wukv_ref, o_ref, kcat_ref, vv_ref = refs
    else:
        q_ref, ckv_ref, krr_ref, wukv_ref, o_ref, kcat_ref, vv_ref = refs
    sk = n_cache + s_new
    kw = MLA_HEADS * QK_NOPE
    zpad = jnp.zeros((TM, QK_PAD - QK_NOPE - QK_ROPE), BF16)

    def put_keys(row0, ckv_rows, kr_rows):
        kv = jnp.dot(ckv_rows.astype(BF16), wukv_ref[...], preferred_element_type=F32)
        rows = pl.ds(row0, TM)
        for hd in range(MLA_HEADS):
            kcat_ref[hd, rows, 0:QK_NOPE] = kv[:, hd * QK_NOPE:(hd + 1) * QK_NOPE].astype(BF16)
            kcat_ref[hd, rows, QK_NOPE:QK_NOPE + QK_ROPE] = kr_rows.astype(BF16)
            kcat_ref[hd, rows, QK_NOPE + QK_ROPE:QK_PAD] = zpad
        vv_ref[rows, :] = kv[:, kw:].astype(BF16)

    @pl.when(pl.program_id(1) == 0)
    def _():
        if has_cache:
            for c in range(n_cache // TM):
                put_keys(c * TM, cckv_ref[0, c * TM:(c + 1) * TM, :], ckr_ref[0, c * TM:(c + 1) * TM, :])

        def body(c, carry):
            r0 = pl.multiple_of(c * TM, TM)
            put_keys(n_cache + r0, ckv_ref[pl.ds(r0, TM), :], krr_ref[pl.ds(r0, TM), :])
            return carry
        lax.fori_loop(0, s_new // TM, body, 0)

    scale = (QK_NOPE + QK_ROPE) ** -0.5
    for hd in range(MLA_HEADS):
        s = lax.dot_general(q_ref[hd], kcat_ref[hd], (((1,), (1,)), ((), ())),
                            preferred_element_type=F32) * scale
        m = jnp.max(s, axis=-1, keepdims=True)
        e = jnp.exp(s - m)
        l = jnp.sum(e, axis=-1, keepdims=True)
        o = jnp.dot(e.astype(BF16), vv_ref[:, hd * V_HEAD:(hd + 1) * V_HEAD], preferred_element_type=F32)
        o_ref[:, hd * V_HEAD:(hd + 1) * V_HEAD] = (o / l).astype(BF16)
    del sk


def _lru_dir(xp_ref, xc_ref, xn_ref, valid_prev, valid_next, d, reverse, cw_ref, cb_ref,
             wr_ref, br_ref, wi_ref, bi_ref, lam_ref, carry):
    xp = jnp.where(valid_prev, xp_ref[...], 0.0)
    xn = jnp.where(valid_next, xn_ref[...], 0.0)
    xx = jnp.concatenate([xp, xc_ref[...], xn], axis=0)
    n = TM + 2 * SUBLANES
    xc = cb_ref[...]
    for k in range(CONV_W):
        sh = (CONV_LEFT - k) % n
        xs = xx if sh == 0 else pltpu.roll(xx, sh, 0)
        xc = xc + xs[SUBLANES:SUBLANES + TM] * cw_ref[k:k + 1, :]
    rs, is_ = [], []
    for b in range(LRU_BLOCKS):
        xb = xc[:, b * LRU_BLOCK:(b + 1) * LRU_BLOCK]
        rs.append(jnp.dot(xb, wr_ref[d, b], preferred_element_type=F32))
        is_.append(jnp.dot(xb, wi_ref[d, b], preferred_element_type=F32))
    r = jax.nn.sigmoid(jnp.concatenate(rs, axis=-1) + br_ref[d:d + 1, :])
    i = jax.nn.sigmoid(jnp.concatenate(is_, axis=-1) + bi_ref[d:d + 1, :])
    nl = -lam_ref[d:d + 1, :]
    softplus = jnp.maximum(nl, 0.0) + jnp.log1p(jnp.exp(-jnp.abs(nl)))
    log_a = -LRU_C * r * softplus
    a = jnp.exp(log_a)
    bx = jnp.sqrt(jnp.tanh(-log_a) * (a * a + 1.0)) * (i * xc)
    t = lax.broadcasted_iota(jnp.int32, (TM, 1), 0)
    step = 1
    while step < TM:
        if reverse:
            keep = t < TM - step
            sh = TM - step
        else:
            keep = t >= step
            sh = step
        a_s = jnp.where(keep, pltpu.roll(a, sh, 0), 1.0)
        b_s = jnp.where(keep, pltpu.roll(bx, sh, 0), 0.0)
        bx = a * b_s + bx
        a = a * a_s
        step *= 2
    return a * carry + bx


def _lru_kernel(fxp, fxc, fxn, bxp, bxc, bxn, cw_ref, cb_ref, wr_ref, br_ref, wi_ref, bi_ref, lam_ref,
                h0_ref, hf_ref, hb_ref, st_ref, cf_ref, cbk_ref, *, nc):
    c = pl.program_id(1)

    @pl.when(c == 0)
    def _():
        cf_ref[...] = h0_ref[0, 0:1, :]
        cbk_ref[...] = h0_ref[0, 1:2, :]

    params = (cw_ref, cb_ref, wr_ref, br_ref, wi_ref, bi_ref, lam_ref)
    hf = _lru_dir(fxp, fxc, fxn, c > 0, c < nc - 1, 0, False, *params, cf_ref[...])
    hf_ref[...] = hf
    cf_ref[...] = hf[TM - 1:TM, :]
    hb = _lru_dir(bxp, bxc, bxn, c < nc - 1, c > 0, 1, True, *params, cbk_ref[...])
    hb_ref[...] = hb
    cbk_ref[...] = hb[0:1, :]
    st_ref[0, 0:1, :] = hf[TM - 1:TM, :]
    st_ref[0, 1:2, :] = hb[0:1, :]


def _oproj_kernel(x_ref, mod_ref, at_ref, hf_ref, hb_ref, ug_ref, woa_ref, wor_ref, o_ref):
    mod = mod_ref[0]
    rec = ((hf_ref[...] + hb_ref[...]) * jax.nn.gelu(ug_ref[...])).astype(BF16)
    out = (jnp.dot(at_ref[...], woa_ref[...], preferred_element_type=F32)
           + jnp.dot(rec, wor_ref[...], preferred_element_type=F32))
    o_ref[...] = x_ref[...] + mod[2:3] * out


def _pool_kernel(xp_ref, xc_ref, xn_ref, mod_ref, g_ref, wp_ref, sp_ref, o_ref, *, tiles_per_seq, n_prompt_tiles):
    i = pl.program_id(0)
    j = jnp.where(i < n_prompt_tiles, 0, (i - n_prompt_tiles) % tiles_per_seq)
    ntile = jnp.where(i < n_prompt_tiles, 1, tiles_per_seq)
    mod = mod_ref[0]
    g = g_ref[...]

    def hmod(x):
        return _rms(x, g) * (1.0 + mod[1:2]) + mod[0:1]

    x = xc_ref[...]
    hc = hmod(x)
    hp = jnp.where(j > 0, hmod(xp_ref[...]), 0.0)
    hn = jnp.where(j < ntile - 1, hmod(xn_ref[...]), 0.0)
    hh = jnp.concatenate([hp, hc, hn], axis=0)
    n = TM + 2 * HALO
    seq_len = ntile * TM
    t = j * TM + lax.broadcasted_iota(jnp.int32, (TM, 1), 0)
    ys = []
    for gi, w in enumerate(POOL_WINDOWS):
        cols = slice(gi * POOL_GROUP, (gi + 1) * POOL_GROUP)
        p = hh[:, cols]
        p = p + pltpu.roll(p, 1, 0)
        half = 1
        while 2 * half < w:
            p = pltpu.roll(p, half, 0) + pltpu.roll(p, n - half, 0)
            half *= 2
        lo = jnp.maximum(t - w // 2, 0)
        hi = jnp.minimum(t + (w - w // 2), seq_len)
        mean = p[HALO:HALO + TM] / (hi - lo).astype(F32)
        dg = (mean - hc[:, cols]).astype(BF16)
        ys.append(jnp.dot(dg, wp_ref[gi], preferred_element_type=F32))
    y = jnp.concatenate(ys, axis=-1) * sp_ref[...]
    o_ref[...] = x + mod[2:3] * y


def _sort_pairs(n):
    pairs = []

    def merge(lo, cnt, r):
        step = r * 2
        if step < cnt:
            merge(lo, cnt, step)
            merge(lo + r, cnt, step)
            for i in range(lo + r, lo + cnt - r, step):
                pairs.append((i, i + r))
        else:
            pairs.append((lo, lo + r))

    def sort(lo, cnt):
        if cnt > 1:
            m = cnt // 2
            sort(lo, m)
            sort(lo + m, m)
            merge(lo, cnt, 1)

    sort(0, n)
    return pairs


_SORT16 = _sort_pairs(PEER_TOPK)
_HYPER = [(a, b) for a in range(PEER_TOPK) for b in range(PEER_TOPK) if (a + 1) * (b + 1) <= PEER_TOPK]


def _top16_sorted(s):
    k = PEER_TOPK
    x = [s[SUBLANES * r:SUBLANES * (r + 1), :] for r in range(N_KEYS // SUBLANES)]
    for (i, j) in _SORT16:
        hi = jnp.maximum(x[i], x[j])
        lo = jnp.minimum(x[i], x[j])
        x[i], x[j] = hi, lo
    for shift in (4, 2, 1):
        y = [jnp.maximum(x[r], pltpu.roll(x[k - 1 - r], shift, 0)) for r in range(k)]
        stride = k // 2
        while stride >= 1:
            for i in range(k):
                if i & stride == 0:
                    hi = jnp.maximum(y[i], y[i + stride])
                    lo = jnp.minimum(y[i], y[i + stride])
                    y[i], y[i + stride] = hi, lo
            stride //= 2
        x = y
    return x


def _pair_counts(sv1, sv2):
    one = jnp.ones_like(sv1[0])
    zero = jnp.zeros_like(sv1[0])
    cand = [sv1[a] + sv2[b] for (a, b) in _HYPER]

    def ordered(i, j):
        (ai, bi), (aj, bj) = _HYPER[i], _HYPER[j]
        return (aj <= ai and bj <= bi) or (ai <= aj and bi <= bj)

    nh = len(_HYPER)
    cnt = [float((a + 1) * (b + 1) - 1 + sum(1 for j in range(i + 1, nh) if not ordered(i, j))) * one
           for i, (a, b) in enumerate(_HYPER)]
    for i in range(nh):
        for jx in range(i):
            if ordered(i, jx):
                continue
            ge = jnp.where(cand[jx] >= cand[i], one, zero)
            cnt[i] = cnt[i] + ge
            cnt[jx] = cnt[jx] - ge
    e1 = [jnp.exp(sv1[a] - sv1[0]) for a in range(PEER_TOPK)]
    e2 = [jnp.exp(sv2[b] - sv2[0]) for b in range(PEER_TOPK)]
    n = [zero for _ in range(PEER_TOPK)]
    z = zero
    for i, (a, b) in enumerate(_HYPER):
        sel = jnp.where(cnt[i] < float(PEER_TOPK), one, zero)
        n[a] = n[a] + sel
        z = z + sel * (e1[a] * e2[b])
    return n, 1.0 / z


def _head_tables(hd, sc_ref, sv_ref, nz_ref, ex_ref, cnt1_ref, e1n_ref, rank2_ref, e2_ref, exact):
    k = PEER_TOPK
    tm = sc_ref.shape[-1]
    nslab = N_KEYS // SUBLANES
    bad = []
    for g in range(tm // LANES):
        lanes = slice(g * LANES, (g + 1) * LANES)

        def bc(ref, *idx):
            return jnp.broadcast_to(ref[idx + (slice(hd, hd + 1), lanes)], (SUBLANES, LANES))

        sv1b = [bc(sv_ref, 0, a) for a in range(k)]
        nb = [bc(nz_ref, a) for a in range(k)]
        izb = bc(nz_ref, k)
        c1 = jnp.zeros((SUBLANES, LANES), F32)
        for r in range(nslab):
            rows = slice(r * SUBLANES, (r + 1) * SUBLANES)
            s1 = sc_ref[0, hd, rows, lanes]
            cnt1 = jnp.zeros_like(s1)
            if exact:
                rank1 = jnp.zeros_like(s1)
                for a in range(k):
                    rank1 = jnp.where(sv1b[a] > s1, float(a + 1), rank1)
                rank1 = rank1 + ex_ref[0, rows, lanes]
                for a in range(k):
                    cnt1 = jnp.where(rank1 == float(a), nb[a], cnt1)
            else:
                for a in range(k):
                    cnt1 = jnp.where(s1 == sv1b[a], nb[a], cnt1)
                c1 = c1 + jnp.where(s1 >= sv1b[k - 1], 1.0, 0.0)
            cnt1_ref[hd, rows, lanes] = cnt1
            e1n_ref[hd, rows, lanes] = jnp.exp(s1 - sv1b[0]) * izb
        sv2b = [bc(sv_ref, 1, a) for a in range(k)]
        c2 = jnp.zeros((SUBLANES, LANES), F32)
        for r in range(N_KEYS // BF16_ROWS):
            rk, e2 = [], []
            for q in range(BF16_ROWS // SUBLANES):
                r0 = r * BF16_ROWS + q * SUBLANES
                rows = slice(r0, r0 + SUBLANES)
                s2 = sc_ref[1, hd, rows, lanes]
                rank2 = jnp.zeros_like(s2)
                for a in range(k):
                    rank2 = jnp.where(sv2b[a] > s2, float(a + 1), rank2)
                if exact:
                    rank2 = jnp.minimum(rank2 + ex_ref[1, rows, lanes], float(k))
                else:
                    c2 = c2 + jnp.where(rank2 < float(k), 1.0, 0.0)
                rk.append(rank2)
                e2.append(jnp.exp(s2 - sv2b[0]))
            rows16 = slice(r * BF16_ROWS, (r + 1) * BF16_ROWS)
            rank2_ref[hd, rows16, lanes] = jnp.concatenate(rk, axis=0).astype(BF16)
            e2_ref[hd, rows16, lanes] = jnp.concatenate(e2, axis=0).astype(BF16)
        if not exact:
            n1 = jnp.sum(c1, axis=0, keepdims=True)
            n2 = jnp.sum(c2, axis=0, keepdims=True)
            b = jnp.where(n1 != float(k), 1.0, 0.0) + jnp.where(n2 != float(k), 1.0, 0.0)
            for svb in (sv1b, sv2b):
                for a in range(k - 1):
                    b = b + jnp.where(svb[a][0:1] == svb[a + 1][0:1], 1.0, 0.0)
            bad.append(b)
    return jnp.concatenate(bad, axis=1) if bad else None


def _tie_offsets(hd, sc_ref, ex_ref):
    tm = sc_ref.shape[-1]
    nidx = lax.broadcasted_iota(jnp.int32, (N_KEYS, tm), 0)
    for p in range(2):
        s = sc_ref[p, hd]

        def body(m, e, s=s, p=p):
            row = sc_ref[p, hd, pl.ds(m, 1), :]
            return e + jnp.where((s == row) & (nidx > m), 1.0, 0.0)

        ex_ref[p] = lax.fori_loop(0, N_KEYS, body, jnp.zeros((N_KEYS, tm), F32))


def _peer_kernel(x_ref, mod_ref, g_ref, wqt_ref, keys_ref, u_ref, vt_ref, gfin_ref, o_ref,
                 h2t_ref, acc_ref, sc_ref, sv_ref, nz_ref, ex_ref, cnt1_ref, e1n_ref, rank2_ref, e2_ref,
                 *, final_norm):
    k = pl.program_id(1)
    tm = PEER_TM
    half = PEER_DKEY // 2

    @pl.when(k == 0)
    def _():
        mod = mod_ref[0]
        h2 = _rms(x_ref[...], g_ref[...]) * (1.0 + mod[4:5]) + mod[3:4]
        h2t_ref[...] = h2.T.astype(BF16)
        acc_ref[...] = jnp.zeros_like(acc_ref)
        qt = jnp.dot(wqt_ref[...], h2t_ref[...], preferred_element_type=F32).astype(BF16)
        for hd in range(PEER_HEADS):
            for p in range(2):
                r0 = (hd * 2 + p) * half
                sc_ref[p, hd] = jnp.dot(keys_ref[hd, p], qt[r0:r0 + half, :],
                                        preferred_element_type=F32)
                for g in range(tm // LANES):
                    lanes = slice(g * LANES, (g + 1) * LANES)
                    top = _top16_sorted(sc_ref[p, hd, :, lanes])
                    for a in range(PEER_TOPK):
                        sv_ref[p, a, hd:hd + 1, lanes] = top[a][0:1, :]
        for g in range(tm // LANES):
            lanes = slice(g * LANES, (g + 1) * LANES)
            sv1 = [sv_ref[0, a, :, lanes] for a in range(PEER_TOPK)]
            sv2 = [sv_ref[1, a, :, lanes] for a in range(PEER_TOPK)]
            n, inv_z = _pair_counts(sv1, sv2)
            for a in range(PEER_TOPK):
                nz_ref[a, :, lanes] = n[a]
            nz_ref[PEER_TOPK, :, lanes] = inv_z
        tabs = (sc_ref, sv_ref, nz_ref, ex_ref, cnt1_ref, e1n_ref, rank2_ref, e2_ref)
        for hd in range(PEER_HEADS):
            bad = _head_tables(hd, *tabs, exact=False)

            @pl.when(jnp.max(bad) > 0.0)
            def _(hd=hd):
                _tie_offsets(hd, sc_ref, ex_ref)
                _head_tables(hd, *tabs, exact=True)

    zero = jnp.zeros((BF16_ROWS, tm), BF16)
    npiece = PEER_EBLK // N_KEYS
    per_sub = PEER_SUB // N_KEYS

    ppu = PEER_UP // N_KEYS

    def up_dot(q):
        return jnp.dot(u_ref[q * PEER_UP:(q + 1) * PEER_UP, :], h2t_ref[...], preferred_element_type=F32)

    raw2 = up_dot(0)
    ys = []
    for p in range(npiece):
        if p % ppu == 0:
            nxt = up_dot(p // ppu + 1) if p + ppu < npiece else None
        raw = raw2[(p % ppu) * N_KEYS:(p % ppu + 1) * N_KEYS, :]
        i1 = k * npiece + p
        cbs = [jnp.broadcast_to(cnt1_ref[hd, pl.ds(i1, 1), :], (BF16_ROWS, tm)).astype(BF16)
               for hd in range(PEER_HEADS)]
        ebs = [jnp.broadcast_to(e1n_ref[hd, pl.ds(i1, 1), :], (BF16_ROWS, tm)).astype(BF16)
               for hd in range(PEER_HEADS)]
        act = jax.nn.gelu(raw.astype(BF16))
        for r in range(N_KEYS // BF16_ROWS):
            rows = slice(r * BF16_ROWS, (r + 1) * BF16_ROWS)
            w = None
            for hd in range(PEER_HEADS):
                t = jnp.where(rank2_ref[hd, rows, :] < cbs[hd], e2_ref[hd, rows, :], zero) * ebs[hd]
                w = t if w is None else w + t
            ys.append(act[rows, :] * w)
        if p % ppu == ppu - 1:
            raw2 = nxt
        if (p + 1) % per_sub == 0:
            e0 = (p + 1 - per_sub) * N_KEYS
            y = jnp.concatenate(ys, axis=0)
            ys = []
            acc_ref[...] += jnp.dot(vt_ref[:, e0:e0 + PEER_SUB], y, preferred_element_type=F32)

    @pl.when(k == pl.num_programs(1) - 1)
    def _():
        out = x_ref[...] + mod_ref[0][5:6] * acc_ref[...].T
        if final_norm:
            out = _rms(out, gfin_ref[...])
        o_ref[...] = out


def _rope_tables(s_prompt_tile, s_sample):
    n_rows = s_sample // GRID_W
    rows = jnp.repeat(jnp.arange(n_rows, dtype=F32), GRID_W)
    cols = jnp.tile(jnp.arange(GRID_W, dtype=F32), n_rows)
    axis_dim = QK_ROPE // 2
    inv_freq = ROPE_BASE ** (-jnp.arange(0, axis_dim, 2, dtype=F32) / axis_dim)
    ang = jnp.concatenate([rows[:, None] * inv_freq, cols[:, None] * inv_freq], axis=-1)
    cos, sin = jnp.cos(ang), jnp.sin(ang)
    ck = jnp.concatenate([cos, cos], axis=-1)
    sk = jnp.concatenate([-sin, sin], axis=-1)
    ident_c = jnp.ones((s_prompt_tile, QK_ROPE), F32)
    ident_s = jnp.zeros((s_prompt_tile, QK_ROPE), F32)
    return jnp.concatenate([ident_c, ck], axis=0), jnp.concatenate([ident_s, sk], axis=0)


def kernel(x_prompt, x_sample, cache_ckv_l0, cache_krope_l0, state_lru_l0, c, c_ctx, w_mod_l0, b_mod_l0, w_mod_l1, b_mod_l1, g_mix_l0, g_ffn_l0, g_mix_l1, g_ffn_l1, w_in_l0, g_q_l0, w_uq_l0, g_kv_l0, w_ukv_l0, conv_w_l0, conv_b_l0, w_rg_l0, b_rg_l0, w_ig_l0, b_ig_l0, lam_l0, w_o_l0, w_pool_l1, s_pool_l1, peer_wq_l0, peer_keys_l0, peer_u_l0, peer_v_l0, peer_wq_l1, peer_keys_l1, peer_u_l1, peer_v_l1, g_final):
    nb_p, s_p, d = x_prompt.shape
    nb_s, s_s, _ = x_sample.shape
    n_cache = cache_ckv_l0.shape[1]
    assert d == D_MODEL and s_p == TM and s_s % TM == 0 and n_cache % TM == 0
    t_p = nb_p * s_p
    t_s = nb_s * s_s
    t_all = t_p + t_s
    npt = t_p // TM
    tps = s_s // TM
    ntile = t_all // TM
    assert t_all % PEER_TM == 0 and t_p % PEER_TM == 0 and s_s % PEER_TM == 0

    x0 = jnp.concatenate([x_prompt.reshape(t_p, d), x_sample.reshape(t_s, d)], axis=0)

    ncond = 1 + nb_s
    cpad = jnp.zeros((2 * SUBLANES, d), F32).at[0].set(c_ctx).at[1:ncond].set(c)
    mod0 = _ada(cpad, w_mod_l0, b_mod_l0).reshape(2 * SUBLANES, 6, d)
    mod1 = _ada(cpad, w_mod_l1, b_mod_l1).reshape(2 * SUBLANES, 6, d)

    def cond_row(i):
        return jnp.where(i < npt, 0, 1 + (i - npt) // tps)

    mod_spec = pl.BlockSpec((1, 6, d), lambda i: (cond_row(i), 0, 0))
    row = lambda a: a.reshape(1, -1)

    perm = np.concatenate([np.arange(0, QK_ROPE, 2), np.arange(1, QK_ROPE, 2)])
    perm_sw = np.concatenate([np.arange(1, QK_ROPE, 2), np.arange(0, QK_ROPE, 2)])
    o1 = Q_LORA + KV_LORA
    w_kr = w_in_l0[:, o1:o1 + QK_ROPE]
    z64 = jnp.zeros((d, 128 - QK_ROPE), F32)
    w_in_ext = jnp.concatenate(
        [w_in_l0[:, :o1], w_kr, z64, w_kr[:, perm], z64, w_kr[:, perm_sw], z64, w_in_l0[:, o1 + QK_ROPE:]],
        axis=1).astype(BF16)
    assert w_in_ext.shape[1] == IN_EXT
    wq3 = w_uq_l0.reshape(Q_LORA, MLA_HEADS, QK_NOPE + QK_ROPE)
    w_uq_ext = jnp.concatenate(
        [wq3[:, :, :QK_NOPE].reshape(Q_LORA, -1),
         wq3[:, :, QK_NOPE:][:, :, perm].reshape(Q_LORA, -1),
         wq3[:, :, QK_NOPE:][:, :, perm_sw].reshape(Q_LORA, -1)], axis=1).astype(BF16)
    ck_tab, sk_tab = _rope_tables(TM, s_s)

    def rope_blk(i):
        return jnp.where(i < npt, 0, 1 + (i - npt) % tps)

    tok = lambda w: pl.BlockSpec((TM, w), lambda i: (i, 0))
    full = lambda a: pl.BlockSpec(a.shape, lambda *_: (0,) * a.ndim)
    q, ckv, kr, krr, ux, ug = pl.pallas_call(
        _inproj_kernel,
        grid=(ntile,),
        in_specs=[tok(d), mod_spec, full(row(g_mix_l0)), full(w_in_ext), full(row(g_q_l0)), full(w_uq_ext),
                  full(row(g_kv_l0)),
                  pl.BlockSpec((TM, QK_ROPE), lambda i: (rope_blk(i), 0)),
                  pl.BlockSpec((TM, QK_ROPE), lambda i: (rope_blk(i), 0))],
        out_specs=[pl.BlockSpec((MLA_HEADS, TM, QK_PAD), lambda i: (0, i, 0)),
                   tok(KV_LORA), tok(QK_ROPE), tok(QK_ROPE), tok(LRU_WIDTH), tok(LRU_WIDTH)],
        out_shape=[jax.ShapeDtypeStruct((MLA_HEADS, t_all, QK_PAD), BF16),
                   jax.ShapeDtypeStruct((t_all, KV_LORA), F32),
                   jax.ShapeDtypeStruct((t_all, QK_ROPE), F32),
                   jax.ShapeDtypeStruct((t_all, QK_ROPE), BF16),
                   jax.ShapeDtypeStruct((t_all, LRU_WIDTH), F32),
                   jax.ShapeDtypeStruct((t_all, LRU_WIDTH), F32)],
        compiler_params=_cparams(("arbitrary",)),
    )(x0, mod0, row(g_mix_l0), w_in_ext, row(g_q_l0), w_uq_ext, row(g_kv_l0), ck_tab, sk_tab)

    wkv3 = w_ukv_l0.reshape(KV_LORA, MLA_HEADS, QK_NOPE + V_HEAD)
    w_ukv_ext = jnp.concatenate([wkv3[:, :, :QK_NOPE].reshape(KV_LORA, -1),
                                 wkv3[:, :, QK_NOPE:].reshape(KV_LORA, -1)], axis=1).astype(BF16)

    def attn_call(nb, s_new, tile0, has_cache):
        nq = s_new // TM
        blk0 = tile0 * TM // s_new
        n_c = n_cache if has_cache else 0
        in_specs = [pl.BlockSpec((MLA_HEADS, TM, QK_PAD), lambda b, qi: (0, tile0 + b * nq + qi, 0)),
                    pl.BlockSpec((s_new, KV_LORA), lambda b, qi: (blk0 + b, 0)),
                    pl.BlockSpec((s_new, QK_ROPE), lambda b, qi: (blk0 + b, 0))]
        args = [q, ckv, krr]
        if has_cache:
            in_specs += [pl.BlockSpec((1, n_cache, KV_LORA), lambda b, qi: (b, 0, 0)),
                         pl.BlockSpec((1, n_cache, QK_ROPE), lambda b, qi: (b, 0, 0))]
            args += [cache_ckv_l0, cache_krope_l0[:, :, perm]]
        in_specs.append(pl.BlockSpec(w_ukv_ext.shape, lambda b, qi: (0, 0)))
        args.append(w_ukv_ext)
        return pl.pallas_call(
            functools.partial(_attn_kernel, has_cache=has_cache, s_new=s_new, n_cache=n_c),
            grid=(nb, nq),
            in_specs=in_specs,
            out_specs=pl.BlockSpec((TM, MLA_WIDTH), lambda b, qi: (b * nq + qi, 0)),
            out_shape=jax.ShapeDtypeStruct((nb * s_new, MLA_WIDTH), BF16),
            scratch_shapes=[pltpu.VMEM((MLA_HEADS, n_c + s_new, QK_PAD), BF16),
                            pltpu.VMEM((n_c + s_new, MLA_WIDTH), BF16)],
            compiler_params=_cparams(("arbitrary", "arbitrary")),
        )(*args)

    attn = jnp.concatenate([attn_call(nb_p, s_p, 0, False), attn_call(nb_s, s_s, npt, True)], axis=0)

    def lru_call(nb, s_new, tile0, h0):
        nc = s_new // TM
        r8 = TM // SUBLANES
        last8 = t_all // SUBLANES - 1

        def cur(rev):
            return pl.BlockSpec((TM, LRU_WIDTH),
                                lambda b, cc: (tile0 + b * nc + (nc - 1 - cc if rev else cc), 0))

        def prev(rev):
            return pl.BlockSpec((SUBLANES, LRU_WIDTH), lambda b, cc: (
                jnp.maximum((tile0 + b * nc + (nc - 1 - cc if rev else cc)) * r8 - 1, 0), 0))

        def nxt(rev):
            return pl.BlockSpec((SUBLANES, LRU_WIDTH), lambda b, cc: (
                jnp.minimum((tile0 + b * nc + (nc - 1 - cc if rev else cc) + 1) * r8, last8), 0))

        small = [conv_w_l0, row(conv_b_l0), w_rg_l0, b_rg_l0, w_ig_l0, b_ig_l0, lam_l0]
        return pl.pallas_call(
            functools.partial(_lru_kernel, nc=nc),
            grid=(nb, nc),
            in_specs=[prev(False), cur(False), nxt(False), prev(True), cur(True), nxt(True)]
                     + [full(a) for a in small]
                     + [pl.BlockSpec((1, 2, LRU_WIDTH), lambda b, cc: (b, 0, 0))],
            out_specs=[pl.BlockSpec((TM, LRU_WIDTH), lambda b, cc: (b * nc + cc, 0)),
                       pl.BlockSpec((TM, LRU_WIDTH), lambda b, cc: (b * nc + nc - 1 - cc, 0)),
                       pl.BlockSpec((1, 2, LRU_WIDTH), lambda b, cc: (b, 0, 0))],
            out_shape=[jax.ShapeDtypeStruct((nb * s_new, LRU_WIDTH), F32),
                       jax.ShapeDtypeStruct((nb * s_new, LRU_WIDTH), F32),
                       jax.ShapeDtypeStruct((nb, 2, LRU_WIDTH), F32)],
            scratch_shapes=[pltpu.VMEM((1, LRU_WIDTH), F32), pltpu.VMEM((1, LRU_WIDTH), F32)],
            compiler_params=_cparams(("arbitrary", "arbitrary")),
        )(ux, ux, ux, ux, ux, ux, *small, h0)

    hf_p, hb_p, new_lru = lru_call(nb_p, s_p, 0, jnp.zeros((nb_p, 2, LRU_WIDTH), F32))
    hf_s, hb_s, _ = lru_call(nb_s, s_s, npt, state_lru_l0.astype(F32))
    hf = jnp.concatenate([hf_p, hf_s], axis=0)
    hb = jnp.concatenate([hb_p, hb_s], axis=0)

    w_o = w_o_l0.astype(BF16)
    x1 = pl.pallas_call(
        _oproj_kernel,
        grid=(ntile,),
        in_specs=[tok(d), mod_spec, tok(MLA_WIDTH), tok(LRU_WIDTH), tok(LRU_WIDTH), tok(LRU_WIDTH),
                  pl.BlockSpec((MLA_WIDTH, d), lambda i: (0, 0)), pl.BlockSpec((LRU_WIDTH, d), lambda i: (1, 0))],
        out_specs=tok(d),
        out_shape=jax.ShapeDtypeStruct((t_all, d), F32),
        compiler_params=_cparams(("arbitrary",)),
    )(x0, mod0, attn, hf, hb, ug, w_o, w_o)

    def peer_call(x, mod, g_ffn, w_q, sub_keys, u, v, final_norm):
        n_exp = u.shape[0]
        assert n_exp == N_KEYS * N_KEYS and n_exp % PEER_EBLK == 0
        wqt = w_q.T.astype(BF16)
        keys = sub_keys.astype(BF16)
        ub = u.astype(BF16)
        vt = v.T.astype(BF16)
        tpp = PEER_TM // TM

        def cond_row_p(i):
            return cond_row(i * tpp)

        big = lambda dt: pltpu.VMEM((PEER_HEADS, N_KEYS, PEER_TM), dt)
        return pl.pallas_call(
            functools.partial(_peer_kernel, final_norm=final_norm),
            grid=(t_all // PEER_TM, n_exp // PEER_EBLK),
            in_specs=[pl.BlockSpec((PEER_TM, d), lambda i, k: (i, 0)),
                      pl.BlockSpec((1, 6, d), lambda i, k: (cond_row_p(i), 0, 0)),
                      pl.BlockSpec((1, d), lambda i, k: (0, 0)),
                      pl.BlockSpec(wqt.shape, lambda i, k: (0, 0)),
                      pl.BlockSpec(keys.shape, lambda i, k: (0, 0, 0, 0)),
                      pl.BlockSpec((PEER_EBLK, d), lambda i, k: (k, 0)),
                      pl.BlockSpec((d, PEER_EBLK), lambda i, k: (0, k)),
                      pl.BlockSpec((1, d), lambda i, k: (0, 0))],
            out_specs=pl.BlockSpec((PEER_TM, d), lambda i, k: (i, 0)),
            out_shape=jax.ShapeDtypeStruct((t_all, d), F32),
            scratch_shapes=[pltpu.VMEM((d, PEER_TM), BF16),
                            pltpu.VMEM((d, PEER_TM), F32),
                            pltpu.VMEM((2, PEER_HEADS, N_KEYS, PEER_TM), F32),
                            pltpu.VMEM((2, PEER_TOPK, PEER_HEADS, PEER_TM), F32),
                            pltpu.VMEM((PEER_TOPK + 1, PEER_HEADS, PEER_TM), F32),
                            pltpu.VMEM((2, N_KEYS, PEER_TM), F32),
                            big(F32), big(F32), big(BF16), big(BF16)],
            compiler_params=_cparams(("arbitrary", "arbitrary")),
        )(x, mod, row(g_ffn), wqt, keys, ub, vt, row(g_final))

    x2 = peer_call(x1, mod0, g_ffn_l0, peer_wq_l0, peer_keys_l0, peer_u_l0, peer_v_l0, False)

    rh = TM // HALO
    lasth = t_all // HALO - 1
    x3 = pl.pallas_call(
        functools.partial(_pool_kernel, tiles_per_seq=tps, n_prompt_tiles=npt),
        grid=(ntile,),
        in_specs=[pl.BlockSpec((HALO, d), lambda i: (jnp.maximum(i * rh - 1, 0), 0)),
                  tok(d),
                  pl.BlockSpec((HALO, d), lambda i: (jnp.minimum((i + 1) * rh, lasth), 0)),
                  mod_spec, full(row(g_mix_l1)),
                  pl.BlockSpec(w_pool_l1.shape, lambda i: (0, 0, 0)), full(row(s_pool_l1))],
        out_specs=tok(d),
        out_shape=jax.ShapeDtypeStruct((t_all, d), F32),
        compiler_params=_cparams(("arbitrary",)),
    )(x2, x2, x2, mod1, row(g_mix_l1), w_pool_l1.astype(BF16), row(s_pool_l1))

    y = peer_call(x3, mod1, g_ffn_l1, peer_wq_l1, peer_keys_l1, peer_u_l1, peer_v_l1, True)

    y_prompt = y[:t_p].reshape(nb_p, s_p, d)
    y_sample = y[t_p:].reshape(nb_s, s_s, d)
    new_ckv = ckv[:t_p].reshape(nb_p, s_p, KV_LORA)
    new_krope = kr[:t_p].reshape(nb_p, s_p, QK_ROPE)
    return (y_prompt, y_sample, new_ckv, new_krope, new_lru)
```

```python
import functools

import numpy as np
import jax
import jax.numpy as jnp
from jax import lax
from jax.experimental import pallas as pl
from jax.experimental.pallas import tpu as pltpu

F32 = jnp.float32
BF16 = jnp.bfloat16

D_MODEL = 1024
EPS = 1e-6
GRID_W = 64
MLA_HEADS = 4
Q_LORA = 384
KV_LORA = 256
QK_NOPE = 128
QK_ROPE = 64
V_HEAD = 128
MLA_WIDTH = MLA_HEADS * V_HEAD
ROPE_BASE = 10000.0
LRU_WIDTH = 512
LRU_BLOCKS = 4
LRU_BLOCK = LRU_WIDTH // LRU_BLOCKS
CONV_W = 4
CONV_LEFT = 2
LRU_C = 8.0
POOL_WINDOWS = (2, 4, 8, 16)
POOL_GROUP = D_MODEL // len(POOL_WINDOWS)
PEER_HEADS = 8
N_KEYS = 128
PEER_DKEY = 256
PEER_TOPK = 16

SUBLANES = 8
LANES = 128
VMEM_LIMIT = 56 * 1024 * 1024

TM = 256
QK_PAD = 256
HALO = 16
PEER_TM = 512
PEER_EBLK = 2048
PEER_SUB = 512
BF16_ROWS = 16
IN_EXT = 2048


def _rms(x, g):
    return x * lax.rsqrt(jnp.mean(x * x, axis=-1, keepdims=True) + EPS) * g


def _cparams(sem):
    return pltpu.CompilerParams(dimension_semantics=sem, vmem_limit_bytes=VMEM_LIMIT)


def _ada_kernel(c_ref, w_ref, b_ref, o_ref):
    c = c_ref[...]
    o_ref[...] = jnp.dot(c * jax.nn.sigmoid(c), w_ref[...], preferred_element_type=F32) + b_ref[...]


def _ada(cpad, w_mod, b_mod):
    n = w_mod.shape[1]
    bn = 768
    return pl.pallas_call(
        _ada_kernel,
        grid=(n // bn,),
        in_specs=[pl.BlockSpec(cpad.shape, lambda j: (0, 0)),
                  pl.BlockSpec((D_MODEL, bn), lambda j: (0, j)),
                  pl.BlockSpec((1, bn), lambda j: (0, j))],
        out_specs=pl.BlockSpec((cpad.shape[0], bn), lambda j: (0, j)),
        out_shape=jax.ShapeDtypeStruct((cpad.shape[0], n), F32),
        compiler_params=_cparams(("arbitrary",)),
    )(cpad, w_mod, b_mod.reshape(1, n))


def _token_tile(xp_ref, xs_ref, n_prompt_tiles):
    return jnp.where(pl.program_id(0) < n_prompt_tiles, xp_ref[...], xs_ref[...])


def _inproj_kernel(xp_ref, xs_ref, mod_ref, g_ref, win_ref, gq_ref, wuq_ref, gkv_ref, ck_ref, sk_ref,
                   q_ref, ckv_ref, kr_ref, krr_ref, ux_ref, ug_ref, *, n_prompt_tiles):
    mod = mod_ref[0]
    h = _rms(_token_tile(xp_ref, xs_ref, n_prompt_tiles), g_ref[...]) * (1.0 + mod[1:2]) + mod[0:1]
    y = jnp.dot(h.astype(BF16), win_ref[...], preferred_element_type=F32)
    cq = y[:, 0:Q_LORA]
    ckv = y[:, Q_LORA:Q_LORA + KV_LORA]
    o = Q_LORA + KV_LORA
    kr = y[:, o:o + QK_ROPE]
    krp = y[:, o + 128:o + 128 + QK_ROPE]
    krs = y[:, o + 256:o + 256 + QK_ROPE]
    ux_ref[...] = y[:, o + 384:o + 384 + LRU_WIDTH]
    ug_ref[...] = y[:, o + 384 + LRU_WIDTH:o + 384 + 2 * LRU_WIDTH]
    ckv_ref[...] = _rms(ckv, gkv_ref[...])
    kr_ref[...] = kr
    ck = ck_ref[...]
    sk = sk_ref[...]
    krr_ref[...] = (krp * ck + krs * sk).astype(BF16)
    q = jnp.dot(_rms(cq, gq_ref[...]).astype(BF16), wuq_ref[...], preferred_element_type=F32)
    nw = MLA_HEADS * QK_NOPE
    rw = MLA_HEADS * QK_ROPE
    for hd in range(MLA_HEADS):
        qp = q[:, nw + hd * QK_ROPE:nw + (hd + 1) * QK_ROPE]
        qs = q[:, nw + rw + hd * QK_ROPE:nw + rw + (hd + 1) * QK_ROPE]
        q_ref[hd, :, 0:QK_NOPE] = q[:, hd * QK_NOPE:(hd + 1) * QK_NOPE].astype(BF16)
        q_ref[hd, :, QK_NOPE:QK_NOPE + QK_ROPE] = (qp * ck + qs * sk).astype(BF16)
        q_ref[hd, :, QK_NOPE + QK_ROPE:QK_PAD] = jnp.zeros((TM, QK_PAD - QK_NOPE - QK_ROPE), BF16)


def _attn_kernel(*refs, has_cache, s_new, n_cache):
    if has_cache:
        q_ref, ckv_ref, krr_ref, cckv_ref, ckr_ref, wukv_ref, o_ref, kcat_ref, vv_ref = refs
    else:
        q_ref, ckv_ref, krr_ref, wukv_ref, o_ref, kcat_ref, vv_ref = refs
    sk = n_cache + s_new
    kw = MLA_HEADS * QK_NOPE
    zpad = jnp.zeros((TM, QK_PAD - QK_NOPE - QK_ROPE), BF16)

    def put_keys(row0, ckv_rows, kr_rows):
        kv = jnp.dot(ckv_rows.astype(BF16), wukv_ref[...], preferred_element_type=F32)
        rows = pl.ds(row0, TM)
        for hd in range(MLA_HEADS):
            kcat_ref[hd, rows, 0:QK_NOPE] = kv[:, hd * QK_NOPE:(hd + 1) * QK_NOPE].astype(BF16)
            kcat_ref[hd, rows, QK_NOPE:QK_NOPE + QK_ROPE] = kr_rows.astype(BF16)
            kcat_ref[hd, rows, QK_NOPE + QK_ROPE:QK_PAD] = zpad
        vv_ref[rows, :] = kv[:, kw:].astype(BF16)

    @pl.when(pl.program_id(1) == 0)
    def _():
        if has_cache:
            for c in range(n_cache // TM):
                put_keys(c * TM, cckv_ref[0, c * TM:(c + 1) * TM, :], ckr_ref[0, c * TM:(c + 1) * TM, :])

        def body(c, carry):
            r0 = pl.multiple_of(c * TM, TM)
            put_keys(n_cache + r0, ckv_ref[pl.ds(r0, TM), :], krr_ref[pl.ds(r0, TM), :])
            return carry
        lax.fori_loop(0, s_new // TM, body, 0)

    scale = (QK_NOPE + QK_ROPE) ** -0.5
    for hd in range(MLA_HEADS):
        s = lax.dot_general(q_ref[hd], kcat_ref[hd], (((1,), (1,)), ((), ())),
                            preferred_element_type=F32) * scale
        m = jnp.max(s, axis=-1, keepdims=True)
        e = jnp.exp(s - m)
        l = jnp.sum(e, axis=-1, keepdims=True)
        o = jnp.dot(e.astype(BF16), vv_ref[:, hd * V_HEAD:(hd + 1) * V_HEAD], preferred_element_type=F32)
        o_ref[:, hd * V_HEAD:(hd + 1) * V_HEAD] = (o / l).astype(BF16)
    del sk


def _lru_dir(xp_ref, xc_ref, xn_ref, valid_prev, valid_next, d, reverse, cw_ref, cb_ref,
             wr_ref, br_ref, wi_ref, bi_ref, lam_ref, carry):
    xp = jnp.where(valid_prev, xp_ref[...], 0.0)
    xn = jnp.where(valid_next, xn_ref[...], 0.0)
    xx = jnp.concatenate([xp, xc_ref[...], xn], axis=0)
    n = TM + 2 * SUBLANES
    xc = cb_ref[...]
    for k in range(CONV_W):
        sh = (CONV_LEFT - k) % n
        xs = xx if sh == 0 else pltpu.roll(xx, sh, 0)
        xc = xc + xs[SUBLANES:SUBLANES + TM] * cw_ref[k:k + 1, :]
    rs, is_ = [], []
    for b in range(LRU_BLOCKS):
        xb = xc[:, b * LRU_BLOCK:(b + 1) * LRU_BLOCK]
        rs.append(jnp.dot(xb, wr_ref[d, b], preferred_element_type=F32))
        is_.append(jnp.dot(xb, wi_ref[d, b], preferred_element_type=F32))
    r = jax.nn.sigmoid(jnp.concatenate(rs, axis=-1) + br_ref[d:d + 1, :])
    i = jax.nn.sigmoid(jnp.concatenate(is_, axis=-1) + bi_ref[d:d + 1, :])
    nl = -lam_ref[d:d + 1, :]
    softplus = jnp.maximum(nl, 0.0) + jnp.log1p(jnp.exp(-jnp.abs(nl)))
    log_a = -LRU_C * r * softplus
    a = jnp.exp(log_a)
    bx = jnp.sqrt(jnp.tanh(-log_a) * (a * a + 1.0)) * (i * xc)
    t = lax.broadcasted_iota(jnp.int32, (TM, 1), 0)
    step = 1
    while step < TM:
        if reverse:
            keep = t < TM - step
            sh = TM - step
        else:
            keep = t >= step
            sh = step
        a_s = jnp.where(keep, pltpu.roll(a, sh, 0), 1.0)
        b_s = jnp.where(keep, pltpu.roll(bx, sh, 0), 0.0)
        bx = a * b_s + bx
        a = a * a_s
        step *= 2
    return a * carry + bx


def _lru_kernel(fxp, fxc, fxn, bxp, bxc, bxn, cw_ref, cb_ref, wr_ref, br_ref, wi_ref, bi_ref, lam_ref,
                h0_ref, hf_ref, hb_ref, st_ref, cf_ref, cbk_ref, *, nc):
    c = pl.program_id(1)

    @pl.when(c == 0)
    def _():
        cf_ref[...] = h0_ref[0, 0:1, :]
        cbk_ref[...] = h0_ref[0, 1:2, :]

    params = (cw_ref, cb_ref, wr_ref, br_ref, wi_ref, bi_ref, lam_ref)
    hf = _lru_dir(fxp, fxc, fxn, c > 0, c < nc - 1, 0, False, *params, cf_ref[...])
    hf_ref[...] = hf
    cf_ref[...] = hf[TM - 1:TM, :]
    hb = _lru_dir(bxp, bxc, bxn, c < nc - 1, c > 0, 1, True, *params, cbk_ref[...])
    hb_ref[...] = hb
    cbk_ref[...] = hb[0:1, :]
    st_ref[0, 0:1, :] = hf[TM - 1:TM, :]
    st_ref[0, 1:2, :] = hb[0:1, :]


def _oproj_kernel(xp_ref, xs_ref, mod_ref, atp_ref, ats_ref, hfp_ref, hfs_ref, hbp_ref, hbs_ref, ug_ref,
                  woa_ref, wor_ref, o_ref, *, n_prompt_tiles):
    mod = mod_ref[0]
    at = _token_tile(atp_ref, ats_ref, n_prompt_tiles)
    hf = _token_tile(hfp_ref, hfs_ref, n_prompt_tiles)
    hb = _token_tile(hbp_ref, hbs_ref, n_prompt_tiles)
    rec = ((hf + hb) * jax.nn.gelu(ug_ref[...])).astype(BF16)
    out = (jnp.dot(at, woa_ref[...], preferred_element_type=F32)
           + jnp.dot(rec, wor_ref[...], preferred_element_type=F32))
    o_ref[...] = _token_tile(xp_ref, xs_ref, n_prompt_tiles) + mod[2:3] * out


def _pool_kernel(xp_ref, xc_ref, xn_ref, mod_ref, g_ref, wp_ref, sp_ref, o_ref, *, tiles_per_seq, n_prompt_tiles):
    i = pl.program_id(0)
    j = jnp.where(i < n_prompt_tiles, 0, (i - n_prompt_tiles) % tiles_per_seq)
    ntile = jnp.where(i < n_prompt_tiles, 1, tiles_per_seq)
    mod = mod_ref[0]
    g = g_ref[...]

    def hmod(x):
        return _rms(x, g) * (1.0 + mod[1:2]) + mod[0:1]

    x = xc_ref[...]
    hc = hmod(x)
    hp = jnp.where(j > 0, hmod(xp_ref[...]), 0.0)
    hn = jnp.where(j < ntile - 1, hmod(xn_ref[...]), 0.0)
    hh = jnp.concatenate([hp, hc, hn], axis=0)
    n = TM + 2 * HALO
    seq_len = ntile * TM
    t = j * TM + lax.broadcasted_iota(jnp.int32, (TM, 1), 0)
    ys = []
    for gi, w in enumerate(POOL_WINDOWS):
        cols = slice(gi * POOL_GROUP, (gi + 1) * POOL_GROUP)
        p = hh[:, cols]
        p = p + pltpu.roll(p, 1, 0)
        half = 1
        while 2 * half < w:
            p = pltpu.roll(p, half, 0) + pltpu.roll(p, n - half, 0)
            half *= 2
        lo = jnp.maximum(t - w // 2, 0)
        hi = jnp.minimum(t + (w - w // 2), seq_len)
        mean = p[HALO:HALO + TM] / (hi - lo).astype(F32)
        dg = (mean - hc[:, cols]).astype(BF16)
        ys.append(jnp.dot(dg, wp_ref[gi], preferred_element_type=F32))
    y = jnp.concatenate(ys, axis=-1) * sp_ref[...]
    o_ref[...] = x + mod[2:3] * y


def _sort_pairs(n):
    pairs = []

    def merge(lo, cnt, r):
        step = r * 2
        if step < cnt:
            merge(lo, cnt, step)
            merge(lo + r, cnt, step)
            for i in range(lo + r, lo + cnt - r, step):
                pairs.append((i, i + r))
        else:
            pairs.append((lo, lo + r))

    def sort(lo, cnt):
        if cnt > 1:
            m = cnt // 2
            sort(lo, m)
            sort(lo + m, m)
            merge(lo, cnt, 1)

    sort(0, n)
    return pairs


_SORT16 = _sort_pairs(PEER_TOPK)
_HYPER = [(a, b) for a in range(PEER_TOPK) for b in range(PEER_TOPK) if (a + 1) * (b + 1) <= PEER_TOPK]


def _top16_sorted(s):
    k = PEER_TOPK
    x = [s[SUBLANES * r:SUBLANES * (r + 1), :] for r in range(N_KEYS // SUBLANES)]
    for (i, j) in _SORT16:
        hi = jnp.maximum(x[i], x[j])
        lo = jnp.minimum(x[i], x[j])
        x[i], x[j] = hi, lo
    for shift in (4, 2, 1):
        y = [jnp.maximum(x[r], pltpu.roll(x[k - 1 - r], shift, 0)) for r in range(k)]
        stride = k // 2
        while stride >= 1:
            for i in range(k):
                if i & stride == 0:
                    hi = jnp.maximum(y[i], y[i + stride])
                    lo = jnp.minimum(y[i], y[i + stride])
                    y[i], y[i + stride] = hi, lo
            stride //= 2
        x = y
    return x


def _pair_counts(sv1, sv2):
    one = jnp.ones_like(sv1[0])
    zero = jnp.zeros_like(sv1[0])
    cand = [sv1[a] + sv2[b] for (a, b) in _HYPER]

    def ordered(i, j):
        (ai, bi), (aj, bj) = _HYPER[i], _HYPER[j]
        return (aj <= ai and bj <= bi) or (ai <= aj and bi <= bj)

    nh = len(_HYPER)
    cnt = [float((a + 1) * (b + 1) - 1 + sum(1 for j in range(i + 1, nh) if not ordered(i, j))) * one
           for i, (a, b) in enumerate(_HYPER)]
    for i in range(nh):
        for jx in range(i):
            if ordered(i, jx):
                continue
            ge = jnp.where(cand[jx] >= cand[i], one, zero)
            cnt[i] = cnt[i] + ge
            cnt[jx] = cnt[jx] - ge
    e1 = [jnp.exp(sv1[a] - sv1[0]) for a in range(PEER_TOPK)]
    e2 = [jnp.exp(sv2[b] - sv2[0]) for b in range(PEER_TOPK)]
    n = [zero for _ in range(PEER_TOPK)]
    z = zero
    for i, (a, b) in enumerate(_HYPER):
        sel = jnp.where(cnt[i] < float(PEER_TOPK), one, zero)
        n[a] = n[a] + sel
        z = z + sel * (e1[a] * e2[b])
    return n, 1.0 / z


def _head_tables(hd, sc_ref, sv_ref, nz_ref, ex_ref, cnt1_ref, e1n_ref, rank2_ref, e2_ref, exact):
    k = PEER_TOPK
    tm = sc_ref.shape[-1]
    nslab = N_KEYS // SUBLANES
    bad = []
    for g in range(tm // LANES):
        lanes = slice(g * LANES, (g + 1) * LANES)

        def bc(ref, *idx):
            return jnp.broadcast_to(ref[idx + (slice(hd, hd + 1), lanes)], (SUBLANES, LANES))

        sv1b = [bc(sv_ref, 0, a) for a in range(k)]
        nb = [bc(nz_ref, a) for a in range(k)]
        izb = bc(nz_ref, k)
        c1 = jnp.zeros((SUBLANES, LANES), F32)
        for r in range(nslab):
            rows = slice(r * SUBLANES, (r + 1) * SUBLANES)
            s1 = sc_ref[0, hd, rows, lanes]
            cnt1 = jnp.zeros_like(s1)
            if exact:
                rank1 = jnp.zeros_like(s1)
                for a in range(k):
                    rank1 = jnp.where(sv1b[a] > s1, float(a + 1), rank1)
                rank1 = rank1 + ex_ref[0, rows, lanes]
                for a in range(k):
                    cnt1 = jnp.where(rank1 == float(a), nb[a], cnt1)
            else:
                for a in range(k):
                    cnt1 = jnp.where(s1 == sv1b[a], nb[a], cnt1)
                c1 = c1 + jnp.where(s1 >= sv1b[k - 1], 1.0, 0.0)
            cnt1_ref[hd, rows, lanes] = cnt1
            e1n_ref[hd, rows, lanes] = jnp.exp(s1 - sv1b[0]) * izb
        sv2b = [bc(sv_ref, 1, a) for a in range(k)]
        c2 = jnp.zeros((SUBLANES, LANES), F32)
        for r in range(N_KEYS // BF16_ROWS):
            rk, e2 = [], []
            for q in range(BF16_ROWS // SUBLANES):
                r0 = r * BF16_ROWS + q * SUBLANES
                rows = slice(r0, r0 + SUBLANES)
                s2 = sc_ref[1, hd, rows, lanes]
                rank2 = jnp.zeros_like(s2)
                for a in range(k):
                    rank2 = jnp.where(sv2b[a] > s2, float(a + 1), rank2)
                if exact:
                    rank2 = jnp.minimum(rank2 + ex_ref[1, rows, lanes], float(k))
                else:
                    c2 = c2 + jnp.where(rank2 < float(k), 1.0, 0.0)
                rk.append(rank2)
                e2.append(jnp.exp(s2 - sv2b[0]))
            rows16 = slice(r * BF16_ROWS, (r + 1) * BF16_ROWS)
            rank2_ref[hd, rows16, lanes] = jnp.concatenate(rk, axis=0).astype(BF16)
            e2_ref[hd, rows16, lanes] = jnp.concatenate(e2, axis=0).astype(BF16)
        if not exact:
            n1 = jnp.sum(c1, axis=0, keepdims=True)
            n2 = jnp.sum(c2, axis=0, keepdims=True)
            b = jnp.where(n1 != float(k), 1.0, 0.0) + jnp.where(n2 != float(k), 1.0, 0.0)
            for svb in (sv1b, sv2b):
                for a in range(k - 1):
                    b = b + jnp.where(svb[a][0:1] == svb[a + 1][0:1], 1.0, 0.0)
            bad.append(b)
    return jnp.concatenate(bad, axis=1) if bad else None


def _tie_offsets(hd, sc_ref, ex_ref):
    tm = sc_ref.shape[-1]
    nidx = lax.broadcasted_iota(jnp.int32, (N_KEYS, tm), 0)
    for p in range(2):
        s = sc_ref[p, hd]

        def body(m, e, s=s, p=p):
            row = sc_ref[p, hd, pl.ds(m, 1), :]
            return e + jnp.where((s == row) & (nidx > m), 1.0, 0.0)

        ex_ref[p] = lax.fori_loop(0, N_KEYS, body, jnp.zeros((N_KEYS, tm), F32))


def _peer_kernel(x_ref, mod_ref, g_ref, wqt_ref, keys_ref, u_ref, vt_ref, gfin_ref, *refs,
                 final_norm, n_prompt_tiles):
    outs, scratch = (refs[:2], refs[2:]) if final_norm else (refs[:1], refs[1:])
    h2t_ref, acc_ref, sc_ref, sv_ref, nz_ref, ex_ref, cnt1_ref, e1n_ref, rank2_ref, e2_ref = scratch
    k = pl.program_id(1)
    tm = PEER_TM
    half = PEER_DKEY // 2

    @pl.when(k == 0)
    def _():
        mod = mod_ref[0]
        h2 = _rms(x_ref[...], g_ref[...]) * (1.0 + mod[4:5]) + mod[3:4]
        h2t_ref[...] = h2.T.astype(BF16)
        acc_ref[...] = jnp.zeros_like(acc_ref)
        qt = jnp.dot(wqt_ref[...], h2t_ref[...], preferred_element_type=F32).astype(BF16)
        for hd in range(PEER_HEADS):
            for p in range(2):
                r0 = (hd * 2 + p) * half
                sc_ref[p, hd] = jnp.dot(keys_ref[hd, p], qt[r0:r0 + half, :],
                                        preferred_element_type=F32)
                for g in range(tm // LANES):
                    lanes = slice(g * LANES, (g + 1) * LANES)
                    top = _top16_sorted(sc_ref[p, hd, :, lanes])
                    for a in range(PEER_TOPK):
                        sv_ref[p, a, hd:hd + 1, lanes] = top[a][0:1, :]
        for g in range(tm // LANES):
            lanes = slice(g * LANES, (g + 1) * LANES)
            sv1 = [sv_ref[0, a, :, lanes] for a in range(PEER_TOPK)]
            sv2 = [sv_ref[1, a, :, lanes] for a in range(PEER_TOPK)]
            n, inv_z = _pair_counts(sv1, sv2)
            for a in range(PEER_TOPK):
                nz_ref[a, :, lanes] = n[a]
            nz_ref[PEER_TOPK, :, lanes] = inv_z
        tabs = (sc_ref, sv_ref, nz_ref, ex_ref, cnt1_ref, e1n_ref, rank2_ref, e2_ref)
        for hd in range(PEER_HEADS):
            bad = _head_tables(hd, *tabs, exact=False)

            @pl.when(jnp.max(bad) > 0.0)
            def _(hd=hd):
                _tie_offsets(hd, sc_ref, ex_ref)
                _head_tables(hd, *tabs, exact=True)

    zero = jnp.zeros((BF16_ROWS, tm), BF16)
    npiece = PEER_EBLK // N_KEYS
    per_sub = PEER_SUB // N_KEYS

    def up_dot(q):
        return jnp.dot(u_ref[2 * q * N_KEYS:2 * (q + 1) * N_KEYS, :], h2t_ref[...], preferred_element_type=F32)

    raw2 = up_dot(0)
    ys = []
    for p in range(npiece):
        if p % 2 == 0:
            nxt = up_dot(p // 2 + 1) if p + 2 < npiece else None
        raw = raw2[(p % 2) * N_KEYS:(p % 2 + 1) * N_KEYS, :]
        i1 = k * npiece + p
        cbs = [jnp.broadcast_to(cnt1_ref[hd, pl.ds(i1, 1), :], (BF16_ROWS, tm)).astype(BF16)
               for hd in range(PEER_HEADS)]
        ebs = [jnp.broadcast_to(e1n_ref[hd, pl.ds(i1, 1), :], (BF16_ROWS, tm)).astype(BF16)
               for hd in range(PEER_HEADS)]
        act = jax.nn.gelu(raw.astype(BF16))
        for r in range(N_KEYS // BF16_ROWS):
            rows = slice(r * BF16_ROWS, (r + 1) * BF16_ROWS)
            w = None
            for hd in range(PEER_HEADS):
                t = jnp.where(rank2_ref[hd, rows, :] < cbs[hd], e2_ref[hd, rows, :], zero) * ebs[hd]
                w = t if w is None else w + t
            ys.append(act[rows, :] * w)
        if p % 2 == 1:
            raw2 = nxt
        if (p + 1) % per_sub == 0:
            e0 = (p + 1 - per_sub) * N_KEYS
            y = jnp.concatenate(ys, axis=0)
            ys = []
            acc_ref[...] += jnp.dot(vt_ref[:, e0:e0 + PEER_SUB], y, preferred_element_type=F32)

    @pl.when(k == pl.num_programs(1) - 1)
    def _():
        out = x_ref[...] + mod_ref[0][5:6] * acc_ref[...].T
        if final_norm:
            out = _rms(out, gfin_ref[...])
            is_prompt = pl.program_id(0) < n_prompt_tiles

            @pl.when(is_prompt)
            def _():
                outs[0][...] = out

            @pl.when(jnp.logical_not(is_prompt))
            def _():
                outs[1][...] = out
        else:
            outs[0][...] = out


def _rope_tables(s_prompt_tile, s_sample):
    n_rows = s_sample // GRID_W
    rows = jnp.repeat(jnp.arange(n_rows, dtype=F32), GRID_W)
    cols = jnp.tile(jnp.arange(GRID_W, dtype=F32), n_rows)
    axis_dim = QK_ROPE // 2
    inv_freq = ROPE_BASE ** (-jnp.arange(0, axis_dim, 2, dtype=F32) / axis_dim)
    ang = jnp.concatenate([rows[:, None] * inv_freq, cols[:, None] * inv_freq], axis=-1)
    cos, sin = jnp.cos(ang), jnp.sin(ang)
    ck = jnp.concatenate([cos, cos], axis=-1)
    sk = jnp.concatenate([-sin, sin], axis=-1)
    ident_c = jnp.ones((s_prompt_tile, QK_ROPE), F32)
    ident_s = jnp.zeros((s_prompt_tile, QK_ROPE), F32)
    return jnp.concatenate([ident_c, ck], axis=0), jnp.concatenate([ident_s, sk], axis=0)


def kernel(x_prompt, x_sample, cache_ckv_l0, cache_krope_l0, state_lru_l0, c, c_ctx, w_mod_l0, b_mod_l0, w_mod_l1, b_mod_l1, g_mix_l0, g_ffn_l0, g_mix_l1, g_ffn_l1, w_in_l0, g_q_l0, w_uq_l0, g_kv_l0, w_ukv_l0, conv_w_l0, conv_b_l0, w_rg_l0, b_rg_l0, w_ig_l0, b_ig_l0, lam_l0, w_o_l0, w_pool_l1, s_pool_l1, peer_wq_l0, peer_keys_l0, peer_u_l0, peer_v_l0, peer_wq_l1, peer_keys_l1, peer_u_l1, peer_v_l1, g_final):
    nb_p, s_p, d = x_prompt.shape
    nb_s, s_s, _ = x_sample.shape
    n_cache = cache_ckv_l0.shape[1]
    assert d == D_MODEL and s_p == TM and s_s % TM == 0 and n_cache % TM == 0
    t_p = nb_p * s_p
    t_s = nb_s * s_s
    t_all = t_p + t_s
    npt = t_p // TM
    tps = s_s // TM
    ntile = t_all // TM
    assert t_all % PEER_TM == 0 and t_p % PEER_TM == 0 and s_s % PEER_TM == 0

    xp2, xs2 = x_prompt.reshape(t_p, d), x_sample.reshape(t_s, d)
    x_specs = [pl.BlockSpec((TM, d), lambda i: (jnp.minimum(i, npt - 1), 0)),
               pl.BlockSpec((TM, d), lambda i: (jnp.maximum(i - npt, 0), 0))]

    ncond = 1 + nb_s
    cpad = jnp.zeros((2 * SUBLANES, d), F32).at[0].set(c_ctx).at[1:ncond].set(c)
    mod0 = _ada(cpad, w_mod_l0, b_mod_l0).reshape(2 * SUBLANES, 6, d)
    mod1 = _ada(cpad, w_mod_l1, b_mod_l1).reshape(2 * SUBLANES, 6, d)

    def cond_row(i):
        return jnp.where(i < npt, 0, 1 + (i - npt) // tps)

    mod_spec = pl.BlockSpec((1, 6, d), lambda i: (cond_row(i), 0, 0))
    row = lambda a: a.reshape(1, -1)

    perm = np.concatenate([np.arange(0, QK_ROPE, 2), np.arange(1, QK_ROPE, 2)])
    perm_sw = np.concatenate([np.arange(1, QK_ROPE, 2), np.arange(0, QK_ROPE, 2)])
    o1 = Q_LORA + KV_LORA
    w_kr = w_in_l0[:, o1:o1 + QK_ROPE]
    z64 = jnp.zeros((d, 128 - QK_ROPE), F32)
    w_in_ext = jnp.concatenate(
        [w_in_l0[:, :o1], w_kr, z64, w_kr[:, perm], z64, w_kr[:, perm_sw], z64, w_in_l0[:, o1 + QK_ROPE:]],
        axis=1).astype(BF16)
    assert w_in_ext.shape[1] == IN_EXT
    wq3 = w_uq_l0.reshape(Q_LORA, MLA_HEADS, QK_NOPE + QK_ROPE)
    w_uq_ext = jnp.concatenate(
        [wq3[:, :, :QK_NOPE].reshape(Q_LORA, -1),
         wq3[:, :, QK_NOPE:][:, :, perm].reshape(Q_LORA, -1),
         wq3[:, :, QK_NOPE:][:, :, perm_sw].reshape(Q_LORA, -1)], axis=1).astype(BF16)
    ck_tab, sk_tab = _rope_tables(TM, s_s)

    def rope_blk(i):
        return jnp.where(i < npt, 0, 1 + (i - npt) % tps)

    tok = lambda w: pl.BlockSpec((TM, w), lambda i: (i, 0))
    full = lambda a: pl.BlockSpec(a.shape, lambda *_: (0,) * a.ndim)
    q, ckv, kr, krr, ux, ug = pl.pallas_call(
        functools.partial(_inproj_kernel, n_prompt_tiles=npt),
        grid=(ntile,),
        in_specs=x_specs + [mod_spec, full(row(g_mix_l0)), full(w_in_ext), full(row(g_q_l0)), full(w_uq_ext),
                  full(row(g_kv_l0)),
                  pl.BlockSpec((TM, QK_ROPE), lambda i: (rope_blk(i), 0)),
                  pl.BlockSpec((TM, QK_ROPE), lambda i: (rope_blk(i), 0))],
        out_specs=[pl.BlockSpec((MLA_HEADS, TM, QK_PAD), lambda i: (0, i, 0)),
                   tok(KV_LORA), tok(QK_ROPE), tok(QK_ROPE), tok(LRU_WIDTH), tok(LRU_WIDTH)],
        out_shape=[jax.ShapeDtypeStruct((MLA_HEADS, t_all, QK_PAD), BF16),
                   jax.ShapeDtypeStruct((t_all, KV_LORA), F32),
                   jax.ShapeDtypeStruct((t_all, QK_ROPE), F32),
                   jax.ShapeDtypeStruct((t_all, QK_ROPE), BF16),
                   jax.ShapeDtypeStruct((t_all, LRU_WIDTH), F32),
                   jax.ShapeDtypeStruct((t_all, LRU_WIDTH), F32)],
        compiler_params=_cparams(("arbitrary",)),
    )(xp2, xs2, mod0, row(g_mix_l0), w_in_ext, row(g_q_l0), w_uq_ext, row(g_kv_l0), ck_tab, sk_tab)

    wkv3 = w_ukv_l0.reshape(KV_LORA, MLA_HEADS, QK_NOPE + V_HEAD)
    w_ukv_ext = jnp.concatenate([wkv3[:, :, :QK_NOPE].reshape(KV_LORA, -1),
                                 wkv3[:, :, QK_NOPE:].reshape(KV_LORA, -1)], axis=1).astype(BF16)

    def attn_call(nb, s_new, tile0, has_cache):
        nq = s_new // TM
        blk0 = tile0 * TM // s_new
        n_c = n_cache if has_cache else 0
        in_specs = [pl.BlockSpec((MLA_HEADS, TM, QK_PAD), lambda b, qi: (0, tile0 + b * nq + qi, 0)),
                    pl.BlockSpec((s_new, KV_LORA), lambda b, qi: (blk0 + b, 0)),
                    pl.BlockSpec((s_new, QK_ROPE), lambda b, qi: (blk0 + b, 0))]
        args = [q, ckv, krr]
        if has_cache:
            in_specs += [pl.BlockSpec((1, n_cache, KV_LORA), lambda b, qi: (b, 0, 0)),
                         pl.BlockSpec((1, n_cache, QK_ROPE), lambda b, qi: (b, 0, 0))]
            args += [cache_ckv_l0, cache_krope_l0[:, :, perm]]
        in_specs.append(pl.BlockSpec(w_ukv_ext.shape, lambda b, qi: (0, 0)))
        args.append(w_ukv_ext)
        return pl.pallas_call(
            functools.partial(_attn_kernel, has_cache=has_cache, s_new=s_new, n_cache=n_c),
            grid=(nb, nq),
            in_specs=in_specs,
            out_specs=pl.BlockSpec((TM, MLA_WIDTH), lambda b, qi: (b * nq + qi, 0)),
            out_shape=jax.ShapeDtypeStruct((nb * s_new, MLA_WIDTH), BF16),
            scratch_shapes=[pltpu.VMEM((MLA_HEADS, n_c + s_new, QK_PAD), BF16),
                            pltpu.VMEM((n_c + s_new, MLA_WIDTH), BF16)],
            compiler_params=_cparams(("arbitrary", "arbitrary")),
        )(*args)

    attn_p, attn_s = attn_call(nb_p, s_p, 0, False), attn_call(nb_s, s_s, npt, True)

    def lru_call(nb, s_new, tile0, h0):
        nc = s_new // TM
        r8 = TM // SUBLANES
        last8 = t_all // SUBLANES - 1

        def cur(rev):
            return pl.BlockSpec((TM, LRU_WIDTH),
                                lambda b, cc: (tile0 + b * nc + (nc - 1 - cc if rev else cc), 0))

        def prev(rev):
            return pl.BlockSpec((SUBLANES, LRU_WIDTH), lambda b, cc: (
                jnp.maximum((tile0 + b * nc + (nc - 1 - cc if rev else cc)) * r8 - 1, 0), 0))

        def nxt(rev):
            return pl.BlockSpec((SUBLANES, LRU_WIDTH), lambda b, cc: (
                jnp.minimum((tile0 + b * nc + (nc - 1 - cc if rev else cc) + 1) * r8, last8), 0))

        small = [conv_w_l0, row(conv_b_l0), w_rg_l0, b_rg_l0, w_ig_l0, b_ig_l0, lam_l0]
        return pl.pallas_call(
            functools.partial(_lru_kernel, nc=nc),
            grid=(nb, nc),
            in_specs=[prev(False), cur(False), nxt(False), prev(True), cur(True), nxt(True)]
                     + [full(a) for a in small]
                     + [pl.BlockSpec((1, 2, LRU_WIDTH), lambda b, cc: (b, 0, 0))],
            out_specs=[pl.BlockSpec((TM, LRU_WIDTH), lambda b, cc: (b * nc + cc, 0)),
                       pl.BlockSpec((TM, LRU_WIDTH), lambda b, cc: (b * nc + nc - 1 - cc, 0)),
                       pl.BlockSpec((1, 2, LRU_WIDTH), lambda b, cc: (b, 0, 0))],
            out_shape=[jax.ShapeDtypeStruct((nb * s_new, LRU_WIDTH), F32),
                       jax.ShapeDtypeStruct((nb * s_new, LRU_WIDTH), F32),
                       jax.ShapeDtypeStruct((nb, 2, LRU_WIDTH), F32)],
            scratch_shapes=[pltpu.VMEM((1, LRU_WIDTH), F32), pltpu.VMEM((1, LRU_WIDTH), F32)],
            compiler_params=_cparams(("arbitrary", "arbitrary")),
        )(ux, ux, ux, ux, ux, ux, *small, h0)

    hf_p, hb_p, new_lru = lru_call(nb_p, s_p, 0, jnp.zeros((nb_p, 2, LRU_WIDTH), F32))
    hf_s, hb_s, _ = lru_call(nb_s, s_s, npt, state_lru_l0.astype(F32))

    w_o = w_o_l0.astype(BF16)

    def split(w):
        return [pl.BlockSpec((TM, w), lambda i: (jnp.minimum(i, npt - 1), 0)),
                pl.BlockSpec((TM, w), lambda i: (jnp.maximum(i - npt, 0), 0))]

    x1 = pl.pallas_call(
        functools.partial(_oproj_kernel, n_prompt_tiles=npt),
        grid=(ntile,),
        in_specs=x_specs + [mod_spec] + split(MLA_WIDTH) + split(LRU_WIDTH) + split(LRU_WIDTH)
                 + [tok(LRU_WIDTH),
                    pl.BlockSpec((MLA_WIDTH, d), lambda i: (0, 0)), pl.BlockSpec((LRU_WIDTH, d), lambda i: (1, 0))],
        out_specs=tok(d),
        out_shape=jax.ShapeDtypeStruct((t_all, d), F32),
        compiler_params=_cparams(("arbitrary",)),
    )(xp2, xs2, mod0, attn_p, attn_s, hf_p, hf_s, hb_p, hb_s, ug, w_o, w_o)

    def peer_call(x, mod, g_ffn, w_q, sub_keys, u, v, final_norm):
        n_exp = u.shape[0]
        assert n_exp == N_KEYS * N_KEYS and n_exp % PEER_EBLK == 0
        wqt = w_q.T.astype(BF16)
        keys = sub_keys.astype(BF16)
        ub = u.astype(BF16)
        vt = v.T.astype(BF16)
        tpp = PEER_TM // TM

        def cond_row_p(i):
            return cond_row(i * tpp)

        big = lambda dt: pltpu.VMEM((PEER_HEADS, N_KEYS, PEER_TM), dt)
        npp = t_p // PEER_TM
        if final_norm:
            out_specs = [pl.BlockSpec((PEER_TM, d), lambda i, k: (jnp.minimum(i, npp - 1), 0)),
                         pl.BlockSpec((PEER_TM, d), lambda i, k: (jnp.maximum(i - npp, 0), 0))]
            out_shape = [jax.ShapeDtypeStruct((t_p, d), F32), jax.ShapeDtypeStruct((t_s, d), F32)]
        else:
            out_specs = pl.BlockSpec((PEER_TM, d), lambda i, k: (i, 0))
            out_shape = jax.ShapeDtypeStruct((t_all, d), F32)
        return pl.pallas_call(
            functools.partial(_peer_kernel, final_norm=final_norm, n_prompt_tiles=npp),
            grid=(t_all // PEER_TM, n_exp // PEER_EBLK),
            in_specs=[pl.BlockSpec((PEER_TM, d), lambda i, k: (i, 0)),
                      pl.BlockSpec((1, 6, d), lambda i, k: (cond_row_p(i), 0, 0)),
                      pl.BlockSpec((1, d), lambda i, k: (0, 0)),
                      pl.BlockSpec(wqt.shape, lambda i, k: (0, 0)),
                      pl.BlockSpec(keys.shape, lambda i, k: (0, 0, 0, 0)),
                      pl.BlockSpec((PEER_EBLK, d), lambda i, k: (k, 0)),
                      pl.BlockSpec((d, PEER_EBLK), lambda i, k: (0, k)),
                      pl.BlockSpec((1, d), lambda i, k: (0, 0))],
            out_specs=out_specs,
            out_shape=out_shape,
            scratch_shapes=[pltpu.VMEM((d, PEER_TM), BF16),
                            pltpu.VMEM((d, PEER_TM), F32),
                            pltpu.VMEM((2, PEER_HEADS, N_KEYS, PEER_TM), F32),
                            pltpu.VMEM((2, PEER_TOPK, PEER_HEADS, PEER_TM), F32),
                            pltpu.VMEM((PEER_TOPK + 1, PEER_HEADS, PEER_TM), F32),
                            pltpu.VMEM((2, N_KEYS, PEER_TM), F32),
                            big(F32), big(F32), big(BF16), big(BF16)],
            compiler_params=_cparams(("arbitrary", "arbitrary")),
        )(x, mod, row(g_ffn), wqt, keys, ub, vt, row(g_final))

    x2 = peer_call(x1, mod0, g_ffn_l0, peer_wq_l0, peer_keys_l0, peer_u_l0, peer_v_l0, False)

    rh = TM // HALO
    lasth = t_all // HALO - 1
    x3 = pl.pallas_call(
        functools.partial(_pool_kernel, tiles_per_seq=tps, n_prompt_tiles=npt),
        grid=(ntile,),
        in_specs=[pl.BlockSpec((HALO, d), lambda i: (jnp.maximum(i * rh - 1, 0), 0)),
                  tok(d),
                  pl.BlockSpec((HALO, d), lambda i: (jnp.minimum((i + 1) * rh, lasth), 0)),
                  mod_spec, full(row(g_mix_l1)),
                  pl.BlockSpec(w_pool_l1.shape, lambda i: (0, 0, 0)), full(row(s_pool_l1))],
        out_specs=tok(d),
        out_shape=jax.ShapeDtypeStruct((t_all, d), F32),
        compiler_params=_cparams(("arbitrary",)),
    )(x2, x2, x2, mod1, row(g_mix_l1), w_pool_l1.astype(BF16), row(s_pool_l1))

    y_p, y_s = peer_call(x3, mod1, g_ffn_l1, peer_wq_l1, peer_keys_l1, peer_u_l1, peer_v_l1, True)

    y_prompt = y_p.reshape(nb_p, s_p, d)
    y_sample = y_s.reshape(nb_s, s_s, d)
    new_ckv = ckv[:t_p].reshape(nb_p, s_p, KV_LORA)
    new_krope = kr[:t_p].reshape(nb_p, s_p, QK_ROPE)
    return (y_prompt, y_sample, new_ckv, new_krope, new_lru)
```
